```python
import jax, jax.numpy as jnp
from jax import lax
import numpy as np

D_MODEL = 2048
BATCH = 4
SEQ = 2048
DEPTH = 1
DEC_BATCH = 128
DEC_SEQ = 8
PAST_LEN = 16384
PAGE_SIZE = 128

D_MIX = D_MODEL
GDN_DK = 128
GDN_DV = 128
GDN_HEADS = (D_MIX // 2) // GDN_DV
GDN_QK = GDN_HEADS * GDN_DK
GDN_V = GDN_HEADS * GDN_DV
GDN_CONV_CH = 2 * GDN_QK + GDN_V
SSM_P = 64
SSM_N = 128
SSM_GROUPS = 2
SSM_DI = D_MIX - GDN_V
SSM_HEADS = SSM_DI // SSM_P
SSM_BC = SSM_GROUPS * SSM_N
SSM_CONV_CH = SSM_DI + 2 * SSM_BC
CONV_W = 4
CHUNK = 64
D_FF = -(-8 * D_MODEL // (3 * 256)) * 256
EPS = 1e-6
SPLITS = [GDN_CONV_CH, GDN_V, GDN_HEADS, GDN_HEADS, SSM_DI, SSM_CONV_CH, SSM_HEADS]
D_IN_PROJ = sum(SPLITS)

kernel_name = 'hybrid_gdn_ssd_parallel_heads_step'


def rmsnorm(x, w):
    x = x.astype(jnp.float32)
    return x * lax.rsqrt(jnp.mean(x * x, axis=-1, keepdims=True) + EPS) * w.astype(jnp.float32)


def l2norm(x):
    return x * lax.rsqrt(jnp.sum(x * x, axis=-1, keepdims=True) + EPS)


def causal_conv(x, buf, w, b):
    L = x.shape[1]
    xp = jnp.concatenate([buf.astype(jnp.float32), x], axis=1)
    out = sum(xp[:, i:i + L] * w[i].astype(jnp.float32) for i in range(CONV_W))
    if b is not None:
        out = out + b.astype(jnp.float32)
    return jax.nn.silu(out), xp[:, xp.shape[1] - (CONV_W - 1):]


def _chunk_len(L):
    return CHUNK if L >= CHUNK else L


def _chunks(t, c):
    B, L = t.shape[:2]
    n = -(-L // c)
    t = jnp.pad(t, [(0, 0), (0, n * c - L)] + [(0, 0)] * (t.ndim - 2))
    t = t.reshape((B, n, c) + t.shape[2:])
    return jnp.moveaxis(jnp.moveaxis(t, 3, 2), 1, 0)


def _unchunks(o, L):
    n, B, H, c, E = o.shape
    o = jnp.swapaxes(jnp.moveaxis(o, 0, 1), 2, 3).reshape(B, n * c, H, E)
    return o[:, :L]


def _decay(G, c):
    tril = jnp.tril(jnp.ones((c, c), bool))
    diff = G[..., :, None] - G[..., None, :]
    return jnp.exp(jnp.where(tril, diff, -jnp.inf))


def gated_delta_chunked(q, k, v, g, beta, S0):
    L = q.shape[1]
    dv = v.shape[-1]
    c = _chunk_len(L)
    qc, kc, vc, gc, bc = (_chunks(t, c) for t in (q, k, v, g, beta))
    G = jnp.cumsum(gc, axis=-1)
    decay = _decay(G, c)
    strict = jnp.tril(jnp.ones((c, c), bool), -1)
    kb = kc * bc[..., None]
    low = jnp.where(strict, jnp.einsum('nbhid,nbhjd->nbhij', kb, kc) * decay, 0.0)
    amat = low + jnp.eye(c, dtype=low.dtype)
    rhs = jnp.concatenate([vc * bc[..., None], kb * jnp.exp(G)[..., None]], axis=-1)
    sol = lax.linalg.triangular_solve(amat, rhs, left_side=True, lower=True, unit_diagonal=True)
    u, w = sol[..., :dv], sol[..., dv:]
    qk = jnp.einsum('nbhid,nbhjd->nbhij', qc, kc) * decay
    q_dec = qc * jnp.exp(G)[..., None]
    k_dec = kc * jnp.exp(G[..., -1:] - G)[..., None]
    g_last = jnp.exp(G[..., -1])

    def step(S, inp):
        u_i, w_i, qk_i, qd_i, kd_i, gl_i = inp
        v_new = u_i - jnp.einsum('bhcd,bhde->bhce', w_i, S)
        o = jnp.einsum('bhcd,bhde->bhce', qd_i, S) + jnp.einsum('bhij,bhje->bhie', qk_i, v_new)
        S = S * gl_i[..., None, None] + jnp.einsum('bhcd,bhce->bhde', kd_i, v_new)
        return S, o

    S, o = lax.scan(step, S0, (u, w, qk, q_dec, k_dec, g_last))
    return _unchunks(o, L), S


def ssd_chunked(x, dt, A, Bh, Ch, h0):
    L = x.shape[1]
    c = _chunk_len(L)
    xc, dtc, Bc, Cc = (_chunks(t, c) for t in (x, dt, Bh, Ch))
    Acum = jnp.cumsum(dtc * A[:, None], axis=-1)
    decay = _decay(Acum, c)
    xdt = xc * dtc[..., None]
    scores = jnp.einsum('nbhis,nbhjs->nbhij', Cc, Bc) * decay
    y_diag = jnp.einsum('nbhij,nbhjp->nbhip', scores, xdt)
    a_last = Acum[..., -1:]
    chunk_states = jnp.einsum('nbhjs,nbhjp->nbhps', Bc * jnp.exp(a_last - Acum)[..., None], xdt)
    c_dec = Cc * jnp.exp(Acum)[..., None]

    def step(h, inp):
        cd, st, al = inp
        y_off = jnp.einsum('bhis,bhps->bhip', cd, h)
        return h * al[..., None, None] + st, y_off

    h, y_off = lax.scan(step, h0, (c_dec, chunk_states, jnp.exp(a_last[..., 0])))
    return _unchunks(y_diag + y_off, L), h


def hybrid_layer(x, gdn_conv, gdn_S, ssm_conv, ssm_h, p):
    B, L, _ = x.shape
    h = rmsnorm(x, p['attn_norm_w'])
    proj = h @ p['w_in'].astype(jnp.float32)
    offs = np.cumsum(SPLITS)[:-1].tolist()
    qkv_raw, z_g, b_raw, a_raw, z_s, xbc_raw, dt_raw = jnp.split(proj, offs, axis=-1)

    qkv, gdn_conv_new = causal_conv(qkv_raw, gdn_conv, p['gdn_conv_w'], None)
    q, k, v = jnp.split(qkv, [GDN_QK, 2 * GDN_QK], axis=-1)
    q = l2norm(q.reshape(B, L, GDN_HEADS, GDN_DK)) * (GDN_DK ** -0.5)
    k = l2norm(k.reshape(B, L, GDN_HEADS, GDN_DK))
    v = v.reshape(B, L, GDN_HEADS, GDN_DV)
    beta = jax.nn.sigmoid(b_raw)
    g = -jnp.exp(p['gdn_A_log'].astype(jnp.float32)) * jax.nn.softplus(a_raw + p['gdn_dt_bias'].astype(jnp.float32))
    o, gdn_S_new = gated_delta_chunked(q, k, v, g, beta, gdn_S.astype(jnp.float32))
    o = rmsnorm(o, p['gdn_norm_w']) * jax.nn.silu(z_g.reshape(B, L, GDN_HEADS, GDN_DV))
    o = o.reshape(B, L, GDN_V)

    xbc, ssm_conv_new = causal_conv(xbc_raw, ssm_conv, p['ssm_conv_w'], p['ssm_conv_b'])
    xs, Bm, Cm = jnp.split(xbc, [SSM_DI, SSM_DI + SSM_BC], axis=-1)
    xs = xs.reshape(B, L, SSM_HEADS, SSM_P)
    rep = SSM_HEADS // SSM_GROUPS
    Bh = jnp.repeat(Bm.reshape(B, L, SSM_GROUPS, SSM_N), rep, axis=2)
    Ch = jnp.repeat(Cm.reshape(B, L, SSM_GROUPS, SSM_N), rep, axis=2)
    dt = jax.nn.softplus(dt_raw + p['ssm_dt_bias'].astype(jnp.float32))
    A = -jnp.exp(p['ssm_A_log'].astype(jnp.float32))
    y, ssm_h_new = ssd_chunked(xs, dt, A, Bh, Ch, ssm_h.astype(jnp.float32))
    y = y + p['ssm_D'].astype(jnp.float32)[:, None] * xs
    y = y.reshape(B, L, SSM_DI) * jax.nn.silu(z_s)
    gs = SSM_DI // SSM_GROUPS
    y = rmsnorm(y.reshape(B, L, SSM_GROUPS, gs), p['ssm_norm_w'].reshape(SSM_GROUPS, gs)).reshape(B, L, SSM_DI)

    mix = jnp.concatenate([o, y], axis=-1)
    x = x + mix @ p['w_out'].astype(jnp.float32)

    h2 = rmsnorm(x, p['ffn_norm_w'])
    ff = jax.nn.silu(h2 @ p['w_gate'].astype(jnp.float32)) * (h2 @ p['w_up'].astype(jnp.float32))
    x = x + ff @ p['w_down'].astype(jnp.float32)
    return x, (gdn_conv_new, gdn_S_new, ssm_conv_new, ssm_h_new)


def trunk(x, gdn_conv, gdn_S, ssm_conv, ssm_h, p, final_norm_w):
    h = x.astype(jnp.float32)
    outs = ([], [], [], [])
    for l in range(DEPTH):
        pl = {name: arr[l] for name, arr in p.items()}
        h, st = hybrid_layer(h, gdn_conv[l], gdn_S[l], ssm_conv[l], ssm_h[l], pl)
        for lst, s in zip(outs, st):
            lst.append(s.astype(x.dtype))
    y = rmsnorm(h, final_norm_w).astype(x.dtype)
    return y, [jnp.stack(lst, axis=0) for lst in outs]


def setup_inputs(seed: int = 0) -> dict:
    key = jax.random.key(seed)
    ks = jax.random.split(key, 32)
    f32 = jnp.float32
    nrm = lambda k, shape, s: jax.random.normal(k, shape, f32) * s

    def inv_softplus_dt(k, n):
        dt = jnp.exp(jax.random.uniform(k, (DEPTH, n), f32, float(np.log(1e-3)), float(np.log(1e-1))))
        return dt + jnp.log(-jnp.expm1(-dt))

    return {
        'x_prompt': nrm(ks[0], (BATCH, SEQ, D_MODEL), 1.0),
        'x_sample': nrm(ks[1], (DEC_BATCH, DEC_SEQ, D_MODEL), 1.0),
        'state_gdn_conv': nrm(ks[2], (DEPTH, DEC_BATCH, CONV_W - 1, GDN_CONV_CH), 1.0),
        'state_gdn': nrm(ks[3], (DEPTH, DEC_BATCH, GDN_HEADS, GDN_DK, GDN_DV), 0.05),
        'state_ssm_conv': nrm(ks[4], (DEPTH, DEC_BATCH, CONV_W - 1, SSM_CONV_CH), 1.0),
        'state_ssm': nrm(ks[5], (DEPTH, DEC_BATCH, SSM_HEADS, SSM_P, SSM_N), 0.1),
        'attn_norm_w': 1.0 + nrm(ks[6], (DEPTH, D_MODEL), 0.02),
        'w_in': nrm(ks[7], (DEPTH, D_MODEL, D_IN_PROJ), D_MODEL ** -0.5),
        'gdn_conv_w': nrm(ks[8], (DEPTH, CONV_W, GDN_CONV_CH), 0.5),
        'gdn_A_log': jnp.log(jax.random.uniform(ks[9], (DEPTH, GDN_HEADS), f32, 1.0, 16.0)),
        'gdn_dt_bias': inv_softplus_dt(ks[10], GDN_HEADS),
        'gdn_norm_w': 1.0 + nrm(ks[11], (DEPTH, GDN_DV), 0.02),
        'ssm_conv_w': nrm(ks[12], (DEPTH, CONV_W, SSM_CONV_CH), 0.5),
        'ssm_conv_b': nrm(ks[13], (DEPTH, SSM_CONV_CH), 0.02),
        'ssm_A_log': jnp.log(jax.random.uniform(ks[14], (DEPTH, SSM_HEADS), f32, 1.0, 16.0)),
        'ssm_dt_bias': inv_softplus_dt(ks[15], SSM_HEADS),
        'ssm_D': 1.0 + nrm(ks[16], (DEPTH, SSM_HEADS), 0.1),
        'ssm_norm_w': 1.0 + nrm(ks[17], (DEPTH, SSM_DI), 0.02),
        'w_out': nrm(ks[18], (DEPTH, D_MIX, D_MODEL), D_MIX ** -0.5),
        'ffn_norm_w': 1.0 + nrm(ks[19], (DEPTH, D_MODEL), 0.02),
        'w_gate': nrm(ks[20], (DEPTH, D_MODEL, D_FF), D_MODEL ** -0.5),
        'w_up': nrm(ks[21], (DEPTH, D_MODEL, D_FF), D_MODEL ** -0.5),
        'w_down': nrm(ks[22], (DEPTH, D_FF, D_MODEL), D_FF ** -0.5),
        'final_norm_w': 1.0 + nrm(ks[23], (D_MODEL,), 0.02),
    }


def reference(x_prompt, x_sample, state_gdn_conv, state_gdn, state_ssm_conv, state_ssm,
              attn_norm_w, w_in, gdn_conv_w, gdn_A_log, gdn_dt_bias, gdn_norm_w,
              ssm_conv_w, ssm_conv_b, ssm_A_log, ssm_dt_bias, ssm_D, ssm_norm_w,
              w_out, ffn_norm_w, w_gate, w_up, w_down, final_norm_w):
    p = dict(attn_norm_w=attn_norm_w, w_in=w_in, gdn_conv_w=gdn_conv_w, gdn_A_log=gdn_A_log,
             gdn_dt_bias=gdn_dt_bias, gdn_norm_w=gdn_norm_w, ssm_conv_w=ssm_conv_w,
             ssm_conv_b=ssm_conv_b, ssm_A_log=ssm_A_log, ssm_dt_bias=ssm_dt_bias, ssm_D=ssm_D,
             ssm_norm_w=ssm_norm_w, w_out=w_out, ffn_norm_w=ffn_norm_w, w_gate=w_gate,
             w_up=w_up, w_down=w_down)
    Bp = x_prompt.shape[0]
    dt_ = x_prompt.dtype
    z_gconv = jnp.zeros((DEPTH, Bp, CONV_W - 1, GDN_CONV_CH), dt_)
    z_gS = jnp.zeros((DEPTH, Bp, GDN_HEADS, GDN_DK, GDN_DV), dt_)
    z_sconv = jnp.zeros((DEPTH, Bp, CONV_W - 1, SSM_CONV_CH), dt_)
    z_sh = jnp.zeros((DEPTH, Bp, SSM_HEADS, SSM_P, SSM_N), dt_)
    y_prompt, st_p = trunk(x_prompt, z_gconv, z_gS, z_sconv, z_sh, p, final_norm_w)
    y_sample, st_s = trunk(x_sample, state_gdn_conv, state_gdn, state_ssm_conv, state_ssm, p, final_norm_w)
    return (y_prompt, y_sample, st_p[0], st_p[1], st_p[2], st_p[3], st_s[0], st_s[1], st_s[2], st_s[3])
```

```python
import functools

import jax
import jax.numpy as jnp
from jax import lax
from jax.experimental import pallas as pl
from jax.experimental.pallas import tpu as pltpu

F32 = jnp.float32
BF16 = jnp.bfloat16

D_MODEL = 2048
GDN_HEADS = 8
GDN_DK = 128
GDN_DV = 128
GDN_QK = GDN_HEADS * GDN_DK
GDN_V = GDN_HEADS * GDN_DV
GDN_CONV_CH = 2 * GDN_QK + GDN_V
SSM_P = 64
SSM_N = 128
SSM_GROUPS = 2
SSM_DI = 1024
SSM_HEADS = SSM_DI // SSM_P
SSM_BC = SSM_GROUPS * SSM_N
SSM_CONV_CH = SSM_DI + 2 * SSM_BC
CONV_W = 4
CHUNK = 64
D_FF = 5632
EPS = 1e-6

OFF_QKV = 0
OFF_ZG = OFF_QKV + GDN_CONV_CH
OFF_ZS = OFF_ZG + GDN_V
OFF_XS = OFF_ZS + SSM_DI
OFF_BC = OFF_XS + SSM_DI
OFF_GATE = OFF_BC + 2 * SSM_BC
LANES = 128
GATE_B = 0
GATE_A = GATE_B + GDN_HEADS
GATE_DT = GATE_A + GDN_HEADS
N_PROJ = 6912

VMEM_LIMIT = 52 * 1024 * 1024

_HIGHEST = lax.Precision.HIGHEST


def _silu(x):
    return x * jax.nn.sigmoid(x)


def _softplus(x):
    return jnp.maximum(x, 0.0) + jnp.log1p(jnp.exp(-jnp.abs(x)))


def _dot(a, b):
    return jnp.dot(a, b, preferred_element_type=F32)


def _dot_nt(a, b):
    return lax.dot_general(a, b, (((1,), (1,)), ((), ())), preferred_element_type=F32)


def _dot_tn(a, b):
    return lax.dot_general(a, b, (((0,), (0,)), ((), ())), preferred_element_type=F32)


def _transpose_rows(a):
    r = a.shape[0]
    if r < LANES:
        a = jnp.concatenate([a, jnp.zeros((LANES - r, LANES), a.dtype)], axis=0)
    return a.T


def _rms_rows(x, w):
    return x * lax.rsqrt(jnp.mean(x * x, axis=-1, keepdims=True) + EPS) * w


def _inproj_kernel(x_ref, nw_ref, w_ref, o_ref, h_ref):
    @pl.when(pl.program_id(1) == 0)
    def _():
        h_ref[...] = _rms_rows(x_ref[...], nw_ref[...]).astype(BF16)

    o_ref[...] = _dot(h_ref[...], w_ref[...])


def _in_proj(x2d, norm_w, w_in_r, *, tm, tn):
    T = x2d.shape[0]
    return pl.pallas_call(
        _inproj_kernel,
        grid=(T // tm, N_PROJ // tn),
        in_specs=[
            pl.BlockSpec((tm, D_MODEL), lambda i, j: (i, 0)),
            pl.BlockSpec((1, D_MODEL), lambda i, j: (0, 0)),
            pl.BlockSpec((D_MODEL, tn), lambda i, j: (0, j)),
        ],
        out_specs=pl.BlockSpec((tm, tn), lambda i, j: (i, j)),
        out_shape=jax.ShapeDtypeStruct((T, N_PROJ), F32),
        scratch_shapes=[pltpu.VMEM((tm, D_MODEL), BF16)],
        compiler_params=pltpu.CompilerParams(
            dimension_semantics=("parallel", "arbitrary"), vmem_limit_bytes=VMEM_LIMIT),
        name="in_proj",
    )(x2d, norm_w, w_in_r)


def _conv_block(x_ref, xpad_ref, cw_ref, cb_ref, dst_ref, Lb, C, post):
    xpad_ref[8:8 + Lb, :] = x_ref[...]
    rs = min(Lb, CHUNK)
    for sb in range(Lb // rs):
        r = sb * rs
        for s in range(C // LANES):
            cols = slice(s * LANES, (s + 1) * LANES)
            acc = xpad_ref[8 + r:8 + r + rs, cols] * cw_ref[3:4, cols]
            for i in range(CONV_W - 1):
                acc = acc + xpad_ref[5 + i + r:5 + i + r + rs, cols] * cw_ref[i:i + 1, cols]
            if cb_ref is not None:
                acc = acc + cb_ref[:, cols]
            dst_ref[r:r + rs, cols] = post(s, _silu(acc))
    last3 = xpad_ref[Lb + 5:Lb + 8, :]
    xpad_ref[5:8, :] = last3
    return last3


def _gdn_kernel(*refs, c, Lb, has_state, out_dtype):
    if has_state:
        (qkv_ref, zg_ref, gate_ref, cst_ref, sst_ref, cw_ref, gbias_ref, galog_ref, nw_ref,
         o_ref, cst_out_ref, sst_out_ref, xpad_ref, qkvc_ref, s_ref) = refs
    else:
        (qkv_ref, zg_ref, gate_ref, cw_ref, gbias_ref, galog_ref, nw_ref,
         o_ref, cst_out_ref, sst_out_ref, xpad_ref, qkvc_ref, s_ref) = refs
    l = pl.program_id(1)
    nl = pl.num_programs(1)

    @pl.when(l == 0)
    def _init():
        if has_state:
            xpad_ref[5:8, :] = cst_ref[0]
            s_ref[...] = sst_ref[0]
        else:
            xpad_ref[0:8, :] = jnp.zeros((8, GDN_CONV_CH), F32)
            s_ref[...] = jnp.zeros(s_ref.shape, F32)

    def post(s, y):
        if s < 2 * GDN_HEADS:
            y = y * lax.rsqrt(jnp.sum(y * y, axis=-1, keepdims=True) + EPS)
            if s < GDN_HEADS:
                y = y * (GDN_DK ** -0.5)
        return y

    last3 = _conv_block(qkv_ref, xpad_ref, cw_ref, None, qkvc_ref, Lb, GDN_CONV_CH, post)

    @pl.when(l == nl - 1)
    def _conv_state():
        cst_out_ref[0] = last3

    row_i = lax.broadcasted_iota(jnp.int32, (c, c), 0)
    col_i = lax.broadcasted_iota(jnp.int32, (c, c), 1)
    tril_incl = row_i >= col_i
    tril_strict = row_i > col_i
    tril_f = tril_incl.astype(F32)
    eye = (row_i == col_i).astype(F32)
    gbias = gbias_ref[...]
    nega = -jnp.exp(galog_ref[...])
    nw = nw_ref[...]
    n_double = max(c.bit_length() - 2, 0)

    def chunk(r0):
        rows = pl.ds(r0, c)
        graw = gate_ref[rows, :]
        sp = _softplus(graw + gbias)
        beta_all = jax.nn.sigmoid(graw)
        G = jnp.dot(tril_f, nega * sp, precision=_HIGHEST, preferred_element_type=F32)
        GT = _transpose_rows(G)
        eG = jnp.exp(G)
        glast = G[c - 1:c, :]
        eGrev = jnp.exp(glast - G)
        egl = jnp.exp(glast)
        for h in range(GDN_HEADS):
            ga = GATE_A + h
            q = qkvc_ref[rows, h * GDN_DK:(h + 1) * GDN_DK]
            k = qkvc_ref[rows, GDN_QK + h * GDN_DK:GDN_QK + (h + 1) * GDN_DK]
            v = qkvc_ref[rows, 2 * GDN_QK + h * GDN_DV:2 * GDN_QK + (h + 1) * GDN_DV]
            beta = beta_all[:, GATE_B + h:GATE_B + h + 1]
            diff = G[:, ga:ga + 1] - GT[ga:ga + 1, :c]
            decay = jnp.exp(jnp.where(tril_incl, diff, -jnp.inf))
            kb = k * beta
            qkk = _dot_nt(jnp.concatenate([q, kb], axis=0).astype(BF16), k.astype(BF16))
            qk = qkk[:c] * decay
            nmat = jnp.where(tril_strict, -(qkk[c:] * decay), 0.0)
            tinv = eye + nmat
            pw = nmat
            for _ in range(n_double):
                pw16 = pw.astype(BF16)
                pw = _dot(pw16, pw16)
                tinv = tinv + _dot(tinv.astype(BF16), pw.astype(BF16))
            rhs = jnp.concatenate([v * beta, kb * eG[:, ga:ga + 1]], axis=1).astype(BF16)
            uw = _dot(tinv.astype(BF16), rhs)
            u = uw[:, :GDN_DV]
            w = uw[:, GDN_DV:]
            S = s_ref[h]
            wq = jnp.concatenate([w, q * eG[:, ga:ga + 1]], axis=0).astype(BF16)
            r = _dot(wq, S.astype(BF16))
            v_new = u - r[:c]
            v16 = v_new.astype(BF16)
            o = r[c:] + _dot(qk.astype(BF16), v16)
            kd = (k * eGrev[:, ga:ga + 1]).astype(BF16)
            s_ref[h] = S * egl[:, ga:ga + 1] + _dot_tn(kd, v16)
            z = zg_ref[rows, h * GDN_DV:(h + 1) * GDN_DV]
            o_ref[rows, h * GDN_DV:(h + 1) * GDN_DV] = (_rms_rows(o, nw) * _silu(z)).astype(out_dtype)

    nchunk = Lb // c
    if nchunk == 1:
        chunk(0)
    else:
        def body(ci, carry):
            chunk(pl.multiple_of(ci * c, c))
            return carry
        lax.fori_loop(0, nchunk, body, 0)

    @pl.when(l == nl - 1)
    def _state_out():
        sst_out_ref[0] = s_ref[...]


def _gdn(proj, conv_state, S_state, cw, gbias, galog, nw, *, B, L, c, Lb, out_dtype):
    nl = L // Lb
    has_state = conv_state is not None
    row = lambda b, l: b * nl + l
    in_specs = [
        pl.BlockSpec((Lb, GDN_CONV_CH), lambda b, l: (row(b, l), OFF_QKV // GDN_CONV_CH)),
        pl.BlockSpec((Lb, GDN_V), lambda b, l: (row(b, l), OFF_ZG // GDN_V)),
        pl.BlockSpec((Lb, LANES), lambda b, l: (row(b, l), OFF_GATE // LANES)),
    ]
    args = [proj, proj, proj]
    if has_state:
        in_specs += [
            pl.BlockSpec((1, CONV_W - 1, GDN_CONV_CH), lambda b, l: (b, 0, 0)),
            pl.BlockSpec((1, GDN_HEADS, GDN_DK, GDN_DV), lambda b, l: (b, 0, 0, 0)),
        ]
        args += [conv_state, S_state]
    in_specs += [
        pl.BlockSpec((CONV_W, GDN_CONV_CH), lambda b, l: (0, 0)),
        pl.BlockSpec((1, LANES), lambda b, l: (0, 0)),
        pl.BlockSpec((1, LANES), lambda b, l: (0, 0)),
        pl.BlockSpec((1, GDN_DV), lambda b, l: (0, 0)),
    ]
    args += [cw, gbias, galog, nw]
    return pl.pallas_call(
        functools.partial(_gdn_kernel, c=c, Lb=Lb, has_state=has_state, out_dtype=out_dtype),
        grid=(B, nl),
        in_specs=in_specs,
        out_specs=[
            pl.BlockSpec((Lb, GDN_V), lambda b, l: (row(b, l), 0)),
            pl.BlockSpec((1, CONV_W - 1, GDN_CONV_CH), lambda b, l: (b, 0, 0)),
            pl.BlockSpec((1, GDN_HEADS, GDN_DK, GDN_DV), lambda b, l: (b, 0, 0, 0)),
        ],
        out_shape=[
            jax.ShapeDtypeStruct((B * L, GDN_V), out_dtype),
            jax.ShapeDtypeStruct((B, CONV_W - 1, GDN_CONV_CH), F32),
            jax.ShapeDtypeStruct((B, GDN_HEADS, GDN_DK, GDN_DV), F32),
        ],
        scratch_shapes=[
            pltpu.VMEM((Lb + 8, GDN_CONV_CH), F32),
            pltpu.VMEM((Lb, GDN_CONV_CH), F32),
            pltpu.VMEM((GDN_HEADS, GDN_DK, GDN_DV), F32),
        ],
        compiler_params=pltpu.CompilerParams(
            dimension_semantics=("parallel", "arbitrary"), vmem_limit_bytes=VMEM_LIMIT),
        name="gdn_c%d" % c,
    )(*args)


N_PAIRS = SSM_HEADS // 2
PAIRS_PER_GROUP = N_PAIRS // SSM_GROUPS
GROUP_W = SSM_DI // SSM_GROUPS


def _ssm_kernel(*refs, c, Lb, has_state, out_dtype):
    if has_state:
        (xs_ref, bc_ref, zs_ref, gate_ref, cst_ref, hst_ref, cwx_ref, cbx_ref, cwbc_ref, cbbc_ref,
         gbias_ref, galog_ref, dcols_ref, nw_ref,
         o_ref, cst_out_ref, hst_out_ref, xpadx_ref, xpadbc_ref, xc_ref, bcc_ref, hh_ref, y_ref) = refs
    else:
        (xs_ref, bc_ref, zs_ref, gate_ref, cwx_ref, cbx_ref, cwbc_ref, cbbc_ref,
         gbias_ref, galog_ref, dcols_ref, nw_ref,
         o_ref, cst_out_ref, hst_out_ref, xpadx_ref, xpadbc_ref, xc_ref, bcc_ref, hh_ref, y_ref) = refs
    l = pl.program_id(1)
    nl = pl.num_programs(1)
    P, N = SSM_P, SSM_N

    @pl.when(l == 0)
    def _init():
        hh_ref[...] = jnp.zeros(hh_ref.shape, F32)
        if has_state:
            xpadx_ref[5:8, :] = cst_ref[0, :, :SSM_DI]
            xpadbc_ref[5:8, :] = cst_ref[0, :, SSM_DI:]
            for e in range(N_PAIRS):
                hh_ref[e, 0:P, 0:N] = hst_ref[0, 2 * e]
                hh_ref[e, P:2 * P, N:2 * N] = hst_ref[0, 2 * e + 1]
        else:
            xpadx_ref[0:8, :] = jnp.zeros((8, SSM_DI), F32)
            xpadbc_ref[0:8, :] = jnp.zeros((8, 2 * SSM_BC), F32)

    ident = lambda s, y: y
    last3x = _conv_block(xs_ref, xpadx_ref, cwx_ref, cbx_ref, xc_ref, Lb, SSM_DI, ident)
    last3bc = _conv_block(bc_ref, xpadbc_ref, cwbc_ref, cbbc_ref, bcc_ref, Lb, 2 * SSM_BC, ident)

    @pl.when(l == nl - 1)
    def _conv_state():
        cst_out_ref[0, :, :SSM_DI] = last3x
        cst_out_ref[0, :, SSM_DI:] = last3bc

    c2 = 2 * c
    row_i = lax.broadcasted_iota(jnp.int32, (c, c), 0)
    col_i = lax.broadcasted_iota(jnp.int32, (c, c), 1)
    tril_f = (row_i >= col_i).astype(F32)
    row2 = lax.broadcasted_iota(jnp.int32, (c, c2), 0)
    lane2 = lax.broadcasted_iota(jnp.int32, (c, c2), 1)
    left2 = lane2 < c
    tril2 = row2 >= jnp.where(left2, lane2, lane2 - c)
    left2_row = lax.broadcasted_iota(jnp.int32, (1, c2), 1) < c
    leftp = lax.broadcasted_iota(jnp.int32, (c, 2 * P), 1) < P
    hrow = lax.broadcasted_iota(jnp.int32, (2 * P, 2 * N), 0) < P
    hcol = lax.broadcasted_iota(jnp.int32, (2 * P, 2 * N), 1) < N
    blockmask = hrow == hcol
    hcol_row = lax.broadcasted_iota(jnp.int32, (1, 2 * N), 1) < N
    gbias = gbias_ref[...]
    nega = -jnp.exp(galog_ref[...])

    def chunk(r0):
        rows = pl.ds(r0, c)
        graw = gate_ref[rows, :]
        sp = _softplus(graw + gbias)
        acum = jnp.dot(tril_f, nega * sp, precision=_HIGHEST, preferred_element_type=F32)
        a2t = _transpose_rows(jnp.concatenate([acum, acum], axis=0))
        ea = jnp.exp(acum)
        alast = acum[c - 1:c, :]
        earev = jnp.exp(alast - acum)
        eal = jnp.exp(alast)
        for g in range(SSM_GROUPS):
            Bg = bcc_ref[rows, g * N:(g + 1) * N]
            Cg = bcc_ref[rows, SSM_BC + g * N:SSM_BC + (g + 1) * N]
            B2 = jnp.concatenate([Bg, Bg], axis=0).astype(BF16)
            cb2 = _dot_nt(Cg.astype(BF16), B2)
            for e4 in range(PAIRS_PER_GROUP):
                e = g * PAIRS_PER_GROUP + e4
                c0 = GATE_DT + 2 * e
                c1 = c0 + 1
                acol = jnp.where(left2, acum[:, c0:c0 + 1], acum[:, c1:c1 + 1])
                arow = jnp.where(left2_row, a2t[c0:c0 + 1, :c2], a2t[c1:c1 + 1, :c2])
                decay2 = jnp.exp(jnp.where(tril2, acol - arow, -jnp.inf))
                scores2 = (cb2 * decay2).astype(BF16)
                xp = xc_ref[rows, e * 2 * P:(e + 1) * 2 * P]
                xdt = xp * jnp.where(leftp, sp[:, c0:c0 + 1], sp[:, c1:c1 + 1])
                bd = jnp.concatenate(
                    [jnp.where(leftp, xdt, 0.0), jnp.where(leftp, 0.0, xdt)], axis=0).astype(BF16)
                y = _dot(scores2, bd)
                hh = hh_ref[e]
                cdec2 = jnp.concatenate(
                    [Cg * ea[:, c0:c0 + 1], Cg * ea[:, c1:c1 + 1]], axis=1).astype(BF16)
                y = y + _dot_nt(cdec2, hh.astype(BF16))
                bdec2 = jnp.concatenate(
                    [Bg * earev[:, c0:c0 + 1], Bg * earev[:, c1:c1 + 1]], axis=1).astype(BF16)
                cs = _dot_tn(xdt.astype(BF16), bdec2)
                ealmat = jnp.where(hcol_row, eal[:, c0:c0 + 1], eal[:, c1:c1 + 1])
                hh_ref[e] = hh * ealmat + jnp.where(blockmask, cs, 0.0)
                y_ref[:, e * 2 * P:(e + 1) * 2 * P] = y + dcols_ref[:, e * 2 * P:(e + 1) * 2 * P] * xp
            gcols = slice(g * GROUP_W, (g + 1) * GROUP_W)
            yg = y_ref[:, gcols] * _silu(zs_ref[rows, gcols])
            o_ref[rows, gcols] = _rms_rows(yg, nw_ref[:, gcols]).astype(out_dtype)

    nchunk = Lb // c
    if nchunk == 1:
        chunk(0)
    else:
        def body(ci, carry):
            chunk(pl.multiple_of(ci * c, c))
            return carry
        lax.fori_loop(0, nchunk, body, 0)

    @pl.when(l == nl - 1)
    def _state_out():
        for e in range(N_PAIRS):
            hst_out_ref[0, 2 * e] = hh_ref[e, 0:P, 0:N]
            hst_out_ref[0, 2 * e + 1] = hh_ref[e, P:2 * P, N:2 * N]


def _ssm(proj, conv_state, h_state, cwx, cbx, cwbc, cbbc, gbias, galog, dcols, nw,
         *, B, L, c, Lb, out_dtype):
    nl = L // Lb
    has_state = conv_state is not None
    row = lambda b, l: b * nl + l
    in_specs = [
        pl.BlockSpec((Lb, SSM_DI), lambda b, l: (row(b, l), OFF_XS // SSM_DI)),
        pl.BlockSpec((Lb, 2 * SSM_BC), lambda b, l: (row(b, l), OFF_BC // (2 * SSM_BC))),
        pl.BlockSpec((Lb, SSM_DI), lambda b, l: (row(b, l), OFF_ZS // SSM_DI)),
        pl.BlockSpec((Lb, LANES), lambda b, l: (row(b, l), OFF_GATE // LANES)),
    ]
    args = [proj, proj, proj, proj]
    if has_state:
        in_specs += [
            pl.BlockSpec((1, CONV_W - 1, SSM_CONV_CH), lambda b, l: (b, 0, 0)),
            pl.BlockSpec((1, SSM_HEADS, SSM_P, SSM_N), lambda b, l: (b, 0, 0, 0)),
        ]
        args += [conv_state, h_state]
    const = lambda shape: pl.BlockSpec(shape, lambda b, l: (0,) * len(shape))
    in_specs += [
        const((CONV_W, SSM_DI)), const((1, SSM_DI)), const((CONV_W, 2 * SSM_BC)), const((1, 2 * SSM_BC)),
        const((1, LANES)), const((1, LANES)), const((1, SSM_DI)), const((1, SSM_DI)),
    ]
    args += [cwx, cbx, cwbc, cbbc, gbias, galog, dcols, nw]
    return pl.pallas_call(
        functools.partial(_ssm_kernel, c=c, Lb=Lb, has_state=has_state, out_dtype=out_dtype),
        grid=(B, nl),
        in_specs=in_specs,
        out_specs=[
            pl.BlockSpec((Lb, SSM_DI), lambda b, l: (row(b, l), 0)),
            pl.BlockSpec((1, CONV_W - 1, SSM_CONV_CH), lambda b, l: (b, 0, 0)),
            pl.BlockSpec((1, SSM_HEADS, SSM_P, SSM_N), lambda b, l: (b, 0, 0, 0)),
        ],
        out_shape=[
            jax.ShapeDtypeStruct((B * L, SSM_DI), out_dtype),
            jax.ShapeDtypeStruct((B, CONV_W - 1, SSM_CONV_CH), F32),
            jax.ShapeDtypeStruct((B, SSM_HEADS, SSM_P, SSM_N), F32),
        ],
        scratch_shapes=[
            pltpu.VMEM((Lb + 8, SSM_DI), F32),
            pltpu.VMEM((Lb + 8, 2 * SSM_BC), F32),
            pltpu.VMEM((Lb, SSM_DI), F32),
            pltpu.VMEM((Lb, 2 * SSM_BC), F32),
            pltpu.VMEM((N_PAIRS, 2 * SSM_P, 2 * SSM_N), F32),
            pltpu.VMEM((c, SSM_DI), F32),
        ],
        compiler_params=pltpu.CompilerParams(
            dimension_semantics=("parallel", "arbitrary"), vmem_limit_bytes=VMEM_LIMIT),
        name="ssm_c%d" % c,
    )(*args)


def _outproj_kernel(x_ref, mg_ref, ms_ref, w_ref, o_ref):
    acc = _dot(mg_ref[...].astype(BF16), w_ref[:GDN_V, :])
    acc = acc + _dot(ms_ref[...].astype(BF16), w_ref[GDN_V:, :])
    o_ref[...] = x_ref[...] + acc


def _out_proj(x2d, mix_g, mix_s, w_out16, *, tm):
    T = x2d.shape[0]
    return pl.pallas_call(
        _outproj_kernel,
        grid=(T // tm,),
        in_specs=[
            pl.BlockSpec((tm, D_MODEL), lambda i: (i, 0)),
            pl.BlockSpec((tm, GDN_V), lambda i: (i, 0)),
            pl.BlockSpec((tm, SSM_DI), lambda i: (i, 0)),
            pl.BlockSpec((D_MODEL, D_MODEL), lambda i: (0, 0)),
        ],
        out_specs=pl.BlockSpec((tm, D_MODEL), lambda i: (i, 0)),
        out_shape=jax.ShapeDtypeStruct((T, D_MODEL), F32),
        compiler_params=pltpu.CompilerParams(
            dimension_semantics=("parallel",), vmem_limit_bytes=VMEM_LIMIT),
        name="out_proj",
    )(x2d, mix_g, mix_s, w_out16)


def _ffn_kernel(x_ref, nw_ref, wg_ref, wu_ref, wd_ref, fnw_ref, o_ref, h_ref, acc_ref):
    f = pl.program_id(1)

    @pl.when(f == 0)
    def _():
        h_ref[...] = _rms_rows(x_ref[...], nw_ref[...]).astype(BF16)
        acc_ref[...] = jnp.zeros(acc_ref.shape, F32)

    h = h_ref[...]
    a = (_silu(_dot(h, wg_ref[...])) * _dot(h, wu_ref[...])).astype(BF16)
    acc_ref[...] += _dot(a, wd_ref[...])

    @pl.when(f == pl.num_programs(1) - 1)
    def _():
        o_ref[...] = _rms_rows(x_ref[...] + acc_ref[...], fnw_ref[...])


def _ffn(x2d, norm_w, wg16, wu16, wd16, final_w, *, tm, tf):
    T = x2d.shape[0]
    return pl.pallas_call(
        _ffn_kernel,
        grid=(T // tm, D_FF // tf),
        in_specs=[
            pl.BlockSpec((tm, D_MODEL), lambda i, f: (i, 0)),
            pl.BlockSpec((1, D_MODEL), lambda i, f: (0, 0)),
            pl.BlockSpec((D_MODEL, tf), lambda i, f: (0, f)),
            pl.BlockSpec((D_MODEL, tf), lambda i, f: (0, f)),
            pl.BlockSpec((tf, D_MODEL), lambda i, f: (f, 0)),
            pl.BlockSpec((1, D_MODEL), lambda i, f: (0, 0)),
        ],
        out_specs=pl.BlockSpec((tm, D_MODEL), lambda i, f: (i, 0)),
        out_shape=jax.ShapeDtypeStruct((T, D_MODEL), F32),
        scratch_shapes=[pltpu.VMEM((tm, D_MODEL), BF16), pltpu.VMEM((tm, D_MODEL), F32)],
        compiler_params=pltpu.CompilerParams(
            dimension_semantics=("parallel", "arbitrary"), vmem_limit_bytes=VMEM_LIMIT),
        name="ffn",
    )(x2d, norm_w, wg16, wu16, wd16, final_w)


def _trunk(x, states, p, *, c, Lb):
    B, L, _ = x.shape
    x2d = x.reshape(B * L, D_MODEL)
    gconv, gS, sconv, sh = states
    out_dtype = BF16 if Lb % 16 == 0 else F32
    proj = _in_proj(x2d, p["attn_norm_w"], p["w_in_r"], tm=1024, tn=768)
    mix_g, gconv_new, gS_new = _gdn(
        proj, gconv, gS, p["gdn_conv_w"], p["gbias"], p["galog"], p["gdn_norm_w"],
        B=B, L=L, c=c, Lb=Lb, out_dtype=out_dtype)
    mix_s, sconv_new, sh_new = _ssm(
        proj, sconv, sh, p["cwx"], p["cbx"], p["cwbc"], p["cbbc"], p["gbias"], p["galog"],
        p["dcols"], p["ssm_norm_w"], B=B, L=L, c=c, Lb=Lb, out_dtype=out_dtype)
    x1 = _out_proj(x2d, mix_g, mix_s, p["w_out16"], tm=512)
    y = _ffn(x1, p["ffn_norm_w"], p["wg16"], p["wu16"], p["wd16"], p["final_norm_w"], tm=512, tf=512)
    return y.reshape(B, L, D_MODEL), (gconv_new[None], gS_new[None], sconv_new[None], sh_new[None])


def kernel(x_prompt, x_sample, state_gdn_conv, state_gdn, state_ssm_conv, state_ssm,
           attn_norm_w, w_in, gdn_conv_w, gdn_A_log, gdn_dt_bias, gdn_norm_w,
           ssm_conv_w, ssm_conv_b, ssm_A_log, ssm_dt_bias, ssm_D, ssm_norm_w,
           w_out, ffn_norm_w, w_gate, w_up, w_down, final_norm_w):
    assert w_in.shape[0] == 1, "single-layer trunk"
    w = w_in[0]
    o_qkv, o_zg, o_b, o_a, o_zs, o_xbc, o_dt = 0, 3072, 4096, 4104, 4112, 5136, 6672
    pad = N_PROJ - (OFF_GATE + GATE_DT + SSM_HEADS)
    w_in_r = jnp.concatenate([
        w[:, o_qkv:o_zg], w[:, o_zg:o_b], w[:, o_zs:o_xbc], w[:, o_xbc:o_dt],
        w[:, o_b:o_a], w[:, o_a:o_zs], w[:, o_dt:], jnp.zeros((D_MODEL, pad), w.dtype),
    ], axis=1).astype(BF16)
    zeros8 = jnp.zeros((GDN_HEADS,), F32)
    tail = jnp.zeros((LANES - GATE_DT - SSM_HEADS,), F32)
    gbias = jnp.concatenate([zeros8, gdn_dt_bias[0], ssm_dt_bias[0], tail])[None]
    galog = jnp.concatenate([zeros8, gdn_A_log[0], ssm_A_log[0], tail])[None]
    p = dict(
        attn_norm_w=attn_norm_w, w_in_r=w_in_r, gdn_conv_w=gdn_conv_w[0], gbias=gbias, galog=galog,
        gdn_norm_w=gdn_norm_w,
        cwx=ssm_conv_w[0][:, :SSM_DI], cbx=ssm_conv_b[:, :SSM_DI],
        cwbc=ssm_conv_w[0][:, SSM_DI:], cbbc=ssm_conv_b[:, SSM_DI:],
        dcols=jnp.repeat(ssm_D[0], SSM_P)[None], ssm_norm_w=ssm_norm_w,
        w_out16=w_out[0].astype(BF16), ffn_norm_w=ffn_norm_w,
        wg16=w_gate[0].astype(BF16), wu16=w_up[0].astype(BF16), wd16=w_down[0].astype(BF16),
        final_norm_w=final_norm_w[None],
    )
    y_p, st_p = _trunk(x_prompt, (None, None, None, None), p, c=CHUNK, Lb=256)
    y_s, st_s = _trunk(
        x_sample, (state_gdn_conv[0], state_gdn[0], state_ssm_conv[0], state_ssm[0]), p,
        c=x_sample.shape[1], Lb=x_sample.shape[1])
    return (y_p, y_s, st_p[0], st_p[1], st_p[2], st_p[3], st_s[0], st_s[1], st_s[2], st_s[3])
```

```python
import functools

import jax
import jax.numpy as jnp
from jax import lax
from jax.experimental import pallas as pl
from jax.experimental.pallas import tpu as pltpu

F32 = jnp.float32
BF16 = jnp.bfloat16

D_MODEL = 2048
GDN_HEADS = 8
GDN_DK = 128
GDN_DV = 128
GDN_QK = GDN_HEADS * GDN_DK
GDN_V = GDN_HEADS * GDN_DV
GDN_CONV_CH = 2 * GDN_QK + GDN_V
SSM_P = 64
SSM_N = 128
SSM_GROUPS = 2
SSM_DI = 1024
SSM_HEADS = SSM_DI // SSM_P
SSM_BC = SSM_GROUPS * SSM_N
SSM_CONV_CH = SSM_DI + 2 * SSM_BC
CONV_W = 4
CHUNK = 64
D_FF = 5632
EPS = 1e-6

OFF_QKV = 0
OFF_ZG = OFF_QKV + GDN_CONV_CH
OFF_ZS = OFF_ZG + GDN_V
OFF_XS = OFF_ZS + SSM_DI
OFF_BC = OFF_XS + SSM_DI
OFF_GATE = OFF_BC + 2 * SSM_BC
LANES = 128
GATE_B = 0
GATE_A = GATE_B + GDN_HEADS
GATE_DT = GATE_A + GDN_HEADS
N_PROJ = 6912

VMEM_LIMIT = 52 * 1024 * 1024

_HIGHEST = lax.Precision.HIGHEST


def _silu(x):
    return x * jax.nn.sigmoid(x)


def _softplus(x):
    return jnp.maximum(x, 0.0) + jnp.log1p(jnp.exp(-jnp.abs(x)))


def _dot(a, b):
    return jnp.dot(a, b, preferred_element_type=F32)


def _dot_nt(a, b):
    return lax.dot_general(a, b, (((1,), (1,)), ((), ())), preferred_element_type=F32)


def _dot_tn(a, b):
    return lax.dot_general(a, b, (((0,), (0,)), ((), ())), preferred_element_type=F32)


def _transpose_rows(a):
    r = a.shape[0]
    if r < LANES:
        a = jnp.concatenate([a, jnp.zeros((LANES - r, LANES), a.dtype)], axis=0)
    return a.T


def _rms_rows(x, w):
    return x * lax.rsqrt(jnp.mean(x * x, axis=-1, keepdims=True) + EPS) * w


def _inproj_kernel(x_ref, nw_ref, w_ref, o_ref, h_ref):
    @pl.when(pl.program_id(1) == 0)
    def _():
        h_ref[...] = _rms_rows(x_ref[...], nw_ref[...]).astype(BF16)

    o_ref[...] = _dot(h_ref[...], w_ref[...])


def _in_proj(x2d, norm_w, w_in_r, *, tm, tn):
    T = x2d.shape[0]
    return pl.pallas_call(
        _inproj_kernel,
        grid=(T // tm, N_PROJ // tn),
        in_specs=[
            pl.BlockSpec((tm, D_MODEL), lambda i, j: (i, 0)),
            pl.BlockSpec((1, D_MODEL), lambda i, j: (0, 0)),
            pl.BlockSpec((D_MODEL, tn), lambda i, j: (0, j)),
        ],
        out_specs=pl.BlockSpec((tm, tn), lambda i, j: (i, j)),
        out_shape=jax.ShapeDtypeStruct((T, N_PROJ), F32),
        scratch_shapes=[pltpu.VMEM((tm, D_MODEL), BF16)],
        compiler_params=pltpu.CompilerParams(
            dimension_semantics=("parallel", "arbitrary"), vmem_limit_bytes=VMEM_LIMIT),
        name="in_proj",
    )(x2d, norm_w, w_in_r)


def _conv_block(x_ref, xpad_ref, cw_ref, cb_ref, dst_ref, Lb, C, post):
    xpad_ref[8:8 + Lb, :] = x_ref[...]
    rs = min(Lb, CHUNK)
    for sb in range(Lb // rs):
        r = sb * rs
        for s in range(C // LANES):
            cols = slice(s * LANES, (s + 1) * LANES)
            acc = xpad_ref[8 + r:8 + r + rs, cols] * cw_ref[3:4, cols]
            for i in range(CONV_W - 1):
                acc = acc + xpad_ref[5 + i + r:5 + i + r + rs, cols] * cw_ref[i:i + 1, cols]
            if cb_ref is not None:
                acc = acc + cb_ref[:, cols]
            dst_ref[r:r + rs, cols] = post(s, _silu(acc))
    last3 = xpad_ref[Lb + 5:Lb + 8, :]
    xpad_ref[5:8, :] = last3
    return last3


STACK = 128


def _gdn_qk_post(s, y):
    if s < 2 * GDN_HEADS:
        y = y * lax.rsqrt(jnp.sum(y * y, axis=-1, keepdims=True) + EPS)
        if s < GDN_HEADS:
            y = y * (GDN_DK ** -0.5)
    return y


def _gdn_local(items, glen):
    sh = glen.bit_length() - 1
    row = lax.broadcasted_iota(jnp.int32, (STACK, STACK), 0)
    col = lax.broadcasted_iota(jnp.int32, (STACK, STACK), 1)
    same = (row >> sh) == (col >> sh)
    incl = same & (row >= col)
    strict = same & (row > col)
    eye = (row == col).astype(F32)

    decay = [jnp.exp(jnp.where(incl, it["g"] - it["g"].T, -jnp.inf)) for it in items]
    kb = [it["k"] * it["beta"] for it in items]
    qkk = [_dot_nt(jnp.concatenate([it["q"], b], axis=0).astype(BF16), it["k"].astype(BF16))
           for it, b in zip(items, kb)]
    qk = [x[:STACK] * d for x, d in zip(qkk, decay)]
    nmat = [jnp.where(strict, -(x[STACK:] * d), 0.0) for x, d in zip(qkk, decay)]
    tinv = [eye + n for n in nmat]
    if sh >= 2:
        pw = [_dot(n.astype(BF16), n.astype(BF16)) for n in nmat]
        for _ in range(sh - 2):
            x = [_dot(jnp.concatenate([t, p], axis=0).astype(BF16), p.astype(BF16))
                 for t, p in zip(tinv, pw)]
            tinv = [t + y[:STACK] for t, y in zip(tinv, x)]
            pw = [y[STACK:] for y in x]
        tinv = [t + _dot(t.astype(BF16), p.astype(BF16)) for t, p in zip(tinv, pw)]
    uw = [_dot(t.astype(BF16),
               jnp.concatenate([it["v"] * it["beta"], b * it["eg"]], axis=1).astype(BF16))
          for t, it, b in zip(tinv, items, kb)]
    return [(x[:, :GDN_DV], x[:, GDN_DV:]) for x in uw], qk


def _gdn_prompt_kernel(qkv_ref, zg_ref, gate_ref, cw_ref, gbias_ref, galog_ref, nw_ref,
                       o_ref, cst_out_ref, sst_out_ref, xpad_ref, qkvc_ref, s_ref, *, Lb):
    c = CHUNK
    l = pl.program_id(1)
    nl = pl.num_programs(1)

    @pl.when(l == 0)
    def _init():
        xpad_ref[0:8, :] = jnp.zeros((8, GDN_CONV_CH), F32)
        s_ref[...] = jnp.zeros(s_ref.shape, F32)

    last3 = _conv_block(qkv_ref, xpad_ref, cw_ref, None, qkvc_ref, Lb, GDN_CONV_CH, _gdn_qk_post)

    @pl.when(l == nl - 1)
    def _conv_state():
        cst_out_ref[0] = last3

    row_i = lax.broadcasted_iota(jnp.int32, (c, c), 0)
    col_i = lax.broadcasted_iota(jnp.int32, (c, c), 1)
    tril_f = (row_i >= col_i).astype(F32)
    gbias = gbias_ref[...]
    nega = -jnp.exp(galog_ref[...])
    nw = nw_ref[...]
    n_pairs = GDN_HEADS // 2

    def chunk(r0):
        rows = pl.ds(r0, c)
        graw = gate_ref[rows, :]
        sp = _softplus(graw + gbias)
        beta_all = jax.nn.sigmoid(graw)
        G = jnp.dot(tril_f, nega * sp, precision=_HIGHEST, preferred_element_type=F32)
        eG = jnp.exp(G)
        glast = G[c - 1:c, :]
        eGrev = jnp.exp(glast - G)
        egl = jnp.exp(glast)

        def heads(off, a, b):
            return jnp.concatenate([qkvc_ref[rows, off + a * LANES:off + (a + 1) * LANES],
                                    qkvc_ref[rows, off + b * LANES:off + (b + 1) * LANES]], axis=0)

        def colstack(m, a, b):
            return jnp.concatenate([jnp.broadcast_to(m[:, a:a + 1], (c, LANES)),
                                    jnp.broadcast_to(m[:, b:b + 1], (c, LANES))], axis=0)

        items = []
        for pr in range(n_pairs):
            a, b = 2 * pr, 2 * pr + 1
            items.append(dict(
                q=heads(0, a, b), k=heads(GDN_QK, a, b), v=heads(2 * GDN_QK, a, b),
                beta=colstack(beta_all, GATE_B + a, GATE_B + b),
                g=colstack(G, GATE_A + a, GATE_A + b),
                eg=colstack(eG, GATE_A + a, GATE_A + b),
                egrev=colstack(eGrev, GATE_A + a, GATE_A + b)))
        uw, qk = _gdn_local(items, c)

        r = []
        for pr, it in enumerate(items):
            w = uw[pr][1]
            qd = it["q"] * it["eg"]
            r.append([
                _dot(jnp.concatenate([w[hh * c:(hh + 1) * c], qd[hh * c:(hh + 1) * c]], axis=0).astype(BF16),
                     s_ref[2 * pr + hh].astype(BF16))
                for hh in range(2)])
        v16 = [(uw[pr][0] - jnp.concatenate([r[pr][0][:c], r[pr][1][:c]], axis=0)).astype(BF16)
               for pr in range(n_pairs)]
        o = [jnp.concatenate([r[pr][0][c:], r[pr][1][c:]], axis=0) + _dot(qk[pr].astype(BF16), v16[pr])
             for pr in range(n_pairs)]
        for pr, it in enumerate(items):
            kd16 = (it["k"] * it["egrev"]).astype(BF16)
            for hh in range(2):
                h = 2 * pr + hh
                ga = GATE_A + h
                s_ref[h] = (s_ref[h] * egl[:, ga:ga + 1]
                            + _dot_tn(kd16[hh * c:(hh + 1) * c], v16[pr][hh * c:(hh + 1) * c]))
        for pr in range(n_pairs):
            for hh in range(2):
                h = 2 * pr + hh
                z = zg_ref[rows, h * GDN_DV:(h + 1) * GDN_DV]
                o_ref[rows, h * GDN_DV:(h + 1) * GDN_DV] = (
                    _rms_rows(o[pr][hh * c:(hh + 1) * c], nw) * _silu(z)).astype(o_ref.dtype)

    def body(ci, carry):
        chunk(pl.multiple_of(ci * c, c))
        return carry
    lax.fori_loop(0, Lb // c, body, 0)

    @pl.when(l == nl - 1)
    def _state_out():
        sst_out_ref[0] = s_ref[...]


def _gdn_prompt(proj, cw, gbias, galog, nw, *, B, L, Lb):
    nl = L // Lb
    row = lambda b, l: b * nl + l
    const = lambda shape: pl.BlockSpec(shape, lambda b, l: (0,) * len(shape))
    return pl.pallas_call(
        functools.partial(_gdn_prompt_kernel, Lb=Lb),
        grid=(B, nl),
        in_specs=[
            pl.BlockSpec((Lb, GDN_CONV_CH), lambda b, l: (row(b, l), OFF_QKV // GDN_CONV_CH)),
            pl.BlockSpec((Lb, GDN_V), lambda b, l: (row(b, l), OFF_ZG // GDN_V)),
            pl.BlockSpec((Lb, LANES), lambda b, l: (row(b, l), OFF_GATE // LANES)),
            const((CONV_W, GDN_CONV_CH)), const((1, LANES)), const((1, LANES)), const((1, GDN_DV)),
        ],
        out_specs=[
            pl.BlockSpec((Lb, GDN_V), lambda b, l: (row(b, l), 0)),
            pl.BlockSpec((1, CONV_W - 1, GDN_CONV_CH), lambda b, l: (b, 0, 0)),
            pl.BlockSpec((1, GDN_HEADS, GDN_DK, GDN_DV), lambda b, l: (b, 0, 0, 0)),
        ],
        out_shape=[
            jax.ShapeDtypeStruct((B * L, GDN_V), BF16),
            jax.ShapeDtypeStruct((B, CONV_W - 1, GDN_CONV_CH), F32),
            jax.ShapeDtypeStruct((B, GDN_HEADS, GDN_DK, GDN_DV), F32),
        ],
        scratch_shapes=[
            pltpu.VMEM((Lb + 8, GDN_CONV_CH), F32),
            pltpu.VMEM((Lb, GDN_CONV_CH), F32),
            pltpu.VMEM((GDN_HEADS, GDN_DK, GDN_DV), F32),
        ],
        compiler_params=pltpu.CompilerParams(
            dimension_semantics=("parallel", "arbitrary"), vmem_limit_bytes=VMEM_LIMIT),
        name="gdn_prompt",
    )(proj, proj, proj, cw, gbias, galog, nw)


def _gdn_sample_kernel(qkv_ref, zg_ref, gate_ref, cst_ref, sst_ref, cw_ref, gbias_ref, galog_ref, nw_ref,
                       o_ref, cst_out_ref, sst_out_ref, xpad_ref, qkvc_ref, *, nb, L):
    R = nb * L
    sh = L.bit_length() - 1
    for bi in range(nb):
        xp = xpad_ref.at[bi]
        xp[5:8, :] = cst_ref[bi]
        cst_out_ref[bi] = _conv_block(
            qkv_ref.at[pl.ds(bi * L, L)], xp, cw_ref, None, qkvc_ref.at[pl.ds(bi * L, L)],
            L, GDN_CONV_CH, _gdn_qk_post)

    row_i = lax.broadcasted_iota(jnp.int32, (R, R), 0)
    col_i = lax.broadcasted_iota(jnp.int32, (R, R), 1)
    tril_f = (((row_i >> sh) == (col_i >> sh)) & (row_i >= col_i)).astype(F32)
    graw = gate_ref[...]
    sp = _softplus(graw + gbias_ref[...])
    beta_all = jax.nn.sigmoid(graw)
    G = jnp.dot(tril_f, -jnp.exp(galog_ref[...]) * sp, precision=_HIGHEST, preferred_element_type=F32)
    glast = [G[bi * L + L - 1:bi * L + L, :] for bi in range(nb)]
    eG = jnp.exp(G)
    eGrev = jnp.exp(jnp.concatenate([jnp.broadcast_to(x, (L, LANES)) for x in glast], axis=0) - G)
    egl = [jnp.exp(x) for x in glast]
    nw = nw_ref[...]

    n_st = R // (2 * L)

    def tiles(ref, st, off):
        return jnp.concatenate(
            [ref[st * 2 * L:(st + 1) * 2 * L, off + h * LANES:off + (h + 1) * LANES]
             for h in range(GDN_HEADS)], axis=0)

    def colstack(m, st, off):
        return jnp.concatenate(
            [jnp.broadcast_to(m[st * 2 * L:(st + 1) * 2 * L, off + h:off + h + 1], (2 * L, LANES))
             for h in range(GDN_HEADS)], axis=0)

    items = [dict(q=tiles(qkvc_ref, st, 0), k=tiles(qkvc_ref, st, GDN_QK), v=tiles(qkvc_ref, st, 2 * GDN_QK),
                  beta=colstack(beta_all, st, GATE_B), g=colstack(G, st, GATE_A),
                  eg=colstack(eG, st, GATE_A), egrev=colstack(eGrev, st, GATE_A))
             for st in range(n_st)]
    uw, qk = _gdn_local(items, L)

    groups = [(h, bi) for h in range(GDN_HEADS) for bi in range(2)]
    r = []
    for st, it in enumerate(items):
        w = uw[st][1]
        qd = it["q"] * it["eg"]
        r.append([
            _dot(jnp.concatenate([w[gi * L:(gi + 1) * L], qd[gi * L:(gi + 1) * L]], axis=0).astype(BF16),
                 sst_ref[2 * st + bi, h].astype(BF16))
            for gi, (h, bi) in enumerate(groups)])
    v_new = [uw[st][0] - jnp.concatenate([x[:L] for x in r[st]], axis=0) for st in range(n_st)]
    o = [jnp.concatenate([x[L:] for x in r[st]], axis=0)
         + _dot(qk[st].astype(BF16), v_new[st].astype(BF16)) for st in range(n_st)]
    for st, it in enumerate(items):
        kd = it["k"] * it["egrev"]
        for gi, (h, bi) in enumerate(groups):
            b = 2 * st + bi
            ga = GATE_A + h
            rs = slice(gi * L, (gi + 1) * L)
            sst_out_ref[b, h] = (sst_ref[b, h] * egl[b][:, ga:ga + 1]
                                 + _dot_tn(kd[rs].astype(BF16), v_new[st][rs].astype(BF16)))
    for st in range(n_st):
        out = (_rms_rows(o[st], nw) * _silu(tiles(zg_ref, st, 0))).astype(o_ref.dtype)
        for h in range(GDN_HEADS):
            o_ref[st * 2 * L:(st + 1) * 2 * L, h * GDN_DV:(h + 1) * GDN_DV] = out[h * 2 * L:(h + 1) * 2 * L]


def _gdn_sample(proj, conv_state, S_state, cw, gbias, galog, nw, *, B, L, nb):
    R = nb * L
    const = lambda shape: pl.BlockSpec(shape, lambda i: (0,) * len(shape))
    return pl.pallas_call(
        functools.partial(_gdn_sample_kernel, nb=nb, L=L),
        grid=(B // nb,),
        in_specs=[
            pl.BlockSpec((R, GDN_CONV_CH), lambda i: (i, OFF_QKV // GDN_CONV_CH)),
            pl.BlockSpec((R, GDN_V), lambda i: (i, OFF_ZG // GDN_V)),
            pl.BlockSpec((R, LANES), lambda i: (i, OFF_GATE // LANES)),
            pl.BlockSpec((nb, CONV_W - 1, GDN_CONV_CH), lambda i: (i, 0, 0)),
            pl.BlockSpec((nb, GDN_HEADS, GDN_DK, GDN_DV), lambda i: (i, 0, 0, 0)),
            const((CONV_W, GDN_CONV_CH)), const((1, LANES)), const((1, LANES)), const((1, GDN_DV)),
        ],
        out_specs=[
            pl.BlockSpec((R, GDN_V), lambda i: (i, 0)),
            pl.BlockSpec((nb, CONV_W - 1, GDN_CONV_CH), lambda i: (i, 0, 0)),
            pl.BlockSpec((nb, GDN_HEADS, GDN_DK, GDN_DV), lambda i: (i, 0, 0, 0)),
        ],
        out_shape=[
            jax.ShapeDtypeStruct((B * L, GDN_V), BF16),
            jax.ShapeDtypeStruct((B, CONV_W - 1, GDN_CONV_CH), F32),
            jax.ShapeDtypeStruct((B, GDN_HEADS, GDN_DK, GDN_DV), F32),
        ],
        scratch_shapes=[
            pltpu.VMEM((nb, L + 8, GDN_CONV_CH), F32),
            pltpu.VMEM((R, GDN_CONV_CH), F32),
        ],
        compiler_params=pltpu.CompilerParams(
            dimension_semantics=("parallel",), vmem_limit_bytes=VMEM_LIMIT),
        name="gdn_sample",
    )(proj, proj, proj, conv_state, S_state, cw, gbias, galog, nw)


N_PAIRS = SSM_HEADS // 2
PAIRS_PER_GROUP = N_PAIRS // SSM_GROUPS
GROUP_W = SSM_DI // SSM_GROUPS


def _ssm_kernel(*refs, c, Lb, has_state, out_dtype):
    if has_state:
        (xs_ref, bc_ref, zs_ref, gate_ref, cst_ref, hst_ref, cwx_ref, cbx_ref, cwbc_ref, cbbc_ref,
         gbias_ref, galog_ref, dcols_ref, nw_ref,
         o_ref, cst_out_ref, hst_out_ref, xpadx_ref, xpadbc_ref, xc_ref, bcc_ref, hh_ref, y_ref) = refs
    else:
        (xs_ref, bc_ref, zs_ref, gate_ref, cwx_ref, cbx_ref, cwbc_ref, cbbc_ref,
         gbias_ref, galog_ref, dcols_ref, nw_ref,
         o_ref, cst_out_ref, hst_out_ref, xpadx_ref, xpadbc_ref, xc_ref, bcc_ref, hh_ref, y_ref) = refs
    l = pl.program_id(1)
    nl = pl.num_programs(1)
    P, N = SSM_P, SSM_N

    @pl.when(l == 0)
    def _init():
        hh_ref[...] = jnp.zeros(hh_ref.shape, F32)
        if has_state:
            xpadx_ref[5:8, :] = cst_ref[0, :, :SSM_DI]
            xpadbc_ref[5:8, :] = cst_ref[0, :, SSM_DI:]
            for e in range(N_PAIRS):
                hh_ref[e, 0:P, 0:N] = hst_ref[0, 2 * e]
                hh_ref[e, P:2 * P, N:2 * N] = hst_ref[0, 2 * e + 1]
        else:
            xpadx_ref[0:8, :] = jnp.zeros((8, SSM_DI), F32)
            xpadbc_ref[0:8, :] = jnp.zeros((8, 2 * SSM_BC), F32)

    ident = lambda s, y: y
    last3x = _conv_block(xs_ref, xpadx_ref, cwx_ref, cbx_ref, xc_ref, Lb, SSM_DI, ident)
    last3bc = _conv_block(bc_ref, xpadbc_ref, cwbc_ref, cbbc_ref, bcc_ref, Lb, 2 * SSM_BC, ident)

    @pl.when(l == nl - 1)
    def _conv_state():
        cst_out_ref[0, :, :SSM_DI] = last3x
        cst_out_ref[0, :, SSM_DI:] = last3bc

    c2 = 2 * c
    row_i = lax.broadcasted_iota(jnp.int32, (c, c), 0)
    col_i = lax.broadcasted_iota(jnp.int32, (c, c), 1)
    tril_f = (row_i >= col_i).astype(F32)
    row2 = lax.broadcasted_iota(jnp.int32, (c, c2), 0)
    lane2 = lax.broadcasted_iota(jnp.int32, (c, c2), 1)
    left2 = lane2 < c
    tril2 = row2 >= jnp.where(left2, lane2, lane2 - c)
    left2_row = lax.broadcasted_iota(jnp.int32, (1, c2), 1) < c
    leftp = lax.broadcasted_iota(jnp.int32, (c, 2 * P), 1) < P
    hrow = lax.broadcasted_iota(jnp.int32, (2 * P, 2 * N), 0) < P
    hcol = lax.broadcasted_iota(jnp.int32, (2 * P, 2 * N), 1) < N
    blockmask = hrow == hcol
    hcol_row = lax.broadcasted_iota(jnp.int32, (1, 2 * N), 1) < N
    gbias = gbias_ref[...]
    nega = -jnp.exp(galog_ref[...])

    def chunk(r0):
        rows = pl.ds(r0, c)
        graw = gate_ref[rows, :]
        sp = _softplus(graw + gbias)
        acum = jnp.dot(tril_f, nega * sp, precision=_HIGHEST, preferred_element_type=F32)
        a2t = _transpose_rows(jnp.concatenate([acum, acum], axis=0))
        ea = jnp.exp(acum)
        alast = acum[c - 1:c, :]
        earev = jnp.exp(alast - acum)
        eal = jnp.exp(alast)
        for g in range(SSM_GROUPS):
            Bg = bcc_ref[rows, g * N:(g + 1) * N]
            Cg = bcc_ref[rows, SSM_BC + g * N:SSM_BC + (g + 1) * N]
            B2 = jnp.concatenate([Bg, Bg], axis=0).astype(BF16)
            cb2 = _dot_nt(Cg.astype(BF16), B2)
            for e4 in range(PAIRS_PER_GROUP):
                e = g * PAIRS_PER_GROUP + e4
                c0 = GATE_DT + 2 * e
                c1 = c0 + 1
                acol = jnp.where(left2, acum[:, c0:c0 + 1], acum[:, c1:c1 + 1])
                arow = jnp.where(left2_row, a2t[c0:c0 + 1, :c2], a2t[c1:c1 + 1, :c2])
                decay2 = jnp.exp(jnp.where(tril2, acol - arow, -jnp.inf))
                scores2 = (cb2 * decay2).astype(BF16)
                xp = xc_ref[rows, e * 2 * P:(e + 1) * 2 * P]
                xdt = xp * jnp.where(leftp, sp[:, c0:c0 + 1], sp[:, c1:c1 + 1])
                bd = jnp.concatenate(
                    [jnp.where(leftp, xdt, 0.0), jnp.where(leftp, 0.0, xdt)], axis=0).astype(BF16)
                y = _dot(scores2, bd)
                hh = hh_ref[e]
                cdec2 = jnp.concatenate(
                    [Cg * ea[:, c0:c0 + 1], Cg * ea[:, c1:c1 + 1]], axis=1).astype(BF16)
                y = y + _dot_nt(cdec2, hh.astype(BF16))
                bdec2 = jnp.concatenate(
                    [Bg * earev[:, c0:c0 + 1], Bg * earev[:, c1:c1 + 1]], axis=1).astype(BF16)
                cs = _dot_tn(xdt.astype(BF16), bdec2)
                ealmat = jnp.where(hcol_row, eal[:, c0:c0 + 1], eal[:, c1:c1 + 1])
                hh_ref[e] = hh * ealmat + jnp.where(blockmask, cs, 0.0)
                y_ref[:, e * 2 * P:(e + 1) * 2 * P] = y + dcols_ref[:, e * 2 * P:(e + 1) * 2 * P] * xp
            gcols = slice(g * GROUP_W, (g + 1) * GROUP_W)
            yg = y_ref[:, gcols] * _silu(zs_ref[rows, gcols])
            o_ref[rows, gcols] = _rms_rows(yg, nw_ref[:, gcols]).astype(out_dtype)

    nchunk = Lb // c
    if nchunk == 1:
        chunk(0)
    else:
        def body(ci, carry):
            chunk(pl.multiple_of(ci * c, c))
            return carry
        lax.fori_loop(0, nchunk, body, 0)

    @pl.when(l == nl - 1)
    def _state_out():
        for e in range(N_PAIRS):
            hst_out_ref[0, 2 * e] = hh_ref[e, 0:P, 0:N]
            hst_out_ref[0, 2 * e + 1] = hh_ref[e, P:2 * P, N:2 * N]


def _ssm(proj, conv_state, h_state, cwx, cbx, cwbc, cbbc, gbias, galog, dcols, nw,
         *, B, L, c, Lb, out_dtype):
    nl = L // Lb
    has_state = conv_state is not None
    row = lambda b, l: b * nl + l
    in_specs = [
        pl.BlockSpec((Lb, SSM_DI), lambda b, l: (row(b, l), OFF_XS // SSM_DI)),
        pl.BlockSpec((Lb, 2 * SSM_BC), lambda b, l: (row(b, l), OFF_BC // (2 * SSM_BC))),
        pl.BlockSpec((Lb, SSM_DI), lambda b, l: (row(b, l), OFF_ZS // SSM_DI)),
        pl.BlockSpec((Lb, LANES), lambda b, l: (row(b, l), OFF_GATE // LANES)),
    ]
    args = [proj, proj, proj, proj]
    if has_state:
        in_specs += [
            pl.BlockSpec((1, CONV_W - 1, SSM_CONV_CH), lambda b, l: (b, 0, 0)),
            pl.BlockSpec((1, SSM_HEADS, SSM_P, SSM_N), lambda b, l: (b, 0, 0, 0)),
        ]
        args += [conv_state, h_state]
    const = lambda shape: pl.BlockSpec(shape, lambda b, l: (0,) * len(shape))
    in_specs += [
        const((CONV_W, SSM_DI)), const((1, SSM_DI)), const((CONV_W, 2 * SSM_BC)), const((1, 2 * SSM_BC)),
        const((1, LANES)), const((1, LANES)), const((1, SSM_DI)), const((1, SSM_DI)),
    ]
    args += [cwx, cbx, cwbc, cbbc, gbias, galog, dcols, nw]
    return pl.pallas_call(
        functools.partial(_ssm_kernel, c=c, Lb=Lb, has_state=has_state, out_dtype=out_dtype),
        grid=(B, nl),
        in_specs=in_specs,
        out_specs=[
            pl.BlockSpec((Lb, SSM_DI), lambda b, l: (row(b, l), 0)),
            pl.BlockSpec((1, CONV_W - 1, SSM_CONV_CH), lambda b, l: (b, 0, 0)),
            pl.BlockSpec((1, SSM_HEADS, SSM_P, SSM_N), lambda b, l: (b, 0, 0, 0)),
        ],
        out_shape=[
            jax.ShapeDtypeStruct((B * L, SSM_DI), out_dtype),
            jax.ShapeDtypeStruct((B, CONV_W - 1, SSM_CONV_CH), F32),
            jax.ShapeDtypeStruct((B, SSM_HEADS, SSM_P, SSM_N), F32),
        ],
        scratch_shapes=[
            pltpu.VMEM((Lb + 8, SSM_DI), F32),
            pltpu.VMEM((Lb + 8, 2 * SSM_BC), F32),
            pltpu.VMEM((Lb, SSM_DI), F32),
            pltpu.VMEM((Lb, 2 * SSM_BC), F32),
            pltpu.VMEM((N_PAIRS, 2 * SSM_P, 2 * SSM_N), F32),
            pltpu.VMEM((c, SSM_DI), F32),
        ],
        compiler_params=pltpu.CompilerParams(
            dimension_semantics=("parallel", "arbitrary"), vmem_limit_bytes=VMEM_LIMIT),
        name="ssm_c%d" % c,
    )(*args)


def _outproj_kernel(x_ref, mg_ref, ms_ref, w_ref, o_ref):
    acc = _dot(mg_ref[...].astype(BF16), w_ref[:GDN_V, :])
    acc = acc + _dot(ms_ref[...].astype(BF16), w_ref[GDN_V:, :])
    o_ref[...] = x_ref[...] + acc


def _out_proj(x2d, mix_g, mix_s, w_out16, *, tm):
    T = x2d.shape[0]
    return pl.pallas_call(
        _outproj_kernel,
        grid=(T // tm,),
        in_specs=[
            pl.BlockSpec((tm, D_MODEL), lambda i: (i, 0)),
            pl.BlockSpec((tm, GDN_V), lambda i: (i, 0)),
            pl.BlockSpec((tm, SSM_DI), lambda i: (i, 0)),
            pl.BlockSpec((D_MODEL, D_MODEL), lambda i: (0, 0)),
        ],
        out_specs=pl.BlockSpec((tm, D_MODEL), lambda i: (i, 0)),
        out_shape=jax.ShapeDtypeStruct((T, D_MODEL), F32),
        compiler_params=pltpu.CompilerParams(
            dimension_semantics=("parallel",), vmem_limit_bytes=VMEM_LIMIT),
        name="out_proj",
    )(x2d, mix_g, mix_s, w_out16)


def _ffn_kernel(x_ref, nw_ref, wg_ref, wu_ref, wd_ref, fnw_ref, o_ref, h_ref, acc_ref):
    f = pl.program_id(1)

    @pl.when(f == 0)
    def _():
        h_ref[...] = _rms_rows(x_ref[...], nw_ref[...]).astype(BF16)
        acc_ref[...] = jnp.zeros(acc_ref.shape, F32)

    h = h_ref[...]
    a = (_silu(_dot(h, wg_ref[...])) * _dot(h, wu_ref[...])).astype(BF16)
    acc_ref[...] += _dot(a, wd_ref[...])

    @pl.when(f == pl.num_programs(1) - 1)
    def _():
        o_ref[...] = _rms_rows(x_ref[...] + acc_ref[...], fnw_ref[...])


def _ffn(x2d, norm_w, wg16, wu16, wd16, final_w, *, tm, tf):
    T = x2d.shape[0]
    return pl.pallas_call(
        _ffn_kernel,
        grid=(T // tm, D_FF // tf),
        in_specs=[
            pl.BlockSpec((tm, D_MODEL), lambda i, f: (i, 0)),
            pl.BlockSpec((1, D_MODEL), lambda i, f: (0, 0)),
            pl.BlockSpec((D_MODEL, tf), lambda i, f: (0, f)),
            pl.BlockSpec((D_MODEL, tf), lambda i, f: (0, f)),
            pl.BlockSpec((tf, D_MODEL), lambda i, f: (f, 0)),
            pl.BlockSpec((1, D_MODEL), lambda i, f: (0, 0)),
        ],
        out_specs=pl.BlockSpec((tm, D_MODEL), lambda i, f: (i, 0)),
        out_shape=jax.ShapeDtypeStruct((T, D_MODEL), F32),
        scratch_shapes=[pltpu.VMEM((tm, D_MODEL), BF16), pltpu.VMEM((tm, D_MODEL), F32)],
        compiler_params=pltpu.CompilerParams(
            dimension_semantics=("parallel", "arbitrary"), vmem_limit_bytes=VMEM_LIMIT),
        name="ffn",
    )(x2d, norm_w, wg16, wu16, wd16, final_w)


def _trunk(x, states, p):
    B, L, _ = x.shape
    x2d = x.reshape(B * L, D_MODEL)
    proj = _in_proj(x2d, p["attn_norm_w"], p["w_in_r"], tm=1024, tn=768)
    gdn_w = (p["gdn_conv_w"], p["gbias"], p["galog"], p["gdn_norm_w"])
    ssm_w = (p["cwx"], p["cbx"], p["cwbc"], p["cbbc"], p["gbias"], p["galog"], p["dcols"], p["ssm_norm_w"])
    if states is None:
        mix_g, gconv_new, gS_new = _gdn_prompt(proj, *gdn_w, B=B, L=L, Lb=256)
        mix_s, sconv_new, sh_new = _ssm(proj, None, None, *ssm_w, B=B, L=L, c=CHUNK, Lb=256, out_dtype=BF16)
    else:
        gconv, gS, sconv, sh = states
        mix_g, gconv_new, gS_new = _gdn_sample(proj, gconv, gS, *gdn_w, B=B, L=L, nb=4)
        mix_s, sconv_new, sh_new = _ssm(proj, sconv, sh, *ssm_w, B=B, L=L, c=L, Lb=L, out_dtype=F32)
    x1 = _out_proj(x2d, mix_g, mix_s, p["w_out16"], tm=512)
    y = _ffn(x1, p["ffn_norm_w"], p["wg16"], p["wu16"], p["wd16"], p["final_norm_w"], tm=512, tf=512)
    return y.reshape(B, L, D_MODEL), (gconv_new[None], gS_new[None], sconv_new[None], sh_new[None])


def kernel(x_prompt, x_sample, state_gdn_conv, state_gdn, state_ssm_conv, state_ssm,
           attn_norm_w, w_in, gdn_conv_w, gdn_A_log, gdn_dt_bias, gdn_norm_w,
           ssm_conv_w, ssm_conv_b, ssm_A_log, ssm_dt_bias, ssm_D, ssm_norm_w,
           w_out, ffn_norm_w, w_gate, w_up, w_down, final_norm_w):
    assert w_in.shape[0] == 1, "single-layer trunk"
    assert x_prompt.shape[1] % 256 == 0 and x_sample.shape[1] == 8 and x_sample.shape[0] % 4 == 0
    w = w_in[0]
    o_qkv, o_zg, o_b, o_a, o_zs, o_xbc, o_dt = 0, 3072, 4096, 4104, 4112, 5136, 6672
    pad = N_PROJ - (OFF_GATE + GATE_DT + SSM_HEADS)
    w_in_r = jnp.concatenate([
        w[:, o_qkv:o_zg], w[:, o_zg:o_b], w[:, o_zs:o_xbc], w[:, o_xbc:o_dt],
        w[:, o_b:o_a], w[:, o_a:o_zs], w[:, o_dt:], jnp.zeros((D_MODEL, pad), w.dtype),
    ], axis=1).astype(BF16)
    zeros8 = jnp.zeros((GDN_HEADS,), F32)
    tail = jnp.zeros((LANES - GATE_DT - SSM_HEADS,), F32)
    gbias = jnp.concatenate([zeros8, gdn_dt_bias[0], ssm_dt_bias[0], tail])[None]
    galog = jnp.concatenate([zeros8, gdn_A_log[0], ssm_A_log[0], tail])[None]
    p = dict(
        attn_norm_w=attn_norm_w, w_in_r=w_in_r, gdn_conv_w=gdn_conv_w[0], gbias=gbias, galog=galog,
        gdn_norm_w=gdn_norm_w,
        cwx=ssm_conv_w[0][:, :SSM_DI], cbx=ssm_conv_b[:, :SSM_DI],
        cwbc=ssm_conv_w[0][:, SSM_DI:], cbbc=ssm_conv_b[:, SSM_DI:],
        dcols=jnp.repeat(ssm_D[0], SSM_P)[None], ssm_norm_w=ssm_norm_w,
        w_out16=w_out[0].astype(BF16), ffn_norm_w=ffn_norm_w,
        wg16=w_gate[0].astype(BF16), wu16=w_up[0].astype(BF16), wd16=w_down[0].astype(BF16),
        final_norm_w=final_norm_w[None],
    )
    y_p, st_p = _trunk(x_prompt, None, p)
    y_s, st_s = _trunk(x_sample, (state_gdn_conv[0], state_gdn[0], state_ssm_conv[0], state_ssm[0]), p)
    return (y_p, y_s, st_p[0], st_p[1], st_p[2], st_p[3], st_s[0], st_s[1], st_s[2], st_s[3])
```

```python
import functools

import jax
import jax.numpy as jnp
from jax import lax
from jax.experimental import pallas as pl
from jax.experimental.pallas import tpu as pltpu

F32 = jnp.float32
BF16 = jnp.bfloat16

D_MODEL = 2048
GDN_HEADS = 8
GDN_DK = 128
GDN_DV = 128
GDN_QK = GDN_HEADS * GDN_DK
GDN_V = GDN_HEADS * GDN_DV
GDN_CONV_CH = 2 * GDN_QK + GDN_V
SSM_P = 64
SSM_N = 128
SSM_GROUPS = 2
SSM_DI = 1024
SSM_HEADS = SSM_DI // SSM_P
SSM_BC = SSM_GROUPS * SSM_N
SSM_CONV_CH = SSM_DI + 2 * SSM_BC
CONV_W = 4
CHUNK = 64
D_FF = 5632
EPS = 1e-6

OFF_QKV = 0
OFF_ZG = OFF_QKV + GDN_CONV_CH
OFF_ZS = OFF_ZG + GDN_V
OFF_XS = OFF_ZS + SSM_DI
OFF_BC = OFF_XS + SSM_DI
OFF_GATE = OFF_BC + 2 * SSM_BC
LANES = 128
GATE_B = 0
GATE_A = GATE_B + GDN_HEADS
GATE_DT = GATE_A + GDN_HEADS
GATE_DT2 = GATE_DT + SSM_HEADS
N_PROJ = 6912

VMEM_LIMIT = 52 * 1024 * 1024

_HIGHEST = lax.Precision.HIGHEST


def _silu(x):
    return x * jax.nn.sigmoid(x)


def _softplus(x):
    return jnp.maximum(x, 0.0) + jnp.log1p(jnp.exp(-jnp.abs(x)))


def _dot(a, b):
    return jnp.dot(a, b, preferred_element_type=F32)


def _dot_nt(a, b):
    return lax.dot_general(a, b, (((1,), (1,)), ((), ())), preferred_element_type=F32)


def _dot_tn(a, b):
    return lax.dot_general(a, b, (((0,), (0,)), ((), ())), preferred_element_type=F32)


def _transpose_rows(a):
    r = a.shape[0]
    if r < LANES:
        a = jnp.concatenate([a, jnp.zeros((LANES - r, LANES), a.dtype)], axis=0)
    return a.T


def _rms_rows(x, w):
    return x * lax.rsqrt(jnp.mean(x * x, axis=-1, keepdims=True) + EPS) * w


def _inproj_kernel(x_ref, nw_ref, w_ref, o_ref, h_ref):
    @pl.when(pl.program_id(1) == 0)
    def _():
        h_ref[...] = _rms_rows(x_ref[...], nw_ref[...]).astype(BF16)

    o_ref[...] = _dot(h_ref[...], w_ref[...])


def _in_proj(x2d, norm_w, w_in_r, *, tm, tn):
    T = x2d.shape[0]
    return pl.pallas_call(
        _inproj_kernel,
        grid=(T // tm, N_PROJ // tn),
        in_specs=[
            pl.BlockSpec((tm, D_MODEL), lambda i, j: (i, 0)),
            pl.BlockSpec((1, D_MODEL), lambda i, j: (0, 0)),
            pl.BlockSpec((D_MODEL, tn), lambda i, j: (0, j)),
        ],
        out_specs=pl.BlockSpec((tm, tn), lambda i, j: (i, j)),
        out_shape=jax.ShapeDtypeStruct((T, N_PROJ), F32),
        scratch_shapes=[pltpu.VMEM((tm, D_MODEL), BF16)],
        compiler_params=pltpu.CompilerParams(
            dimension_semantics=("parallel", "arbitrary"), vmem_limit_bytes=VMEM_LIMIT),
        name="in_proj",
    )(x2d, norm_w, w_in_r)


def _conv_block(x_ref, xpad_ref, cw_ref, cb_ref, dst_ref, Lb, C, post):
    xpad_ref[8:8 + Lb, :] = x_ref[...]
    rs = min(Lb, CHUNK)
    for sb in range(Lb // rs):
        r = sb * rs
        for s in range(C // LANES):
            cols = slice(s * LANES, (s + 1) * LANES)
            acc = xpad_ref[8 + r:8 + r + rs, cols] * cw_ref[3:4, cols]
            for i in range(CONV_W - 1):
                acc = acc + xpad_ref[5 + i + r:5 + i + r + rs, cols] * cw_ref[i:i + 1, cols]
            if cb_ref is not None:
                acc = acc + cb_ref[:, cols]
            dst_ref[r:r + rs, cols] = post(s, _silu(acc))
    last3 = xpad_ref[Lb + 5:Lb + 8, :]
    xpad_ref[5:8, :] = last3
    return last3


STACK = 128


def _gdn_qk_post(s, y):
    if s < 2 * GDN_HEADS:
        y = y * lax.rsqrt(jnp.sum(y * y, axis=-1, keepdims=True) + EPS)
        if s < GDN_HEADS:
            y = y * (GDN_DK ** -0.5)
    return y


def _gdn_local(items, glen):
    sh = glen.bit_length() - 1
    row = lax.broadcasted_iota(jnp.int32, (STACK, STACK), 0)
    col = lax.broadcasted_iota(jnp.int32, (STACK, STACK), 1)
    same = (row >> sh) == (col >> sh)
    incl = same & (row >= col)
    strict = same & (row > col)
    eye = (row == col).astype(F32)

    decay = [jnp.exp(jnp.where(incl, it["g"] - it["g"].T, -jnp.inf)) for it in items]
    kb = [it["k"] * it["beta"] for it in items]
    qkk = [_dot_nt(jnp.concatenate([it["q"], b], axis=0).astype(BF16), it["k"].astype(BF16))
           for it, b in zip(items, kb)]
    qk = [x[:STACK] * d for x, d in zip(qkk, decay)]
    nmat = [jnp.where(strict, -(x[STACK:] * d), 0.0) for x, d in zip(qkk, decay)]
    tinv = [eye + n for n in nmat]
    if sh >= 2:
        pw = [_dot(n.astype(BF16), n.astype(BF16)) for n in nmat]
        for _ in range(sh - 2):
            x = [_dot(jnp.concatenate([t, p], axis=0).astype(BF16), p.astype(BF16))
                 for t, p in zip(tinv, pw)]
            tinv = [t + y[:STACK] for t, y in zip(tinv, x)]
            pw = [y[STACK:] for y in x]
        tinv = [t + _dot(t.astype(BF16), p.astype(BF16)) for t, p in zip(tinv, pw)]
    uw = [_dot(t.astype(BF16),
               jnp.concatenate([it["v"] * it["beta"], b * it["eg"]], axis=1).astype(BF16))
          for t, it, b in zip(tinv, items, kb)]
    return [(x[:, :GDN_DV], x[:, GDN_DV:]) for x in uw], qk


def _gdn_prompt_kernel(qkv_ref, zg_ref, gate_ref, cw_ref, gbias_ref, galog_ref, nw_ref,
                       o_ref, cst_out_ref, sst_out_ref, xpad_ref, qkvc_ref, s_ref, *, Lb):
    c = CHUNK
    l = pl.program_id(1)
    nl = pl.num_programs(1)

    @pl.when(l == 0)
    def _init():
        xpad_ref[0:8, :] = jnp.zeros((8, GDN_CONV_CH), F32)
        s_ref[...] = jnp.zeros(s_ref.shape, F32)

    last3 = _conv_block(qkv_ref, xpad_ref, cw_ref, None, qkvc_ref, Lb, GDN_CONV_CH, _gdn_qk_post)

    @pl.when(l == nl - 1)
    def _conv_state():
        cst_out_ref[0] = last3

    row_i = lax.broadcasted_iota(jnp.int32, (c, c), 0)
    col_i = lax.broadcasted_iota(jnp.int32, (c, c), 1)
    tril_f = (row_i >= col_i).astype(F32)
    gbias = gbias_ref[...]
    nega = -jnp.exp(galog_ref[...])
    nw = nw_ref[...]
    n_pairs = GDN_HEADS // 2

    def chunk(r0):
        rows = pl.ds(r0, c)
        graw = gate_ref[rows, :]
        sp = _softplus(graw + gbias)
        beta_all = jax.nn.sigmoid(graw)
        G = jnp.dot(tril_f, nega * sp, precision=_HIGHEST, preferred_element_type=F32)
        eG = jnp.exp(G)
        glast = G[c - 1:c, :]
        eGrev = jnp.exp(glast - G)
        egl = jnp.exp(glast)

        def heads(off, a, b):
            return jnp.concatenate([qkvc_ref[rows, off + a * LANES:off + (a + 1) * LANES],
                                    qkvc_ref[rows, off + b * LANES:off + (b + 1) * LANES]], axis=0)

        def colstack(m, a, b):
            return jnp.concatenate([jnp.broadcast_to(m[:, a:a + 1], (c, LANES)),
                                    jnp.broadcast_to(m[:, b:b + 1], (c, LANES))], axis=0)

        items = []
        for pr in range(n_pairs):
            a, b = 2 * pr, 2 * pr + 1
            items.append(dict(
                q=heads(0, a, b), k=heads(GDN_QK, a, b), v=heads(2 * GDN_QK, a, b),
                beta=colstack(beta_all, GATE_B + a, GATE_B + b),
                g=colstack(G, GATE_A + a, GATE_A + b),
                eg=colstack(eG, GATE_A + a, GATE_A + b),
                egrev=colstack(eGrev, GATE_A + a, GATE_A + b)))
        uw, qk = _gdn_local(items, c)

        r = []
        for pr, it in enumerate(items):
            w = uw[pr][1]
            qd = it["q"] * it["eg"]
            r.append([
                _dot(jnp.concatenate([w[hh * c:(hh + 1) * c], qd[hh * c:(hh + 1) * c]], axis=0).astype(BF16),
                     s_ref[2 * pr + hh].astype(BF16))
                for hh in range(2)])
        v16 = [(uw[pr][0] - jnp.concatenate([r[pr][0][:c], r[pr][1][:c]], axis=0)).astype(BF16)
               for pr in range(n_pairs)]
        o = [jnp.concatenate([r[pr][0][c:], r[pr][1][c:]], axis=0) + _dot(qk[pr].astype(BF16), v16[pr])
             for pr in range(n_pairs)]
        for pr, it in enumerate(items):
            kd16 = (it["k"] * it["egrev"]).astype(BF16)
            for hh in range(2):
                h = 2 * pr + hh
                ga = GATE_A + h
                s_ref[h] = (s_ref[h] * egl[:, ga:ga + 1]
                            + _dot_tn(kd16[hh * c:(hh + 1) * c], v16[pr][hh * c:(hh + 1) * c]))
        for pr in range(n_pairs):
            for hh in range(2):
                h = 2 * pr + hh
                z = zg_ref[rows, h * GDN_DV:(h + 1) * GDN_DV]
                o_ref[rows, h * GDN_DV:(h + 1) * GDN_DV] = (
                    _rms_rows(o[pr][hh * c:(hh + 1) * c], nw) * _silu(z)).astype(o_ref.dtype)

    def body(ci, carry):
        chunk(pl.multiple_of(ci * c, c))
        return carry
    lax.fori_loop(0, Lb // c, body, 0)

    @pl.when(l == nl - 1)
    def _state_out():
        sst_out_ref[0] = s_ref[...]


def _gdn_prompt(proj, cw, gbias, galog, nw, *, B, L, Lb):
    nl = L // Lb
    row = lambda b, l: b * nl + l
    const = lambda shape: pl.BlockSpec(shape, lambda b, l: (0,) * len(shape))
    return pl.pallas_call(
        functools.partial(_gdn_prompt_kernel, Lb=Lb),
        grid=(B, nl),
        in_specs=[
            pl.BlockSpec((Lb, GDN_CONV_CH), lambda b, l: (row(b, l), OFF_QKV // GDN_CONV_CH)),
            pl.BlockSpec((Lb, GDN_V), lambda b, l: (row(b, l), OFF_ZG // GDN_V)),
            pl.BlockSpec((Lb, LANES), lambda b, l: (row(b, l), OFF_GATE // LANES)),
            const((CONV_W, GDN_CONV_CH)), const((1, LANES)), const((1, LANES)), const((1, GDN_DV)),
        ],
        out_specs=[
            pl.BlockSpec((Lb, GDN_V), lambda b, l: (row(b, l), 0)),
            pl.BlockSpec((1, CONV_W - 1, GDN_CONV_CH), lambda b, l: (b, 0, 0)),
            pl.BlockSpec((1, GDN_HEADS, GDN_DK, GDN_DV), lambda b, l: (b, 0, 0, 0)),
        ],
        out_shape=[
            jax.ShapeDtypeStruct((B * L, GDN_V), BF16),
            jax.ShapeDtypeStruct((B, CONV_W - 1, GDN_CONV_CH), F32),
            jax.ShapeDtypeStruct((B, GDN_HEADS, GDN_DK, GDN_DV), F32),
        ],
        scratch_shapes=[
            pltpu.VMEM((Lb + 8, GDN_CONV_CH), F32),
            pltpu.VMEM((Lb, GDN_CONV_CH), F32),
            pltpu.VMEM((GDN_HEADS, GDN_DK, GDN_DV), F32),
        ],
        compiler_params=pltpu.CompilerParams(
            dimension_semantics=("parallel", "arbitrary"), vmem_limit_bytes=VMEM_LIMIT),
        name="gdn_prompt",
    )(proj, proj, proj, cw, gbias, galog, nw)


def _gdn_sample_kernel(qkv_ref, zg_ref, gate_ref, cst_ref, sst_ref, cw_ref, gbias_ref, galog_ref, nw_ref,
                       o_ref, cst_out_ref, sst_out_ref, xpad_ref, qkvc_ref, *, nb, L):
    R = nb * L
    sh = L.bit_length() - 1
    for bi in range(nb):
        xp = xpad_ref.at[bi]
        xp[5:8, :] = cst_ref[bi]
        cst_out_ref[bi] = _conv_block(
            qkv_ref.at[pl.ds(bi * L, L)], xp, cw_ref, None, qkvc_ref.at[pl.ds(bi * L, L)],
            L, GDN_CONV_CH, _gdn_qk_post)

    row_i = lax.broadcasted_iota(jnp.int32, (R, R), 0)
    col_i = lax.broadcasted_iota(jnp.int32, (R, R), 1)
    tril_f = (((row_i >> sh) == (col_i >> sh)) & (row_i >= col_i)).astype(F32)
    graw = gate_ref[...]
    sp = _softplus(graw + gbias_ref[...])
    beta_all = jax.nn.sigmoid(graw)
    G = jnp.dot(tril_f, -jnp.exp(galog_ref[...]) * sp, precision=_HIGHEST, preferred_element_type=F32)
    glast = [G[bi * L + L - 1:bi * L + L, :] for bi in range(nb)]
    eG = jnp.exp(G)
    eGrev = jnp.exp(jnp.concatenate([jnp.broadcast_to(x, (L, LANES)) for x in glast], axis=0) - G)
    egl = [jnp.exp(x) for x in glast]
    nw = nw_ref[...]

    n_st = R // (2 * L)

    def tiles(ref, st, off):
        return jnp.concatenate(
            [ref[st * 2 * L:(st + 1) * 2 * L, off + h * LANES:off + (h + 1) * LANES]
             for h in range(GDN_HEADS)], axis=0)

    def colstack(m, st, off):
        return jnp.concatenate(
            [jnp.broadcast_to(m[st * 2 * L:(st + 1) * 2 * L, off + h:off + h + 1], (2 * L, LANES))
             for h in range(GDN_HEADS)], axis=0)

    items = [dict(q=tiles(qkvc_ref, st, 0), k=tiles(qkvc_ref, st, GDN_QK), v=tiles(qkvc_ref, st, 2 * GDN_QK),
                  beta=colstack(beta_all, st, GATE_B), g=colstack(G, st, GATE_A),
                  eg=colstack(eG, st, GATE_A), egrev=colstack(eGrev, st, GATE_A))
             for st in range(n_st)]
    uw, qk = _gdn_local(items, L)

    groups = [(h, bi) for h in range(GDN_HEADS) for bi in range(2)]
    r = []
    for st, it in enumerate(items):
        w = uw[st][1]
        qd = it["q"] * it["eg"]
        r.append([
            _dot(jnp.concatenate([w[gi * L:(gi + 1) * L], qd[gi * L:(gi + 1) * L]], axis=0).astype(BF16),
                 sst_ref[2 * st + bi, h].astype(BF16))
            for gi, (h, bi) in enumerate(groups)])
    v_new = [uw[st][0] - jnp.concatenate([x[:L] for x in r[st]], axis=0) for st in range(n_st)]
    o = [jnp.concatenate([x[L:] for x in r[st]], axis=0)
         + _dot(qk[st].astype(BF16), v_new[st].astype(BF16)) for st in range(n_st)]
    for st, it in enumerate(items):
        kd = it["k"] * it["egrev"]
        for gi, (h, bi) in enumerate(groups):
            b = 2 * st + bi
            ga = GATE_A + h
            rs = slice(gi * L, (gi + 1) * L)
            sst_out_ref[b, h] = (sst_ref[b, h] * egl[b][:, ga:ga + 1]
                                 + _dot_tn(kd[rs].astype(BF16), v_new[st][rs].astype(BF16)))
    for st in range(n_st):
        out = (_rms_rows(o[st], nw) * _silu(tiles(zg_ref, st, 0))).astype(o_ref.dtype)
        for h in range(GDN_HEADS):
            o_ref[st * 2 * L:(st + 1) * 2 * L, h * GDN_DV:(h + 1) * GDN_DV] = out[h * 2 * L:(h + 1) * 2 * L]


def _gdn_sample(proj, conv_state, S_state, cw, gbias, galog, nw, *, B, L, nb):
    R = nb * L
    const = lambda shape: pl.BlockSpec(shape, lambda i: (0,) * len(shape))
    return pl.pallas_call(
        functools.partial(_gdn_sample_kernel, nb=nb, L=L),
        grid=(B // nb,),
        in_specs=[
            pl.BlockSpec((R, GDN_CONV_CH), lambda i: (i, OFF_QKV // GDN_CONV_CH)),
            pl.BlockSpec((R, GDN_V), lambda i: (i, OFF_ZG // GDN_V)),
            pl.BlockSpec((R, LANES), lambda i: (i, OFF_GATE // LANES)),
            pl.BlockSpec((nb, CONV_W - 1, GDN_CONV_CH), lambda i: (i, 0, 0)),
            pl.BlockSpec((nb, GDN_HEADS, GDN_DK, GDN_DV), lambda i: (i, 0, 0, 0)),
            const((CONV_W, GDN_CONV_CH)), const((1, LANES)), const((1, LANES)), const((1, GDN_DV)),
        ],
        out_specs=[
            pl.BlockSpec((R, GDN_V), lambda i: (i, 0)),
            pl.BlockSpec((nb, CONV_W - 1, GDN_CONV_CH), lambda i: (i, 0, 0)),
            pl.BlockSpec((nb, GDN_HEADS, GDN_DK, GDN_DV), lambda i: (i, 0, 0, 0)),
        ],
        out_shape=[
            jax.ShapeDtypeStruct((B * L, GDN_V), BF16),
            jax.ShapeDtypeStruct((B, CONV_W - 1, GDN_CONV_CH), F32),
            jax.ShapeDtypeStruct((B, GDN_HEADS, GDN_DK, GDN_DV), F32),
        ],
        scratch_shapes=[
            pltpu.VMEM((nb, L + 8, GDN_CONV_CH), F32),
            pltpu.VMEM((R, GDN_CONV_CH), F32),
        ],
        compiler_params=pltpu.CompilerParams(
            dimension_semantics=("parallel",), vmem_limit_bytes=VMEM_LIMIT),
        name="gdn_sample",
    )(proj, proj, proj, conv_state, S_state, cw, gbias, galog, nw)


N_PAIRS = SSM_HEADS // 2
PAIRS_PER_GROUP = N_PAIRS // SSM_GROUPS
GROUP_W = SSM_DI // SSM_GROUPS


def _ssm_tile(graw, gbias, nega, ld_x, ld_b, ld_c, ld_z, dcols_ref, nw_ref, st_o, get_h, set_h, glen):
    c = CHUNK
    P = SSM_P
    nseq = c // glen
    sh = glen.bit_length() - 1
    ri = lax.broadcasted_iota(jnp.int32, (c, c), 0)
    ci = lax.broadcasted_iota(jnp.int32, (c, c), 1)
    tril_f = (((ri >> sh) == (ci >> sh)) & (ri >= ci)).astype(F32)
    sp = _softplus(graw + gbias)
    acum = jnp.dot(tril_f, nega * sp, precision=_HIGHEST, preferred_element_type=F32)
    lasts = [acum[s * glen + glen - 1:(s + 1) * glen, :] for s in range(nseq)]
    alast = jnp.concatenate([jnp.broadcast_to(x, (glen, LANES)) for x in lasts], axis=0)
    dtrev = sp * jnp.exp(alast - acum)
    eal = [jnp.exp(x) for x in lasts]
    lane = lax.broadcasted_iota(jnp.int32, (c, LANES), 1)
    row = lax.broadcasted_iota(jnp.int32, (c, LANES), 0)
    m = jnp.where(lane < GATE_DT2, acum, sp)
    mt = jnp.concatenate([m, m], axis=0).T
    left = lane < P
    left_row = left[0:1]
    j = jnp.where(left, lane, lane - P)
    tril2 = ((row >> sh) == (j >> sh)) & (row >= j)
    rowh = lax.broadcasted_iota(jnp.int32, (2 * P, SSM_N), 0) < P

    def expand(mat, c0):
        return jnp.where(left, jnp.broadcast_to(mat[:, c0:c0 + 1], (c, LANES)),
                         jnp.broadcast_to(mat[:, c0 + 1:c0 + 2], (c, LANES)))

    def rowsel(base, e):
        return jnp.where(left_row, mt[base + 2 * e:base + 2 * e + 1, :], mt[base + 2 * e + 1:base + 2 * e + 2, :])

    for g in range(SSM_GROUPS):
        Bg = ld_b(g)
        Cg = ld_c(g)
        Bg16 = Bg.astype(BF16)
        Cg16 = Cg.astype(BF16)
        cb2 = _dot_nt(Cg16, jnp.concatenate([Bg16, Bg16], axis=0))
        pairs = [g * PAIRS_PER_GROUP + e4 for e4 in range(PAIRS_PER_GROUP)]
        acol = [expand(acum, GATE_DT + 2 * e) for e in pairs]
        scores16 = [
            (cb2 * jnp.exp(jnp.where(tril2, a - rowsel(GATE_DT, e), -jnp.inf)) * rowsel(GATE_DT2, e)).astype(BF16)
            for a, e in zip(acol, pairs)]
        xp = [ld_x(e) for e in pairs]
        bd16 = [jnp.concatenate([jnp.where(left, x, 0.0), jnp.where(left, 0.0, x)], axis=0).astype(BF16)
                for x in xp]
        ydiag = [_dot(s, b) for s, b in zip(scores16, bd16)]
        if nseq == 1:
            yoff = [_dot_nt(Cg16, get_h(0, e).astype(BF16)) for e in pairs]
        else:
            yoff = [jnp.concatenate(
                [_dot_nt(Cg[s * glen:(s + 1) * glen].astype(BF16), get_h(s, e).astype(BF16))
                 for s in range(nseq)], axis=0) for e in pairs]
        y = [yd + yo * jnp.exp(a) + dcols_ref[:, e * 2 * P:(e + 1) * 2 * P] * x
             for yd, yo, a, e, x in zip(ydiag, yoff, acol, pairs, xp)]
        xdr = [x * expand(dtrev, GATE_DT + 2 * e) for x, e in zip(xp, pairs)]
        for e, xd in zip(pairs, xdr):
            c0 = GATE_DT + 2 * e
            for s in range(nseq):
                rs = slice(s * glen, (s + 1) * glen)
                ealcol = jnp.where(rowh, eal[s][:, c0:c0 + 1], eal[s][:, c0 + 1:c0 + 2])
                set_h(s, e, get_h(s, e) * ealcol + _dot_tn(xd[rs].astype(BF16), Bg[rs].astype(BF16)))
        yg = jnp.concatenate(y, axis=1) * _silu(ld_z(g))
        gcols = slice(g * GROUP_W, (g + 1) * GROUP_W)
        st_o(g, _rms_rows(yg, nw_ref[:, gcols]))


def _ssm_prompt_kernel(xs_ref, bc_ref, zs_ref, gate_ref, cwx_ref, cbx_ref, cwbc_ref, cbbc_ref,
                       gbias_ref, galog_ref, dcols_ref, nw_ref,
                       o_ref, cst_out_ref, hst_out_ref, xpadx_ref, xpadbc_ref, xc_ref, bcc_ref, hh_ref, *, Lb):
    c = CHUNK
    l = pl.program_id(1)
    nl = pl.num_programs(1)

    @pl.when(l == 0)
    def _init():
        hh_ref[...] = jnp.zeros(hh_ref.shape, F32)
        xpadx_ref[0:8, :] = jnp.zeros((8, SSM_DI), F32)
        xpadbc_ref[0:8, :] = jnp.zeros((8, 2 * SSM_BC), F32)

    ident = lambda s, y: y
    last3x = _conv_block(xs_ref, xpadx_ref, cwx_ref, cbx_ref, xc_ref, Lb, SSM_DI, ident)
    last3bc = _conv_block(bc_ref, xpadbc_ref, cwbc_ref, cbbc_ref, bcc_ref, Lb, 2 * SSM_BC, ident)

    @pl.when(l == nl - 1)
    def _conv_state():
        cst_out_ref[0, :, :SSM_DI] = last3x
        cst_out_ref[0, :, SSM_DI:] = last3bc

    gbias = gbias_ref[...]
    nega = -jnp.exp(galog_ref[...])

    def set_h(s, e, val):
        hh_ref[e] = val

    def body(ci, carry):
        rows = pl.ds(pl.multiple_of(ci * c, c), c)

        def st_o(g, val):
            o_ref[rows, g * GROUP_W:(g + 1) * GROUP_W] = val.astype(o_ref.dtype)

        _ssm_tile(
            gate_ref[rows, :], gbias, nega,
            lambda e: xc_ref[rows, e * LANES:(e + 1) * LANES],
            lambda g: bcc_ref[rows, g * SSM_N:(g + 1) * SSM_N],
            lambda g: bcc_ref[rows, SSM_BC + g * SSM_N:SSM_BC + (g + 1) * SSM_N],
            lambda g: zs_ref[rows, g * GROUP_W:(g + 1) * GROUP_W],
            dcols_ref, nw_ref, st_o, lambda s, e: hh_ref[e], set_h, c)
        return carry
    lax.fori_loop(0, Lb // c, body, 0)

    @pl.when(l == nl - 1)
    def _state_out():
        hst_out_ref[0] = hh_ref[...]


def _ssm_sample_kernel(xs_ref, bc_ref, zs_ref, gate_ref, cst_ref, hst_ref, cwx_ref, cbx_ref, cwbc_ref, cbbc_ref,
                       gbias_ref, galog_ref, dcols_ref, nw_ref,
                       o_ref, cst_out_ref, hst_out_ref, xpadx_ref, xpadbc_ref, xc_ref, bcc_ref, *, L):
    ident = lambda s, y: y
    for bi in range(CHUNK // L):
        rs = pl.ds(bi * L, L)
        xpx = xpadx_ref.at[bi]
        xpb = xpadbc_ref.at[bi]
        xpx[5:8, :] = cst_ref[bi, :, :SSM_DI]
        xpb[5:8, :] = cst_ref[bi, :, SSM_DI:]
        cst_out_ref[bi, :, :SSM_DI] = _conv_block(
            xs_ref.at[rs], xpx, cwx_ref, cbx_ref, xc_ref.at[rs], L, SSM_DI, ident)
        cst_out_ref[bi, :, SSM_DI:] = _conv_block(
            bc_ref.at[rs], xpb, cwbc_ref, cbbc_ref, bcc_ref.at[rs], L, 2 * SSM_BC, ident)

    def st_o(g, val):
        o_ref[:, g * GROUP_W:(g + 1) * GROUP_W] = val.astype(o_ref.dtype)

    def set_h(s, e, val):
        hst_out_ref[s, e] = val

    _ssm_tile(
        gate_ref[...], gbias_ref[...], -jnp.exp(galog_ref[...]),
        lambda e: xc_ref[:, e * LANES:(e + 1) * LANES],
        lambda g: bcc_ref[:, g * SSM_N:(g + 1) * SSM_N],
        lambda g: bcc_ref[:, SSM_BC + g * SSM_N:SSM_BC + (g + 1) * SSM_N],
        lambda g: zs_ref[:, g * GROUP_W:(g + 1) * GROUP_W],
        dcols_ref, nw_ref, st_o, lambda s, e: hst_ref[s, e], set_h, L)


def _ssm_const_specs():
    const = lambda shape: pl.BlockSpec(shape, lambda *idx: (0,) * len(shape))
    return [
        const((CONV_W, SSM_DI)), const((1, SSM_DI)), const((CONV_W, 2 * SSM_BC)), const((1, 2 * SSM_BC)),
        const((1, LANES)), const((1, LANES)), const((1, SSM_DI)), const((1, SSM_DI)),
    ]


def _ssm_prompt(proj, cwx, cbx, cwbc, cbbc, gbias, galog, dcols, nw, *, B, L, Lb):
    nl = L // Lb
    row = lambda b, l: b * nl + l
    return pl.pallas_call(
        functools.partial(_ssm_prompt_kernel, Lb=Lb),
        grid=(B, nl),
        in_specs=[
            pl.BlockSpec((Lb, SSM_DI), lambda b, l: (row(b, l), OFF_XS // SSM_DI)),
            pl.BlockSpec((Lb, 2 * SSM_BC), lambda b, l: (row(b, l), OFF_BC // (2 * SSM_BC))),
            pl.BlockSpec((Lb, SSM_DI), lambda b, l: (row(b, l), OFF_ZS // SSM_DI)),
            pl.BlockSpec((Lb, LANES), lambda b, l: (row(b, l), OFF_GATE // LANES)),
        ] + _ssm_const_specs(),
        out_specs=[
            pl.BlockSpec((Lb, SSM_DI), lambda b, l: (row(b, l), 0)),
            pl.BlockSpec((1, CONV_W - 1, SSM_CONV_CH), lambda b, l: (b, 0, 0)),
            pl.BlockSpec((1, N_PAIRS, 2 * SSM_P, SSM_N), lambda b, l: (b, 0, 0, 0)),
        ],
        out_shape=[
            jax.ShapeDtypeStruct((B * L, SSM_DI), BF16),
            jax.ShapeDtypeStruct((B, CONV_W - 1, SSM_CONV_CH), F32),
            jax.ShapeDtypeStruct((B, N_PAIRS, 2 * SSM_P, SSM_N), F32),
        ],
        scratch_shapes=[
            pltpu.VMEM((Lb + 8, SSM_DI), F32),
            pltpu.VMEM((Lb + 8, 2 * SSM_BC), F32),
            pltpu.VMEM((Lb, SSM_DI), F32),
            pltpu.VMEM((Lb, 2 * SSM_BC), F32),
            pltpu.VMEM((N_PAIRS, 2 * SSM_P, SSM_N), F32),
        ],
        compiler_params=pltpu.CompilerParams(
            dimension_semantics=("parallel", "arbitrary"), vmem_limit_bytes=VMEM_LIMIT),
        name="ssm_prompt",
    )(proj, proj, proj, proj, cwx, cbx, cwbc, cbbc, gbias, galog, dcols, nw)


def _ssm_sample(proj, conv_state, h_pairs, cwx, cbx, cwbc, cbbc, gbias, galog, dcols, nw, *, B, L):
    nb = CHUNK // L
    return pl.pallas_call(
        functools.partial(_ssm_sample_kernel, L=L),
        grid=(B // nb,),
        in_specs=[
            pl.BlockSpec((CHUNK, SSM_DI), lambda i: (i, OFF_XS // SSM_DI)),
            pl.BlockSpec((CHUNK, 2 * SSM_BC), lambda i: (i, OFF_BC // (2 * SSM_BC))),
            pl.BlockSpec((CHUNK, SSM_DI), lambda i: (i, OFF_ZS // SSM_DI)),
            pl.BlockSpec((CHUNK, LANES), lambda i: (i, OFF_GATE // LANES)),
            pl.BlockSpec((nb, CONV_W - 1, SSM_CONV_CH), lambda i: (i, 0, 0)),
            pl.BlockSpec((nb, N_PAIRS, 2 * SSM_P, SSM_N), lambda i: (i, 0, 0, 0)),
        ] + _ssm_const_specs(),
        out_specs=[
            pl.BlockSpec((CHUNK, SSM_DI), lambda i: (i, 0)),
            pl.BlockSpec((nb, CONV_W - 1, SSM_CONV_CH), lambda i: (i, 0, 0)),
            pl.BlockSpec((nb, N_PAIRS, 2 * SSM_P, SSM_N), lambda i: (i, 0, 0, 0)),
        ],
        out_shape=[
            jax.ShapeDtypeStruct((B * L, SSM_DI), BF16),
            jax.ShapeDtypeStruct((B, CONV_W - 1, SSM_CONV_CH), F32),
            jax.ShapeDtypeStruct((B, N_PAIRS, 2 * SSM_P, SSM_N), F32),
        ],
        scratch_shapes=[
            pltpu.VMEM((nb, L + 8, SSM_DI), F32),
            pltpu.VMEM((nb, L + 8, 2 * SSM_BC), F32),
            pltpu.VMEM((CHUNK, SSM_DI), F32),
            pltpu.VMEM((CHUNK, 2 * SSM_BC), F32),
        ],
        compiler_params=pltpu.CompilerParams(
            dimension_semantics=("parallel",), vmem_limit_bytes=VMEM_LIMIT),
        name="ssm_sample",
    )(proj, proj, proj, proj, conv_state, h_pairs, cwx, cbx, cwbc, cbbc, gbias, galog, dcols, nw)


def _outproj_kernel(x_ref, mg_ref, ms_ref, w_ref, o_ref):
    acc = _dot(mg_ref[...].astype(BF16), w_ref[:GDN_V, :])
    acc = acc + _dot(ms_ref[...].astype(BF16), w_ref[GDN_V:, :])
    o_ref[...] = x_ref[...] + acc


def _out_proj(x2d, mix_g, mix_s, w_out16, *, tm):
    T = x2d.shape[0]
    return pl.pallas_call(
        _outproj_kernel,
        grid=(T // tm,),
        in_specs=[
            pl.BlockSpec((tm, D_MODEL), lambda i: (i, 0)),
            pl.BlockSpec((tm, GDN_V), lambda i: (i, 0)),
            pl.BlockSpec((tm, SSM_DI), lambda i: (i, 0)),
            pl.BlockSpec((D_MODEL, D_MODEL), lambda i: (0, 0)),
        ],
        out_specs=pl.BlockSpec((tm, D_MODEL), lambda i: (i, 0)),
        out_shape=jax.ShapeDtypeStruct((T, D_MODEL), F32),
        compiler_params=pltpu.CompilerParams(
            dimension_semantics=("parallel",), vmem_limit_bytes=VMEM_LIMIT),
        name="out_proj",
    )(x2d, mix_g, mix_s, w_out16)


def _ffn_kernel(x_ref, nw_ref, wg_ref, wu_ref, wd_ref, fnw_ref, o_ref, h_ref, acc_ref):
    f = pl.program_id(1)

    @pl.when(f == 0)
    def _():
        h_ref[...] = _rms_rows(x_ref[...], nw_ref[...]).astype(BF16)
        acc_ref[...] = jnp.zeros(acc_ref.shape, F32)

    h = h_ref[...]
    a = (_silu(_dot(h, wg_ref[...])) * _dot(h, wu_ref[...])).astype(BF16)
    acc_ref[...] += _dot(a, wd_ref[...])

    @pl.when(f == pl.num_programs(1) - 1)
    def _():
        o_ref[...] = _rms_rows(x_ref[...] + acc_ref[...], fnw_ref[...])


def _ffn(x2d, norm_w, wg16, wu16, wd16, final_w, *, tm, tf):
    T = x2d.shape[0]
    return pl.pallas_call(
        _ffn_kernel,
        grid=(T // tm, D_FF // tf),
        in_specs=[
            pl.BlockSpec((tm, D_MODEL), lambda i, f: (i, 0)),
            pl.BlockSpec((1, D_MODEL), lambda i, f: (0, 0)),
            pl.BlockSpec((D_MODEL, tf), lambda i, f: (0, f)),
            pl.BlockSpec((D_MODEL, tf), lambda i, f: (0, f)),
            pl.BlockSpec((tf, D_MODEL), lambda i, f: (f, 0)),
            pl.BlockSpec((1, D_MODEL), lambda i, f: (0, 0)),
        ],
        out_specs=pl.BlockSpec((tm, D_MODEL), lambda i, f: (i, 0)),
        out_shape=jax.ShapeDtypeStruct((T, D_MODEL), F32),
        scratch_shapes=[pltpu.VMEM((tm, D_MODEL), BF16), pltpu.VMEM((tm, D_MODEL), F32)],
        compiler_params=pltpu.CompilerParams(
            dimension_semantics=("parallel", "arbitrary"), vmem_limit_bytes=VMEM_LIMIT),
        name="ffn",
    )(x2d, norm_w, wg16, wu16, wd16, final_w)


def _trunk(x, states, p):
    B, L, _ = x.shape
    x2d = x.reshape(B * L, D_MODEL)
    proj = _in_proj(x2d, p["attn_norm_w"], p["w_in_r"], tm=1024, tn=768)
    gdn_w = (p["gdn_conv_w"], p["gbias"], p["galog"], p["gdn_norm_w"])
    ssm_w = (p["cwx"], p["cbx"], p["cwbc"], p["cbbc"], p["gbias"], p["galog"], p["dcols"], p["ssm_norm_w"])
    pair_shape = (B, N_PAIRS, 2 * SSM_P, SSM_N)
    if states is None:
        mix_g, gconv_new, gS_new = _gdn_prompt(proj, *gdn_w, B=B, L=L, Lb=256)
        mix_s, sconv_new, sh_new = _ssm_prompt(proj, *ssm_w, B=B, L=L, Lb=256)
    else:
        gconv, gS, sconv, sh = states
        mix_g, gconv_new, gS_new = _gdn_sample(proj, gconv, gS, *gdn_w, B=B, L=L, nb=4)
        mix_s, sconv_new, sh_new = _ssm_sample(proj, sconv, sh.reshape(pair_shape), *ssm_w, B=B, L=L)
    sh_new = sh_new.reshape(B, SSM_HEADS, SSM_P, SSM_N)
    x1 = _out_proj(x2d, mix_g, mix_s, p["w_out16"], tm=512)
    y = _ffn(x1, p["ffn_norm_w"], p["wg16"], p["wu16"], p["wd16"], p["final_norm_w"], tm=512, tf=512)
    return y.reshape(B, L, D_MODEL), (gconv_new[None], gS_new[None], sconv_new[None], sh_new[None])


def kernel(x_prompt, x_sample, state_gdn_conv, state_gdn, state_ssm_conv, state_ssm,
           attn_norm_w, w_in, gdn_conv_w, gdn_A_log, gdn_dt_bias, gdn_norm_w,
           ssm_conv_w, ssm_conv_b, ssm_A_log, ssm_dt_bias, ssm_D, ssm_norm_w,
           w_out, ffn_norm_w, w_gate, w_up, w_down, final_norm_w):
    assert w_in.shape[0] == 1, "single-layer trunk"
    assert x_prompt.shape[1] % 256 == 0 and x_sample.shape[1] == 8 and x_sample.shape[0] % 8 == 0
    w = w_in[0]
    o_qkv, o_zg, o_b, o_a, o_zs, o_xbc, o_dt = 0, 3072, 4096, 4104, 4112, 5136, 6672
    pad = N_PROJ - (OFF_GATE + GATE_DT2 + SSM_HEADS)
    w_in_r = jnp.concatenate([
        w[:, o_qkv:o_zg], w[:, o_zg:o_b], w[:, o_zs:o_xbc], w[:, o_xbc:o_dt],
        w[:, o_b:o_a], w[:, o_a:o_zs], w[:, o_dt:], w[:, o_dt:], jnp.zeros((D_MODEL, pad), w.dtype),
    ], axis=1).astype(BF16)
    zeros8 = jnp.zeros((GDN_HEADS,), F32)
    tail = jnp.zeros((LANES - GATE_DT2 - SSM_HEADS,), F32)
    gbias = jnp.concatenate([zeros8, gdn_dt_bias[0], ssm_dt_bias[0], ssm_dt_bias[0], tail])[None]
    galog = jnp.concatenate([zeros8, gdn_A_log[0], ssm_A_log[0], ssm_A_log[0], tail])[None]
    p = dict(
        attn_norm_w=attn_norm_w, w_in_r=w_in_r, gdn_conv_w=gdn_conv_w[0], gbias=gbias, galog=galog,
        gdn_norm_w=gdn_norm_w,
        cwx=ssm_conv_w[0][:, :SSM_DI], cbx=ssm_conv_b[:, :SSM_DI],
        cwbc=ssm_conv_w[0][:, SSM_DI:], cbbc=ssm_conv_b[:, SSM_DI:],
        dcols=jnp.repeat(ssm_D[0], SSM_P)[None], ssm_norm_w=ssm_norm_w,
        w_out16=w_out[0].astype(BF16), ffn_norm_w=ffn_norm_w,
        wg16=w_gate[0].astype(BF16), wu16=w_up[0].astype(BF16), wd16=w_down[0].astype(BF16),
        final_norm_w=final_norm_w[None],
    )
    y_p, st_p = _trunk(x_prompt, None, p)
    y_s, st_s = _trunk(x_sample, (state_gdn_conv[0], state_gdn[0], state_ssm_conv[0], state_ssm[0]), p)
    return (y_p, y_s, st_p[0], st_p[1], st_p[2], st_p[3], st_s[0], st_s[1], st_s[2], st_s[3])
```

```python
import functools

import jax
import jax.numpy as jnp
from jax import lax
from jax.experimental import pallas as pl
from jax.experimental.pallas import tpu as pltpu

F32 = jnp.float32
BF16 = jnp.bfloat16

D_MODEL = 2048
GDN_HEADS = 8
GDN_DK = 128
GDN_DV = 128
GDN_QK = GDN_HEADS * GDN_DK
GDN_V = GDN_HEADS * GDN_DV
GDN_CONV_CH = 2 * GDN_QK + GDN_V
SSM_P = 64
SSM_N = 128
SSM_GROUPS = 2
SSM_DI = 1024
SSM_HEADS = SSM_DI // SSM_P
SSM_BC = SSM_GROUPS * SSM_N
SSM_CONV_CH = SSM_DI + 2 * SSM_BC
CONV_W = 4
CHUNK = 64
D_FF = 5632
EPS = 1e-6

OFF_QKV = 0
OFF_ZG = OFF_QKV + GDN_CONV_CH
OFF_ZS = OFF_ZG + GDN_V
OFF_XS = OFF_ZS + SSM_DI
OFF_BC = OFF_XS + SSM_DI
OFF_GATE = OFF_BC + 2 * SSM_BC
LANES = 128
GATE_B = 0
GATE_A = GATE_B + GDN_HEADS
GATE_DT = GATE_A + GDN_HEADS
GATE_DT2 = GATE_DT + SSM_HEADS
N_PROJ = 6912

VMEM_LIMIT = 52 * 1024 * 1024

_HIGHEST = lax.Precision.HIGHEST


def _silu(x):
    return x * jax.nn.sigmoid(x)


def _softplus(x):
    return jnp.maximum(x, 0.0) + jnp.log1p(jnp.exp(-jnp.abs(x)))


def _dot(a, b):
    return jnp.dot(a, b, preferred_element_type=F32)


def _dot_nt(a, b):
    return lax.dot_general(a, b, (((1,), (1,)), ((), ())), preferred_element_type=F32)


def _dot_tn(a, b):
    return lax.dot_general(a, b, (((0,), (0,)), ((), ())), preferred_element_type=F32)


def _transpose_rows(a):
    r = a.shape[0]
    if r < LANES:
        a = jnp.concatenate([a, jnp.zeros((LANES - r, LANES), a.dtype)], axis=0)
    return a.T


def _rms_rows(x, w):
    return x * lax.rsqrt(jnp.mean(x * x, axis=-1, keepdims=True) + EPS) * w


def _inproj_kernel(x_ref, nw_ref, w_ref, o_ref, h_ref):
    @pl.when(pl.program_id(1) == 0)
    def _():
        h_ref[...] = _rms_rows(x_ref[...], nw_ref[...]).astype(BF16)

    o_ref[...] = _dot(h_ref[...], w_ref[...])


def _in_proj(x2d, norm_w, w_in_r, *, tm, tn):
    T = x2d.shape[0]
    return pl.pallas_call(
        _inproj_kernel,
        grid=(T // tm, N_PROJ // tn),
        in_specs=[
            pl.BlockSpec((tm, D_MODEL), lambda i, j: (i, 0)),
            pl.BlockSpec((1, D_MODEL), lambda i, j: (0, 0)),
            pl.BlockSpec((D_MODEL, tn), lambda i, j: (0, j)),
        ],
        out_specs=pl.BlockSpec((tm, tn), lambda i, j: (i, j)),
        out_shape=jax.ShapeDtypeStruct((T, N_PROJ), F32),
        scratch_shapes=[pltpu.VMEM((tm, D_MODEL), BF16)],
        compiler_params=pltpu.CompilerParams(
            dimension_semantics=("parallel", "arbitrary"), vmem_limit_bytes=VMEM_LIMIT),
        name="in_proj",
    )(x2d, norm_w, w_in_r)


def _conv_block(x_ref, xpad_ref, cw_ref, cb_ref, dst_ref, Lb, C, post):
    xpad_ref[8:8 + Lb, :] = x_ref[...]
    rs = min(Lb, CHUNK)
    for sb in range(Lb // rs):
        r = sb * rs
        for s in range(C // LANES):
            cols = slice(s * LANES, (s + 1) * LANES)
            acc = xpad_ref[8 + r:8 + r + rs, cols] * cw_ref[3:4, cols]
            for i in range(CONV_W - 1):
                acc = acc + xpad_ref[5 + i + r:5 + i + r + rs, cols] * cw_ref[i:i + 1, cols]
            if cb_ref is not None:
                acc = acc + cb_ref[:, cols]
            dst_ref[r:r + rs, cols] = post(s, _silu(acc))
    last3 = xpad_ref[Lb + 5:Lb + 8, :]
    xpad_ref[5:8, :] = last3
    return last3


STACK = 128


def _gdn_qk_post(s, y):
    if s < 2 * GDN_HEADS:
        y = y * lax.rsqrt(jnp.sum(y * y, axis=-1, keepdims=True) + EPS)
        if s < GDN_HEADS:
            y = y * (GDN_DK ** -0.5)
    return y


def _gdn_local(items, glen):
    sh = glen.bit_length() - 1
    row = lax.broadcasted_iota(jnp.int32, (STACK, STACK), 0)
    col = lax.broadcasted_iota(jnp.int32, (STACK, STACK), 1)
    same = (row >> sh) == (col >> sh)
    incl = same & (row >= col)
    strict = same & (row > col)
    eye = (row == col).astype(F32)

    decay = [jnp.exp(jnp.where(incl, it["g"] - it["g"].T, -jnp.inf)) for it in items]
    kb = [it["k"] * it["beta"] for it in items]
    qkk = [_dot_nt(jnp.concatenate([it["q"], b], axis=0).astype(BF16), it["k"].astype(BF16))
           for it, b in zip(items, kb)]
    qk = [x[:STACK] * d for x, d in zip(qkk, decay)]
    nmat = [jnp.where(strict, -(x[STACK:] * d), 0.0) for x, d in zip(qkk, decay)]
    tinv = [eye + n for n in nmat]
    if sh >= 2:
        pw = [_dot(n.astype(BF16), n.astype(BF16)) for n in nmat]
        for _ in range(sh - 2):
            x = [_dot(jnp.concatenate([t, p], axis=0).astype(BF16), p.astype(BF16))
                 for t, p in zip(tinv, pw)]
            tinv = [t + y[:STACK] for t, y in zip(tinv, x)]
            pw = [y[STACK:] for y in x]
        tinv = [t + _dot(t.astype(BF16), p.astype(BF16)) for t, p in zip(tinv, pw)]
    uw = [_dot(t.astype(BF16),
               jnp.concatenate([it["v"] * it["beta"], b * it["eg"]], axis=1).astype(BF16))
          for t, it, b in zip(tinv, items, kb)]
    return [(x[:, :GDN_DV], x[:, GDN_DV:]) for x in uw], qk


GDN_LOCAL_CHUNKS = 2


def _gdn_prompt_kernel(qkv_ref, zg_ref, gate_ref, cw_ref, gbias_ref, galog_ref, nw_ref,
                       o_ref, cst_out_ref, sst_out_ref,
                       xpad_ref, qkvc_ref, s_ref, u_ref, wq16_ref, kd16_ref, qk16_ref, egl_ref, *, Lb):
    c = CHUNK
    l = pl.program_id(1)
    nl = pl.num_programs(1)

    @pl.when(l == 0)
    def _init():
        xpad_ref[0:8, :] = jnp.zeros((8, GDN_CONV_CH), F32)
        s_ref[...] = jnp.zeros(s_ref.shape, F32)

    _conv_block(qkv_ref, xpad_ref, cw_ref, None, qkvc_ref, Lb, GDN_CONV_CH, _gdn_qk_post)

    row_i = lax.broadcasted_iota(jnp.int32, (c, c), 0)
    col_i = lax.broadcasted_iota(jnp.int32, (c, c), 1)
    tril_f = (row_i >= col_i).astype(F32)
    gbias = gbias_ref[...]
    nega = -jnp.exp(galog_ref[...])
    nw = nw_ref[...]
    n_pairs = GDN_HEADS // 2

    def local_chunk_items(ci):
        rows = slice(ci * c, (ci + 1) * c)
        graw = gate_ref[rows, :]
        sp = _softplus(graw + gbias)
        beta_all = jax.nn.sigmoid(graw)
        G = jnp.dot(tril_f, nega * sp, precision=_HIGHEST, preferred_element_type=F32)
        eG = jnp.exp(G)
        glast = G[c - 1:c, :]
        eGrev = jnp.exp(glast - G)
        egl_ref[ci] = jnp.broadcast_to(jnp.exp(glast), (8, LANES))

        def heads(off, a, b):
            return jnp.concatenate([qkvc_ref[rows, off + a * LANES:off + (a + 1) * LANES],
                                    qkvc_ref[rows, off + b * LANES:off + (b + 1) * LANES]], axis=0)

        def colstack(m, a, b):
            return jnp.concatenate([jnp.broadcast_to(m[:, a:a + 1], (c, LANES)),
                                    jnp.broadcast_to(m[:, b:b + 1], (c, LANES))], axis=0)

        items = []
        for pr in range(n_pairs):
            a, b = 2 * pr, 2 * pr + 1
            items.append(dict(
                q=heads(0, a, b), k=heads(GDN_QK, a, b), v=heads(2 * GDN_QK, a, b),
                beta=colstack(beta_all, GATE_B + a, GATE_B + b),
                g=colstack(G, GATE_A + a, GATE_A + b),
                eg=colstack(eG, GATE_A + a, GATE_A + b),
                egrev=colstack(eGrev, GATE_A + a, GATE_A + b)))
        return items

    def local_group(cis):
        items = [it for ci in cis for it in local_chunk_items(ci)]
        uw, qk = _gdn_local(items, c)
        for n, it in enumerate(items):
            idx = cis[0] * n_pairs + n
            u, w = uw[n]
            qd = it["q"] * it["eg"]
            u_ref[idx] = u
            for hh in range(2):
                hs = slice(hh * c, (hh + 1) * c)
                wq16_ref[2 * idx + hh] = jnp.concatenate([w[hs], qd[hs]], axis=0).astype(BF16)
            kd16_ref[idx] = (it["k"] * it["egrev"]).astype(BF16)
            qk16_ref[idx] = qk[n].astype(BF16)

    def recurrent(ci):
        rows = slice(ci * c, (ci + 1) * c)
        egl = egl_ref[ci][0:1]
        r = [[_dot(wq16_ref[2 * (ci * n_pairs + pr) + hh], s_ref[2 * pr + hh].astype(BF16)) for hh in range(2)]
             for pr in range(n_pairs)]
        v16 = [(u_ref[ci * n_pairs + pr] - jnp.concatenate([r[pr][0][:c], r[pr][1][:c]], axis=0)).astype(BF16)
               for pr in range(n_pairs)]
        o = [jnp.concatenate([r[pr][0][c:], r[pr][1][c:]], axis=0) + _dot(qk16_ref[ci * n_pairs + pr], v16[pr])
             for pr in range(n_pairs)]
        for pr in range(n_pairs):
            kd16 = kd16_ref[ci * n_pairs + pr]
            for hh in range(2):
                h = 2 * pr + hh
                ga = GATE_A + h
                hs = slice(hh * c, (hh + 1) * c)
                s_ref[h] = s_ref[h] * egl[:, ga:ga + 1] + _dot_tn(kd16[hs], v16[pr][hs])
        for pr in range(n_pairs):
            for hh in range(2):
                h = 2 * pr + hh
                z = zg_ref[rows, h * GDN_DV:(h + 1) * GDN_DV]
                o_ref[rows, h * GDN_DV:(h + 1) * GDN_DV] = (
                    _rms_rows(o[pr][hh * c:(hh + 1) * c], nw) * _silu(z)).astype(o_ref.dtype)

    for g0 in range(0, Lb // c, GDN_LOCAL_CHUNKS):
        cis = list(range(g0, g0 + GDN_LOCAL_CHUNKS))
        local_group(cis)
        for ci in cis:
            recurrent(ci)

    @pl.when(l == nl - 1)
    def _final():
        cst_out_ref[0] = xpad_ref[5:8, :]
        sst_out_ref[0] = s_ref[...]


def _gdn_prompt(proj, cw, gbias, galog, nw, *, B, L, Lb):
    nl = L // Lb
    n_tiles = (Lb // CHUNK) * (GDN_HEADS // 2)
    row = lambda b, l: b * nl + l
    const = lambda shape: pl.BlockSpec(shape, lambda b, l: (0,) * len(shape))
    return pl.pallas_call(
        functools.partial(_gdn_prompt_kernel, Lb=Lb),
        grid=(B, nl),
        in_specs=[
            pl.BlockSpec((Lb, GDN_CONV_CH), lambda b, l: (row(b, l), OFF_QKV // GDN_CONV_CH)),
            pl.BlockSpec((Lb, GDN_V), lambda b, l: (row(b, l), OFF_ZG // GDN_V)),
            pl.BlockSpec((Lb, LANES), lambda b, l: (row(b, l), OFF_GATE // LANES)),
            const((CONV_W, GDN_CONV_CH)), const((1, LANES)), const((1, LANES)), const((1, GDN_DV)),
        ],
        out_specs=[
            pl.BlockSpec((Lb, GDN_V), lambda b, l: (row(b, l), 0)),
            pl.BlockSpec((1, CONV_W - 1, GDN_CONV_CH), lambda b, l: (b, 0, 0)),
            pl.BlockSpec((1, GDN_HEADS, GDN_DK, GDN_DV), lambda b, l: (b, 0, 0, 0)),
        ],
        out_shape=[
            jax.ShapeDtypeStruct((B * L, GDN_V), BF16),
            jax.ShapeDtypeStruct((B, CONV_W - 1, GDN_CONV_CH), F32),
            jax.ShapeDtypeStruct((B, GDN_HEADS, GDN_DK, GDN_DV), F32),
        ],
        scratch_shapes=[
            pltpu.VMEM((Lb + 8, GDN_CONV_CH), F32),
            pltpu.VMEM((Lb, GDN_CONV_CH), F32),
            pltpu.VMEM((GDN_HEADS, GDN_DK, GDN_DV), F32),
            pltpu.VMEM((n_tiles, STACK, GDN_DV), F32),
            pltpu.VMEM((2 * n_tiles, STACK, GDN_DK), BF16),
            pltpu.VMEM((n_tiles, STACK, GDN_DK), BF16),
            pltpu.VMEM((n_tiles, STACK, STACK), BF16),
            pltpu.VMEM((Lb // CHUNK, 8, LANES), F32),
        ],
        compiler_params=pltpu.CompilerParams(
            dimension_semantics=("parallel", "arbitrary"), vmem_limit_bytes=VMEM_LIMIT),
        name="gdn_prompt",
    )(proj, proj, proj, cw, gbias, galog, nw)


def _gdn_sample_kernel(qkv_ref, zg_ref, gate_ref, cst_ref, sst_ref, cw_ref, gbias_ref, galog_ref, nw_ref,
                       o_ref, cst_out_ref, sst_out_ref, xpad_ref, qkvc_ref, *, nb, L):
    R = nb * L
    sh = L.bit_length() - 1
    for bi in range(nb):
        xp = xpad_ref.at[bi]
        xp[5:8, :] = cst_ref[bi]
        cst_out_ref[bi] = _conv_block(
            qkv_ref.at[pl.ds(bi * L, L)], xp, cw_ref, None, qkvc_ref.at[pl.ds(bi * L, L)],
            L, GDN_CONV_CH, _gdn_qk_post)

    row_i = lax.broadcasted_iota(jnp.int32, (R, R), 0)
    col_i = lax.broadcasted_iota(jnp.int32, (R, R), 1)
    tril_f = (((row_i >> sh) == (col_i >> sh)) & (row_i >= col_i)).astype(F32)
    graw = gate_ref[...]
    sp = _softplus(graw + gbias_ref[...])
    beta_all = jax.nn.sigmoid(graw)
    G = jnp.dot(tril_f, -jnp.exp(galog_ref[...]) * sp, precision=_HIGHEST, preferred_element_type=F32)
    glast = [G[bi * L + L - 1:bi * L + L, :] for bi in range(nb)]
    eG = jnp.exp(G)
    eGrev = jnp.exp(jnp.concatenate([jnp.broadcast_to(x, (L, LANES)) for x in glast], axis=0) - G)
    egl = [jnp.exp(x) for x in glast]
    nw = nw_ref[...]

    n_st = R // (2 * L)

    def tiles(ref, st, off):
        return jnp.concatenate(
            [ref[st * 2 * L:(st + 1) * 2 * L, off + h * LANES:off + (h + 1) * LANES]
             for h in range(GDN_HEADS)], axis=0)

    def colstack(m, st, off):
        return jnp.concatenate(
            [jnp.broadcast_to(m[st * 2 * L:(st + 1) * 2 * L, off + h:off + h + 1], (2 * L, LANES))
             for h in range(GDN_HEADS)], axis=0)

    items = [dict(q=tiles(qkvc_ref, st, 0), k=tiles(qkvc_ref, st, GDN_QK), v=tiles(qkvc_ref, st, 2 * GDN_QK),
                  beta=colstack(beta_all, st, GATE_B), g=colstack(G, st, GATE_A),
                  eg=colstack(eG, st, GATE_A), egrev=colstack(eGrev, st, GATE_A))
             for st in range(n_st)]
    uw, qk = _gdn_local(items, L)

    groups = [(h, bi) for h in range(GDN_HEADS) for bi in range(2)]
    r = []
    for st, it in enumerate(items):
        w = uw[st][1]
        qd = it["q"] * it["eg"]
        r.append([
            _dot(jnp.concatenate([w[gi * L:(gi + 1) * L], qd[gi * L:(gi + 1) * L]], axis=0).astype(BF16),
                 sst_ref[2 * st + bi, h].astype(BF16))
            for gi, (h, bi) in enumerate(groups)])
    v_new = [uw[st][0] - jnp.concatenate([x[:L] for x in r[st]], axis=0) for st in range(n_st)]
    o = [jnp.concatenate([x[L:] for x in r[st]], axis=0)
         + _dot(qk[st].astype(BF16), v_new[st].astype(BF16)) for st in range(n_st)]
    for st, it in enumerate(items):
        kd = it["k"] * it["egrev"]
        for gi, (h, bi) in enumerate(groups):
            b = 2 * st + bi
            ga = GATE_A + h
            rs = slice(gi * L, (gi + 1) * L)
            sst_out_ref[b, h] = (sst_ref[b, h] * egl[b][:, ga:ga + 1]
                                 + _dot_tn(kd[rs].astype(BF16), v_new[st][rs].astype(BF16)))
    for st in range(n_st):
        out = (_rms_rows(o[st], nw) * _silu(tiles(zg_ref, st, 0))).astype(o_ref.dtype)
        for h in range(GDN_HEADS):
            o_ref[st * 2 * L:(st + 1) * 2 * L, h * GDN_DV:(h + 1) * GDN_DV] = out[h * 2 * L:(h + 1) * 2 * L]


def _gdn_sample(proj, conv_state, S_state, cw, gbias, galog, nw, *, B, L, nb):
    R = nb * L
    const = lambda shape: pl.BlockSpec(shape, lambda i: (0,) * len(shape))
    return pl.pallas_call(
        functools.partial(_gdn_sample_kernel, nb=nb, L=L),
        grid=(B // nb,),
        in_specs=[
            pl.BlockSpec((R, GDN_CONV_CH), lambda i: (i, OFF_QKV // GDN_CONV_CH)),
            pl.BlockSpec((R, GDN_V), lambda i: (i, OFF_ZG // GDN_V)),
            pl.BlockSpec((R, LANES), lambda i: (i, OFF_GATE // LANES)),
            pl.BlockSpec((nb, CONV_W - 1, GDN_CONV_CH), lambda i: (i, 0, 0)),
            pl.BlockSpec((nb, GDN_HEADS, GDN_DK, GDN_DV), lambda i: (i, 0, 0, 0)),
            const((CONV_W, GDN_CONV_CH)), const((1, LANES)), const((1, LANES)), const((1, GDN_DV)),
        ],
        out_specs=[
            pl.BlockSpec((R, GDN_V), lambda i: (i, 0)),
            pl.BlockSpec((nb, CONV_W - 1, GDN_CONV_CH), lambda i: (i, 0, 0)),
            pl.BlockSpec((nb, GDN_HEADS, GDN_DK, GDN_DV), lambda i: (i, 0, 0, 0)),
        ],
        out_shape=[
            jax.ShapeDtypeStruct((B * L, GDN_V), BF16),
            jax.ShapeDtypeStruct((B, CONV_W - 1, GDN_CONV_CH), F32),
            jax.ShapeDtypeStruct((B, GDN_HEADS, GDN_DK, GDN_DV), F32),
        ],
        scratch_shapes=[
            pltpu.VMEM((nb, L + 8, GDN_CONV_CH), F32),
            pltpu.VMEM((R, GDN_CONV_CH), F32),
        ],
        compiler_params=pltpu.CompilerParams(
            dimension_semantics=("parallel",), vmem_limit_bytes=VMEM_LIMIT),
        name="gdn_sample",
    )(proj, proj, proj, conv_state, S_state, cw, gbias, galog, nw)


N_PAIRS = SSM_HEADS // 2
PAIRS_PER_GROUP = N_PAIRS // SSM_GROUPS
GROUP_W = SSM_DI // SSM_GROUPS


def _ssm_tile(graw, gbias, nega, ld_x, ld_b, ld_c, ld_z, dcols_ref, nw_ref, st_o, get_h, set_h, glen):
    c = CHUNK
    P = SSM_P
    nseq = c // glen
    sh = glen.bit_length() - 1
    ri = lax.broadcasted_iota(jnp.int32, (c, c), 0)
    ci = lax.broadcasted_iota(jnp.int32, (c, c), 1)
    tril_f = (((ri >> sh) == (ci >> sh)) & (ri >= ci)).astype(F32)
    sp = _softplus(graw + gbias)
    acum = jnp.dot(tril_f, nega * sp, precision=_HIGHEST, preferred_element_type=F32)
    lasts = [acum[s * glen + glen - 1:(s + 1) * glen, :] for s in range(nseq)]
    alast = jnp.concatenate([jnp.broadcast_to(x, (glen, LANES)) for x in lasts], axis=0)
    dtrev = sp * jnp.exp(alast - acum)
    eal = [jnp.exp(x) for x in lasts]
    lane = lax.broadcasted_iota(jnp.int32, (c, LANES), 1)
    row = lax.broadcasted_iota(jnp.int32, (c, LANES), 0)
    m = jnp.where(lane < GATE_DT2, acum, sp)
    mt = jnp.concatenate([m, m], axis=0).T
    left = lane < P
    left_row = left[0:1]
    j = jnp.where(left, lane, lane - P)
    tril2 = ((row >> sh) == (j >> sh)) & (row >= j)
    rowh = lax.broadcasted_iota(jnp.int32, (2 * P, SSM_N), 0) < P

    def expand(mat, c0):
        return jnp.where(left, jnp.broadcast_to(mat[:, c0:c0 + 1], (c, LANES)),
                         jnp.broadcast_to(mat[:, c0 + 1:c0 + 2], (c, LANES)))

    def rowsel(base, e):
        return jnp.where(left_row, mt[base + 2 * e:base + 2 * e + 1, :], mt[base + 2 * e + 1:base + 2 * e + 2, :])

    for g in range(SSM_GROUPS):
        Bg = ld_b(g)
        Cg = ld_c(g)
        Bg16 = Bg.astype(BF16)
        Cg16 = Cg.astype(BF16)
        cb2 = _dot_nt(Cg16, jnp.concatenate([Bg16, Bg16], axis=0))
        pairs = [g * PAIRS_PER_GROUP + e4 for e4 in range(PAIRS_PER_GROUP)]
        acol = [expand(acum, GATE_DT + 2 * e) for e in pairs]
        scores16 = [
            (cb2 * jnp.exp(jnp.where(tril2, a - rowsel(GATE_DT, e), -jnp.inf)) * rowsel(GATE_DT2, e)).astype(BF16)
            for a, e in zip(acol, pairs)]
        xp = [ld_x(e) for e in pairs]
        bd16 = [jnp.concatenate([jnp.where(left, x, 0.0), jnp.where(left, 0.0, x)], axis=0).astype(BF16)
                for x in xp]
        ydiag = [_dot(s, b) for s, b in zip(scores16, bd16)]
        if nseq == 1:
            yoff = [_dot_nt(Cg16, get_h(0, e).astype(BF16)) for e in pairs]
        else:
            yoff = [jnp.concatenate(
                [_dot_nt(Cg[s * glen:(s + 1) * glen].astype(BF16), get_h(s, e).astype(BF16))
                 for s in range(nseq)], axis=0) for e in pairs]
        y = [yd + yo * jnp.exp(a) + dcols_ref[:, e * 2 * P:(e + 1) * 2 * P] * x
             for yd, yo, a, e, x in zip(ydiag, yoff, acol, pairs, xp)]
        xdr = [x * expand(dtrev, GATE_DT + 2 * e) for x, e in zip(xp, pairs)]
        for e, xd in zip(pairs, xdr):
            c0 = GATE_DT + 2 * e
            for s in range(nseq):
                rs = slice(s * glen, (s + 1) * glen)
                ealcol = jnp.where(rowh, eal[s][:, c0:c0 + 1], eal[s][:, c0 + 1:c0 + 2])
                set_h(s, e, get_h(s, e) * ealcol + _dot_tn(xd[rs].astype(BF16), Bg[rs].astype(BF16)))
        yg = jnp.concatenate(y, axis=1) * _silu(ld_z(g))
        gcols = slice(g * GROUP_W, (g + 1) * GROUP_W)
        st_o(g, _rms_rows(yg, nw_ref[:, gcols]))


def _ssm_prompt_kernel(xs_ref, bc_ref, zs_ref, gate_ref, cwx_ref, cbx_ref, cwbc_ref, cbbc_ref,
                       gbias_ref, galog_ref, dcols_ref, nw_ref,
                       o_ref, cst_out_ref, hst_out_ref, xpadx_ref, xpadbc_ref, xc_ref, bcc_ref, hh_ref, *, Lb):
    c = CHUNK
    l = pl.program_id(1)
    nl = pl.num_programs(1)

    @pl.when(l == 0)
    def _init():
        hh_ref[...] = jnp.zeros(hh_ref.shape, F32)
        xpadx_ref[0:8, :] = jnp.zeros((8, SSM_DI), F32)
        xpadbc_ref[0:8, :] = jnp.zeros((8, 2 * SSM_BC), F32)

    ident = lambda s, y: y
    last3x = _conv_block(xs_ref, xpadx_ref, cwx_ref, cbx_ref, xc_ref, Lb, SSM_DI, ident)
    last3bc = _conv_block(bc_ref, xpadbc_ref, cwbc_ref, cbbc_ref, bcc_ref, Lb, 2 * SSM_BC, ident)

    @pl.when(l == nl - 1)
    def _conv_state():
        cst_out_ref[0, :, :SSM_DI] = last3x
        cst_out_ref[0, :, SSM_DI:] = last3bc

    gbias = gbias_ref[...]
    nega = -jnp.exp(galog_ref[...])

    def set_h(s, e, val):
        hh_ref[e] = val

    def body(ci, carry):
        rows = pl.ds(pl.multiple_of(ci * c, c), c)

        def st_o(g, val):
            o_ref[rows, g * GROUP_W:(g + 1) * GROUP_W] = val.astype(o_ref.dtype)

        _ssm_tile(
            gate_ref[rows, :], gbias, nega,
            lambda e: xc_ref[rows, e * LANES:(e + 1) * LANES],
            lambda g: bcc_ref[rows, g * SSM_N:(g + 1) * SSM_N],
            lambda g: bcc_ref[rows, SSM_BC + g * SSM_N:SSM_BC + (g + 1) * SSM_N],
            lambda g: zs_ref[rows, g * GROUP_W:(g + 1) * GROUP_W],
            dcols_ref, nw_ref, st_o, lambda s, e: hh_ref[e], set_h, c)
        return carry
    lax.fori_loop(0, Lb // c, body, 0)

    @pl.when(l == nl - 1)
    def _state_out():
        hst_out_ref[0] = hh_ref[...]


def _ssm_sample_kernel(xs_ref, bc_ref, zs_ref, gate_ref, cst_ref, hst_ref, cwx_ref, cbx_ref, cwbc_ref, cbbc_ref,
                       gbias_ref, galog_ref, dcols_ref, nw_ref,
                       o_ref, cst_out_ref, hst_out_ref, xpadx_ref, xpadbc_ref, xc_ref, bcc_ref, *, L):
    ident = lambda s, y: y
    for bi in range(CHUNK // L):
        rs = pl.ds(bi * L, L)
        xpx = xpadx_ref.at[bi]
        xpb = xpadbc_ref.at[bi]
        xpx[5:8, :] = cst_ref[bi, :, :SSM_DI]
        xpb[5:8, :] = cst_ref[bi, :, SSM_DI:]
        cst_out_ref[bi, :, :SSM_DI] = _conv_block(
            xs_ref.at[rs], xpx, cwx_ref, cbx_ref, xc_ref.at[rs], L, SSM_DI, ident)
        cst_out_ref[bi, :, SSM_DI:] = _conv_block(
            bc_ref.at[rs], xpb, cwbc_ref, cbbc_ref, bcc_ref.at[rs], L, 2 * SSM_BC, ident)

    def st_o(g, val):
        o_ref[:, g * GROUP_W:(g + 1) * GROUP_W] = val.astype(o_ref.dtype)

    def set_h(s, e, val):
        hst_out_ref[s, e] = val

    _ssm_tile(
        gate_ref[...], gbias_ref[...], -jnp.exp(galog_ref[...]),
        lambda e: xc_ref[:, e * LANES:(e + 1) * LANES],
        lambda g: bcc_ref[:, g * SSM_N:(g + 1) * SSM_N],
        lambda g: bcc_ref[:, SSM_BC + g * SSM_N:SSM_BC + (g + 1) * SSM_N],
        lambda g: zs_ref[:, g * GROUP_W:(g + 1) * GROUP_W],
        dcols_ref, nw_ref, st_o, lambda s, e: hst_ref[s, e], set_h, L)


def _ssm_const_specs():
    const = lambda shape: pl.BlockSpec(shape, lambda *idx: (0,) * len(shape))
    return [
        const((CONV_W, SSM_DI)), const((1, SSM_DI)), const((CONV_W, 2 * SSM_BC)), const((1, 2 * SSM_BC)),
        const((1, LANES)), const((1, LANES)), const((1, SSM_DI)), const((1, SSM_DI)),
    ]


def _ssm_prompt(proj, cwx, cbx, cwbc, cbbc, gbias, galog, dcols, nw, *, B, L, Lb):
    nl = L // Lb
    row = lambda b, l: b * nl + l
    return pl.pallas_call(
        functools.partial(_ssm_prompt_kernel, Lb=Lb),
        grid=(B, nl),
        in_specs=[
            pl.BlockSpec((Lb, SSM_DI), lambda b, l: (row(b, l), OFF_XS // SSM_DI)),
            pl.BlockSpec((Lb, 2 * SSM_BC), lambda b, l: (row(b, l), OFF_BC // (2 * SSM_BC))),
            pl.BlockSpec((Lb, SSM_DI), lambda b, l: (row(b, l), OFF_ZS // SSM_DI)),
            pl.BlockSpec((Lb, LANES), lambda b, l: (row(b, l), OFF_GATE // LANES)),
        ] + _ssm_const_specs(),
        out_specs=[
            pl.BlockSpec((Lb, SSM_DI), lambda b, l: (row(b, l), 0)),
            pl.BlockSpec((1, CONV_W - 1, SSM_CONV_CH), lambda b, l: (b, 0, 0)),
            pl.BlockSpec((1, N_PAIRS, 2 * SSM_P, SSM_N), lambda b, l: (b, 0, 0, 0)),
        ],
        out_shape=[
            jax.ShapeDtypeStruct((B * L, SSM_DI), BF16),
            jax.ShapeDtypeStruct((B, CONV_W - 1, SSM_CONV_CH), F32),
            jax.ShapeDtypeStruct((B, N_PAIRS, 2 * SSM_P, SSM_N), F32),
        ],
        scratch_shapes=[
            pltpu.VMEM((Lb + 8, SSM_DI), F32),
            pltpu.VMEM((Lb + 8, 2 * SSM_BC), F32),
            pltpu.VMEM((Lb, SSM_DI), F32),
            pltpu.VMEM((Lb, 2 * SSM_BC), F32),
            pltpu.VMEM((N_PAIRS, 2 * SSM_P, SSM_N), F32),
        ],
        compiler_params=pltpu.CompilerParams(
            dimension_semantics=("parallel", "arbitrary"), vmem_limit_bytes=VMEM_LIMIT),
        name="ssm_prompt",
    )(proj, proj, proj, proj, cwx, cbx, cwbc, cbbc, gbias, galog, dcols, nw)


def _ssm_sample(proj, conv_state, h_pairs, cwx, cbx, cwbc, cbbc, gbias, galog, dcols, nw, *, B, L):
    nb = CHUNK // L
    return pl.pallas_call(
        functools.partial(_ssm_sample_kernel, L=L),
        grid=(B // nb,),
        in_specs=[
            pl.BlockSpec((CHUNK, SSM_DI), lambda i: (i, OFF_XS // SSM_DI)),
            pl.BlockSpec((CHUNK, 2 * SSM_BC), lambda i: (i, OFF_BC // (2 * SSM_BC))),
            pl.BlockSpec((CHUNK, SSM_DI), lambda i: (i, OFF_ZS // SSM_DI)),
            pl.BlockSpec((CHUNK, LANES), lambda i: (i, OFF_GATE // LANES)),
            pl.BlockSpec((nb, CONV_W - 1, SSM_CONV_CH), lambda i: (i, 0, 0)),
            pl.BlockSpec((nb, N_PAIRS, 2 * SSM_P, SSM_N), lambda i: (i, 0, 0, 0)),
        ] + _ssm_const_specs(),
        out_specs=[
            pl.BlockSpec((CHUNK, SSM_DI), lambda i: (i, 0)),
            pl.BlockSpec((nb, CONV_W - 1, SSM_CONV_CH), lambda i: (i, 0, 0)),
            pl.BlockSpec((nb, N_PAIRS, 2 * SSM_P, SSM_N), lambda i: (i, 0, 0, 0)),
        ],
        out_shape=[
            jax.ShapeDtypeStruct((B * L, SSM_DI), BF16),
            jax.ShapeDtypeStruct((B, CONV_W - 1, SSM_CONV_CH), F32),
            jax.ShapeDtypeStruct((B, N_PAIRS, 2 * SSM_P, SSM_N), F32),
        ],
        scratch_shapes=[
            pltpu.VMEM((nb, L + 8, SSM_DI), F32),
            pltpu.VMEM((nb, L + 8, 2 * SSM_BC), F32),
            pltpu.VMEM((CHUNK, SSM_DI), F32),
            pltpu.VMEM((CHUNK, 2 * SSM_BC), F32),
        ],
        compiler_params=pltpu.CompilerParams(
            dimension_semantics=("parallel",), vmem_limit_bytes=VMEM_LIMIT),
        name="ssm_sample",
    )(proj, proj, proj, proj, conv_state, h_pairs, cwx, cbx, cwbc, cbbc, gbias, galog, dcols, nw)


def _outproj_kernel(x_ref, mg_ref, ms_ref, w_ref, o_ref):
    acc = _dot(mg_ref[...].astype(BF16), w_ref[:GDN_V, :])
    acc = acc + _dot(ms_ref[...].astype(BF16), w_ref[GDN_V:, :])
    o_ref[...] = x_ref[...] + acc


def _out_proj(x2d, mix_g, mix_s, w_out16, *, tm):
    T = x2d.shape[0]
    return pl.pallas_call(
        _outproj_kernel,
        grid=(T // tm,),
        in_specs=[
            pl.BlockSpec((tm, D_MODEL), lambda i: (i, 0)),
            pl.BlockSpec((tm, GDN_V), lambda i: (i, 0)),
            pl.BlockSpec((tm, SSM_DI), lambda i: (i, 0)),
            pl.BlockSpec((D_MODEL, D_MODEL), lambda i: (0, 0)),
        ],
        out_specs=pl.BlockSpec((tm, D_MODEL), lambda i: (i, 0)),
        out_shape=jax.ShapeDtypeStruct((T, D_MODEL), F32),
        compiler_params=pltpu.CompilerParams(
            dimension_semantics=("parallel",), vmem_limit_bytes=VMEM_LIMIT),
        name="out_proj",
    )(x2d, mix_g, mix_s, w_out16)


def _ffn_kernel(x_ref, nw_ref, wg_ref, wu_ref, wd_ref, fnw_ref, o_ref, h_ref, acc_ref):
    f = pl.program_id(1)

    @pl.when(f == 0)
    def _():
        h_ref[...] = _rms_rows(x_ref[...], nw_ref[...]).astype(BF16)
        acc_ref[...] = jnp.zeros(acc_ref.shape, F32)

    h = h_ref[...]
    a = (_silu(_dot(h, wg_ref[...])) * _dot(h, wu_ref[...])).astype(BF16)
    acc_ref[...] += _dot(a, wd_ref[...])

    @pl.when(f == pl.num_programs(1) - 1)
    def _():
        o_ref[...] = _rms_rows(x_ref[...] + acc_ref[...], fnw_ref[...])


def _ffn(x2d, norm_w, wg16, wu16, wd16, final_w, *, tm, tf):
    T = x2d.shape[0]
    return pl.pallas_call(
        _ffn_kernel,
        grid=(T // tm, D_FF // tf),
        in_specs=[
            pl.BlockSpec((tm, D_MODEL), lambda i, f: (i, 0)),
            pl.BlockSpec((1, D_MODEL), lambda i, f: (0, 0)),
            pl.BlockSpec((D_MODEL, tf), lambda i, f: (0, f)),
            pl.BlockSpec((D_MODEL, tf), lambda i, f: (0, f)),
            pl.BlockSpec((tf, D_MODEL), lambda i, f: (f, 0)),
            pl.BlockSpec((1, D_MODEL), lambda i, f: (0, 0)),
        ],
        out_specs=pl.BlockSpec((tm, D_MODEL), lambda i, f: (i, 0)),
        out_shape=jax.ShapeDtypeStruct((T, D_MODEL), F32),
        scratch_shapes=[pltpu.VMEM((tm, D_MODEL), BF16), pltpu.VMEM((tm, D_MODEL), F32)],
        compiler_params=pltpu.CompilerParams(
            dimension_semantics=("parallel", "arbitrary"), vmem_limit_bytes=VMEM_LIMIT),
        name="ffn",
    )(x2d, norm_w, wg16, wu16, wd16, final_w)


def _trunk(x, states, p):
    B, L, _ = x.shape
    x2d = x.reshape(B * L, D_MODEL)
    proj = _in_proj(x2d, p["attn_norm_w"], p["w_in_r"], tm=1024, tn=768)
    gdn_w = (p["gdn_conv_w"], p["gbias"], p["galog"], p["gdn_norm_w"])
    ssm_w = (p["cwx"], p["cbx"], p["cwbc"], p["cbbc"], p["gbias"], p["galog"], p["dcols"], p["ssm_norm_w"])
    pair_shape = (B, N_PAIRS, 2 * SSM_P, SSM_N)
    if states is None:
        mix_g, gconv_new, gS_new = _gdn_prompt(proj, *gdn_w, B=B, L=L, Lb=256)
        mix_s, sconv_new, sh_new = _ssm_prompt(proj, *ssm_w, B=B, L=L, Lb=256)
    else:
        gconv, gS, sconv, sh = states
        mix_g, gconv_new, gS_new = _gdn_sample(proj, gconv, gS, *gdn_w, B=B, L=L, nb=4)
        mix_s, sconv_new, sh_new = _ssm_sample(proj, sconv, sh.reshape(pair_shape), *ssm_w, B=B, L=L)
    sh_new = sh_new.reshape(B, SSM_HEADS, SSM_P, SSM_N)
    x1 = _out_proj(x2d, mix_g, mix_s, p["w_out16"], tm=512)
    y = _ffn(x1, p["ffn_norm_w"], p["wg16"], p["wu16"], p["wd16"], p["final_norm_w"], tm=512, tf=512)
    return y.reshape(B, L, D_MODEL), (gconv_new[None], gS_new[None], sconv_new[None], sh_new[None])


def kernel(x_prompt, x_sample, state_gdn_conv, state_gdn, state_ssm_conv, state_ssm,
           attn_norm_w, w_in, gdn_conv_w, gdn_A_log, gdn_dt_bias, gdn_norm_w,
           ssm_conv_w, ssm_conv_b, ssm_A_log, ssm_dt_bias, ssm_D, ssm_norm_w,
           w_out, ffn_norm_w, w_gate, w_up, w_down, final_norm_w):
    assert w_in.shape[0] == 1, "single-layer trunk"
    assert x_prompt.shape[1] % 256 == 0 and x_sample.shape[1] == 8 and x_sample.shape[0] % 8 == 0
    w = w_in[0]
    o_qkv, o_zg, o_b, o_a, o_zs, o_xbc, o_dt = 0, 3072, 4096, 4104, 4112, 5136, 6672
    pad = N_PROJ - (OFF_GATE + GATE_DT2 + SSM_HEADS)
    w_in_r = jnp.concatenate([
        w[:, o_qkv:o_zg], w[:, o_zg:o_b], w[:, o_zs:o_xbc], w[:, o_xbc:o_dt],
        w[:, o_b:o_a], w[:, o_a:o_zs], w[:, o_dt:], w[:, o_dt:], jnp.zeros((D_MODEL, pad), w.dtype),
    ], axis=1).astype(BF16)
    zeros8 = jnp.zeros((GDN_HEADS,), F32)
    tail = jnp.zeros((LANES - GATE_DT2 - SSM_HEADS,), F32)
    gbias = jnp.concatenate([zeros8, gdn_dt_bias[0], ssm_dt_bias[0], ssm_dt_bias[0], tail])[None]
    galog = jnp.concatenate([zeros8, gdn_A_log[0], ssm_A_log[0], ssm_A_log[0], tail])[None]
    p = dict(
        attn_norm_w=attn_norm_w, w_in_r=w_in_r, gdn_conv_w=gdn_conv_w[0], gbias=gbias, galog=galog,
        gdn_norm_w=gdn_norm_w,
        cwx=ssm_conv_w[0][:, :SSM_DI], cbx=ssm_conv_b[:, :SSM_DI],
        cwbc=ssm_conv_w[0][:, SSM_DI:], cbbc=ssm_conv_b[:, SSM_DI:],
        dcols=jnp.repeat(ssm_D[0], SSM_P)[None], ssm_norm_w=ssm_norm_w,
        w_out16=w_out[0].astype(BF16), ffn_norm_w=ffn_norm_w,
        wg16=w_gate[0].astype(BF16), wu16=w_up[0].astype(BF16), wd16=w_down[0].astype(BF16),
        final_norm_w=final_norm_w[None],
    )
    y_p, st_p = _trunk(x_prompt, None, p)
    y_s, st_s = _trunk(x_sample, (state_gdn_conv[0], state_gdn[0], state_ssm_conv[0], state_ssm[0]), p)
    return (y_p, y_s, st_p[0], st_p[1], st_p[2], st_p[3], st_s[0], st_s[1], st_s[2], st_s[3])
```

```python
import functools

import jax
import jax.numpy as jnp
from jax import lax
from jax.experimental import pallas as pl
from jax.experimental.pallas import tpu as pltpu

F32 = jnp.float32
BF16 = jnp.bfloat16

D_MODEL = 2048
GDN_HEADS = 8
GDN_DK = 128
GDN_DV = 128
GDN_QK = GDN_HEADS * GDN_DK
GDN_V = GDN_HEADS * GDN_DV
GDN_CONV_CH = 2 * GDN_QK + GDN_V
SSM_P = 64
SSM_N = 128
SSM_GROUPS = 2
SSM_DI = 1024
SSM_HEADS = SSM_DI // SSM_P
SSM_BC = SSM_GROUPS * SSM_N
SSM_CONV_CH = SSM_DI + 2 * SSM_BC
CONV_W = 4
CHUNK = 64
D_FF = 5632
EPS = 1e-6

OFF_QKV = 0
OFF_ZG = OFF_QKV + GDN_CONV_CH
OFF_ZS = OFF_ZG + GDN_V
OFF_XS = OFF_ZS + SSM_DI
OFF_BC = OFF_XS + SSM_DI
OFF_GATE = OFF_BC + 2 * SSM_BC
LANES = 128
GATE_B = 0
GATE_A = GATE_B + GDN_HEADS
GATE_DT = GATE_A + GDN_HEADS
GATE_DT2 = GATE_DT + SSM_HEADS
N_PROJ = 6912

VMEM_LIMIT = 52 * 1024 * 1024

_HIGHEST = lax.Precision.HIGHEST


def _silu(x):
    return x * jax.nn.sigmoid(x)


def _softplus(x):
    return jnp.maximum(x, 0.0) + jnp.log1p(jnp.exp(-jnp.abs(x)))


def _dot(a, b):
    return jnp.dot(a, b, preferred_element_type=F32)


def _dot_nt(a, b):
    return lax.dot_general(a, b, (((1,), (1,)), ((), ())), preferred_element_type=F32)


def _dot_tn(a, b):
    return lax.dot_general(a, b, (((0,), (0,)), ((), ())), preferred_element_type=F32)


def _transpose_rows(a):
    r = a.shape[0]
    if r < LANES:
        a = jnp.concatenate([a, jnp.zeros((LANES - r, LANES), a.dtype)], axis=0)
    return a.T


def _rms_rows(x, w):
    return x * lax.rsqrt(jnp.mean(x * x, axis=-1, keepdims=True) + EPS) * w


W_B = OFF_ZS
W_ZS = W_B + 2 * GDN_HEADS
W_DT = W_ZS + SSM_DI + SSM_CONV_CH
D_IN_PROJ = W_DT + SSM_HEADS


def _wprep_kernel(w_ref, o_ref):
    rows = o_ref.shape[0]
    o_ref[:, :W_B] = w_ref[0, :, :W_B].astype(BF16)
    o_ref[:, W_B:OFF_GATE] = w_ref[0, :, W_ZS:W_DT].astype(BF16)
    lane = lax.broadcasted_iota(jnp.int32, (rows, LANES), 1)
    head = w_ref[0, :, W_B:W_B + LANES]
    tail = w_ref[0, :, D_IN_PROJ - LANES:D_IN_PROJ]
    tail_dt = LANES - SSM_HEADS
    gate = jnp.where(lane < GATE_DT, head,
                     jnp.where(lane < GATE_DT2, pltpu.roll(tail, (GATE_DT - tail_dt) % LANES, 1),
                               jnp.where(lane < GATE_DT2 + SSM_HEADS,
                                         pltpu.roll(tail, (GATE_DT2 - tail_dt) % LANES, 1), 0.0)))
    o_ref[:, OFF_GATE:OFF_GATE + LANES] = gate.astype(BF16)
    o_ref[:, OFF_GATE + LANES:] = jnp.zeros((rows, N_PROJ - OFF_GATE - LANES), BF16)


def _w_in_prep(w_in, *, tk):
    return pl.pallas_call(
        _wprep_kernel,
        grid=(D_MODEL // tk,),
        in_specs=[pl.BlockSpec((1, tk, D_IN_PROJ), lambda i: (0, i, 0))],
        out_specs=pl.BlockSpec((tk, N_PROJ), lambda i: (i, 0)),
        out_shape=jax.ShapeDtypeStruct((D_MODEL, N_PROJ), BF16),
        compiler_params=pltpu.CompilerParams(
            dimension_semantics=("parallel",), vmem_limit_bytes=VMEM_LIMIT),
        name="w_in_prep",
    )(w_in)


def _inproj_kernel(x_ref, nw_ref, w_ref, o_ref, h_ref):
    @pl.when(pl.program_id(1) == 0)
    def _():
        h_ref[...] = _rms_rows(x_ref[...], nw_ref[...]).astype(BF16)

    o_ref[...] = _dot(h_ref[...], w_ref[...])


def _in_proj(x2d, norm_w, w_in_r, *, tm, tn):
    T = x2d.shape[0]
    return pl.pallas_call(
        _inproj_kernel,
        grid=(T // tm, N_PROJ // tn),
        in_specs=[
            pl.BlockSpec((tm, D_MODEL), lambda i, j: (i, 0)),
            pl.BlockSpec((1, D_MODEL), lambda i, j: (0, 0)),
            pl.BlockSpec((D_MODEL, tn), lambda i, j: (0, j)),
        ],
        out_specs=pl.BlockSpec((tm, tn), lambda i, j: (i, j)),
        out_shape=jax.ShapeDtypeStruct((T, N_PROJ), F32),
        scratch_shapes=[pltpu.VMEM((tm, D_MODEL), BF16)],
        compiler_params=pltpu.CompilerParams(
            dimension_semantics=("parallel", "arbitrary"), vmem_limit_bytes=VMEM_LIMIT),
        name="in_proj",
    )(x2d, norm_w, w_in_r)


def _conv_block(x_ref, xpad_ref, cw_ref, cb_ref, dst_ref, Lb, C, post):
    xpad_ref[8:8 + Lb, :] = x_ref[...]
    rs = min(Lb, CHUNK)
    for sb in range(Lb // rs):
        r = sb * rs
        for s in range(C // LANES):
            cols = slice(s * LANES, (s + 1) * LANES)
            acc = xpad_ref[8 + r:8 + r + rs, cols] * cw_ref[3:4, cols]
            for i in range(CONV_W - 1):
                acc = acc + xpad_ref[5 + i + r:5 + i + r + rs, cols] * cw_ref[i:i + 1, cols]
            if cb_ref is not None:
                acc = acc + cb_ref[:, cols]
            dst_ref[r:r + rs, cols] = post(s, _silu(acc))
    last3 = xpad_ref[Lb + 5:Lb + 8, :]
    xpad_ref[5:8, :] = last3
    return last3


STACK = 128


def _gdn_qk_post(s, y):
    if s < 2 * GDN_HEADS:
        y = y * lax.rsqrt(jnp.sum(y * y, axis=-1, keepdims=True) + EPS)
        if s < GDN_HEADS:
            y = y * (GDN_DK ** -0.5)
    return y


def _gdn_local(items, glen):
    sh = glen.bit_length() - 1
    row = lax.broadcasted_iota(jnp.int32, (STACK, STACK), 0)
    col = lax.broadcasted_iota(jnp.int32, (STACK, STACK), 1)
    same = (row >> sh) == (col >> sh)
    incl = same & (row >= col)
    strict = same & (row > col)
    eye = (row == col).astype(F32)

    decay = [jnp.exp(jnp.where(incl, it["g"] - it["g"].T, -jnp.inf)) for it in items]
    kb = [it["k"] * it["beta"] for it in items]
    qkk = [_dot_nt(jnp.concatenate([it["q"], b], axis=0).astype(BF16), it["k"].astype(BF16))
           for it, b in zip(items, kb)]
    qk = [x[:STACK] * d for x, d in zip(qkk, decay)]
    nmat = [jnp.where(strict, -(x[STACK:] * d), 0.0) for x, d in zip(qkk, decay)]
    tinv = [eye + n for n in nmat]
    if sh >= 2:
        pw = [_dot(n.astype(BF16), n.astype(BF16)) for n in nmat]
        for _ in range(sh - 2):
            x = [_dot(jnp.concatenate([t, p], axis=0).astype(BF16), p.astype(BF16))
                 for t, p in zip(tinv, pw)]
            tinv = [t + y[:STACK] for t, y in zip(tinv, x)]
            pw = [y[STACK:] for y in x]
        tinv = [t + _dot(t.astype(BF16), p.astype(BF16)) for t, p in zip(tinv, pw)]
    uw = [_dot(t.astype(BF16),
               jnp.concatenate([it["v"] * it["beta"], b * it["eg"]], axis=1).astype(BF16))
          for t, it, b in zip(tinv, items, kb)]
    return [(x[:, :GDN_DV], x[:, GDN_DV:]) for x in uw], qk


GDN_LOCAL_CHUNKS = 2


def _gdn_prompt_kernel(qkv_ref, zg_ref, gate_ref, cw_ref, gbias_ref, galog_ref, nw_ref,
                       o_ref, cst_out_ref, sst_out_ref,
                       xpad_ref, qkvc_ref, s_ref, u_ref, wq16_ref, kd16_ref, qk16_ref, egl_ref, *, Lb):
    c = CHUNK
    l = pl.program_id(1)
    nl = pl.num_programs(1)

    @pl.when(l == 0)
    def _init():
        xpad_ref[0:8, :] = jnp.zeros((8, GDN_CONV_CH), F32)
        s_ref[...] = jnp.zeros(s_ref.shape, F32)

    _conv_block(qkv_ref, xpad_ref, cw_ref, None, qkvc_ref, Lb, GDN_CONV_CH, _gdn_qk_post)

    row_i = lax.broadcasted_iota(jnp.int32, (c, c), 0)
    col_i = lax.broadcasted_iota(jnp.int32, (c, c), 1)
    tril_f = (row_i >= col_i).astype(F32)
    gbias = gbias_ref[...]
    nega = -jnp.exp(galog_ref[...])
    nw = nw_ref[...]
    n_pairs = GDN_HEADS // 2

    def local_chunk_items(ci):
        rows = slice(ci * c, (ci + 1) * c)
        graw = gate_ref[rows, :]
        sp = _softplus(graw + gbias)
        beta_all = jax.nn.sigmoid(graw)
        G = jnp.dot(tril_f, nega * sp, precision=_HIGHEST, preferred_element_type=F32)
        eG = jnp.exp(G)
        glast = G[c - 1:c, :]
        eGrev = jnp.exp(glast - G)
        egl_ref[ci] = jnp.broadcast_to(jnp.exp(glast), (8, LANES))

        def heads(off, a, b):
            return jnp.concatenate([qkvc_ref[rows, off + a * LANES:off + (a + 1) * LANES],
                                    qkvc_ref[rows, off + b * LANES:off + (b + 1) * LANES]], axis=0)

        def colstack(m, a, b):
            return jnp.concatenate([jnp.broadcast_to(m[:, a:a + 1], (c, LANES)),
                                    jnp.broadcast_to(m[:, b:b + 1], (c, LANES))], axis=0)

        items = []
        for pr in range(n_pairs):
            a, b = 2 * pr, 2 * pr + 1
            items.append(dict(
                q=heads(0, a, b), k=heads(GDN_QK, a, b), v=heads(2 * GDN_QK, a, b),
                beta=colstack(beta_all, GATE_B + a, GATE_B + b),
                g=colstack(G, GATE_A + a, GATE_A + b),
                eg=colstack(eG, GATE_A + a, GATE_A + b),
                egrev=colstack(eGrev, GATE_A + a, GATE_A + b)))
        return items

    def local_group(cis):
        items = [it for ci in cis for it in local_chunk_items(ci)]
        uw, qk = _gdn_local(items, c)
        for n, it in enumerate(items):
            idx = cis[0] * n_pairs + n
            u, w = uw[n]
            qd = it["q"] * it["eg"]
            u_ref[idx] = u
            for hh in range(2):
                hs = slice(hh * c, (hh + 1) * c)
                wq16_ref[2 * idx + hh] = jnp.concatenate([w[hs], qd[hs]], axis=0).astype(BF16)
            kd16_ref[idx] = (it["k"] * it["egrev"]).astype(BF16)
            qk16_ref[idx] = qk[n].astype(BF16)

    def recurrent(ci):
        rows = slice(ci * c, (ci + 1) * c)
        egl = egl_ref[ci][0:1]
        r = [[_dot(wq16_ref[2 * (ci * n_pairs + pr) + hh], s_ref[2 * pr + hh].astype(BF16)) for hh in range(2)]
             for pr in range(n_pairs)]
        v16 = [(u_ref[ci * n_pairs + pr] - jnp.concatenate([r[pr][0][:c], r[pr][1][:c]], axis=0)).astype(BF16)
               for pr in range(n_pairs)]
        o = [jnp.concatenate([r[pr][0][c:], r[pr][1][c:]], axis=0) + _dot(qk16_ref[ci * n_pairs + pr], v16[pr])
             for pr in range(n_pairs)]
        for pr in range(n_pairs):
            kd16 = kd16_ref[ci * n_pairs + pr]
            for hh in range(2):
                h = 2 * pr + hh
                ga = GATE_A + h
                hs = slice(hh * c, (hh + 1) * c)
                s_ref[h] = s_ref[h] * egl[:, ga:ga + 1] + _dot_tn(kd16[hs], v16[pr][hs])
        for pr in range(n_pairs):
            for hh in range(2):
                h = 2 * pr + hh
                z = zg_ref[rows, h * GDN_DV:(h + 1) * GDN_DV]
                o_ref[rows, h * GDN_DV:(h + 1) * GDN_DV] = (
                    _rms_rows(o[pr][hh * c:(hh + 1) * c], nw) * _silu(z)).astype(o_ref.dtype)

    for g0 in range(0, Lb // c, GDN_LOCAL_CHUNKS):
        cis = list(range(g0, g0 + GDN_LOCAL_CHUNKS))
        local_group(cis)
        for ci in cis:
            recurrent(ci)

    @pl.when(l == nl - 1)
    def _final():
        cst_out_ref[0] = xpad_ref[5:8, :]
        sst_out_ref[0] = s_ref[...]


def _gdn_prompt(proj, cw, gbias, galog, nw, *, B, L, Lb):
    nl = L // Lb
    n_tiles = (Lb // CHUNK) * (GDN_HEADS // 2)
    row = lambda b, l: b * nl + l
    const = lambda shape: pl.BlockSpec(shape, lambda b, l: (0,) * len(shape))
    return pl.pallas_call(
        functools.partial(_gdn_prompt_kernel, Lb=Lb),
        grid=(B, nl),
        in_specs=[
            pl.BlockSpec((Lb, GDN_CONV_CH), lambda b, l: (row(b, l), OFF_QKV // GDN_CONV_CH)),
            pl.BlockSpec((Lb, GDN_V), lambda b, l: (row(b, l), OFF_ZG // GDN_V)),
            pl.BlockSpec((Lb, LANES), lambda b, l: (row(b, l), OFF_GATE // LANES)),
            const((CONV_W, GDN_CONV_CH)), const((1, LANES)), const((1, LANES)), const((1, GDN_DV)),
        ],
        out_specs=[
            pl.BlockSpec((Lb, GDN_V), lambda b, l: (row(b, l), 0)),
            pl.BlockSpec((1, CONV_W - 1, GDN_CONV_CH), lambda b, l: (b, 0, 0)),
            pl.BlockSpec((1, GDN_HEADS, GDN_DK, GDN_DV), lambda b, l: (b, 0, 0, 0)),
        ],
        out_shape=[
            jax.ShapeDtypeStruct((B * L, GDN_V), BF16),
            jax.ShapeDtypeStruct((B, CONV_W - 1, GDN_CONV_CH), F32),
            jax.ShapeDtypeStruct((B, GDN_HEADS, GDN_DK, GDN_DV), F32),
        ],
        scratch_shapes=[
            pltpu.VMEM((Lb + 8, GDN_CONV_CH), F32),
            pltpu.VMEM((Lb, GDN_CONV_CH), F32),
            pltpu.VMEM((GDN_HEADS, GDN_DK, GDN_DV), F32),
            pltpu.VMEM((n_tiles, STACK, GDN_DV), F32),
            pltpu.VMEM((2 * n_tiles, STACK, GDN_DK), BF16),
            pltpu.VMEM((n_tiles, STACK, GDN_DK), BF16),
            pltpu.VMEM((n_tiles, STACK, STACK), BF16),
            pltpu.VMEM((Lb // CHUNK, 8, LANES), F32),
        ],
        compiler_params=pltpu.CompilerParams(
            dimension_semantics=("parallel", "arbitrary"), vmem_limit_bytes=VMEM_LIMIT),
        name="gdn_prompt",
    )(proj, proj, proj, cw, gbias, galog, nw)


def _gdn_sample_kernel(qkv_ref, zg_ref, gate_ref, cst_ref, sst_ref, cw_ref, gbias_ref, galog_ref, nw_ref,
                       o_ref, cst_out_ref, sst_out_ref, xpad_ref, qkvc_ref, *, nb, L):
    R = nb * L
    sh = L.bit_length() - 1
    for bi in range(nb):
        xp = xpad_ref.at[bi]
        xp[5:8, :] = cst_ref[bi]
        cst_out_ref[bi] = _conv_block(
            qkv_ref.at[pl.ds(bi * L, L)], xp, cw_ref, None, qkvc_ref.at[pl.ds(bi * L, L)],
            L, GDN_CONV_CH, _gdn_qk_post)

    row_i = lax.broadcasted_iota(jnp.int32, (R, R), 0)
    col_i = lax.broadcasted_iota(jnp.int32, (R, R), 1)
    tril_f = (((row_i >> sh) == (col_i >> sh)) & (row_i >= col_i)).astype(F32)
    graw = gate_ref[...]
    sp = _softplus(graw + gbias_ref[...])
    beta_all = jax.nn.sigmoid(graw)
    G = jnp.dot(tril_f, -jnp.exp(galog_ref[...]) * sp, precision=_HIGHEST, preferred_element_type=F32)
    glast = [G[bi * L + L - 1:bi * L + L, :] for bi in range(nb)]
    eG = jnp.exp(G)
    eGrev = jnp.exp(jnp.concatenate([jnp.broadcast_to(x, (L, LANES)) for x in glast], axis=0) - G)
    egl = [jnp.exp(x) for x in glast]
    nw = nw_ref[...]

    n_st = R // (2 * L)

    def tiles(ref, st, off):
        return jnp.concatenate(
            [ref[st * 2 * L:(st + 1) * 2 * L, off + h * LANES:off + (h + 1) * LANES]
             for h in range(GDN_HEADS)], axis=0)

    def colstack(m, st, off):
        return jnp.concatenate(
            [jnp.broadcast_to(m[st * 2 * L:(st + 1) * 2 * L, off + h:off + h + 1], (2 * L, LANES))
             for h in range(GDN_HEADS)], axis=0)

    items = [dict(q=tiles(qkvc_ref, st, 0), k=tiles(qkvc_ref, st, GDN_QK), v=tiles(qkvc_ref, st, 2 * GDN_QK),
                  beta=colstack(beta_all, st, GATE_B), g=colstack(G, st, GATE_A),
                  eg=colstack(eG, st, GATE_A), egrev=colstack(eGrev, st, GATE_A))
             for st in range(n_st)]
    uw, qk = _gdn_local(items, L)

    groups = [(h, bi) for h in range(GDN_HEADS) for bi in range(2)]
    r = []
    for st, it in enumerate(items):
        w = uw[st][1]
        qd = it["q"] * it["eg"]
        r.append([
            _dot(jnp.concatenate([w[gi * L:(gi + 1) * L], qd[gi * L:(gi + 1) * L]], axis=0).astype(BF16),
                 sst_ref[2 * st + bi, h].astype(BF16))
            for gi, (h, bi) in enumerate(groups)])
    v_new = [uw[st][0] - jnp.concatenate([x[:L] for x in r[st]], axis=0) for st in range(n_st)]
    o = [jnp.concatenate([x[L:] for x in r[st]], axis=0)
         + _dot(qk[st].astype(BF16), v_new[st].astype(BF16)) for st in range(n_st)]
    for st, it in enumerate(items):
        kd = it["k"] * it["egrev"]
        for gi, (h, bi) in enumerate(groups):
            b = 2 * st + bi
            ga = GATE_A + h
            rs = slice(gi * L, (gi + 1) * L)
            sst_out_ref[b, h] = (sst_ref[b, h] * egl[b][:, ga:ga + 1]
                                 + _dot_tn(kd[rs].astype(BF16), v_new[st][rs].astype(BF16)))
    for st in range(n_st):
        out = (_rms_rows(o[st], nw) * _silu(tiles(zg_ref, st, 0))).astype(o_ref.dtype)
        for h in range(GDN_HEADS):
            o_ref[st * 2 * L:(st + 1) * 2 * L, h * GDN_DV:(h + 1) * GDN_DV] = out[h * 2 * L:(h + 1) * 2 * L]


def _gdn_sample(proj, conv_state, S_state, cw, gbias, galog, nw, *, B, L, nb):
    R = nb * L
    const = lambda shape: pl.BlockSpec(shape, lambda i: (0,) * len(shape))
    return pl.pallas_call(
        functools.partial(_gdn_sample_kernel, nb=nb, L=L),
        grid=(B // nb,),
        in_specs=[
            pl.BlockSpec((R, GDN_CONV_CH), lambda i: (i, OFF_QKV // GDN_CONV_CH)),
            pl.BlockSpec((R, GDN_V), lambda i: (i, OFF_ZG // GDN_V)),
            pl.BlockSpec((R, LANES), lambda i: (i, OFF_GATE // LANES)),
            pl.BlockSpec((nb, CONV_W - 1, GDN_CONV_CH), lambda i: (i, 0, 0)),
            pl.BlockSpec((nb, GDN_HEADS, GDN_DK, GDN_DV), lambda i: (i, 0, 0, 0)),
            const((CONV_W, GDN_CONV_CH)), const((1, LANES)), const((1, LANES)), const((1, GDN_DV)),
        ],
        out_specs=[
            pl.BlockSpec((R, GDN_V), lambda i: (i, 0)),
            pl.BlockSpec((nb, CONV_W - 1, GDN_CONV_CH), lambda i: (i, 0, 0)),
            pl.BlockSpec((nb, GDN_HEADS, GDN_DK, GDN_DV), lambda i: (i, 0, 0, 0)),
        ],
        out_shape=[
            jax.ShapeDtypeStruct((B * L, GDN_V), BF16),
            jax.ShapeDtypeStruct((B, CONV_W - 1, GDN_CONV_CH), F32),
            jax.ShapeDtypeStruct((B, GDN_HEADS, GDN_DK, GDN_DV), F32),
        ],
        scratch_shapes=[
            pltpu.VMEM((nb, L + 8, GDN_CONV_CH), F32),
            pltpu.VMEM((R, GDN_CONV_CH), F32),
        ],
        compiler_params=pltpu.CompilerParams(
            dimension_semantics=("parallel",), vmem_limit_bytes=VMEM_LIMIT),
        name="gdn_sample",
    )(proj, proj, proj, conv_state, S_state, cw, gbias, galog, nw)


N_PAIRS = SSM_HEADS // 2
PAIRS_PER_GROUP = N_PAIRS // SSM_GROUPS
GROUP_W = SSM_DI // SSM_GROUPS


def _ssm_tile(graw, gbias, nega, ld_x, ld_b, ld_c, ld_z, dcols_ref, nw_ref, st_o, get_h, set_h, glen):
    c = CHUNK
    P = SSM_P
    nseq = c // glen
    sh = glen.bit_length() - 1
    ri = lax.broadcasted_iota(jnp.int32, (c, c), 0)
    ci = lax.broadcasted_iota(jnp.int32, (c, c), 1)
    tril_f = (((ri >> sh) == (ci >> sh)) & (ri >= ci)).astype(F32)
    sp = _softplus(graw + gbias)
    acum = jnp.dot(tril_f, nega * sp, precision=_HIGHEST, preferred_element_type=F32)
    lasts = [acum[s * glen + glen - 1:(s + 1) * glen, :] for s in range(nseq)]
    alast = jnp.concatenate([jnp.broadcast_to(x, (glen, LANES)) for x in lasts], axis=0)
    dtrev = sp * jnp.exp(alast - acum)
    eal = [jnp.exp(x) for x in lasts]
    lane = lax.broadcasted_iota(jnp.int32, (c, LANES), 1)
    row = lax.broadcasted_iota(jnp.int32, (c, LANES), 0)
    m = jnp.where(lane < GATE_DT2, acum, sp)
    mt = jnp.concatenate([m, m], axis=0).T
    left = lane < P
    left_row = left[0:1]
    j = jnp.where(left, lane, lane - P)
    tril2 = ((row >> sh) == (j >> sh)) & (row >= j)
    rowh = lax.broadcasted_iota(jnp.int32, (2 * P, SSM_N), 0) < P

    def expand(mat, c0):
        return jnp.where(left, jnp.broadcast_to(mat[:, c0:c0 + 1], (c, LANES)),
                         jnp.broadcast_to(mat[:, c0 + 1:c0 + 2], (c, LANES)))

    def rowsel(base, e):
        return jnp.where(left_row, mt[base + 2 * e:base + 2 * e + 1, :], mt[base + 2 * e + 1:base + 2 * e + 2, :])

    for g in range(SSM_GROUPS):
        Bg = ld_b(g)
        Cg = ld_c(g)
        Bg16 = Bg.astype(BF16)
        Cg16 = Cg.astype(BF16)
        cb2 = _dot_nt(Cg16, jnp.concatenate([Bg16, Bg16], axis=0))
        pairs = [g * PAIRS_PER_GROUP + e4 for e4 in range(PAIRS_PER_GROUP)]
        acol = [expand(acum, GATE_DT + 2 * e) for e in pairs]
        scores16 = [
            (cb2 * jnp.exp(jnp.where(tril2, a - rowsel(GATE_DT, e), -jnp.inf)) * rowsel(GATE_DT2, e)).astype(BF16)
            for a, e in zip(acol, pairs)]
        xp = [ld_x(e) for e in pairs]
        bd16 = [jnp.concatenate([jnp.where(left, x, 0.0), jnp.where(left, 0.0, x)], axis=0).astype(BF16)
                for x in xp]
        ydiag = [_dot(s, b) for s, b in zip(scores16, bd16)]
        if nseq == 1:
            yoff = [_dot_nt(Cg16, get_h(0, e).astype(BF16)) for e in pairs]
        else:
            yoff = [jnp.concatenate(
                [_dot_nt(Cg[s * glen:(s + 1) * glen].astype(BF16), get_h(s, e).astype(BF16))
                 for s in range(nseq)], axis=0) for e in pairs]
        y = [yd + yo * jnp.exp(a) + dcols_ref[:, e * 2 * P:(e + 1) * 2 * P] * x
             for yd, yo, a, e, x in zip(ydiag, yoff, acol, pairs, xp)]
        xdr = [x * expand(dtrev, GATE_DT + 2 * e) for x, e in zip(xp, pairs)]
        for e, xd in zip(pairs, xdr):
            c0 = GATE_DT + 2 * e
            for s in range(nseq):
                rs = slice(s * glen, (s + 1) * glen)
                ealcol = jnp.where(rowh, eal[s][:, c0:c0 + 1], eal[s][:, c0 + 1:c0 + 2])
                set_h(s, e, get_h(s, e) * ealcol + _dot_tn(xd[rs].astype(BF16), Bg[rs].astype(BF16)))
        yg = jnp.concatenate(y, axis=1) * _silu(ld_z(g))
        gcols = slice(g * GROUP_W, (g + 1) * GROUP_W)
        st_o(g, _rms_rows(yg, nw_ref[:, gcols]))


def _ssm_prompt_kernel(xs_ref, bc_ref, zs_ref, gate_ref, cwx_ref, cbx_ref, cwbc_ref, cbbc_ref,
                       gbias_ref, galog_ref, dcols_ref, nw_ref,
                       o_ref, cst_out_ref, hst_out_ref, xpadx_ref, xpadbc_ref, xc_ref, bcc_ref, hh_ref, *, Lb):
    c = CHUNK
    l = pl.program_id(1)
    nl = pl.num_programs(1)

    @pl.when(l == 0)
    def _init():
        hh_ref[...] = jnp.zeros(hh_ref.shape, F32)
        xpadx_ref[0:8, :] = jnp.zeros((8, SSM_DI), F32)
        xpadbc_ref[0:8, :] = jnp.zeros((8, 2 * SSM_BC), F32)

    ident = lambda s, y: y
    last3x = _conv_block(xs_ref, xpadx_ref, cwx_ref, cbx_ref, xc_ref, Lb, SSM_DI, ident)
    last3bc = _conv_block(bc_ref, xpadbc_ref, cwbc_ref, cbbc_ref, bcc_ref, Lb, 2 * SSM_BC, ident)

    @pl.when(l == nl - 1)
    def _conv_state():
        cst_out_ref[0, :, :SSM_DI] = last3x
        cst_out_ref[0, :, SSM_DI:] = last3bc

    gbias = gbias_ref[...]
    nega = -jnp.exp(galog_ref[...])

    def set_h(s, e, val):
        hh_ref[e] = val

    def body(ci, carry):
        rows = pl.ds(pl.multiple_of(ci * c, c), c)

        def st_o(g, val):
            o_ref[rows, g * GROUP_W:(g + 1) * GROUP_W] = val.astype(o_ref.dtype)

        _ssm_tile(
            gate_ref[rows, :], gbias, nega,
            lambda e: xc_ref[rows, e * LANES:(e + 1) * LANES],
            lambda g: bcc_ref[rows, g * SSM_N:(g + 1) * SSM_N],
            lambda g: bcc_ref[rows, SSM_BC + g * SSM_N:SSM_BC + (g + 1) * SSM_N],
            lambda g: zs_ref[rows, g * GROUP_W:(g + 1) * GROUP_W],
            dcols_ref, nw_ref, st_o, lambda s, e: hh_ref[e], set_h, c)
        return carry
    lax.fori_loop(0, Lb // c, body, 0)

    @pl.when(l == nl - 1)
    def _state_out():
        hst_out_ref[0] = hh_ref[...]


def _ssm_sample_kernel(xs_ref, bc_ref, zs_ref, gate_ref, cst_ref, hst_ref, cwx_ref, cbx_ref, cwbc_ref, cbbc_ref,
                       gbias_ref, galog_ref, dcols_ref, nw_ref,
                       o_ref, cst_out_ref, hst_out_ref, xpadx_ref, xpadbc_ref, xc_ref, bcc_ref, *, L):
    ident = lambda s, y: y
    for bi in range(CHUNK // L):
        rs = pl.ds(bi * L, L)
        xpx = xpadx_ref.at[bi]
        xpb = xpadbc_ref.at[bi]
        xpx[5:8, :] = cst_ref[bi, :, :SSM_DI]
        xpb[5:8, :] = cst_ref[bi, :, SSM_DI:]
        cst_out_ref[bi, :, :SSM_DI] = _conv_block(
            xs_ref.at[rs], xpx, cwx_ref, cbx_ref, xc_ref.at[rs], L, SSM_DI, ident)
        cst_out_ref[bi, :, SSM_DI:] = _conv_block(
            bc_ref.at[rs], xpb, cwbc_ref, cbbc_ref, bcc_ref.at[rs], L, 2 * SSM_BC, ident)

    def st_o(g, val):
        o_ref[:, g * GROUP_W:(g + 1) * GROUP_W] = val.astype(o_ref.dtype)

    def set_h(s, e, val):
        hst_out_ref[s, e] = val

    _ssm_tile(
        gate_ref[...], gbias_ref[...], -jnp.exp(galog_ref[...]),
        lambda e: xc_ref[:, e * LANES:(e + 1) * LANES],
        lambda g: bcc_ref[:, g * SSM_N:(g + 1) * SSM_N],
        lambda g: bcc_ref[:, SSM_BC + g * SSM_N:SSM_BC + (g + 1) * SSM_N],
        lambda g: zs_ref[:, g * GROUP_W:(g + 1) * GROUP_W],
        dcols_ref, nw_ref, st_o, lambda s, e: hst_ref[s, e], set_h, L)


def _ssm_const_specs():
    const = lambda shape: pl.BlockSpec(shape, lambda *idx: (0,) * len(shape))
    return [
        const((CONV_W, SSM_DI)), const((1, SSM_DI)), const((CONV_W, 2 * SSM_BC)), const((1, 2 * SSM_BC)),
        const((1, LANES)), const((1, LANES)), const((1, SSM_DI)), const((1, SSM_DI)),
    ]


def _ssm_prompt(proj, cwx, cbx, cwbc, cbbc, gbias, galog, dcols, nw, *, B, L, Lb):
    nl = L // Lb
    row = lambda b, l: b * nl + l
    return pl.pallas_call(
        functools.partial(_ssm_prompt_kernel, Lb=Lb),
        grid=(B, nl),
        in_specs=[
            pl.BlockSpec((Lb, SSM_DI), lambda b, l: (row(b, l), OFF_XS // SSM_DI)),
            pl.BlockSpec((Lb, 2 * SSM_BC), lambda b, l: (row(b, l), OFF_BC // (2 * SSM_BC))),
            pl.BlockSpec((Lb, SSM_DI), lambda b, l: (row(b, l), OFF_ZS // SSM_DI)),
            pl.BlockSpec((Lb, LANES), lambda b, l: (row(b, l), OFF_GATE // LANES)),
        ] + _ssm_const_specs(),
        out_specs=[
            pl.BlockSpec((Lb, SSM_DI), lambda b, l: (row(b, l), 0)),
            pl.BlockSpec((1, CONV_W - 1, SSM_CONV_CH), lambda b, l: (b, 0, 0)),
            pl.BlockSpec((1, N_PAIRS, 2 * SSM_P, SSM_N), lambda b, l: (b, 0, 0, 0)),
        ],
        out_shape=[
            jax.ShapeDtypeStruct((B * L, SSM_DI), BF16),
            jax.ShapeDtypeStruct((B, CONV_W - 1, SSM_CONV_CH), F32),
            jax.ShapeDtypeStruct((B, N_PAIRS, 2 * SSM_P, SSM_N), F32),
        ],
        scratch_shapes=[
            pltpu.VMEM((Lb + 8, SSM_DI), F32),
            pltpu.VMEM((Lb + 8, 2 * SSM_BC), F32),
            pltpu.VMEM((Lb, SSM_DI), F32),
            pltpu.VMEM((Lb, 2 * SSM_BC), F32),
            pltpu.VMEM((N_PAIRS, 2 * SSM_P, SSM_N), F32),
        ],
        compiler_params=pltpu.CompilerParams(
            dimension_semantics=("parallel", "arbitrary"), vmem_limit_bytes=VMEM_LIMIT),
        name="ssm_prompt",
    )(proj, proj, proj, proj, cwx, cbx, cwbc, cbbc, gbias, galog, dcols, nw)


def _ssm_sample(proj, conv_state, h_pairs, cwx, cbx, cwbc, cbbc, gbias, galog, dcols, nw, *, B, L):
    nb = CHUNK // L
    return pl.pallas_call(
        functools.partial(_ssm_sample_kernel, L=L),
        grid=(B // nb,),
        in_specs=[
            pl.BlockSpec((CHUNK, SSM_DI), lambda i: (i, OFF_XS // SSM_DI)),
            pl.BlockSpec((CHUNK, 2 * SSM_BC), lambda i: (i, OFF_BC // (2 * SSM_BC))),
            pl.BlockSpec((CHUNK, SSM_DI), lambda i: (i, OFF_ZS // SSM_DI)),
            pl.BlockSpec((CHUNK, LANES), lambda i: (i, OFF_GATE // LANES)),
            pl.BlockSpec((nb, CONV_W - 1, SSM_CONV_CH), lambda i: (i, 0, 0)),
            pl.BlockSpec((nb, N_PAIRS, 2 * SSM_P, SSM_N), lambda i: (i, 0, 0, 0)),
        ] + _ssm_const_specs(),
        out_specs=[
            pl.BlockSpec((CHUNK, SSM_DI), lambda i: (i, 0)),
            pl.BlockSpec((nb, CONV_W - 1, SSM_CONV_CH), lambda i: (i, 0, 0)),
            pl.BlockSpec((nb, N_PAIRS, 2 * SSM_P, SSM_N), lambda i: (i, 0, 0, 0)),
        ],
        out_shape=[
            jax.ShapeDtypeStruct((B * L, SSM_DI), BF16),
            jax.ShapeDtypeStruct((B, CONV_W - 1, SSM_CONV_CH), F32),
            jax.ShapeDtypeStruct((B, N_PAIRS, 2 * SSM_P, SSM_N), F32),
        ],
        scratch_shapes=[
            pltpu.VMEM((nb, L + 8, SSM_DI), F32),
            pltpu.VMEM((nb, L + 8, 2 * SSM_BC), F32),
            pltpu.VMEM((CHUNK, SSM_DI), F32),
            pltpu.VMEM((CHUNK, 2 * SSM_BC), F32),
        ],
        compiler_params=pltpu.CompilerParams(
            dimension_semantics=("parallel",), vmem_limit_bytes=VMEM_LIMIT),
        name="ssm_sample",
    )(proj, proj, proj, proj, conv_state, h_pairs, cwx, cbx, cwbc, cbbc, gbias, galog, dcols, nw)


def _outproj_kernel(x_ref, mg_ref, ms_ref, w_ref, o_ref):
    acc = _dot(mg_ref[...].astype(BF16), w_ref[:GDN_V, :])
    acc = acc + _dot(ms_ref[...].astype(BF16), w_ref[GDN_V:, :])
    o_ref[...] = x_ref[...] + acc


def _out_proj(x2d, mix_g, mix_s, w_out16, *, tm):
    T = x2d.shape[0]
    return pl.pallas_call(
        _outproj_kernel,
        grid=(T // tm,),
        in_specs=[
            pl.BlockSpec((tm, D_MODEL), lambda i: (i, 0)),
            pl.BlockSpec((tm, GDN_V), lambda i: (i, 0)),
            pl.BlockSpec((tm, SSM_DI), lambda i: (i, 0)),
            pl.BlockSpec((D_MODEL, D_MODEL), lambda i: (0, 0)),
        ],
        out_specs=pl.BlockSpec((tm, D_MODEL), lambda i: (i, 0)),
        out_shape=jax.ShapeDtypeStruct((T, D_MODEL), F32),
        compiler_params=pltpu.CompilerParams(
            dimension_semantics=("parallel",), vmem_limit_bytes=VMEM_LIMIT),
        name="out_proj",
    )(x2d, mix_g, mix_s, w_out16)


def _ffn_kernel(x_ref, nw_ref, wg_ref, wu_ref, wd_ref, fnw_ref, o_ref, h_ref, acc_ref):
    f = pl.program_id(1)

    @pl.when(f == 0)
    def _():
        h_ref[...] = _rms_rows(x_ref[...], nw_ref[...]).astype(BF16)
        acc_ref[...] = jnp.zeros(acc_ref.shape, F32)

    h = h_ref[...]
    a = (_silu(_dot(h, wg_ref[...])) * _dot(h, wu_ref[...])).astype(BF16)
    acc_ref[...] += _dot(a, wd_ref[...])

    @pl.when(f == pl.num_programs(1) - 1)
    def _():
        o_ref[...] = _rms_rows(x_ref[...] + acc_ref[...], fnw_ref[...])


def _ffn(x2d, norm_w, wg16, wu16, wd16, final_w, *, tm, tf):
    T = x2d.shape[0]
    return pl.pallas_call(
        _ffn_kernel,
        grid=(T // tm, D_FF // tf),
        in_specs=[
            pl.BlockSpec((tm, D_MODEL), lambda i, f: (i, 0)),
            pl.BlockSpec((1, D_MODEL), lambda i, f: (0, 0)),
            pl.BlockSpec((D_MODEL, tf), lambda i, f: (0, f)),
            pl.BlockSpec((D_MODEL, tf), lambda i, f: (0, f)),
            pl.BlockSpec((tf, D_MODEL), lambda i, f: (f, 0)),
            pl.BlockSpec((1, D_MODEL), lambda i, f: (0, 0)),
        ],
        out_specs=pl.BlockSpec((tm, D_MODEL), lambda i, f: (i, 0)),
        out_shape=jax.ShapeDtypeStruct((T, D_MODEL), F32),
        scratch_shapes=[pltpu.VMEM((tm, D_MODEL), BF16), pltpu.VMEM((tm, D_MODEL), F32)],
        compiler_params=pltpu.CompilerParams(
            dimension_semantics=("parallel", "arbitrary"), vmem_limit_bytes=VMEM_LIMIT),
        name="ffn",
    )(x2d, norm_w, wg16, wu16, wd16, final_w)


def _trunk(x, states, p):
    B, L, _ = x.shape
    x2d = x.reshape(B * L, D_MODEL)
    proj = _in_proj(x2d, p["attn_norm_w"], p["w_in_r"], tm=1024, tn=768)
    gdn_w = (p["gdn_conv_w"], p["gbias"], p["galog"], p["gdn_norm_w"])
    ssm_w = (p["cwx"], p["cbx"], p["cwbc"], p["cbbc"], p["gbias"], p["galog"], p["dcols"], p["ssm_norm_w"])
    pair_shape = (B, N_PAIRS, 2 * SSM_P, SSM_N)
    if states is None:
        mix_g, gconv_new, gS_new = _gdn_prompt(proj, *gdn_w, B=B, L=L, Lb=256)
        mix_s, sconv_new, sh_new = _ssm_prompt(proj, *ssm_w, B=B, L=L, Lb=256)
    else:
        gconv, gS, sconv, sh = states
        mix_g, gconv_new, gS_new = _gdn_sample(proj, gconv, gS, *gdn_w, B=B, L=L, nb=4)
        mix_s, sconv_new, sh_new = _ssm_sample(proj, sconv, sh.reshape(pair_shape), *ssm_w, B=B, L=L)
    sh_new = sh_new.reshape(B, SSM_HEADS, SSM_P, SSM_N)
    x1 = _out_proj(x2d, mix_g, mix_s, p["w_out16"], tm=512)
    y = _ffn(x1, p["ffn_norm_w"], p["wg16"], p["wu16"], p["wd16"], p["final_norm_w"], tm=512, tf=512)
    return y.reshape(B, L, D_MODEL), (gconv_new[None], gS_new[None], sconv_new[None], sh_new[None])


def kernel(x_prompt, x_sample, state_gdn_conv, state_gdn, state_ssm_conv, state_ssm,
           attn_norm_w, w_in, gdn_conv_w, gdn_A_log, gdn_dt_bias, gdn_norm_w,
           ssm_conv_w, ssm_conv_b, ssm_A_log, ssm_dt_bias, ssm_D, ssm_norm_w,
           w_out, ffn_norm_w, w_gate, w_up, w_down, final_norm_w):
    assert w_in.shape[0] == 1, "single-layer trunk"
    assert x_prompt.shape[1] % 256 == 0 and x_sample.shape[1] == 8 and x_sample.shape[0] % 8 == 0
    assert w_in.shape[2] == D_IN_PROJ
    w_in_r = _w_in_prep(w_in, tk=256)
    zeros8 = jnp.zeros((GDN_HEADS,), F32)
    tail = jnp.zeros((LANES - GATE_DT2 - SSM_HEADS,), F32)
    gbias = jnp.concatenate([zeros8, gdn_dt_bias[0], ssm_dt_bias[0], ssm_dt_bias[0], tail])[None]
    galog = jnp.concatenate([zeros8, gdn_A_log[0], ssm_A_log[0], ssm_A_log[0], tail])[None]
    p = dict(
        attn_norm_w=attn_norm_w, w_in_r=w_in_r, gdn_conv_w=gdn_conv_w[0], gbias=gbias, galog=galog,
        gdn_norm_w=gdn_norm_w,
        cwx=ssm_conv_w[0][:, :SSM_DI], cbx=ssm_conv_b[:, :SSM_DI],
        cwbc=ssm_conv_w[0][:, SSM_DI:], cbbc=ssm_conv_b[:, SSM_DI:],
        dcols=jnp.repeat(ssm_D[0], SSM_P)[None], ssm_norm_w=ssm_norm_w,
        w_out16=w_out[0].astype(BF16), ffn_norm_w=ffn_norm_w,
        wg16=w_gate[0].astype(BF16), wu16=w_up[0].astype(BF16), wd16=w_down[0].astype(BF16),
        final_norm_w=final_norm_w[None],
    )
    y_p, st_p = _trunk(x_prompt, None, p)
    y_s, st_s = _trunk(x_sample, (state_gdn_conv[0], state_gdn[0], state_ssm_conv[0], state_ssm[0]), p)
    return (y_p, y_s, st_p[0], st_p[1], st_p[2], st_p[3], st_s[0], st_s[1], st_s[2], st_s[3])
```

```python
import functools

import jax
import jax.numpy as jnp
from jax import lax
from jax.experimental import pallas as pl
from jax.experimental.pallas import tpu as pltpu

F32 = jnp.float32
BF16 = jnp.bfloat16

D_MODEL = 2048
GDN_HEADS = 8
GDN_DK = 128
GDN_DV = 128
GDN_QK = GDN_HEADS * GDN_DK
GDN_V = GDN_HEADS * GDN_DV
GDN_CONV_CH = 2 * GDN_QK + GDN_V
SSM_P = 64
SSM_N = 128
SSM_GROUPS = 2
SSM_DI = 1024
SSM_HEADS = SSM_DI // SSM_P
SSM_BC = SSM_GROUPS * SSM_N
SSM_CONV_CH = SSM_DI + 2 * SSM_BC
CONV_W = 4
CHUNK = 64
D_FF = 5632
EPS = 1e-6

OFF_QKV = 0
OFF_ZG = OFF_QKV + GDN_CONV_CH
OFF_ZS = OFF_ZG + GDN_V
OFF_XS = OFF_ZS + SSM_DI
OFF_BC = OFF_XS + SSM_DI
OFF_GATE = OFF_BC + 2 * SSM_BC
LANES = 128
GATE_B = 0
GATE_A = GATE_B + GDN_HEADS
GATE_DT = GATE_A + GDN_HEADS
GATE_DT2 = GATE_DT + SSM_HEADS
N_PROJ = 6912

VMEM_LIMIT = 52 * 1024 * 1024

_HIGHEST = lax.Precision.HIGHEST


def _silu(x):
    return x * jax.nn.sigmoid(x)


def _softplus(x):
    return jnp.maximum(x, 0.0) + jnp.log1p(jnp.exp(-jnp.abs(x)))


def _dot(a, b):
    return jnp.dot(a, b, preferred_element_type=F32)


def _dot_nt(a, b):
    return lax.dot_general(a, b, (((1,), (1,)), ((), ())), preferred_element_type=F32)


def _dot_tn(a, b):
    return lax.dot_general(a, b, (((0,), (0,)), ((), ())), preferred_element_type=F32)


def _transpose_rows(a):
    r = a.shape[0]
    if r < LANES:
        a = jnp.concatenate([a, jnp.zeros((LANES - r, LANES), a.dtype)], axis=0)
    return a.T


def _rms_rows(x, w):
    return x * lax.rsqrt(jnp.mean(x * x, axis=-1, keepdims=True) + EPS) * w


W_B = OFF_ZS
W_ZS = W_B + 2 * GDN_HEADS
W_DT = W_ZS + SSM_DI + SSM_CONV_CH
D_IN_PROJ = W_DT + SSM_HEADS


def _wprep_kernel(w_ref, o_ref):
    cols = o_ref.shape[1]
    o_ref[:W_B, :] = w_ref[0, :W_B, :].astype(BF16)
    o_ref[W_B:OFF_GATE, :] = w_ref[0, W_ZS:W_DT, :].astype(BF16)
    dt = w_ref[0, W_DT:D_IN_PROJ, :].astype(BF16)
    o_ref[OFF_GATE:OFF_GATE + GATE_DT, :] = w_ref[0, W_B:W_ZS, :].astype(BF16)
    o_ref[OFF_GATE + GATE_DT:OFF_GATE + GATE_DT2, :] = dt
    o_ref[OFF_GATE + GATE_DT2:OFF_GATE + GATE_DT2 + SSM_HEADS, :] = dt
    o_ref[OFF_GATE + GATE_DT2 + SSM_HEADS:, :] = jnp.zeros((N_PROJ - OFF_GATE - GATE_DT2 - SSM_HEADS, cols), BF16)


def _w_in_prep(w_in_t, *, tk):
    return pl.pallas_call(
        _wprep_kernel,
        grid=(D_MODEL // tk,),
        in_specs=[pl.BlockSpec((1, D_IN_PROJ, tk), lambda i: (0, 0, i))],
        out_specs=pl.BlockSpec((N_PROJ, tk), lambda i: (0, i)),
        out_shape=jax.ShapeDtypeStruct((N_PROJ, D_MODEL), BF16),
        compiler_params=pltpu.CompilerParams(
            dimension_semantics=("parallel",), vmem_limit_bytes=VMEM_LIMIT),
        name="w_in_prep",
    )(w_in_t)


def _inproj_kernel(x_ref, nw_ref, wt_ref, o_ref, h_ref):
    @pl.when(pl.program_id(1) == 0)
    def _():
        h_ref[...] = _rms_rows(x_ref[...], nw_ref[...]).astype(BF16)

    o_ref[...] = _dot_nt(h_ref[...], wt_ref[...])


def _in_proj(x2d, norm_w, w_in_r, *, tm, tn):
    T = x2d.shape[0]
    return pl.pallas_call(
        _inproj_kernel,
        grid=(T // tm, N_PROJ // tn),
        in_specs=[
            pl.BlockSpec((tm, D_MODEL), lambda i, j: (i, 0)),
            pl.BlockSpec((1, D_MODEL), lambda i, j: (0, 0)),
            pl.BlockSpec((tn, D_MODEL), lambda i, j: (j, 0)),
        ],
        out_specs=pl.BlockSpec((tm, tn), lambda i, j: (i, j)),
        out_shape=jax.ShapeDtypeStruct((T, N_PROJ), F32),
        scratch_shapes=[pltpu.VMEM((tm, D_MODEL), BF16)],
        compiler_params=pltpu.CompilerParams(
            dimension_semantics=("parallel", "arbitrary"), vmem_limit_bytes=VMEM_LIMIT),
        name="in_proj",
    )(x2d, norm_w, w_in_r)


def _conv_block(x_ref, xpad_ref, cw_ref, cb_ref, dst_ref, Lb, C, post):
    xpad_ref[8:8 + Lb, :] = x_ref[...]
    rs = min(Lb, CHUNK)
    for sb in range(Lb // rs):
        r = sb * rs
        for s in range(C // LANES):
            cols = slice(s * LANES, (s + 1) * LANES)
            acc = xpad_ref[8 + r:8 + r + rs, cols] * cw_ref[3:4, cols]
            for i in range(CONV_W - 1):
                acc = acc + xpad_ref[5 + i + r:5 + i + r + rs, cols] * cw_ref[i:i + 1, cols]
            if cb_ref is not None:
                acc = acc + cb_ref[:, cols]
            dst_ref[r:r + rs, cols] = post(s, _silu(acc))
    last3 = xpad_ref[Lb + 5:Lb + 8, :]
    xpad_ref[5:8, :] = last3
    return last3


STACK = 128


def _gdn_qk_post(s, y):
    if s < 2 * GDN_HEADS:
        y = y * lax.rsqrt(jnp.sum(y * y, axis=-1, keepdims=True) + EPS)
        if s < GDN_HEADS:
            y = y * (GDN_DK ** -0.5)
    return y


def _gdn_local(items, glen):
    sh = glen.bit_length() - 1
    row = lax.broadcasted_iota(jnp.int32, (STACK, STACK), 0)
    col = lax.broadcasted_iota(jnp.int32, (STACK, STACK), 1)
    same = (row >> sh) == (col >> sh)
    incl = same & (row >= col)
    strict = same & (row > col)
    eye = (row == col).astype(F32)

    decay = [jnp.exp(jnp.where(incl, it["g"] - it["g"].T, -jnp.inf)) for it in items]
    kb = [it["k"] * it["beta"] for it in items]
    qkk = [_dot_nt(jnp.concatenate([it["q"], b], axis=0).astype(BF16), it["k"].astype(BF16))
           for it, b in zip(items, kb)]
    qk = [x[:STACK] * d for x, d in zip(qkk, decay)]
    nmat = [jnp.where(strict, -(x[STACK:] * d), 0.0) for x, d in zip(qkk, decay)]
    tinv = [eye + n for n in nmat]
    if sh >= 2:
        pw = [_dot(n.astype(BF16), n.astype(BF16)) for n in nmat]
        for _ in range(sh - 2):
            x = [_dot(jnp.concatenate([t, p], axis=0).astype(BF16), p.astype(BF16))
                 for t, p in zip(tinv, pw)]
            tinv = [t + y[:STACK] for t, y in zip(tinv, x)]
            pw = [y[STACK:] for y in x]
        tinv = [t + _dot(t.astype(BF16), p.astype(BF16)) for t, p in zip(tinv, pw)]
    uw = [_dot(t.astype(BF16),
               jnp.concatenate([it["v"] * it["beta"], b * it["eg"]], axis=1).astype(BF16))
          for t, it, b in zip(tinv, items, kb)]
    return [(x[:, :GDN_DV], x[:, GDN_DV:]) for x in uw], qk


GDN_LOCAL_CHUNKS = 2


def _gdn_prompt_kernel(qkv_ref, zg_ref, gate_ref, cw_ref, gbias_ref, galog_ref, nw_ref,
                       o_ref, cst_out_ref, sst_out_ref,
                       xpad_ref, qkvc_ref, s_ref, u_ref, wq16_ref, kd16_ref, qk16_ref, egl_ref, *, Lb):
    c = CHUNK
    l = pl.program_id(1)
    nl = pl.num_programs(1)

    @pl.when(l == 0)
    def _init():
        xpad_ref[0:8, :] = jnp.zeros((8, GDN_CONV_CH), F32)
        s_ref[...] = jnp.zeros(s_ref.shape, F32)

    _conv_block(qkv_ref, xpad_ref, cw_ref, None, qkvc_ref, Lb, GDN_CONV_CH, _gdn_qk_post)

    row_i = lax.broadcasted_iota(jnp.int32, (c, c), 0)
    col_i = lax.broadcasted_iota(jnp.int32, (c, c), 1)
    tril_f = (row_i >= col_i).astype(F32)
    gbias = gbias_ref[...]
    nega = -jnp.exp(galog_ref[...])
    nw = nw_ref[...]
    n_pairs = GDN_HEADS // 2

    def local_chunk_items(ci):
        rows = slice(ci * c, (ci + 1) * c)
        graw = gate_ref[rows, :]
        sp = _softplus(graw + gbias)
        beta_all = jax.nn.sigmoid(graw)
        G = jnp.dot(tril_f, nega * sp, precision=_HIGHEST, preferred_element_type=F32)
        eG = jnp.exp(G)
        glast = G[c - 1:c, :]
        eGrev = jnp.exp(glast - G)
        egl_ref[ci] = jnp.broadcast_to(jnp.exp(glast), (8, LANES))

        def heads(off, a, b):
            return jnp.concatenate([qkvc_ref[rows, off + a * LANES:off + (a + 1) * LANES],
                                    qkvc_ref[rows, off + b * LANES:off + (b + 1) * LANES]], axis=0)

        def colstack(m, a, b):
            return jnp.concatenate([jnp.broadcast_to(m[:, a:a + 1], (c, LANES)),
                                    jnp.broadcast_to(m[:, b:b + 1], (c, LANES))], axis=0)

        items = []
        for pr in range(n_pairs):
            a, b = 2 * pr, 2 * pr + 1
            items.append(dict(
                q=heads(0, a, b), k=heads(GDN_QK, a, b), v=heads(2 * GDN_QK, a, b),
                beta=colstack(beta_all, GATE_B + a, GATE_B + b),
                g=colstack(G, GATE_A + a, GATE_A + b),
                eg=colstack(eG, GATE_A + a, GATE_A + b),
                egrev=colstack(eGrev, GATE_A + a, GATE_A + b)))
        return items

    def local_group(cis):
        items = [it for ci in cis for it in local_chunk_items(ci)]
        uw, qk = _gdn_local(items, c)
        for n, it in enumerate(items):
            idx = cis[0] * n_pairs + n
            u, w = uw[n]
            qd = it["q"] * it["eg"]
            u_ref[idx] = u
            for hh in range(2):
                hs = slice(hh * c, (hh + 1) * c)
                wq16_ref[2 * idx + hh] = jnp.concatenate([w[hs], qd[hs]], axis=0).astype(BF16)
            kd16_ref[idx] = (it["k"] * it["egrev"]).astype(BF16)
            qk16_ref[idx] = qk[n].astype(BF16)

    def recurrent(ci):
        rows = slice(ci * c, (ci + 1) * c)
        egl = egl_ref[ci][0:1]
        r = [[_dot(wq16_ref[2 * (ci * n_pairs + pr) + hh], s_ref[2 * pr + hh].astype(BF16)) for hh in range(2)]
             for pr in range(n_pairs)]
        v16 = [(u_ref[ci * n_pairs + pr] - jnp.concatenate([r[pr][0][:c], r[pr][1][:c]], axis=0)).astype(BF16)
               for pr in range(n_pairs)]
        o = [jnp.concatenate([r[pr][0][c:], r[pr][1][c:]], axis=0) + _dot(qk16_ref[ci * n_pairs + pr], v16[pr])
             for pr in range(n_pairs)]
        for pr in range(n_pairs):
            kd16 = kd16_ref[ci * n_pairs + pr]
            for hh in range(2):
                h = 2 * pr + hh
                ga = GATE_A + h
                hs = slice(hh * c, (hh + 1) * c)
                s_ref[h] = s_ref[h] * egl[:, ga:ga + 1] + _dot_tn(kd16[hs], v16[pr][hs])
        for pr in range(n_pairs):
            for hh in range(2):
                h = 2 * pr + hh
                z = zg_ref[rows, h * GDN_DV:(h + 1) * GDN_DV]
                o_ref[rows, h * GDN_DV:(h + 1) * GDN_DV] = (
                    _rms_rows(o[pr][hh * c:(hh + 1) * c], nw) * _silu(z)).astype(o_ref.dtype)

    for g0 in range(0, Lb // c, GDN_LOCAL_CHUNKS):
        cis = list(range(g0, g0 + GDN_LOCAL_CHUNKS))
        local_group(cis)
        for ci in cis:
            recurrent(ci)

    @pl.when(l == nl - 1)
    def _final():
        cst_out_ref[0] = xpad_ref[5:8, :]
        sst_out_ref[0] = s_ref[...]


def _gdn_prompt(proj, cw, gbias, galog, nw, *, B, L, Lb):
    nl = L // Lb
    n_tiles = (Lb // CHUNK) * (GDN_HEADS // 2)
    row = lambda b, l: b * nl + l
    const = lambda shape: pl.BlockSpec(shape, lambda b, l: (0,) * len(shape))
    return pl.pallas_call(
        functools.partial(_gdn_prompt_kernel, Lb=Lb),
        grid=(B, nl),
        in_specs=[
            pl.BlockSpec((Lb, GDN_CONV_CH), lambda b, l: (row(b, l), OFF_QKV // GDN_CONV_CH)),
            pl.BlockSpec((Lb, GDN_V), lambda b, l: (row(b, l), OFF_ZG // GDN_V)),
            pl.BlockSpec((Lb, LANES), lambda b, l: (row(b, l), OFF_GATE // LANES)),
            const((CONV_W, GDN_CONV_CH)), const((1, LANES)), const((1, LANES)), const((1, GDN_DV)),
        ],
        out_specs=[
            pl.BlockSpec((Lb, GDN_V), lambda b, l: (row(b, l), 0)),
            pl.BlockSpec((1, CONV_W - 1, GDN_CONV_CH), lambda b, l: (b, 0, 0)),
            pl.BlockSpec((1, GDN_HEADS, GDN_DK, GDN_DV), lambda b, l: (b, 0, 0, 0)),
        ],
        out_shape=[
            jax.ShapeDtypeStruct((B * L, GDN_V), BF16),
            jax.ShapeDtypeStruct((B, CONV_W - 1, GDN_CONV_CH), F32),
            jax.ShapeDtypeStruct((B, GDN_HEADS, GDN_DK, GDN_DV), F32),
        ],
        scratch_shapes=[
            pltpu.VMEM((Lb + 8, GDN_CONV_CH), F32),
            pltpu.VMEM((Lb, GDN_CONV_CH), F32),
            pltpu.VMEM((GDN_HEADS, GDN_DK, GDN_DV), F32),
            pltpu.VMEM((n_tiles, STACK, GDN_DV), F32),
            pltpu.VMEM((2 * n_tiles, STACK, GDN_DK), BF16),
            pltpu.VMEM((n_tiles, STACK, GDN_DK), BF16),
            pltpu.VMEM((n_tiles, STACK, STACK), BF16),
            pltpu.VMEM((Lb // CHUNK, 8, LANES), F32),
        ],
        compiler_params=pltpu.CompilerParams(
            dimension_semantics=("parallel", "arbitrary"), vmem_limit_bytes=VMEM_LIMIT),
        name="gdn_prompt",
    )(proj, proj, proj, cw, gbias, galog, nw)


def _gdn_sample_kernel(qkv_ref, zg_ref, gate_ref, cst_ref, sst_ref, cw_ref, gbias_ref, galog_ref, nw_ref,
                       o_ref, cst_out_ref, sst_out_ref, xpad_ref, qkvc_ref, *, nb, L):
    R = nb * L
    sh = L.bit_length() - 1
    for bi in range(nb):
        xp = xpad_ref.at[bi]
        xp[5:8, :] = cst_ref[bi]
        cst_out_ref[bi] = _conv_block(
            qkv_ref.at[pl.ds(bi * L, L)], xp, cw_ref, None, qkvc_ref.at[pl.ds(bi * L, L)],
            L, GDN_CONV_CH, _gdn_qk_post)

    row_i = lax.broadcasted_iota(jnp.int32, (R, R), 0)
    col_i = lax.broadcasted_iota(jnp.int32, (R, R), 1)
    tril_f = (((row_i >> sh) == (col_i >> sh)) & (row_i >= col_i)).astype(F32)
    graw = gate_ref[...]
    sp = _softplus(graw + gbias_ref[...])
    beta_all = jax.nn.sigmoid(graw)
    G = jnp.dot(tril_f, -jnp.exp(galog_ref[...]) * sp, precision=_HIGHEST, preferred_element_type=F32)
    glast = [G[bi * L + L - 1:bi * L + L, :] for bi in range(nb)]
    eG = jnp.exp(G)
    eGrev = jnp.exp(jnp.concatenate([jnp.broadcast_to(x, (L, LANES)) for x in glast], axis=0) - G)
    egl = [jnp.exp(x) for x in glast]
    nw = nw_ref[...]

    n_st = R // (2 * L)

    def tiles(ref, st, off):
        return jnp.concatenate(
            [ref[st * 2 * L:(st + 1) * 2 * L, off + h * LANES:off + (h + 1) * LANES]
             for h in range(GDN_HEADS)], axis=0)

    def colstack(m, st, off):
        return jnp.concatenate(
            [jnp.broadcast_to(m[st * 2 * L:(st + 1) * 2 * L, off + h:off + h + 1], (2 * L, LANES))
             for h in range(GDN_HEADS)], axis=0)

    items = [dict(q=tiles(qkvc_ref, st, 0), k=tiles(qkvc_ref, st, GDN_QK), v=tiles(qkvc_ref, st, 2 * GDN_QK),
                  beta=colstack(beta_all, st, GATE_B), g=colstack(G, st, GATE_A),
                  eg=colstack(eG, st, GATE_A), egrev=colstack(eGrev, st, GATE_A))
             for st in range(n_st)]
    uw, qk = _gdn_local(items, L)

    groups = [(h, bi) for h in range(GDN_HEADS) for bi in range(2)]
    r = []
    for st, it in enumerate(items):
        w = uw[st][1]
        qd = it["q"] * it["eg"]
        r.append([
            _dot(jnp.concatenate([w[gi * L:(gi + 1) * L], qd[gi * L:(gi + 1) * L]], axis=0).astype(BF16),
                 sst_ref[2 * st + bi, h].astype(BF16))
            for gi, (h, bi) in enumerate(groups)])
    v_new = [uw[st][0] - jnp.concatenate([x[:L] for x in r[st]], axis=0) for st in range(n_st)]
    o = [jnp.concatenate([x[L:] for x in r[st]], axis=0)
         + _dot(qk[st].astype(BF16), v_new[st].astype(BF16)) for st in range(n_st)]
    for st, it in enumerate(items):
        kd = it["k"] * it["egrev"]
        for gi, (h, bi) in enumerate(groups):
            b = 2 * st + bi
            ga = GATE_A + h
            rs = slice(gi * L, (gi + 1) * L)
            sst_out_ref[b, h] = (sst_ref[b, h] * egl[b][:, ga:ga + 1]
                                 + _dot_tn(kd[rs].astype(BF16), v_new[st][rs].astype(BF16)))
    for st in range(n_st):
        out = (_rms_rows(o[st], nw) * _silu(tiles(zg_ref, st, 0))).astype(o_ref.dtype)
        for h in range(GDN_HEADS):
            o_ref[st * 2 * L:(st + 1) * 2 * L, h * GDN_DV:(h + 1) * GDN_DV] = out[h * 2 * L:(h + 1) * 2 * L]


def _gdn_sample(proj, conv_state, S_state, cw, gbias, galog, nw, *, B, L, nb):
    R = nb * L
    const = lambda shape: pl.BlockSpec(shape, lambda i: (0,) * len(shape))
    return pl.pallas_call(
        functools.partial(_gdn_sample_kernel, nb=nb, L=L),
        grid=(B // nb,),
        in_specs=[
            pl.BlockSpec((R, GDN_CONV_CH), lambda i: (i, OFF_QKV // GDN_CONV_CH)),
            pl.BlockSpec((R, GDN_V), lambda i: (i, OFF_ZG // GDN_V)),
            pl.BlockSpec((R, LANES), lambda i: (i, OFF_GATE // LANES)),
            pl.BlockSpec((nb, CONV_W - 1, GDN_CONV_CH), lambda i: (i, 0, 0)),
            pl.BlockSpec((nb, GDN_HEADS, GDN_DK, GDN_DV), lambda i: (i, 0, 0, 0)),
            const((CONV_W, GDN_CONV_CH)), const((1, LANES)), const((1, LANES)), const((1, GDN_DV)),
        ],
        out_specs=[
            pl.BlockSpec((R, GDN_V), lambda i: (i, 0)),
            pl.BlockSpec((nb, CONV_W - 1, GDN_CONV_CH), lambda i: (i, 0, 0)),
            pl.BlockSpec((nb, GDN_HEADS, GDN_DK, GDN_DV), lambda i: (i, 0, 0, 0)),
        ],
        out_shape=[
            jax.ShapeDtypeStruct((B * L, GDN_V), BF16),
            jax.ShapeDtypeStruct((B, CONV_W - 1, GDN_CONV_CH), F32),
            jax.ShapeDtypeStruct((B, GDN_HEADS, GDN_DK, GDN_DV), F32),
        ],
        scratch_shapes=[
            pltpu.VMEM((nb, L + 8, GDN_CONV_CH), F32),
            pltpu.VMEM((R, GDN_CONV_CH), F32),
        ],
        compiler_params=pltpu.CompilerParams(
            dimension_semantics=("parallel",), vmem_limit_bytes=VMEM_LIMIT),
        name="gdn_sample",
    )(proj, proj, proj, conv_state, S_state, cw, gbias, galog, nw)


N_PAIRS = SSM_HEADS // 2
PAIRS_PER_GROUP = N_PAIRS // SSM_GROUPS
GROUP_W = SSM_DI // SSM_GROUPS


def _ssm_tile(graw, gbias, nega, ld_x, ld_b, ld_c, ld_z, dcols_ref, nw_ref, st_o, get_h, set_h, glen):
    c = CHUNK
    P = SSM_P
    nseq = c // glen
    sh = glen.bit_length() - 1
    ri = lax.broadcasted_iota(jnp.int32, (c, c), 0)
    ci = lax.broadcasted_iota(jnp.int32, (c, c), 1)
    tril_f = (((ri >> sh) == (ci >> sh)) & (ri >= ci)).astype(F32)
    sp = _softplus(graw + gbias)
    acum = jnp.dot(tril_f, nega * sp, precision=_HIGHEST, preferred_element_type=F32)
    lasts = [acum[s * glen + glen - 1:(s + 1) * glen, :] for s in range(nseq)]
    alast = jnp.concatenate([jnp.broadcast_to(x, (glen, LANES)) for x in lasts], axis=0)
    dtrev = sp * jnp.exp(alast - acum)
    eal = [jnp.exp(x) for x in lasts]
    lane = lax.broadcasted_iota(jnp.int32, (c, LANES), 1)
    row = lax.broadcasted_iota(jnp.int32, (c, LANES), 0)
    m = jnp.where(lane < GATE_DT2, acum, sp)
    mt = jnp.concatenate([m, m], axis=0).T
    left = lane < P
    left_row = left[0:1]
    j = jnp.where(left, lane, lane - P)
    tril2 = ((row >> sh) == (j >> sh)) & (row >= j)
    rowh = lax.broadcasted_iota(jnp.int32, (2 * P, SSM_N), 0) < P

    def expand(mat, c0):
        return jnp.where(left, jnp.broadcast_to(mat[:, c0:c0 + 1], (c, LANES)),
                         jnp.broadcast_to(mat[:, c0 + 1:c0 + 2], (c, LANES)))

    def rowsel(base, e):
        return jnp.where(left_row, mt[base + 2 * e:base + 2 * e + 1, :], mt[base + 2 * e + 1:base + 2 * e + 2, :])

    for g in range(SSM_GROUPS):
        Bg = ld_b(g)
        Cg = ld_c(g)
        Bg16 = Bg.astype(BF16)
        Cg16 = Cg.astype(BF16)
        cb2 = _dot_nt(Cg16, jnp.concatenate([Bg16, Bg16], axis=0))
        pairs = [g * PAIRS_PER_GROUP + e4 for e4 in range(PAIRS_PER_GROUP)]
        acol = [expand(acum, GATE_DT + 2 * e) for e in pairs]
        scores16 = [
            (cb2 * jnp.exp(jnp.where(tril2, a - rowsel(GATE_DT, e), -jnp.inf)) * rowsel(GATE_DT2, e)).astype(BF16)
            for a, e in zip(acol, pairs)]
        xp = [ld_x(e) for e in pairs]
        bd16 = [jnp.concatenate([jnp.where(left, x, 0.0), jnp.where(left, 0.0, x)], axis=0).astype(BF16)
                for x in xp]
        ydiag = [_dot(s, b) for s, b in zip(scores16, bd16)]
        if nseq == 1:
            yoff = [_dot_nt(Cg16, get_h(0, e).astype(BF16)) for e in pairs]
        else:
            yoff = [jnp.concatenate(
                [_dot_nt(Cg[s * glen:(s + 1) * glen].astype(BF16), get_h(s, e).astype(BF16))
                 for s in range(nseq)], axis=0) for e in pairs]
        y = [yd + yo * jnp.exp(a) + dcols_ref[:, e * 2 * P:(e + 1) * 2 * P] * x
             for yd, yo, a, e, x in zip(ydiag, yoff, acol, pairs, xp)]
        xdr = [x * expand(dtrev, GATE_DT + 2 * e) for x, e in zip(xp, pairs)]
        for e, xd in zip(pairs, xdr):
            c0 = GATE_DT + 2 * e
            for s in range(nseq):
                rs = slice(s * glen, (s + 1) * glen)
                ealcol = jnp.where(rowh, eal[s][:, c0:c0 + 1], eal[s][:, c0 + 1:c0 + 2])
                set_h(s, e, get_h(s, e) * ealcol + _dot_tn(xd[rs].astype(BF16), Bg[rs].astype(BF16)))
        yg = jnp.concatenate(y, axis=1) * _silu(ld_z(g))
        gcols = slice(g * GROUP_W, (g + 1) * GROUP_W)
        st_o(g, _rms_rows(yg, nw_ref[:, gcols]))


def _ssm_prompt_kernel(xs_ref, bc_ref, zs_ref, gate_ref, cwx_ref, cbx_ref, cwbc_ref, cbbc_ref,
                       gbias_ref, galog_ref, dcols_ref, nw_ref,
                       o_ref, cst_out_ref, hst_out_ref, xpadx_ref, xpadbc_ref, xc_ref, bcc_ref, hh_ref, *, Lb):
    c = CHUNK
    l = pl.program_id(1)
    nl = pl.num_programs(1)

    @pl.when(l == 0)
    def _init():
        hh_ref[...] = jnp.zeros(hh_ref.shape, F32)
        xpadx_ref[0:8, :] = jnp.zeros((8, SSM_DI), F32)
        xpadbc_ref[0:8, :] = jnp.zeros((8, 2 * SSM_BC), F32)

    ident = lambda s, y: y
    last3x = _conv_block(xs_ref, xpadx_ref, cwx_ref, cbx_ref, xc_ref, Lb, SSM_DI, ident)
    last3bc = _conv_block(bc_ref, xpadbc_ref, cwbc_ref, cbbc_ref, bcc_ref, Lb, 2 * SSM_BC, ident)

    @pl.when(l == nl - 1)
    def _conv_state():
        cst_out_ref[0, :, :SSM_DI] = last3x
        cst_out_ref[0, :, SSM_DI:] = last3bc

    gbias = gbias_ref[...]
    nega = -jnp.exp(galog_ref[...])

    def set_h(s, e, val):
        hh_ref[e] = val

    def body(ci, carry):
        rows = pl.ds(pl.multiple_of(ci * c, c), c)

        def st_o(g, val):
            o_ref[rows, g * GROUP_W:(g + 1) * GROUP_W] = val.astype(o_ref.dtype)

        _ssm_tile(
            gate_ref[rows, :], gbias, nega,
            lambda e: xc_ref[rows, e * LANES:(e + 1) * LANES],
            lambda g: bcc_ref[rows, g * SSM_N:(g + 1) * SSM_N],
            lambda g: bcc_ref[rows, SSM_BC + g * SSM_N:SSM_BC + (g + 1) * SSM_N],
            lambda g: zs_ref[rows, g * GROUP_W:(g + 1) * GROUP_W],
            dcols_ref, nw_ref, st_o, lambda s, e: hh_ref[e], set_h, c)
        return carry
    lax.fori_loop(0, Lb // c, body, 0)

    @pl.when(l == nl - 1)
    def _state_out():
        hst_out_ref[0] = hh_ref[...]


def _ssm_sample_kernel(xs_ref, bc_ref, zs_ref, gate_ref, cst_ref, hst_ref, cwx_ref, cbx_ref, cwbc_ref, cbbc_ref,
                       gbias_ref, galog_ref, dcols_ref, nw_ref,
                       o_ref, cst_out_ref, hst_out_ref, xpadx_ref, xpadbc_ref, xc_ref, bcc_ref, *, L):
    ident = lambda s, y: y
    for bi in range(CHUNK // L):
        rs = pl.ds(bi * L, L)
        xpx = xpadx_ref.at[bi]
        xpb = xpadbc_ref.at[bi]
        xpx[5:8, :] = cst_ref[bi, :, :SSM_DI]
        xpb[5:8, :] = cst_ref[bi, :, SSM_DI:]
        cst_out_ref[bi, :, :SSM_DI] = _conv_block(
            xs_ref.at[rs], xpx, cwx_ref, cbx_ref, xc_ref.at[rs], L, SSM_DI, ident)
        cst_out_ref[bi, :, SSM_DI:] = _conv_block(
            bc_ref.at[rs], xpb, cwbc_ref, cbbc_ref, bcc_ref.at[rs], L, 2 * SSM_BC, ident)

    def st_o(g, val):
        o_ref[:, g * GROUP_W:(g + 1) * GROUP_W] = val.astype(o_ref.dtype)

    def set_h(s, e, val):
        hst_out_ref[s, e] = val

    _ssm_tile(
        gate_ref[...], gbias_ref[...], -jnp.exp(galog_ref[...]),
        lambda e: xc_ref[:, e * LANES:(e + 1) * LANES],
        lambda g: bcc_ref[:, g * SSM_N:(g + 1) * SSM_N],
        lambda g: bcc_ref[:, SSM_BC + g * SSM_N:SSM_BC + (g + 1) * SSM_N],
        lambda g: zs_ref[:, g * GROUP_W:(g + 1) * GROUP_W],
        dcols_ref, nw_ref, st_o, lambda s, e: hst_ref[s, e], set_h, L)


def _ssm_const_specs():
    const = lambda shape: pl.BlockSpec(shape, lambda *idx: (0,) * len(shape))
    return [
        const((CONV_W, SSM_DI)), const((1, SSM_DI)), const((CONV_W, 2 * SSM_BC)), const((1, 2 * SSM_BC)),
        const((1, LANES)), const((1, LANES)), const((1, SSM_DI)), const((1, SSM_DI)),
    ]


def _ssm_prompt(proj, cwx, cbx, cwbc, cbbc, gbias, galog, dcols, nw, *, B, L, Lb):
    nl = L // Lb
    row = lambda b, l: b * nl + l
    return pl.pallas_call(
        functools.partial(_ssm_prompt_kernel, Lb=Lb),
        grid=(B, nl),
        in_specs=[
            pl.BlockSpec((Lb, SSM_DI), lambda b, l: (row(b, l), OFF_XS // SSM_DI)),
            pl.BlockSpec((Lb, 2 * SSM_BC), lambda b, l: (row(b, l), OFF_BC // (2 * SSM_BC))),
            pl.BlockSpec((Lb, SSM_DI), lambda b, l: (row(b, l), OFF_ZS // SSM_DI)),
            pl.BlockSpec((Lb, LANES), lambda b, l: (row(b, l), OFF_GATE // LANES)),
        ] + _ssm_const_specs(),
        out_specs=[
            pl.BlockSpec((Lb, SSM_DI), lambda b, l: (row(b, l), 0)),
            pl.BlockSpec((1, CONV_W - 1, SSM_CONV_CH), lambda b, l: (b, 0, 0)),
            pl.BlockSpec((1, N_PAIRS, 2 * SSM_P, SSM_N), lambda b, l: (b, 0, 0, 0)),
        ],
        out_shape=[
            jax.ShapeDtypeStruct((B * L, SSM_DI), BF16),
            jax.ShapeDtypeStruct((B, CONV_W - 1, SSM_CONV_CH), F32),
            jax.ShapeDtypeStruct((B, N_PAIRS, 2 * SSM_P, SSM_N), F32),
        ],
        scratch_shapes=[
            pltpu.VMEM((Lb + 8, SSM_DI), F32),
            pltpu.VMEM((Lb + 8, 2 * SSM_BC), F32),
            pltpu.VMEM((Lb, SSM_DI), F32),
            pltpu.VMEM((Lb, 2 * SSM_BC), F32),
            pltpu.VMEM((N_PAIRS, 2 * SSM_P, SSM_N), F32),
        ],
        compiler_params=pltpu.CompilerParams(
            dimension_semantics=("parallel", "arbitrary"), vmem_limit_bytes=VMEM_LIMIT),
        name="ssm_prompt",
    )(proj, proj, proj, proj, cwx, cbx, cwbc, cbbc, gbias, galog, dcols, nw)


def _ssm_sample(proj, conv_state, h_pairs, cwx, cbx, cwbc, cbbc, gbias, galog, dcols, nw, *, B, L):
    nb = CHUNK // L
    return pl.pallas_call(
        functools.partial(_ssm_sample_kernel, L=L),
        grid=(B // nb,),
        in_specs=[
            pl.BlockSpec((CHUNK, SSM_DI), lambda i: (i, OFF_XS // SSM_DI)),
            pl.BlockSpec((CHUNK, 2 * SSM_BC), lambda i: (i, OFF_BC // (2 * SSM_BC))),
            pl.BlockSpec((CHUNK, SSM_DI), lambda i: (i, OFF_ZS // SSM_DI)),
            pl.BlockSpec((CHUNK, LANES), lambda i: (i, OFF_GATE // LANES)),
            pl.BlockSpec((nb, CONV_W - 1, SSM_CONV_CH), lambda i: (i, 0, 0)),
            pl.BlockSpec((nb, N_PAIRS, 2 * SSM_P, SSM_N), lambda i: (i, 0, 0, 0)),
        ] + _ssm_const_specs(),
        out_specs=[
            pl.BlockSpec((CHUNK, SSM_DI), lambda i: (i, 0)),
            pl.BlockSpec((nb, CONV_W - 1, SSM_CONV_CH), lambda i: (i, 0, 0)),
            pl.BlockSpec((nb, N_PAIRS, 2 * SSM_P, SSM_N), lambda i: (i, 0, 0, 0)),
        ],
        out_shape=[
            jax.ShapeDtypeStruct((B * L, SSM_DI), BF16),
            jax.ShapeDtypeStruct((B, CONV_W - 1, SSM_CONV_CH), F32),
            jax.ShapeDtypeStruct((B, N_PAIRS, 2 * SSM_P, SSM_N), F32),
        ],
        scratch_shapes=[
            pltpu.VMEM((nb, L + 8, SSM_DI), F32),
            pltpu.VMEM((nb, L + 8, 2 * SSM_BC), F32),
            pltpu.VMEM((CHUNK, SSM_DI), F32),
            pltpu.VMEM((CHUNK, 2 * SSM_BC), F32),
        ],
        compiler_params=pltpu.CompilerParams(
            dimension_semantics=("parallel",), vmem_limit_bytes=VMEM_LIMIT),
        name="ssm_sample",
    )(proj, proj, proj, proj, conv_state, h_pairs, cwx, cbx, cwbc, cbbc, gbias, galog, dcols, nw)


def _outproj_kernel(x_ref, mg_ref, ms_ref, w_ref, o_ref):
    acc = _dot(mg_ref[...].astype(BF16), w_ref[:GDN_V, :])
    acc = acc + _dot(ms_ref[...].astype(BF16), w_ref[GDN_V:, :])
    o_ref[...] = x_ref[...] + acc


def _out_proj(x2d, mix_g, mix_s, w_out16, *, tm):
    T = x2d.shape[0]
    return pl.pallas_call(
        _outproj_kernel,
        grid=(T // tm,),
        in_specs=[
            pl.BlockSpec((tm, D_MODEL), lambda i: (i, 0)),
            pl.BlockSpec((tm, GDN_V), lambda i: (i, 0)),
            pl.BlockSpec((tm, SSM_DI), lambda i: (i, 0)),
            pl.BlockSpec((D_MODEL, D_MODEL), lambda i: (0, 0)),
        ],
        out_specs=pl.BlockSpec((tm, D_MODEL), lambda i: (i, 0)),
        out_shape=jax.ShapeDtypeStruct((T, D_MODEL), F32),
        compiler_params=pltpu.CompilerParams(
            dimension_semantics=("parallel",), vmem_limit_bytes=VMEM_LIMIT),
        name="out_proj",
    )(x2d, mix_g, mix_s, w_out16)


def _ffn_kernel(x_ref, nw_ref, wg_ref, wu_ref, wd_ref, fnw_ref, o_ref, h_ref, acc_ref):
    f = pl.program_id(1)

    @pl.when(f == 0)
    def _():
        h_ref[...] = _rms_rows(x_ref[...], nw_ref[...]).astype(BF16)
        acc_ref[...] = jnp.zeros(acc_ref.shape, F32)

    h = h_ref[...]
    a = (_silu(_dot(h, wg_ref[...])) * _dot(h, wu_ref[...])).astype(BF16)
    acc_ref[...] += _dot(a, wd_ref[...])

    @pl.when(f == pl.num_programs(1) - 1)
    def _():
        o_ref[...] = _rms_rows(x_ref[...] + acc_ref[...], fnw_ref[...])


def _ffn(x2d, norm_w, wg16, wu16, wd16, final_w, *, tm, tf):
    T = x2d.shape[0]
    return pl.pallas_call(
        _ffn_kernel,
        grid=(T // tm, D_FF // tf),
        in_specs=[
            pl.BlockSpec((tm, D_MODEL), lambda i, f: (i, 0)),
            pl.BlockSpec((1, D_MODEL), lambda i, f: (0, 0)),
            pl.BlockSpec((D_MODEL, tf), lambda i, f: (0, f)),
            pl.BlockSpec((D_MODEL, tf), lambda i, f: (0, f)),
            pl.BlockSpec((tf, D_MODEL), lambda i, f: (f, 0)),
            pl.BlockSpec((1, D_MODEL), lambda i, f: (0, 0)),
        ],
        out_specs=pl.BlockSpec((tm, D_MODEL), lambda i, f: (i, 0)),
        out_shape=jax.ShapeDtypeStruct((T, D_MODEL), F32),
        scratch_shapes=[pltpu.VMEM((tm, D_MODEL), BF16), pltpu.VMEM((tm, D_MODEL), F32)],
        compiler_params=pltpu.CompilerParams(
            dimension_semantics=("parallel", "arbitrary"), vmem_limit_bytes=VMEM_LIMIT),
        name="ffn",
    )(x2d, norm_w, wg16, wu16, wd16, final_w)


def _trunk(x, states, p):
    B, L, _ = x.shape
    x2d = x.reshape(B * L, D_MODEL)
    proj = _in_proj(x2d, p["attn_norm_w"], p["w_in_r"], tm=1024, tn=768)
    gdn_w = (p["gdn_conv_w"], p["gbias"], p["galog"], p["gdn_norm_w"])
    ssm_w = (p["cwx"], p["cbx"], p["cwbc"], p["cbbc"], p["gbias"], p["galog"], p["dcols"], p["ssm_norm_w"])
    pair_shape = (B, N_PAIRS, 2 * SSM_P, SSM_N)
    if states is None:
        mix_g, gconv_new, gS_new = _gdn_prompt(proj, *gdn_w, B=B, L=L, Lb=256)
        mix_s, sconv_new, sh_new = _ssm_prompt(proj, *ssm_w, B=B, L=L, Lb=256)
    else:
        gconv, gS, sconv, sh = states
        mix_g, gconv_new, gS_new = _gdn_sample(proj, gconv, gS, *gdn_w, B=B, L=L, nb=4)
        mix_s, sconv_new, sh_new = _ssm_sample(proj, sconv, sh.reshape(pair_shape), *ssm_w, B=B, L=L)
    sh_new = sh_new.reshape(B, SSM_HEADS, SSM_P, SSM_N)
    x1 = _out_proj(x2d, mix_g, mix_s, p["w_out16"], tm=512)
    y = _ffn(x1, p["ffn_norm_w"], p["wg16"], p["wu16"], p["wd16"], p["final_norm_w"], tm=512, tf=512)
    return y.reshape(B, L, D_MODEL), (gconv_new[None], gS_new[None], sconv_new[None], sh_new[None])


def kernel(x_prompt, x_sample, state_gdn_conv, state_gdn, state_ssm_conv, state_ssm,
           attn_norm_w, w_in, gdn_conv_w, gdn_A_log, gdn_dt_bias, gdn_norm_w,
           ssm_conv_w, ssm_conv_b, ssm_A_log, ssm_dt_bias, ssm_D, ssm_norm_w,
           w_out, ffn_norm_w, w_gate, w_up, w_down, final_norm_w):
    assert w_in.shape[0] == 1, "single-layer trunk"
    assert x_prompt.shape[1] % 256 == 0 and x_sample.shape[1] == 8 and x_sample.shape[0] % 8 == 0
    assert w_in.shape[2] == D_IN_PROJ
    w_in_r = _w_in_prep(jnp.swapaxes(w_in, 1, 2), tk=256)
    zeros8 = jnp.zeros((GDN_HEADS,), F32)
    tail = jnp.zeros((LANES - GATE_DT2 - SSM_HEADS,), F32)
    gbias = jnp.concatenate([zeros8, gdn_dt_bias[0], ssm_dt_bias[0], ssm_dt_bias[0], tail])[None]
    galog = jnp.concatenate([zeros8, gdn_A_log[0], ssm_A_log[0], ssm_A_log[0], tail])[None]
    p = dict(
        attn_norm_w=attn_norm_w, w_in_r=w_in_r, gdn_conv_w=gdn_conv_w[0], gbias=gbias, galog=galog,
        gdn_norm_w=gdn_norm_w,
        cwx=ssm_conv_w[0][:, :SSM_DI], cbx=ssm_conv_b[:, :SSM_DI],
        cwbc=ssm_conv_w[0][:, SSM_DI:], cbbc=ssm_conv_b[:, SSM_DI:],
        dcols=jnp.repeat(ssm_D[0], SSM_P)[None], ssm_norm_w=ssm_norm_w,
        w_out16=w_out[0].astype(BF16), ffn_norm_w=ffn_norm_w,
        wg16=w_gate[0].astype(BF16), wu16=w_up[0].astype(BF16), wd16=w_down[0].astype(BF16),
        final_norm_w=final_norm_w[None],
    )
    y_p, st_p = _trunk(x_prompt, None, p)
    y_s, st_s = _trunk(x_sample, (state_gdn_conv[0], state_gdn[0], state_ssm_conv[0], state_ssm[0]), p)
    return (y_p, y_s, st_p[0], st_p[1], st_p[2], st_p[3], st_s[0], st_s[1], st_s[2], st_s[3])
```

```python
import functools

import jax
import jax.numpy as jnp
from jax import lax
from jax.experimental import pallas as pl
from jax.experimental.pallas import tpu as pltpu

F32 = jnp.float32
BF16 = jnp.bfloat16

D_MODEL = 2048
GDN_HEADS = 8
GDN_DK = 128
GDN_DV = 128
GDN_QK = GDN_HEADS * GDN_DK
GDN_V = GDN_HEADS * GDN_DV
GDN_CONV_CH = 2 * GDN_QK + GDN_V
SSM_P = 64
SSM_N = 128
SSM_GROUPS = 2
SSM_DI = 1024
SSM_HEADS = SSM_DI // SSM_P
SSM_BC = SSM_GROUPS * SSM_N
SSM_CONV_CH = SSM_DI + 2 * SSM_BC
CONV_W = 4
CHUNK = 64
D_FF = 5632
EPS = 1e-6

OFF_QKV = 0
OFF_ZG = OFF_QKV + GDN_CONV_CH
OFF_ZS = OFF_ZG + GDN_V
OFF_XS = OFF_ZS + SSM_DI
OFF_BC = OFF_XS + SSM_DI
OFF_GATE = OFF_BC + 2 * SSM_BC
LANES = 128
GATE_B = 0
GATE_A = GATE_B + GDN_HEADS
GATE_DT = GATE_A + GDN_HEADS
GATE_DT2 = GATE_DT + SSM_HEADS
N_PROJ = 6912

VMEM_LIMIT = 52 * 1024 * 1024

_HIGHEST = lax.Precision.HIGHEST


def _silu(x):
    return x * jax.nn.sigmoid(x)


def _softplus(x):
    return jnp.maximum(x, 0.0) + jnp.log1p(jnp.exp(-jnp.abs(x)))


def _dot(a, b):
    return jnp.dot(a, b, preferred_element_type=F32)


def _dot_nt(a, b):
    return lax.dot_general(a, b, (((1,), (1,)), ((), ())), preferred_element_type=F32)


def _dot_tn(a, b):
    return lax.dot_general(a, b, (((0,), (0,)), ((), ())), preferred_element_type=F32)


def _transpose_rows(a):
    r = a.shape[0]
    if r < LANES:
        a = jnp.concatenate([a, jnp.zeros((LANES - r, LANES), a.dtype)], axis=0)
    return a.T


def _rms_rows(x, w):
    return x * lax.rsqrt(jnp.mean(x * x, axis=-1, keepdims=True) + EPS) * w


W_B = OFF_ZS
W_ZS = W_B + 2 * GDN_HEADS
W_DT = W_ZS + SSM_DI + SSM_CONV_CH
D_IN_PROJ = W_DT + SSM_HEADS


def _wprep_kernel(w_ref, o_ref):
    cols = o_ref.shape[1]
    o_ref[:W_B, :] = w_ref[0, :W_B, :].astype(BF16)
    o_ref[W_B:OFF_GATE, :] = w_ref[0, W_ZS:W_DT, :].astype(BF16)
    dt = w_ref[0, W_DT:D_IN_PROJ, :].astype(BF16)
    o_ref[OFF_GATE:OFF_GATE + GATE_DT, :] = w_ref[0, W_B:W_ZS, :].astype(BF16)
    o_ref[OFF_GATE + GATE_DT:OFF_GATE + GATE_DT2, :] = dt
    o_ref[OFF_GATE + GATE_DT2:OFF_GATE + GATE_DT2 + SSM_HEADS, :] = dt
    o_ref[OFF_GATE + GATE_DT2 + SSM_HEADS:, :] = jnp.zeros((N_PROJ - OFF_GATE - GATE_DT2 - SSM_HEADS, cols), BF16)


def _w_in_prep(w_in_t, *, tk):
    return pl.pallas_call(
        _wprep_kernel,
        grid=(D_MODEL // tk,),
        in_specs=[pl.BlockSpec((1, D_IN_PROJ, tk), lambda i: (0, 0, i))],
        out_specs=pl.BlockSpec((N_PROJ, tk), lambda i: (0, i)),
        out_shape=jax.ShapeDtypeStruct((N_PROJ, D_MODEL), BF16),
        compiler_params=pltpu.CompilerParams(
            dimension_semantics=("parallel",), vmem_limit_bytes=VMEM_LIMIT),
        name="w_in_prep",
    )(w_in_t)


def _inproj_kernel(x_ref, nw_ref, wt_ref, o_ref, h_ref):
    @pl.when(pl.program_id(1) == 0)
    def _():
        h_ref[...] = _rms_rows(x_ref[...], nw_ref[...]).astype(BF16)

    o_ref[...] = _dot_nt(h_ref[...], wt_ref[...])


def _in_proj(x2d, norm_w, w_in_r, *, tm, tn):
    T = x2d.shape[0]
    return pl.pallas_call(
        _inproj_kernel,
        grid=(T // tm, N_PROJ // tn),
        in_specs=[
            pl.BlockSpec((tm, D_MODEL), lambda i, j: (i, 0)),
            pl.BlockSpec((1, D_MODEL), lambda i, j: (0, 0)),
            pl.BlockSpec((tn, D_MODEL), lambda i, j: (j, 0)),
        ],
        out_specs=pl.BlockSpec((tm, tn), lambda i, j: (i, j)),
        out_shape=jax.ShapeDtypeStruct((T, N_PROJ), F32),
        scratch_shapes=[pltpu.VMEM((tm, D_MODEL), BF16)],
        compiler_params=pltpu.CompilerParams(
            dimension_semantics=("parallel", "arbitrary"), vmem_limit_bytes=VMEM_LIMIT),
        name="in_proj",
    )(x2d, norm_w, w_in_r)


def _conv_block(x_ref, xpad_ref, cw_ref, cb_ref, dst_ref, Lb, C, post):
    xpad_ref[8:8 + Lb, :] = x_ref[...]
    rs = min(Lb, CHUNK)
    for sb in range(Lb // rs):
        r = sb * rs
        for s in range(C // LANES):
            cols = slice(s * LANES, (s + 1) * LANES)
            acc = xpad_ref[8 + r:8 + r + rs, cols] * cw_ref[3:4, cols]
            for i in range(CONV_W - 1):
                acc = acc + xpad_ref[5 + i + r:5 + i + r + rs, cols] * cw_ref[i:i + 1, cols]
            if cb_ref is not None:
                acc = acc + cb_ref[:, cols]
            dst_ref[r:r + rs, cols] = post(s, _silu(acc))
    last3 = xpad_ref[Lb + 5:Lb + 8, :]
    xpad_ref[5:8, :] = last3
    return last3


STACK = 128


def _gdn_qk_post(s, y):
    if s < 2 * GDN_HEADS:
        y = y * lax.rsqrt(jnp.sum(y * y, axis=-1, keepdims=True) + EPS)
        if s < GDN_HEADS:
            y = y * (GDN_DK ** -0.5)
    return y


def _gdn_local(items, glen):
    sh = glen.bit_length() - 1
    row = lax.broadcasted_iota(jnp.int32, (STACK, STACK), 0)
    col = lax.broadcasted_iota(jnp.int32, (STACK, STACK), 1)
    same = (row >> sh) == (col >> sh)
    incl = same & (row >= col)
    strict = same & (row > col)
    eye = (row == col).astype(F32)

    decay = [jnp.exp(jnp.where(incl, it["g"] - it["g"].T, -jnp.inf)) for it in items]
    kb = [it["k"] * it["beta"] for it in items]
    qkk = [_dot_nt(jnp.concatenate([it["q"], b], axis=0).astype(BF16), it["k"].astype(BF16))
           for it, b in zip(items, kb)]
    qk = [x[:STACK] * d for x, d in zip(qkk, decay)]
    nmat = [jnp.where(strict, -(x[STACK:] * d), 0.0) for x, d in zip(qkk, decay)]
    tinv = [eye + n for n in nmat]
    if sh >= 2:
        pw = [_dot(n.astype(BF16), n.astype(BF16)) for n in nmat]
        for _ in range(sh - 2):
            x = [_dot(jnp.concatenate([t, p], axis=0).astype(BF16), p.astype(BF16))
                 for t, p in zip(tinv, pw)]
            tinv = [t + y[:STACK] for t, y in zip(tinv, x)]
            pw = [y[STACK:] for y in x]
        tinv = [t + _dot(t.astype(BF16), p.astype(BF16)) for t, p in zip(tinv, pw)]
    uw = [_dot(t.astype(BF16),
               jnp.concatenate([it["v"] * it["beta"], b * it["eg"]], axis=1).astype(BF16))
          for t, it, b in zip(tinv, items, kb)]
    return [(x[:, :GDN_DV], x[:, GDN_DV:]) for x in uw], qk


GDN_LOCAL_CHUNKS = 2


def _gdn_prompt_kernel(qkv_ref, zg_ref, gate_ref, cw_ref, gbias_ref, galog_ref, nw_ref,
                       o_ref, cst_out_ref, sst_out_ref,
                       xpad_ref, qkvc_ref, s_ref, u_ref, wq16_ref, kd16_ref, qk16_ref, egl_ref, *, Lb):
    c = CHUNK
    l = pl.program_id(1)
    nl = pl.num_programs(1)

    @pl.when(l == 0)
    def _init():
        xpad_ref[0:8, :] = jnp.zeros((8, GDN_CONV_CH), F32)
        s_ref[...] = jnp.zeros(s_ref.shape, F32)

    _conv_block(qkv_ref, xpad_ref, cw_ref, None, qkvc_ref, Lb, GDN_CONV_CH, _gdn_qk_post)

    row_i = lax.broadcasted_iota(jnp.int32, (c, c), 0)
    col_i = lax.broadcasted_iota(jnp.int32, (c, c), 1)
    tril_f = (row_i >= col_i).astype(F32)
    gbias = gbias_ref[...]
    nega = -jnp.exp(galog_ref[...])
    nw = nw_ref[...]
    n_pairs = GDN_HEADS // 2

    def local_chunk_items(ci):
        rows = slice(ci * c, (ci + 1) * c)
        graw = gate_ref[rows, :]
        sp = _softplus(graw + gbias)
        beta_all = jax.nn.sigmoid(graw)
        G = jnp.dot(tril_f, nega * sp, precision=_HIGHEST, preferred_element_type=F32)
        eG = jnp.exp(G)
        glast = G[c - 1:c, :]
        eGrev = jnp.exp(glast - G)
        egl_ref[ci] = jnp.broadcast_to(jnp.exp(glast), (8, LANES))

        def heads(off, a, b):
            return jnp.concatenate([qkvc_ref[rows, off + a * LANES:off + (a + 1) * LANES],
                                    qkvc_ref[rows, off + b * LANES:off + (b + 1) * LANES]], axis=0)

        def colstack(m, a, b):
            return jnp.concatenate([jnp.broadcast_to(m[:, a:a + 1], (c, LANES)),
                                    jnp.broadcast_to(m[:, b:b + 1], (c, LANES))], axis=0)

        items = []
        for pr in range(n_pairs):
            a, b = 2 * pr, 2 * pr + 1
            items.append(dict(
                q=heads(0, a, b), k=heads(GDN_QK, a, b), v=heads(2 * GDN_QK, a, b),
                beta=colstack(beta_all, GATE_B + a, GATE_B + b),
                g=colstack(G, GATE_A + a, GATE_A + b),
                eg=colstack(eG, GATE_A + a, GATE_A + b),
                egrev=colstack(eGrev, GATE_A + a, GATE_A + b)))
        return items

    def local_group(cis):
        items = [it for ci in cis for it in local_chunk_items(ci)]
        uw, qk = _gdn_local(items, c)
        for n, it in enumerate(items):
            idx = cis[0] * n_pairs + n
            u, w = uw[n]
            qd = it["q"] * it["eg"]
            u_ref[idx] = u
            for hh in range(2):
                hs = slice(hh * c, (hh + 1) * c)
                wq16_ref[2 * idx + hh] = jnp.concatenate([w[hs], qd[hs]], axis=0).astype(BF16)
            kd16_ref[idx] = (it["k"] * it["egrev"]).astype(BF16)
            qk16_ref[idx] = qk[n].astype(BF16)

    def recurrent(ci):
        rows = slice(ci * c, (ci + 1) * c)
        egl = egl_ref[ci][0:1]
        r = [[_dot(wq16_ref[2 * (ci * n_pairs + pr) + hh], s_ref[2 * pr + hh].astype(BF16)) for hh in range(2)]
             for pr in range(n_pairs)]
        v16 = [(u_ref[ci * n_pairs + pr] - jnp.concatenate([r[pr][0][:c], r[pr][1][:c]], axis=0)).astype(BF16)
               for pr in range(n_pairs)]
        o = [jnp.concatenate([r[pr][0][c:], r[pr][1][c:]], axis=0) + _dot(qk16_ref[ci * n_pairs + pr], v16[pr])
             for pr in range(n_pairs)]
        for pr in range(n_pairs):
            kd16 = kd16_ref[ci * n_pairs + pr]
            for hh in range(2):
                h = 2 * pr + hh
                ga = GATE_A + h
                hs = slice(hh * c, (hh + 1) * c)
                s_ref[h] = s_ref[h] * egl[:, ga:ga + 1] + _dot_tn(kd16[hs], v16[pr][hs])
        for pr in range(n_pairs):
            for hh in range(2):
                h = 2 * pr + hh
                z = zg_ref[rows, h * GDN_DV:(h + 1) * GDN_DV]
                o_ref[rows, h * GDN_DV:(h + 1) * GDN_DV] = (
                    _rms_rows(o[pr][hh * c:(hh + 1) * c], nw) * _silu(z)).astype(o_ref.dtype)

    for g0 in range(0, Lb // c, GDN_LOCAL_CHUNKS):
        cis = list(range(g0, g0 + GDN_LOCAL_CHUNKS))
        local_group(cis)
        for ci in cis:
            recurrent(ci)

    @pl.when(l == nl - 1)
    def _final():
        cst_out_ref[0] = xpad_ref[5:8, :]
        sst_out_ref[0] = s_ref[...]


def _gdn_prompt(proj, cw, gbias, galog, nw, *, B, L, Lb):
    nl = L // Lb
    n_tiles = (Lb // CHUNK) * (GDN_HEADS // 2)
    row = lambda b, l: b * nl + l
    const = lambda shape: pl.BlockSpec(shape, lambda b, l: (0,) * len(shape))
    return pl.pallas_call(
        functools.partial(_gdn_prompt_kernel, Lb=Lb),
        grid=(B, nl),
        in_specs=[
            pl.BlockSpec((Lb, GDN_CONV_CH), lambda b, l: (row(b, l), OFF_QKV // GDN_CONV_CH)),
            pl.BlockSpec((Lb, GDN_V), lambda b, l: (row(b, l), OFF_ZG // GDN_V)),
            pl.BlockSpec((Lb, LANES), lambda b, l: (row(b, l), OFF_GATE // LANES)),
            const((CONV_W, GDN_CONV_CH)), const((1, LANES)), const((1, LANES)), const((1, GDN_DV)),
        ],
        out_specs=[
            pl.BlockSpec((Lb, GDN_V), lambda b, l: (row(b, l), 0)),
            pl.BlockSpec((1, CONV_W - 1, GDN_CONV_CH), lambda b, l: (b, 0, 0)),
            pl.BlockSpec((1, GDN_HEADS, GDN_DK, GDN_DV), lambda b, l: (b, 0, 0, 0)),
        ],
        out_shape=[
            jax.ShapeDtypeStruct((B * L, GDN_V), BF16),
            jax.ShapeDtypeStruct((B, CONV_W - 1, GDN_CONV_CH), F32),
            jax.ShapeDtypeStruct((B, GDN_HEADS, GDN_DK, GDN_DV), F32),
        ],
        scratch_shapes=[
            pltpu.VMEM((Lb + 8, GDN_CONV_CH), F32),
            pltpu.VMEM((Lb, GDN_CONV_CH), F32),
            pltpu.VMEM((GDN_HEADS, GDN_DK, GDN_DV), F32),
            pltpu.VMEM((n_tiles, STACK, GDN_DV), F32),
            pltpu.VMEM((2 * n_tiles, STACK, GDN_DK), BF16),
            pltpu.VMEM((n_tiles, STACK, GDN_DK), BF16),
            pltpu.VMEM((n_tiles, STACK, STACK), BF16),
            pltpu.VMEM((Lb // CHUNK, 8, LANES), F32),
        ],
        compiler_params=pltpu.CompilerParams(
            dimension_semantics=("parallel", "arbitrary"), vmem_limit_bytes=VMEM_LIMIT),
        name="gdn_prompt",
    )(proj, proj, proj, cw, gbias, galog, nw)


def _gdn_sample_kernel(qkv_ref, zg_ref, gate_ref, cst_ref, sst_ref, cw_ref, gbias_ref, galog_ref, nw_ref,
                       o_ref, cst_out_ref, sst_out_ref, xpad_ref, qkvc_ref, *, nb, L):
    R = nb * L
    sh = L.bit_length() - 1
    for bi in range(nb):
        xp = xpad_ref.at[bi]
        xp[5:8, :] = cst_ref[bi]
        cst_out_ref[bi] = _conv_block(
            qkv_ref.at[pl.ds(bi * L, L)], xp, cw_ref, None, qkvc_ref.at[pl.ds(bi * L, L)],
            L, GDN_CONV_CH, _gdn_qk_post)

    row_i = lax.broadcasted_iota(jnp.int32, (R, R), 0)
    col_i = lax.broadcasted_iota(jnp.int32, (R, R), 1)
    tril_f = (((row_i >> sh) == (col_i >> sh)) & (row_i >= col_i)).astype(F32)
    graw = gate_ref[...]
    sp = _softplus(graw + gbias_ref[...])
    beta_all = jax.nn.sigmoid(graw)
    G = jnp.dot(tril_f, -jnp.exp(galog_ref[...]) * sp, precision=_HIGHEST, preferred_element_type=F32)
    glast = [G[bi * L + L - 1:bi * L + L, :] for bi in range(nb)]
    eG = jnp.exp(G)
    eGrev = jnp.exp(jnp.concatenate([jnp.broadcast_to(x, (L, LANES)) for x in glast], axis=0) - G)
    egl = [jnp.exp(x) for x in glast]
    nw = nw_ref[...]

    n_st = R // (2 * L)

    def tiles(ref, st, off):
        return jnp.concatenate(
            [ref[st * 2 * L:(st + 1) * 2 * L, off + h * LANES:off + (h + 1) * LANES]
             for h in range(GDN_HEADS)], axis=0)

    def colstack(m, st, off):
        return jnp.concatenate(
            [jnp.broadcast_to(m[st * 2 * L:(st + 1) * 2 * L, off + h:off + h + 1], (2 * L, LANES))
             for h in range(GDN_HEADS)], axis=0)

    items = [dict(q=tiles(qkvc_ref, st, 0), k=tiles(qkvc_ref, st, GDN_QK), v=tiles(qkvc_ref, st, 2 * GDN_QK),
                  beta=colstack(beta_all, st, GATE_B), g=colstack(G, st, GATE_A),
                  eg=colstack(eG, st, GATE_A), egrev=colstack(eGrev, st, GATE_A))
             for st in range(n_st)]
    uw, qk = _gdn_local(items, L)

    groups = [(h, bi) for h in range(GDN_HEADS) for bi in range(2)]
    r = []
    for st, it in enumerate(items):
        w = uw[st][1]
        qd = it["q"] * it["eg"]
        r.append([
            _dot(jnp.concatenate([w[gi * L:(gi + 1) * L], qd[gi * L:(gi + 1) * L]], axis=0).astype(BF16),
                 sst_ref[2 * st + bi, h].astype(BF16))
            for gi, (h, bi) in enumerate(groups)])
    v_new = [uw[st][0] - jnp.concatenate([x[:L] for x in r[st]], axis=0) for st in range(n_st)]
    o = [jnp.concatenate([x[L:] for x in r[st]], axis=0)
         + _dot(qk[st].astype(BF16), v_new[st].astype(BF16)) for st in range(n_st)]
    for st, it in enumerate(items):
        kd = it["k"] * it["egrev"]
        for gi, (h, bi) in enumerate(groups):
            b = 2 * st + bi
            ga = GATE_A + h
            rs = slice(gi * L, (gi + 1) * L)
            sst_out_ref[b, h] = (sst_ref[b, h] * egl[b][:, ga:ga + 1]
                                 + _dot_tn(kd[rs].astype(BF16), v_new[st][rs].astype(BF16)))
    for st in range(n_st):
        out = (_rms_rows(o[st], nw) * _silu(tiles(zg_ref, st, 0))).astype(o_ref.dtype)
        for h in range(GDN_HEADS):
            o_ref[st * 2 * L:(st + 1) * 2 * L, h * GDN_DV:(h + 1) * GDN_DV] = out[h * 2 * L:(h + 1) * 2 * L]


def _gdn_sample(proj, conv_state, S_state, cw, gbias, galog, nw, *, B, L, nb):
    R = nb * L
    const = lambda shape: pl.BlockSpec(shape, lambda i: (0,) * len(shape))
    return pl.pallas_call(
        functools.partial(_gdn_sample_kernel, nb=nb, L=L),
        grid=(B // nb,),
        in_specs=[
            pl.BlockSpec((R, GDN_CONV_CH), lambda i: (i, OFF_QKV // GDN_CONV_CH)),
            pl.BlockSpec((R, GDN_V), lambda i: (i, OFF_ZG // GDN_V)),
            pl.BlockSpec((R, LANES), lambda i: (i, OFF_GATE // LANES)),
            pl.BlockSpec((nb, CONV_W - 1, GDN_CONV_CH), lambda i: (i, 0, 0)),
            pl.BlockSpec((nb, GDN_HEADS, GDN_DK, GDN_DV), lambda i: (i, 0, 0, 0)),
            const((CONV_W, GDN_CONV_CH)), const((1, LANES)), const((1, LANES)), const((1, GDN_DV)),
        ],
        out_specs=[
            pl.BlockSpec((R, GDN_V), lambda i: (i, 0)),
            pl.BlockSpec((nb, CONV_W - 1, GDN_CONV_CH), lambda i: (i, 0, 0)),
            pl.BlockSpec((nb, GDN_HEADS, GDN_DK, GDN_DV), lambda i: (i, 0, 0, 0)),
        ],
        out_shape=[
            jax.ShapeDtypeStruct((B * L, GDN_V), BF16),
            jax.ShapeDtypeStruct((B, CONV_W - 1, GDN_CONV_CH), F32),
            jax.ShapeDtypeStruct((B, GDN_HEADS, GDN_DK, GDN_DV), F32),
        ],
        scratch_shapes=[
            pltpu.VMEM((nb, L + 8, GDN_CONV_CH), F32),
            pltpu.VMEM((R, GDN_CONV_CH), F32),
        ],
        compiler_params=pltpu.CompilerParams(
            dimension_semantics=("parallel",), vmem_limit_bytes=VMEM_LIMIT),
        name="gdn_sample",
    )(proj, proj, proj, conv_state, S_state, cw, gbias, galog, nw)


N_PAIRS = SSM_HEADS // 2
PAIRS_PER_GROUP = N_PAIRS // SSM_GROUPS
GROUP_W = SSM_DI // SSM_GROUPS


def _ssm_tile(graw, gbias, nega, ld_x, ld_b, ld_c, ld_z, dcols_ref, nw_ref, st_o, get_h, set_h, glen):
    c = CHUNK
    P = SSM_P
    nseq = c // glen
    sh = glen.bit_length() - 1
    ri = lax.broadcasted_iota(jnp.int32, (c, c), 0)
    ci = lax.broadcasted_iota(jnp.int32, (c, c), 1)
    tril_f = (((ri >> sh) == (ci >> sh)) & (ri >= ci)).astype(F32)
    sp = _softplus(graw + gbias)
    acum = jnp.dot(tril_f, nega * sp, precision=_HIGHEST, preferred_element_type=F32)
    lasts = [acum[s * glen + glen - 1:(s + 1) * glen, :] for s in range(nseq)]
    alast = jnp.concatenate([jnp.broadcast_to(x, (glen, LANES)) for x in lasts], axis=0)
    dtrev = sp * jnp.exp(alast - acum)
    eal = [jnp.exp(x) for x in lasts]
    lane = lax.broadcasted_iota(jnp.int32, (c, LANES), 1)
    row = lax.broadcasted_iota(jnp.int32, (c, LANES), 0)
    m = jnp.where(lane < GATE_DT2, acum, sp)
    mt = jnp.concatenate([m, m], axis=0).T
    left = lane < P
    left_row = left[0:1]
    j = jnp.where(left, lane, lane - P)
    tril2 = ((row >> sh) == (j >> sh)) & (row >= j)
    rowh = lax.broadcasted_iota(jnp.int32, (2 * P, SSM_N), 0) < P

    def expand(mat, c0):
        return jnp.where(left, jnp.broadcast_to(mat[:, c0:c0 + 1], (c, LANES)),
                         jnp.broadcast_to(mat[:, c0 + 1:c0 + 2], (c, LANES)))

    def rowsel(base, e):
        return jnp.where(left_row, mt[base + 2 * e:base + 2 * e + 1, :], mt[base + 2 * e + 1:base + 2 * e + 2, :])

    for g in range(SSM_GROUPS):
        Bg = ld_b(g)
        Cg = ld_c(g)
        Bg16 = Bg.astype(BF16)
        Cg16 = Cg.astype(BF16)
        cb2 = _dot_nt(Cg16, jnp.concatenate([Bg16, Bg16], axis=0))
        pairs = [g * PAIRS_PER_GROUP + e4 for e4 in range(PAIRS_PER_GROUP)]
        acol = [expand(acum, GATE_DT + 2 * e) for e in pairs]
        scores16 = [
            (cb2 * jnp.exp(jnp.where(tril2, a - rowsel(GATE_DT, e), -jnp.inf)) * rowsel(GATE_DT2, e)).astype(BF16)
            for a, e in zip(acol, pairs)]
        xp = [ld_x(e) for e in pairs]
        bd16 = [jnp.concatenate([jnp.where(left, x, 0.0), jnp.where(left, 0.0, x)], axis=0).astype(BF16)
                for x in xp]
        ydiag = [_dot(s, b) for s, b in zip(scores16, bd16)]
        if nseq == 1:
            yoff = [_dot_nt(Cg16, get_h(0, e).astype(BF16)) for e in pairs]
        else:
            yoff = [jnp.concatenate(
                [_dot_nt(Cg[s * glen:(s + 1) * glen].astype(BF16), get_h(s, e).astype(BF16))
                 for s in range(nseq)], axis=0) for e in pairs]
        y = [yd + yo * jnp.exp(a) + dcols_ref[:, e * 2 * P:(e + 1) * 2 * P] * x
             for yd, yo, a, e, x in zip(ydiag, yoff, acol, pairs, xp)]
        xdr = [x * expand(dtrev, GATE_DT + 2 * e) for x, e in zip(xp, pairs)]
        for e, xd in zip(pairs, xdr):
            c0 = GATE_DT + 2 * e
            for s in range(nseq):
                rs = slice(s * glen, (s + 1) * glen)
                ealcol = jnp.where(rowh, eal[s][:, c0:c0 + 1], eal[s][:, c0 + 1:c0 + 2])
                set_h(s, e, get_h(s, e) * ealcol + _dot_tn(xd[rs].astype(BF16), Bg[rs].astype(BF16)))
        yg = jnp.concatenate(y, axis=1) * _silu(ld_z(g))
        gcols = slice(g * GROUP_W, (g + 1) * GROUP_W)
        st_o(g, _rms_rows(yg, nw_ref[:, gcols]))


def _ssm_prompt_kernel(xs_ref, bc_ref, zs_ref, gate_ref, cwx_ref, cbx_ref, cwbc_ref, cbbc_ref,
                       gbias_ref, galog_ref, dcols_ref, nw_ref,
                       o_ref, cst_out_ref, hst_out_ref, xpadx_ref, xpadbc_ref, xc_ref, bcc_ref, hh_ref, *, Lb):
    c = CHUNK
    l = pl.program_id(1)
    nl = pl.num_programs(1)

    @pl.when(l == 0)
    def _init():
        hh_ref[...] = jnp.zeros(hh_ref.shape, F32)
        xpadx_ref[0:8, :] = jnp.zeros((8, SSM_DI), F32)
        xpadbc_ref[0:8, :] = jnp.zeros((8, 2 * SSM_BC), F32)

    ident = lambda s, y: y
    _conv_block(xs_ref, xpadx_ref, cwx_ref, cbx_ref, xc_ref, Lb, SSM_DI, ident)
    _conv_block(bc_ref, xpadbc_ref, cwbc_ref, cbbc_ref, bcc_ref, Lb, 2 * SSM_BC, ident)

    gbias = gbias_ref[...]
    nega = -jnp.exp(galog_ref[...])

    def set_h(s, e, val):
        hh_ref[e] = val

    for ci in range(Lb // c):
        rows = slice(ci * c, (ci + 1) * c)

        def st_o(g, val, rows=rows):
            o_ref[rows, g * GROUP_W:(g + 1) * GROUP_W] = val.astype(o_ref.dtype)

        _ssm_tile(
            gate_ref[rows, :], gbias, nega,
            lambda e, rows=rows: xc_ref[rows, e * LANES:(e + 1) * LANES],
            lambda g, rows=rows: bcc_ref[rows, g * SSM_N:(g + 1) * SSM_N],
            lambda g, rows=rows: bcc_ref[rows, SSM_BC + g * SSM_N:SSM_BC + (g + 1) * SSM_N],
            lambda g, rows=rows: zs_ref[rows, g * GROUP_W:(g + 1) * GROUP_W],
            dcols_ref, nw_ref, st_o, lambda s, e: hh_ref[e], set_h, c)

    @pl.when(l == nl - 1)
    def _final():
        cst_out_ref[0, :, :SSM_DI] = xpadx_ref[5:8, :]
        cst_out_ref[0, :, SSM_DI:] = xpadbc_ref[5:8, :]
        hst_out_ref[0] = hh_ref[...]


def _ssm_sample_kernel(xs_ref, bc_ref, zs_ref, gate_ref, cst_ref, hst_ref, cwx_ref, cbx_ref, cwbc_ref, cbbc_ref,
                       gbias_ref, galog_ref, dcols_ref, nw_ref,
                       o_ref, cst_out_ref, hst_out_ref, xpadx_ref, xpadbc_ref, xc_ref, bcc_ref, *, L):
    ident = lambda s, y: y
    for bi in range(CHUNK // L):
        rs = pl.ds(bi * L, L)
        xpx = xpadx_ref.at[bi]
        xpb = xpadbc_ref.at[bi]
        xpx[5:8, :] = cst_ref[bi, :, :SSM_DI]
        xpb[5:8, :] = cst_ref[bi, :, SSM_DI:]
        cst_out_ref[bi, :, :SSM_DI] = _conv_block(
            xs_ref.at[rs], xpx, cwx_ref, cbx_ref, xc_ref.at[rs], L, SSM_DI, ident)
        cst_out_ref[bi, :, SSM_DI:] = _conv_block(
            bc_ref.at[rs], xpb, cwbc_ref, cbbc_ref, bcc_ref.at[rs], L, 2 * SSM_BC, ident)

    def st_o(g, val):
        o_ref[:, g * GROUP_W:(g + 1) * GROUP_W] = val.astype(o_ref.dtype)

    def set_h(s, e, val):
        hst_out_ref[s, e] = val

    _ssm_tile(
        gate_ref[...], gbias_ref[...], -jnp.exp(galog_ref[...]),
        lambda e: xc_ref[:, e * LANES:(e + 1) * LANES],
        lambda g: bcc_ref[:, g * SSM_N:(g + 1) * SSM_N],
        lambda g: bcc_ref[:, SSM_BC + g * SSM_N:SSM_BC + (g + 1) * SSM_N],
        lambda g: zs_ref[:, g * GROUP_W:(g + 1) * GROUP_W],
        dcols_ref, nw_ref, st_o, lambda s, e: hst_ref[s, e], set_h, L)


def _ssm_const_specs():
    const = lambda shape: pl.BlockSpec(shape, lambda *idx: (0,) * len(shape))
    return [
        const((CONV_W, SSM_DI)), const((1, SSM_DI)), const((CONV_W, 2 * SSM_BC)), const((1, 2 * SSM_BC)),
        const((1, LANES)), const((1, LANES)), const((1, SSM_DI)), const((1, SSM_DI)),
    ]


def _ssm_prompt(proj, cwx, cbx, cwbc, cbbc, gbias, galog, dcols, nw, *, B, L, Lb):
    nl = L // Lb
    row = lambda b, l: b * nl + l
    return pl.pallas_call(
        functools.partial(_ssm_prompt_kernel, Lb=Lb),
        grid=(B, nl),
        in_specs=[
            pl.BlockSpec((Lb, SSM_DI), lambda b, l: (row(b, l), OFF_XS // SSM_DI)),
            pl.BlockSpec((Lb, 2 * SSM_BC), lambda b, l: (row(b, l), OFF_BC // (2 * SSM_BC))),
            pl.BlockSpec((Lb, SSM_DI), lambda b, l: (row(b, l), OFF_ZS // SSM_DI)),
            pl.BlockSpec((Lb, LANES), lambda b, l: (row(b, l), OFF_GATE // LANES)),
        ] + _ssm_const_specs(),
        out_specs=[
            pl.BlockSpec((Lb, SSM_DI), lambda b, l: (row(b, l), 0)),
            pl.BlockSpec((1, CONV_W - 1, SSM_CONV_CH), lambda b, l: (b, 0, 0)),
            pl.BlockSpec((1, N_PAIRS, 2 * SSM_P, SSM_N), lambda b, l: (b, 0, 0, 0)),
        ],
        out_shape=[
            jax.ShapeDtypeStruct((B * L, SSM_DI), BF16),
            jax.ShapeDtypeStruct((B, CONV_W - 1, SSM_CONV_CH), F32),
            jax.ShapeDtypeStruct((B, N_PAIRS, 2 * SSM_P, SSM_N), F32),
        ],
        scratch_shapes=[
            pltpu.VMEM((Lb + 8, SSM_DI), F32),
            pltpu.VMEM((Lb + 8, 2 * SSM_BC), F32),
            pltpu.VMEM((Lb, SSM_DI), F32),
            pltpu.VMEM((Lb, 2 * SSM_BC), F32),
            pltpu.VMEM((N_PAIRS, 2 * SSM_P, SSM_N), F32),
        ],
        compiler_params=pltpu.CompilerParams(
            dimension_semantics=("parallel", "arbitrary"), vmem_limit_bytes=VMEM_LIMIT),
        name="ssm_prompt",
    )(proj, proj, proj, proj, cwx, cbx, cwbc, cbbc, gbias, galog, dcols, nw)


def _ssm_sample(proj, conv_state, h_pairs, cwx, cbx, cwbc, cbbc, gbias, galog, dcols, nw, *, B, L):
    nb = CHUNK // L
    return pl.pallas_call(
        functools.partial(_ssm_sample_kernel, L=L),
        grid=(B // nb,),
        in_specs=[
            pl.BlockSpec((CHUNK, SSM_DI), lambda i: (i, OFF_XS // SSM_DI)),
            pl.BlockSpec((CHUNK, 2 * SSM_BC), lambda i: (i, OFF_BC // (2 * SSM_BC))),
            pl.BlockSpec((CHUNK, SSM_DI), lambda i: (i, OFF_ZS // SSM_DI)),
            pl.BlockSpec((CHUNK, LANES), lambda i: (i, OFF_GATE // LANES)),
            pl.BlockSpec((nb, CONV_W - 1, SSM_CONV_CH), lambda i: (i, 0, 0)),
            pl.BlockSpec((nb, N_PAIRS, 2 * SSM_P, SSM_N), lambda i: (i, 0, 0, 0)),
        ] + _ssm_const_specs(),
        out_specs=[
            pl.BlockSpec((CHUNK, SSM_DI), lambda i: (i, 0)),
            pl.BlockSpec((nb, CONV_W - 1, SSM_CONV_CH), lambda i: (i, 0, 0)),
            pl.BlockSpec((nb, N_PAIRS, 2 * SSM_P, SSM_N), lambda i: (i, 0, 0, 0)),
        ],
        out_shape=[
            jax.ShapeDtypeStruct((B * L, SSM_DI), BF16),
            jax.ShapeDtypeStruct((B, CONV_W - 1, SSM_CONV_CH), F32),
            jax.ShapeDtypeStruct((B, N_PAIRS, 2 * SSM_P, SSM_N), F32),
        ],
        scratch_shapes=[
            pltpu.VMEM((nb, L + 8, SSM_DI), F32),
            pltpu.VMEM((nb, L + 8, 2 * SSM_BC), F32),
            pltpu.VMEM((CHUNK, SSM_DI), F32),
            pltpu.VMEM((CHUNK, 2 * SSM_BC), F32),
        ],
        compiler_params=pltpu.CompilerParams(
            dimension_semantics=("parallel",), vmem_limit_bytes=VMEM_LIMIT),
        name="ssm_sample",
    )(proj, proj, proj, proj, conv_state, h_pairs, cwx, cbx, cwbc, cbbc, gbias, galog, dcols, nw)


def _outproj_kernel(x_ref, mg_ref, ms_ref, w_ref, o_ref):
    acc = _dot(mg_ref[...].astype(BF16), w_ref[:GDN_V, :])
    acc = acc + _dot(ms_ref[...].astype(BF16), w_ref[GDN_V:, :])
    o_ref[...] = x_ref[...] + acc


def _out_proj(x2d, mix_g, mix_s, w_out16, *, tm):
    T = x2d.shape[0]
    return pl.pallas_call(
        _outproj_kernel,
        grid=(T // tm,),
        in_specs=[
            pl.BlockSpec((tm, D_MODEL), lambda i: (i, 0)),
            pl.BlockSpec((tm, GDN_V), lambda i: (i, 0)),
            pl.BlockSpec((tm, SSM_DI), lambda i: (i, 0)),
            pl.BlockSpec((D_MODEL, D_MODEL), lambda i: (0, 0)),
        ],
        out_specs=pl.BlockSpec((tm, D_MODEL), lambda i: (i, 0)),
        out_shape=jax.ShapeDtypeStruct((T, D_MODEL), F32),
        compiler_params=pltpu.CompilerParams(
            dimension_semantics=("parallel",), vmem_limit_bytes=VMEM_LIMIT),
        name="out_proj",
    )(x2d, mix_g, mix_s, w_out16)


FFN_SUB = 4


def _ffn_kernel(x_ref, nw_ref, wg_ref, wu_ref, wd_ref, fnw_ref, o_ref, h_ref, acc_ref):
    f = pl.program_id(1)
    nf = pl.num_programs(1)
    rs = h_ref.shape[0] // FFN_SUB

    def step(first, last):
        def gate_up(r):
            rows = slice(r * rs, (r + 1) * rs)
            if first:
                h = _rms_rows(x_ref[rows, :], nw_ref[...]).astype(BF16)
                h_ref[rows, :] = h
            else:
                h = h_ref[rows, :]
            return _dot(h, wg_ref[...]), _dot(h, wu_ref[...])

        def down(r, gu):
            rows = slice(r * rs, (r + 1) * rs)
            d = _dot((_silu(gu[0]) * gu[1]).astype(BF16), wd_ref[...])
            acc = d if first else acc_ref[rows, :] + d
            if last:
                o_ref[rows, :] = _rms_rows(x_ref[rows, :] + acc, fnw_ref[...])
            else:
                acc_ref[rows, :] = acc

        gu = gate_up(0)
        for r in range(1, FFN_SUB):
            gu_next = gate_up(r)
            down(r - 1, gu)
            gu = gu_next
        down(FFN_SUB - 1, gu)

    pl.when(f == 0)(lambda: step(True, False))
    pl.when((f > 0) & (f < nf - 1))(lambda: step(False, False))
    pl.when(f == nf - 1)(lambda: step(False, True))


def _ffn(x2d, norm_w, wg16, wu16, wd16, final_w, *, tm, tf):
    T = x2d.shape[0]
    return pl.pallas_call(
        _ffn_kernel,
        grid=(T // tm, D_FF // tf),
        in_specs=[
            pl.BlockSpec((tm, D_MODEL), lambda i, f: (i, 0)),
            pl.BlockSpec((1, D_MODEL), lambda i, f: (0, 0)),
            pl.BlockSpec((D_MODEL, tf), lambda i, f: (0, f)),
            pl.BlockSpec((D_MODEL, tf), lambda i, f: (0, f)),
            pl.BlockSpec((tf, D_MODEL), lambda i, f: (f, 0)),
            pl.BlockSpec((1, D_MODEL), lambda i, f: (0, 0)),
        ],
        out_specs=pl.BlockSpec((tm, D_MODEL), lambda i, f: (i, 0)),
        out_shape=jax.ShapeDtypeStruct((T, D_MODEL), F32),
        scratch_shapes=[pltpu.VMEM((tm, D_MODEL), BF16), pltpu.VMEM((tm, D_MODEL), F32)],
        compiler_params=pltpu.CompilerParams(
            dimension_semantics=("parallel", "arbitrary"), vmem_limit_bytes=VMEM_LIMIT),
        name="ffn",
    )(x2d, norm_w, wg16, wu16, wd16, final_w)


def _trunk(x, states, p):
    B, L, _ = x.shape
    x2d = x.reshape(B * L, D_MODEL)
    proj = _in_proj(x2d, p["attn_norm_w"], p["w_in_r"], tm=1024, tn=768)
    gdn_w = (p["gdn_conv_w"], p["gbias"], p["galog"], p["gdn_norm_w"])
    ssm_w = (p["cwx"], p["cbx"], p["cwbc"], p["cbbc"], p["gbias"], p["galog"], p["dcols"], p["ssm_norm_w"])
    pair_shape = (B, N_PAIRS, 2 * SSM_P, SSM_N)
    if states is None:
        mix_g, gconv_new, gS_new = _gdn_prompt(proj, *gdn_w, B=B, L=L, Lb=256)
        mix_s, sconv_new, sh_new = _ssm_prompt(proj, *ssm_w, B=B, L=L, Lb=256)
    else:
        gconv, gS, sconv, sh = states
        mix_g, gconv_new, gS_new = _gdn_sample(proj, gconv, gS, *gdn_w, B=B, L=L, nb=8)
        mix_s, sconv_new, sh_new = _ssm_sample(proj, sconv, sh.reshape(pair_shape), *ssm_w, B=B, L=L)
    sh_new = sh_new.reshape(B, SSM_HEADS, SSM_P, SSM_N)
    x1 = _out_proj(x2d, mix_g, mix_s, p["w_out16"], tm=512)
    y = _ffn(x1, p["ffn_norm_w"], p["wg16"], p["wu16"], p["wd16"], p["final_norm_w"], tm=512, tf=512)
    return y.reshape(B, L, D_MODEL), (gconv_new[None], gS_new[None], sconv_new[None], sh_new[None])


def kernel(x_prompt, x_sample, state_gdn_conv, state_gdn, state_ssm_conv, state_ssm,
           attn_norm_w, w_in, gdn_conv_w, gdn_A_log, gdn_dt_bias, gdn_norm_w,
           ssm_conv_w, ssm_conv_b, ssm_A_log, ssm_dt_bias, ssm_D, ssm_norm_w,
           w_out, ffn_norm_w, w_gate, w_up, w_down, final_norm_w):
    assert w_in.shape[0] == 1, "single-layer trunk"
    assert x_prompt.shape[1] % 256 == 0 and x_sample.shape[1] == 8 and x_sample.shape[0] % 8 == 0
    assert w_in.shape[2] == D_IN_PROJ
    w_in_r = _w_in_prep(jnp.swapaxes(w_in, 1, 2), tk=256)
    zeros8 = jnp.zeros((GDN_HEADS,), F32)
    tail = jnp.zeros((LANES - GATE_DT2 - SSM_HEADS,), F32)
    gbias = jnp.concatenate([zeros8, gdn_dt_bias[0], ssm_dt_bias[0], ssm_dt_bias[0], tail])[None]
    galog = jnp.concatenate([zeros8, gdn_A_log[0], ssm_A_log[0], ssm_A_log[0], tail])[None]
    p = dict(
        attn_norm_w=attn_norm_w, w_in_r=w_in_r, gdn_conv_w=gdn_conv_w[0], gbias=gbias, galog=galog,
        gdn_norm_w=gdn_norm_w,
        cwx=ssm_conv_w[0][:, :SSM_DI], cbx=ssm_conv_b[:, :SSM_DI],
        cwbc=ssm_conv_w[0][:, SSM_DI:], cbbc=ssm_conv_b[:, SSM_DI:],
        dcols=jnp.repeat(ssm_D[0], SSM_P)[None], ssm_norm_w=ssm_norm_w,
        w_out16=w_out[0].astype(BF16), ffn_norm_w=ffn_norm_w,
        wg16=w_gate[0].astype(BF16), wu16=w_up[0].astype(BF16), wd16=w_down[0].astype(BF16),
        final_norm_w=final_norm_w[None],
    )
    y_p, st_p = _trunk(x_prompt, None, p)
    y_s, st_s = _trunk(x_sample, (state_gdn_conv[0], state_gdn[0], state_ssm_conv[0], state_ssm[0]), p)
    return (y_p, y_s, st_p[0], st_p[1], st_p[2], st_p[3], st_s[0], st_s[1], st_s[2], st_s[3])
```

```python
import functools

import jax
import jax.numpy as jnp
from jax import lax
from jax.experimental import pallas as pl
from jax.experimental.pallas import tpu as pltpu

F32 = jnp.float32
BF16 = jnp.bfloat16

D_MODEL = 2048
GDN_HEADS = 8
GDN_DK = 128
GDN_DV = 128
GDN_QK = GDN_HEADS * GDN_DK
GDN_V = GDN_HEADS * GDN_DV
GDN_CONV_CH = 2 * GDN_QK + GDN_V
SSM_P = 64
SSM_N = 128
SSM_GROUPS = 2
SSM_DI = 1024
SSM_HEADS = SSM_DI // SSM_P
SSM_BC = SSM_GROUPS * SSM_N
SSM_CONV_CH = SSM_DI + 2 * SSM_BC
CONV_W = 4
CHUNK = 64
D_FF = 5632
EPS = 1e-6

OFF_QKV = 0
OFF_ZG = OFF_QKV + GDN_CONV_CH
OFF_ZS = OFF_ZG + GDN_V
OFF_XS = OFF_ZS + SSM_DI
OFF_BC = OFF_XS + SSM_DI
OFF_GATE = OFF_BC + 2 * SSM_BC
LANES = 128
GATE_B = 0
GATE_A = GATE_B + GDN_HEADS
GATE_DT = GATE_A + GDN_HEADS
GATE_DT2 = GATE_DT + SSM_HEADS
N_PROJ = 6912

VMEM_LIMIT = 52 * 1024 * 1024

_HIGHEST = lax.Precision.HIGHEST


def _silu(x):
    return x * jax.nn.sigmoid(x)


def _softplus(x):
    return jnp.maximum(x, 0.0) + jnp.log1p(jnp.exp(-jnp.abs(x)))


def _dot(a, b):
    return jnp.dot(a, b, preferred_element_type=F32)


def _dot_nt(a, b):
    return lax.dot_general(a, b, (((1,), (1,)), ((), ())), preferred_element_type=F32)


def _dot_tn(a, b):
    return lax.dot_general(a, b, (((0,), (0,)), ((), ())), preferred_element_type=F32)


def _transpose_rows(a):
    r = a.shape[0]
    if r < LANES:
        a = jnp.concatenate([a, jnp.zeros((LANES - r, LANES), a.dtype)], axis=0)
    return a.T


def _rms_rows(x, w):
    return x * lax.rsqrt(jnp.mean(x * x, axis=-1, keepdims=True) + EPS) * w


W_B = OFF_ZS
W_ZS = W_B + 2 * GDN_HEADS
W_DT = W_ZS + SSM_DI + SSM_CONV_CH
D_IN_PROJ = W_DT + SSM_HEADS


def _wprep_kernel(w_ref, o_ref):
    cols = o_ref.shape[1]
    o_ref[:W_B, :] = w_ref[0, :W_B, :].astype(BF16)
    o_ref[W_B:OFF_GATE, :] = w_ref[0, W_ZS:W_DT, :].astype(BF16)
    dt = w_ref[0, W_DT:D_IN_PROJ, :].astype(BF16)
    o_ref[OFF_GATE:OFF_GATE + GATE_DT, :] = w_ref[0, W_B:W_ZS, :].astype(BF16)
    o_ref[OFF_GATE + GATE_DT:OFF_GATE + GATE_DT2, :] = dt
    o_ref[OFF_GATE + GATE_DT2:OFF_GATE + GATE_DT2 + SSM_HEADS, :] = dt
    o_ref[OFF_GATE + GATE_DT2 + SSM_HEADS:, :] = jnp.zeros((N_PROJ - OFF_GATE - GATE_DT2 - SSM_HEADS, cols), BF16)


def _w_in_prep(w_in_t, *, tk):
    return pl.pallas_call(
        _wprep_kernel,
        grid=(D_MODEL // tk,),
        in_specs=[pl.BlockSpec((1, D_IN_PROJ, tk), lambda i: (0, 0, i))],
        out_specs=pl.BlockSpec((N_PROJ, tk), lambda i: (0, i)),
        out_shape=jax.ShapeDtypeStruct((N_PROJ, D_MODEL), BF16),
        compiler_params=pltpu.CompilerParams(
            dimension_semantics=("parallel",), vmem_limit_bytes=VMEM_LIMIT),
        name="w_in_prep",
    )(w_in_t)


def _inproj_kernel(x_ref, nw_ref, wt_ref, o_ref, h_ref):
    @pl.when(pl.program_id(1) == 0)
    def _():
        h_ref[...] = _rms_rows(x_ref[...], nw_ref[...]).astype(BF16)

    o_ref[...] = _dot_nt(h_ref[...], wt_ref[...])


def _in_proj(x2d, norm_w, w_in_r, *, tm, tn):
    T = x2d.shape[0]
    return pl.pallas_call(
        _inproj_kernel,
        grid=(T // tm, N_PROJ // tn),
        in_specs=[
            pl.BlockSpec((tm, D_MODEL), lambda i, j: (i, 0)),
            pl.BlockSpec((1, D_MODEL), lambda i, j: (0, 0)),
            pl.BlockSpec((tn, D_MODEL), lambda i, j: (j, 0)),
        ],
        out_specs=pl.BlockSpec((tm, tn), lambda i, j: (i, j)),
        out_shape=jax.ShapeDtypeStruct((T, N_PROJ), F32),
        scratch_shapes=[pltpu.VMEM((tm, D_MODEL), BF16)],
        compiler_params=pltpu.CompilerParams(
            dimension_semantics=("parallel", "arbitrary"), vmem_limit_bytes=VMEM_LIMIT),
        name="in_proj",
    )(x2d, norm_w, w_in_r)


def _conv_block(x_ref, xpad_ref, cw_ref, cb_ref, dst_ref, Lb, C, post):
    xpad_ref[8:8 + Lb, :] = x_ref[...]
    rs = min(Lb, CHUNK)
    for sb in range(Lb // rs):
        r = sb * rs
        for s in range(C // LANES):
            cols = slice(s * LANES, (s + 1) * LANES)
            acc = xpad_ref[8 + r:8 + r + rs, cols] * cw_ref[3:4, cols]
            for i in range(CONV_W - 1):
                acc = acc + xpad_ref[5 + i + r:5 + i + r + rs, cols] * cw_ref[i:i + 1, cols]
            if cb_ref is not None:
                acc = acc + cb_ref[:, cols]
            dst_ref[r:r + rs, cols] = post(s, _silu(acc))
    last3 = xpad_ref[Lb + 5:Lb + 8, :]
    xpad_ref[5:8, :] = last3
    return last3


STACK = 128


def _gdn_qk_post(s, y):
    if s < 2 * GDN_HEADS:
        y = y * lax.rsqrt(jnp.sum(y * y, axis=-1, keepdims=True) + EPS)
        if s < GDN_HEADS:
            y = y * (GDN_DK ** -0.5)
    return y


def _gdn_local(items, glen):
    sh = glen.bit_length() - 1
    row = lax.broadcasted_iota(jnp.int32, (STACK, STACK), 0)
    col = lax.broadcasted_iota(jnp.int32, (STACK, STACK), 1)
    same = (row >> sh) == (col >> sh)
    incl = same & (row >= col)
    strict = same & (row > col)
    eye = (row == col).astype(F32)

    decay = [jnp.exp(jnp.where(incl, it["g"] - it["g"].T, -jnp.inf)) for it in items]
    kb = [it["k"] * it["beta"] for it in items]
    qkk = [_dot_nt(jnp.concatenate([it["q"], b], axis=0).astype(BF16), it["k"].astype(BF16))
           for it, b in zip(items, kb)]
    qk = [x[:STACK] * d for x, d in zip(qkk, decay)]
    nmat = [jnp.where(strict, -(x[STACK:] * d), 0.0) for x, d in zip(qkk, decay)]
    tinv = [eye + n for n in nmat]
    if sh >= 2:
        pw = [_dot(n.astype(BF16), n.astype(BF16)) for n in nmat]
        for _ in range(sh - 2):
            x = [_dot(jnp.concatenate([t, p], axis=0).astype(BF16), p.astype(BF16))
                 for t, p in zip(tinv, pw)]
            tinv = [t + y[:STACK] for t, y in zip(tinv, x)]
            pw = [y[STACK:] for y in x]
        tinv = [t + _dot(t.astype(BF16), p.astype(BF16)) for t, p in zip(tinv, pw)]
    uw = [_dot(t.astype(BF16),
               jnp.concatenate([it["v"] * it["beta"], b * it["eg"]], axis=1).astype(BF16))
          for t, it, b in zip(tinv, items, kb)]
    return [(x[:, :GDN_DV], x[:, GDN_DV:]) for x in uw], qk


GDN_LOCAL_CHUNKS = 2


def _gdn_prompt_kernel(qkv_ref, zg_ref, gate_ref, cw_ref, gbias_ref, galog_ref, nw_ref,
                       o_ref, cst_out_ref, sst_out_ref,
                       xpad_ref, qkvc_ref, s_ref, u_ref, wq16_ref, kd16_ref, qk16_ref, egl_ref, *, Lb):
    c = CHUNK
    l = pl.program_id(1)
    nl = pl.num_programs(1)

    @pl.when(l == 0)
    def _init():
        xpad_ref[0:8, :] = jnp.zeros((8, GDN_CONV_CH), F32)
        s_ref[...] = jnp.zeros(s_ref.shape, F32)

    _conv_block(qkv_ref, xpad_ref, cw_ref, None, qkvc_ref, Lb, GDN_CONV_CH, _gdn_qk_post)

    row_i = lax.broadcasted_iota(jnp.int32, (c, c), 0)
    col_i = lax.broadcasted_iota(jnp.int32, (c, c), 1)
    tril_f = (row_i >= col_i).astype(F32)
    gbias = gbias_ref[...]
    nega = -jnp.exp(galog_ref[...])
    nw = nw_ref[...]
    n_pairs = GDN_HEADS // 2

    def local_chunk_items(ci):
        rows = slice(ci * c, (ci + 1) * c)
        graw = gate_ref[rows, :]
        sp = _softplus(graw + gbias)
        beta_all = jax.nn.sigmoid(graw)
        G = jnp.dot(tril_f, nega * sp, precision=_HIGHEST, preferred_element_type=F32)
        eG = jnp.exp(G)
        glast = G[c - 1:c, :]
        eGrev = jnp.exp(glast - G)
        egl_ref[ci] = jnp.broadcast_to(jnp.exp(glast), (8, LANES))

        def heads(off, a, b):
            return jnp.concatenate([qkvc_ref[rows, off + a * LANES:off + (a + 1) * LANES],
                                    qkvc_ref[rows, off + b * LANES:off + (b + 1) * LANES]], axis=0)

        def colstack(m, a, b):
            return jnp.concatenate([jnp.broadcast_to(m[:, a:a + 1], (c, LANES)),
                                    jnp.broadcast_to(m[:, b:b + 1], (c, LANES))], axis=0)

        items = []
        for pr in range(n_pairs):
            a, b = 2 * pr, 2 * pr + 1
            items.append(dict(
                q=heads(0, a, b), k=heads(GDN_QK, a, b), v=heads(2 * GDN_QK, a, b),
                beta=colstack(beta_all, GATE_B + a, GATE_B + b),
                g=colstack(G, GATE_A + a, GATE_A + b),
                eg=colstack(eG, GATE_A + a, GATE_A + b),
                egrev=colstack(eGrev, GATE_A + a, GATE_A + b)))
        return items

    def local_group(cis):
        items = [it for ci in cis for it in local_chunk_items(ci)]
        uw, qk = _gdn_local(items, c)
        for n, it in enumerate(items):
            idx = cis[0] * n_pairs + n
            u, w = uw[n]
            qd = it["q"] * it["eg"]
            u_ref[idx] = u
            for hh in range(2):
                hs = slice(hh * c, (hh + 1) * c)
                wq16_ref[2 * idx + hh] = jnp.concatenate([w[hs], qd[hs]], axis=0).astype(BF16)
            kd16_ref[idx] = (it["k"] * it["egrev"]).astype(BF16)
            qk16_ref[idx] = qk[n].astype(BF16)

    def recurrent(ci):
        rows = slice(ci * c, (ci + 1) * c)
        egl = egl_ref[ci][0:1]
        r = [[_dot(wq16_ref[2 * (ci * n_pairs + pr) + hh], s_ref[2 * pr + hh].astype(BF16)) for hh in range(2)]
             for pr in range(n_pairs)]
        v16 = [(u_ref[ci * n_pairs + pr] - jnp.concatenate([r[pr][0][:c], r[pr][1][:c]], axis=0)).astype(BF16)
               for pr in range(n_pairs)]
        o = [jnp.concatenate([r[pr][0][c:], r[pr][1][c:]], axis=0) + _dot(qk16_ref[ci * n_pairs + pr], v16[pr])
             for pr in range(n_pairs)]
        for pr in range(n_pairs):
            kd16 = kd16_ref[ci * n_pairs + pr]
            for hh in range(2):
                h = 2 * pr + hh
                ga = GATE_A + h
                hs = slice(hh * c, (hh + 1) * c)
                s_ref[h] = s_ref[h] * egl[:, ga:ga + 1] + _dot_tn(kd16[hs], v16[pr][hs])
        for pr in range(n_pairs):
            for hh in range(2):
                h = 2 * pr + hh
                z = zg_ref[rows, h * GDN_DV:(h + 1) * GDN_DV]
                o_ref[rows, h * GDN_DV:(h + 1) * GDN_DV] = (
                    _rms_rows(o[pr][hh * c:(hh + 1) * c], nw) * _silu(z)).astype(o_ref.dtype)

    for g0 in range(0, Lb // c, GDN_LOCAL_CHUNKS):
        cis = list(range(g0, g0 + GDN_LOCAL_CHUNKS))
        local_group(cis)
        for ci in cis:
            recurrent(ci)

    @pl.when(l == nl - 1)
    def _final():
        cst_out_ref[0] = xpad_ref[5:8, :]
        sst_out_ref[0] = s_ref[...]


def _gdn_prompt(proj, cw, gbias, galog, nw, *, B, L, Lb):
    nl = L // Lb
    n_tiles = (Lb // CHUNK) * (GDN_HEADS // 2)
    row = lambda b, l: b * nl + l
    const = lambda shape: pl.BlockSpec(shape, lambda b, l: (0,) * len(shape))
    return pl.pallas_call(
        functools.partial(_gdn_prompt_kernel, Lb=Lb),
        grid=(B, nl),
        in_specs=[
            pl.BlockSpec((Lb, GDN_CONV_CH), lambda b, l: (row(b, l), OFF_QKV // GDN_CONV_CH)),
            pl.BlockSpec((Lb, GDN_V), lambda b, l: (row(b, l), OFF_ZG // GDN_V)),
            pl.BlockSpec((Lb, LANES), lambda b, l: (row(b, l), OFF_GATE // LANES)),
            const((CONV_W, GDN_CONV_CH)), const((1, LANES)), const((1, LANES)), const((1, GDN_DV)),
        ],
        out_specs=[
            pl.BlockSpec((Lb, GDN_V), lambda b, l: (row(b, l), 0)),
            pl.BlockSpec((1, CONV_W - 1, GDN_CONV_CH), lambda b, l: (b, 0, 0)),
            pl.BlockSpec((1, GDN_HEADS, GDN_DK, GDN_DV), lambda b, l: (b, 0, 0, 0)),
        ],
        out_shape=[
            jax.ShapeDtypeStruct((B * L, GDN_V), BF16),
            jax.ShapeDtypeStruct((B, CONV_W - 1, GDN_CONV_CH), F32),
            jax.ShapeDtypeStruct((B, GDN_HEADS, GDN_DK, GDN_DV), F32),
        ],
        scratch_shapes=[
            pltpu.VMEM((Lb + 8, GDN_CONV_CH), F32),
            pltpu.VMEM((Lb, GDN_CONV_CH), F32),
            pltpu.VMEM((GDN_HEADS, GDN_DK, GDN_DV), F32),
            pltpu.VMEM((n_tiles, STACK, GDN_DV), F32),
            pltpu.VMEM((2 * n_tiles, STACK, GDN_DK), BF16),
            pltpu.VMEM((n_tiles, STACK, GDN_DK), BF16),
            pltpu.VMEM((n_tiles, STACK, STACK), BF16),
            pltpu.VMEM((Lb // CHUNK, 8, LANES), F32),
        ],
        compiler_params=pltpu.CompilerParams(
            dimension_semantics=("parallel", "arbitrary"), vmem_limit_bytes=VMEM_LIMIT),
        name="gdn_prompt",
    )(proj, proj, proj, cw, gbias, galog, nw)


def _gdn_sample_kernel(qkv_ref, zg_ref, gate_ref, cst_ref, sst_ref, cw_ref, gbias_ref, galog_ref, nw_ref,
                       o_ref, cst_out_ref, sst_out_ref, xpad_ref, qkvc_ref, *, nb, L):
    R = nb * L
    sh = L.bit_length() - 1
    for bi in range(nb):
        xp = xpad_ref.at[bi]
        xp[5:8, :] = cst_ref[bi]
        cst_out_ref[bi] = _conv_block(
            qkv_ref.at[pl.ds(bi * L, L)], xp, cw_ref, None, qkvc_ref.at[pl.ds(bi * L, L)],
            L, GDN_CONV_CH, _gdn_qk_post)

    row_i = lax.broadcasted_iota(jnp.int32, (R, R), 0)
    col_i = lax.broadcasted_iota(jnp.int32, (R, R), 1)
    tril_f = (((row_i >> sh) == (col_i >> sh)) & (row_i >= col_i)).astype(F32)
    graw = gate_ref[...]
    sp = _softplus(graw + gbias_ref[...])
    beta_all = jax.nn.sigmoid(graw)
    G = jnp.dot(tril_f, -jnp.exp(galog_ref[...]) * sp, precision=_HIGHEST, preferred_element_type=F32)
    glast = [G[bi * L + L - 1:bi * L + L, :] for bi in range(nb)]
    eG = jnp.exp(G)
    eGrev = jnp.exp(jnp.concatenate([jnp.broadcast_to(x, (L, LANES)) for x in glast], axis=0) - G)
    egl = [jnp.exp(x) for x in glast]
    nw = nw_ref[...]

    n_st = R // (2 * L)

    def tiles(ref, st, off):
        return jnp.concatenate(
            [ref[st * 2 * L:(st + 1) * 2 * L, off + h * LANES:off + (h + 1) * LANES]
             for h in range(GDN_HEADS)], axis=0)

    def colstack(m, st, off):
        return jnp.concatenate(
            [jnp.broadcast_to(m[st * 2 * L:(st + 1) * 2 * L, off + h:off + h + 1], (2 * L, LANES))
             for h in range(GDN_HEADS)], axis=0)

    items = [dict(q=tiles(qkvc_ref, st, 0), k=tiles(qkvc_ref, st, GDN_QK), v=tiles(qkvc_ref, st, 2 * GDN_QK),
                  beta=colstack(beta_all, st, GATE_B), g=colstack(G, st, GATE_A),
                  eg=colstack(eG, st, GATE_A), egrev=colstack(eGrev, st, GATE_A))
             for st in range(n_st)]
    uw, qk = _gdn_local(items, L)

    groups = [(h, bi) for h in range(GDN_HEADS) for bi in range(2)]
    r = []
    for st, it in enumerate(items):
        w = uw[st][1]
        qd = it["q"] * it["eg"]
        r.append([
            _dot(jnp.concatenate([w[gi * L:(gi + 1) * L], qd[gi * L:(gi + 1) * L]], axis=0).astype(BF16),
                 sst_ref[2 * st + bi, h].astype(BF16))
            for gi, (h, bi) in enumerate(groups)])
    v_new = [uw[st][0] - jnp.concatenate([x[:L] for x in r[st]], axis=0) for st in range(n_st)]
    o = [jnp.concatenate([x[L:] for x in r[st]], axis=0)
         + _dot(qk[st].astype(BF16), v_new[st].astype(BF16)) for st in range(n_st)]
    for st, it in enumerate(items):
        kd = it["k"] * it["egrev"]
        for gi, (h, bi) in enumerate(groups):
            b = 2 * st + bi
            ga = GATE_A + h
            rs = slice(gi * L, (gi + 1) * L)
            sst_out_ref[b, h] = (sst_ref[b, h] * egl[b][:, ga:ga + 1]
                                 + _dot_tn(kd[rs].astype(BF16), v_new[st][rs].astype(BF16)))
    for st in range(n_st):
        out = (_rms_rows(o[st], nw) * _silu(tiles(zg_ref, st, 0))).astype(o_ref.dtype)
        for h in range(GDN_HEADS):
            o_ref[st * 2 * L:(st + 1) * 2 * L, h * GDN_DV:(h + 1) * GDN_DV] = out[h * 2 * L:(h + 1) * 2 * L]


def _gdn_sample(proj, conv_state, S_state, cw, gbias, galog, nw, *, B, L, nb):
    R = nb * L
    const = lambda shape: pl.BlockSpec(shape, lambda i: (0,) * len(shape))
    return pl.pallas_call(
        functools.partial(_gdn_sample_kernel, nb=nb, L=L),
        grid=(B // nb,),
        in_specs=[
            pl.BlockSpec((R, GDN_CONV_CH), lambda i: (i, OFF_QKV // GDN_CONV_CH)),
            pl.BlockSpec((R, GDN_V), lambda i: (i, OFF_ZG // GDN_V)),
            pl.BlockSpec((R, LANES), lambda i: (i, OFF_GATE // LANES)),
            pl.BlockSpec((nb, CONV_W - 1, GDN_CONV_CH), lambda i: (i, 0, 0)),
            pl.BlockSpec((nb, GDN_HEADS, GDN_DK, GDN_DV), lambda i: (i, 0, 0, 0)),
            const((CONV_W, GDN_CONV_CH)), const((1, LANES)), const((1, LANES)), const((1, GDN_DV)),
        ],
        out_specs=[
            pl.BlockSpec((R, GDN_V), lambda i: (i, 0)),
            pl.BlockSpec((nb, CONV_W - 1, GDN_CONV_CH), lambda i: (i, 0, 0)),
            pl.BlockSpec((nb, GDN_HEADS, GDN_DK, GDN_DV), lambda i: (i, 0, 0, 0)),
        ],
        out_shape=[
            jax.ShapeDtypeStruct((B * L, GDN_V), BF16),
            jax.ShapeDtypeStruct((B, CONV_W - 1, GDN_CONV_CH), F32),
            jax.ShapeDtypeStruct((B, GDN_HEADS, GDN_DK, GDN_DV), F32),
        ],
        scratch_shapes=[
            pltpu.VMEM((nb, L + 8, GDN_CONV_CH), F32),
            pltpu.VMEM((R, GDN_CONV_CH), F32),
        ],
        compiler_params=pltpu.CompilerParams(
            dimension_semantics=("parallel",), vmem_limit_bytes=VMEM_LIMIT),
        name="gdn_sample",
    )(proj, proj, proj, conv_state, S_state, cw, gbias, galog, nw)


N_PAIRS = SSM_HEADS // 2
PAIRS_PER_GROUP = N_PAIRS // SSM_GROUPS
GROUP_W = SSM_DI // SSM_GROUPS


def _ssm_tile(graw, gbias, nega, ld_x, ld_b, ld_c, ld_z, dcols_ref, nw_ref, st_o, get_h, set_h, glen):
    c = CHUNK
    P = SSM_P
    nseq = c // glen
    sh = glen.bit_length() - 1
    ri = lax.broadcasted_iota(jnp.int32, (c, c), 0)
    ci = lax.broadcasted_iota(jnp.int32, (c, c), 1)
    tril_f = (((ri >> sh) == (ci >> sh)) & (ri >= ci)).astype(F32)
    sp = _softplus(graw + gbias)
    acum = jnp.dot(tril_f, nega * sp, precision=_HIGHEST, preferred_element_type=F32)
    lasts = [acum[s * glen + glen - 1:(s + 1) * glen, :] for s in range(nseq)]
    alast = jnp.concatenate([jnp.broadcast_to(x, (glen, LANES)) for x in lasts], axis=0)
    dtrev = sp * jnp.exp(alast - acum)
    eal = [jnp.exp(x) for x in lasts]
    lane = lax.broadcasted_iota(jnp.int32, (c, LANES), 1)
    row = lax.broadcasted_iota(jnp.int32, (c, LANES), 0)
    m = jnp.where(lane < GATE_DT2, acum, sp)
    mt = jnp.concatenate([m, m], axis=0).T
    left = lane < P
    left_row = left[0:1]
    j = jnp.where(left, lane, lane - P)
    tril2 = ((row >> sh) == (j >> sh)) & (row >= j)
    rowh = lax.broadcasted_iota(jnp.int32, (2 * P, SSM_N), 0) < P

    def expand(mat, c0):
        return jnp.where(left, jnp.broadcast_to(mat[:, c0:c0 + 1], (c, LANES)),
                         jnp.broadcast_to(mat[:, c0 + 1:c0 + 2], (c, LANES)))

    def rowsel(base, e):
        return jnp.where(left_row, mt[base + 2 * e:base + 2 * e + 1, :], mt[base + 2 * e + 1:base + 2 * e + 2, :])

    for g in range(SSM_GROUPS):
        Bg = ld_b(g)
        Cg = ld_c(g)
        Bg16 = Bg.astype(BF16)
        Cg16 = Cg.astype(BF16)
        cb2 = _dot_nt(Cg16, jnp.concatenate([Bg16, Bg16], axis=0))
        pairs = [g * PAIRS_PER_GROUP + e4 for e4 in range(PAIRS_PER_GROUP)]
        acol = [expand(acum, GATE_DT + 2 * e) for e in pairs]
        scores16 = [
            (cb2 * jnp.exp(jnp.where(tril2, a - rowsel(GATE_DT, e), -jnp.inf)) * rowsel(GATE_DT2, e)).astype(BF16)
            for a, e in zip(acol, pairs)]
        xp = [ld_x(e) for e in pairs]
        bd16 = [jnp.concatenate([jnp.where(left, x, 0.0), jnp.where(left, 0.0, x)], axis=0).astype(BF16)
                for x in xp]
        ydiag = [_dot(s, b) for s, b in zip(scores16, bd16)]
        if nseq == 1:
            yoff = [_dot_nt(Cg16, get_h(0, e).astype(BF16)) for e in pairs]
        else:
            yoff = [jnp.concatenate(
                [_dot_nt(Cg[s * glen:(s + 1) * glen].astype(BF16), get_h(s, e).astype(BF16))
                 for s in range(nseq)], axis=0) for e in pairs]
        y = [yd + yo * jnp.exp(a) + dcols_ref[:, e * 2 * P:(e + 1) * 2 * P] * x
             for yd, yo, a, e, x in zip(ydiag, yoff, acol, pairs, xp)]
        xdr = [x * expand(dtrev, GATE_DT + 2 * e) for x, e in zip(xp, pairs)]
        for e, xd in zip(pairs, xdr):
            c0 = GATE_DT + 2 * e
            for s in range(nseq):
                rs = slice(s * glen, (s + 1) * glen)
                ealcol = jnp.where(rowh, eal[s][:, c0:c0 + 1], eal[s][:, c0 + 1:c0 + 2])
                set_h(s, e, get_h(s, e) * ealcol + _dot_tn(xd[rs].astype(BF16), Bg[rs].astype(BF16)))
        yg = jnp.concatenate(y, axis=1) * _silu(ld_z(g))
        gcols = slice(g * GROUP_W, (g + 1) * GROUP_W)
        st_o(g, _rms_rows(yg, nw_ref[:, gcols]))


def _ssm_prompt_kernel(xs_ref, bc_ref, zs_ref, gate_ref, cwx_ref, cbx_ref, cwbc_ref, cbbc_ref,
                       gbias_ref, galog_ref, dcols_ref, nw_ref,
                       o_ref, cst_out_ref, hst_out_ref, xpadx_ref, xpadbc_ref, xc_ref, bcc_ref, hh_ref, *, Lb):
    c = CHUNK
    l = pl.program_id(1)
    nl = pl.num_programs(1)

    @pl.when(l == 0)
    def _init():
        hh_ref[...] = jnp.zeros(hh_ref.shape, F32)
        xpadx_ref[0:8, :] = jnp.zeros((8, SSM_DI), F32)
        xpadbc_ref[0:8, :] = jnp.zeros((8, 2 * SSM_BC), F32)

    ident = lambda s, y: y
    _conv_block(xs_ref, xpadx_ref, cwx_ref, cbx_ref, xc_ref, Lb, SSM_DI, ident)
    _conv_block(bc_ref, xpadbc_ref, cwbc_ref, cbbc_ref, bcc_ref, Lb, 2 * SSM_BC, ident)

    gbias = gbias_ref[...]
    nega = -jnp.exp(galog_ref[...])

    def set_h(s, e, val):
        hh_ref[e] = val

    for ci in range(Lb // c):
        rows = slice(ci * c, (ci + 1) * c)

        def st_o(g, val, rows=rows):
            o_ref[rows, g * GROUP_W:(g + 1) * GROUP_W] = val.astype(o_ref.dtype)

        _ssm_tile(
            gate_ref[rows, :], gbias, nega,
            lambda e, rows=rows: xc_ref[rows, e * LANES:(e + 1) * LANES],
            lambda g, rows=rows: bcc_ref[rows, g * SSM_N:(g + 1) * SSM_N],
            lambda g, rows=rows: bcc_ref[rows, SSM_BC + g * SSM_N:SSM_BC + (g + 1) * SSM_N],
            lambda g, rows=rows: zs_ref[rows, g * GROUP_W:(g + 1) * GROUP_W],
            dcols_ref, nw_ref, st_o, lambda s, e: hh_ref[e], set_h, c)

    @pl.when(l == nl - 1)
    def _final():
        cst_out_ref[0, :, :SSM_DI] = xpadx_ref[5:8, :]
        cst_out_ref[0, :, SSM_DI:] = xpadbc_ref[5:8, :]
        hst_out_ref[0] = hh_ref[...]


def _ssm_sample_kernel(xs_ref, bc_ref, zs_ref, gate_ref, cst_ref, hst_ref, cwx_ref, cbx_ref, cwbc_ref, cbbc_ref,
                       gbias_ref, galog_ref, dcols_ref, nw_ref,
                       o_ref, cst_out_ref, hst_out_ref, xpadx_ref, xpadbc_ref, xc_ref, bcc_ref, *, L):
    ident = lambda s, y: y
    for bi in range(CHUNK // L):
        rs = pl.ds(bi * L, L)
        xpx = xpadx_ref.at[bi]
        xpb = xpadbc_ref.at[bi]
        xpx[5:8, :] = cst_ref[bi, :, :SSM_DI]
        xpb[5:8, :] = cst_ref[bi, :, SSM_DI:]
        cst_out_ref[bi, :, :SSM_DI] = _conv_block(
            xs_ref.at[rs], xpx, cwx_ref, cbx_ref, xc_ref.at[rs], L, SSM_DI, ident)
        cst_out_ref[bi, :, SSM_DI:] = _conv_block(
            bc_ref.at[rs], xpb, cwbc_ref, cbbc_ref, bcc_ref.at[rs], L, 2 * SSM_BC, ident)

    def st_o(g, val):
        o_ref[:, g * GROUP_W:(g + 1) * GROUP_W] = val.astype(o_ref.dtype)

    def set_h(s, e, val):
        hst_out_ref[s, e] = val

    _ssm_tile(
        gate_ref[...], gbias_ref[...], -jnp.exp(galog_ref[...]),
        lambda e: xc_ref[:, e * LANES:(e + 1) * LANES],
        lambda g: bcc_ref[:, g * SSM_N:(g + 1) * SSM_N],
        lambda g: bcc_ref[:, SSM_BC + g * SSM_N:SSM_BC + (g + 1) * SSM_N],
        lambda g: zs_ref[:, g * GROUP_W:(g + 1) * GROUP_W],
        dcols_ref, nw_ref, st_o, lambda s, e: hst_ref[s, e], set_h, L)


def _ssm_const_specs():
    const = lambda shape: pl.BlockSpec(shape, lambda *idx: (0,) * len(shape))
    return [
        const((CONV_W, SSM_DI)), const((1, SSM_DI)), const((CONV_W, 2 * SSM_BC)), const((1, 2 * SSM_BC)),
        const((1, LANES)), const((1, LANES)), const((1, SSM_DI)), const((1, SSM_DI)),
    ]


def _ssm_prompt(proj, cwx, cbx, cwbc, cbbc, gbias, galog, dcols, nw, *, B, L, Lb):
    nl = L // Lb
    row = lambda b, l: b * nl + l
    return pl.pallas_call(
        functools.partial(_ssm_prompt_kernel, Lb=Lb),
        grid=(B, nl),
        in_specs=[
            pl.BlockSpec((Lb, SSM_DI), lambda b, l: (row(b, l), OFF_XS // SSM_DI)),
            pl.BlockSpec((Lb, 2 * SSM_BC), lambda b, l: (row(b, l), OFF_BC // (2 * SSM_BC))),
            pl.BlockSpec((Lb, SSM_DI), lambda b, l: (row(b, l), OFF_ZS // SSM_DI)),
            pl.BlockSpec((Lb, LANES), lambda b, l: (row(b, l), OFF_GATE // LANES)),
        ] + _ssm_const_specs(),
        out_specs=[
            pl.BlockSpec((Lb, SSM_DI), lambda b, l: (row(b, l), 0)),
            pl.BlockSpec((1, CONV_W - 1, SSM_CONV_CH), lambda b, l: (b, 0, 0)),
            pl.BlockSpec((1, N_PAIRS, 2 * SSM_P, SSM_N), lambda b, l: (b, 0, 0, 0)),
        ],
        out_shape=[
            jax.ShapeDtypeStruct((B * L, SSM_DI), BF16),
            jax.ShapeDtypeStruct((B, CONV_W - 1, SSM_CONV_CH), F32),
            jax.ShapeDtypeStruct((B, N_PAIRS, 2 * SSM_P, SSM_N), F32),
        ],
        scratch_shapes=[
            pltpu.VMEM((Lb + 8, SSM_DI), F32),
            pltpu.VMEM((Lb + 8, 2 * SSM_BC), F32),
            pltpu.VMEM((Lb, SSM_DI), F32),
            pltpu.VMEM((Lb, 2 * SSM_BC), F32),
            pltpu.VMEM((N_PAIRS, 2 * SSM_P, SSM_N), F32),
        ],
        compiler_params=pltpu.CompilerParams(
            dimension_semantics=("parallel", "arbitrary"), vmem_limit_bytes=VMEM_LIMIT),
        name="ssm_prompt",
    )(proj, proj, proj, proj, cwx, cbx, cwbc, cbbc, gbias, galog, dcols, nw)


def _ssm_sample(proj, conv_state, h_pairs, cwx, cbx, cwbc, cbbc, gbias, galog, dcols, nw, *, B, L):
    nb = CHUNK // L
    return pl.pallas_call(
        functools.partial(_ssm_sample_kernel, L=L),
        grid=(B // nb,),
        in_specs=[
            pl.BlockSpec((CHUNK, SSM_DI), lambda i: (i, OFF_XS // SSM_DI)),
            pl.BlockSpec((CHUNK, 2 * SSM_BC), lambda i: (i, OFF_BC // (2 * SSM_BC))),
            pl.BlockSpec((CHUNK, SSM_DI), lambda i: (i, OFF_ZS // SSM_DI)),
            pl.BlockSpec((CHUNK, LANES), lambda i: (i, OFF_GATE // LANES)),
            pl.BlockSpec((nb, CONV_W - 1, SSM_CONV_CH), lambda i: (i, 0, 0)),
            pl.BlockSpec((nb, N_PAIRS, 2 * SSM_P, SSM_N), lambda i: (i, 0, 0, 0)),
        ] + _ssm_const_specs(),
        out_specs=[
            pl.BlockSpec((CHUNK, SSM_DI), lambda i: (i, 0)),
            pl.BlockSpec((nb, CONV_W - 1, SSM_CONV_CH), lambda i: (i, 0, 0)),
            pl.BlockSpec((nb, N_PAIRS, 2 * SSM_P, SSM_N), lambda i: (i, 0, 0, 0)),
        ],
        out_shape=[
            jax.ShapeDtypeStruct((B * L, SSM_DI), BF16),
            jax.ShapeDtypeStruct((B, CONV_W - 1, SSM_CONV_CH), F32),
            jax.ShapeDtypeStruct((B, N_PAIRS, 2 * SSM_P, SSM_N), F32),
        ],
        scratch_shapes=[
            pltpu.VMEM((nb, L + 8, SSM_DI), F32),
            pltpu.VMEM((nb, L + 8, 2 * SSM_BC), F32),
            pltpu.VMEM((CHUNK, SSM_DI), F32),
            pltpu.VMEM((CHUNK, 2 * SSM_BC), F32),
        ],
        compiler_params=pltpu.CompilerParams(
            dimension_semantics=("parallel",), vmem_limit_bytes=VMEM_LIMIT),
        name="ssm_sample",
    )(proj, proj, proj, proj, conv_state, h_pairs, cwx, cbx, cwbc, cbbc, gbias, galog, dcols, nw)


def _outproj_kernel(x_ref, mg_ref, ms_ref, w_ref, o_ref):
    acc = _dot(mg_ref[...].astype(BF16), w_ref[:GDN_V, :])
    acc = acc + _dot(ms_ref[...].astype(BF16), w_ref[GDN_V:, :])
    o_ref[...] = x_ref[...] + acc


def _out_proj(x2d, mix_g, mix_s, w_out16, *, tm):
    T = x2d.shape[0]
    return pl.pallas_call(
        _outproj_kernel,
        grid=(T // tm,),
        in_specs=[
            pl.BlockSpec((tm, D_MODEL), lambda i: (i, 0)),
            pl.BlockSpec((tm, GDN_V), lambda i: (i, 0)),
            pl.BlockSpec((tm, SSM_DI), lambda i: (i, 0)),
            pl.BlockSpec((D_MODEL, D_MODEL), lambda i: (0, 0)),
        ],
        out_specs=pl.BlockSpec((tm, D_MODEL), lambda i: (i, 0)),
        out_shape=jax.ShapeDtypeStruct((T, D_MODEL), F32),
        compiler_params=pltpu.CompilerParams(
            dimension_semantics=("parallel",), vmem_limit_bytes=VMEM_LIMIT),
        name="out_proj",
    )(x2d, mix_g, mix_s, w_out16)


FFN_SUB = 4


def _ffn_kernel(x_ref, nw_ref, wg_ref, wu_ref, wd_ref, fnw_ref, o_ref, h_ref):
    f = pl.program_id(1)
    nf = pl.num_programs(1)
    rs = h_ref.shape[0] // FFN_SUB

    def step(first, last):
        def gate_up(r):
            rows = slice(r * rs, (r + 1) * rs)
            if first:
                h = _rms_rows(x_ref[rows, :], nw_ref[...]).astype(BF16)
                h_ref[rows, :] = h
            else:
                h = h_ref[rows, :]
            return _dot(h, wg_ref[...]), _dot(h, wu_ref[...])

        def down(r, gu):
            rows = slice(r * rs, (r + 1) * rs)
            d = _dot((_silu(gu[0]) * gu[1]).astype(BF16), wd_ref[...])
            acc = d if first else o_ref[rows, :] + d
            if last:
                o_ref[rows, :] = _rms_rows(x_ref[rows, :] + acc, fnw_ref[...])
            else:
                o_ref[rows, :] = acc

        gu = gate_up(0)
        for r in range(1, FFN_SUB):
            gu_next = gate_up(r)
            down(r - 1, gu)
            gu = gu_next
        down(FFN_SUB - 1, gu)

    pl.when(f == 0)(lambda: step(True, False))
    pl.when((f > 0) & (f < nf - 1))(lambda: step(False, False))
    pl.when(f == nf - 1)(lambda: step(False, True))


def _ffn(x2d, norm_w, wg16, wu16, wd16, final_w, *, tm, tf):
    T = x2d.shape[0]
    return pl.pallas_call(
        _ffn_kernel,
        grid=(T // tm, D_FF // tf),
        in_specs=[
            pl.BlockSpec((tm, D_MODEL), lambda i, f: (i, 0)),
            pl.BlockSpec((1, D_MODEL), lambda i, f: (0, 0)),
            pl.BlockSpec((D_MODEL, tf), lambda i, f: (0, f)),
            pl.BlockSpec((D_MODEL, tf), lambda i, f: (0, f)),
            pl.BlockSpec((tf, D_MODEL), lambda i, f: (f, 0)),
            pl.BlockSpec((1, D_MODEL), lambda i, f: (0, 0)),
        ],
        out_specs=pl.BlockSpec((tm, D_MODEL), lambda i, f: (i, 0)),
        out_shape=jax.ShapeDtypeStruct((T, D_MODEL), F32),
        scratch_shapes=[pltpu.VMEM((tm, D_MODEL), BF16)],
        compiler_params=pltpu.CompilerParams(
            dimension_semantics=("parallel", "arbitrary"), vmem_limit_bytes=VMEM_LIMIT),
        name="ffn",
    )(x2d, norm_w, wg16, wu16, wd16, final_w)


def _trunk(x, states, p):
    B, L, _ = x.shape
    x2d = x.reshape(B * L, D_MODEL)
    proj = _in_proj(x2d, p["attn_norm_w"], p["w_in_r"], tm=1024, tn=768)
    gdn_w = (p["gdn_conv_w"], p["gbias"], p["galog"], p["gdn_norm_w"])
    ssm_w = (p["cwx"], p["cbx"], p["cwbc"], p["cbbc"], p["gbias"], p["galog"], p["dcols"], p["ssm_norm_w"])
    pair_shape = (B, N_PAIRS, 2 * SSM_P, SSM_N)
    if states is None:
        mix_g, gconv_new, gS_new = _gdn_prompt(proj, *gdn_w, B=B, L=L, Lb=256)
        mix_s, sconv_new, sh_new = _ssm_prompt(proj, *ssm_w, B=B, L=L, Lb=256)
    else:
        gconv, gS, sconv, sh = states
        mix_g, gconv_new, gS_new = _gdn_sample(proj, gconv, gS, *gdn_w, B=B, L=L, nb=8)
        mix_s, sconv_new, sh_new = _ssm_sample(proj, sconv, sh.reshape(pair_shape), *ssm_w, B=B, L=L)
    sh_new = sh_new.reshape(B, SSM_HEADS, SSM_P, SSM_N)
    x1 = _out_proj(x2d, mix_g, mix_s, p["w_out16"], tm=512)
    y = _ffn(x1, p["ffn_norm_w"], p["wg16"], p["wu16"], p["wd16"], p["final_norm_w"], tm=1024, tf=512)
    return y.reshape(B, L, D_MODEL), (gconv_new[None], gS_new[None], sconv_new[None], sh_new[None])


def kernel(x_prompt, x_sample, state_gdn_conv, state_gdn, state_ssm_conv, state_ssm,
           attn_norm_w, w_in, gdn_conv_w, gdn_A_log, gdn_dt_bias, gdn_norm_w,
           ssm_conv_w, ssm_conv_b, ssm_A_log, ssm_dt_bias, ssm_D, ssm_norm_w,
           w_out, ffn_norm_w, w_gate, w_up, w_down, final_norm_w):
    assert w_in.shape[0] == 1, "single-layer trunk"
    assert x_prompt.shape[1] % 256 == 0 and x_sample.shape[1] == 8 and x_sample.shape[0] % 8 == 0
    assert w_in.shape[2] == D_IN_PROJ
    w_in_r = _w_in_prep(jnp.swapaxes(w_in, 1, 2), tk=256)
    zeros8 = jnp.zeros((GDN_HEADS,), F32)
    tail = jnp.zeros((LANES - GATE_DT2 - SSM_HEADS,), F32)
    gbias = jnp.concatenate([zeros8, gdn_dt_bias[0], ssm_dt_bias[0], ssm_dt_bias[0], tail])[None]
    galog = jnp.concatenate([zeros8, gdn_A_log[0], ssm_A_log[0], ssm_A_log[0], tail])[None]
    p = dict(
        attn_norm_w=attn_norm_w, w_in_r=w_in_r, gdn_conv_w=gdn_conv_w[0], gbias=gbias, galog=galog,
        gdn_norm_w=gdn_norm_w,
        cwx=ssm_conv_w[0][:, :SSM_DI], cbx=ssm_conv_b[:, :SSM_DI],
        cwbc=ssm_conv_w[0][:, SSM_DI:], cbbc=ssm_conv_b[:, SSM_DI:],
        dcols=jnp.repeat(ssm_D[0], SSM_P)[None], ssm_norm_w=ssm_norm_w,
        w_out16=w_out[0].astype(BF16), ffn_norm_w=ffn_norm_w,
        wg16=w_gate[0].astype(BF16), wu16=w_up[0].astype(BF16), wd16=w_down[0].astype(BF16),
        final_norm_w=final_norm_w[None],
    )
    y_p, st_p = _trunk(x_prompt, None, p)
    y_s, st_s = _trunk(x_sample, (state_gdn_conv[0], state_gdn[0], state_ssm_conv[0], state_ssm[0]), p)
    return (y_p, y_s, st_p[0], st_p[1], st_p[2], st_p[3], st_s[0], st_s[1], st_s[2], st_s[3])
```

```python
import functools

import jax
import jax.numpy as jnp
from jax import lax
from jax.experimental import pallas as pl
from jax.experimental.pallas import tpu as pltpu

F32 = jnp.float32
BF16 = jnp.bfloat16

D_MODEL = 2048
GDN_HEADS = 8
GDN_DK = 128
GDN_DV = 128
GDN_QK = GDN_HEADS * GDN_DK
GDN_V = GDN_HEADS * GDN_DV
GDN_CONV_CH = 2 * GDN_QK + GDN_V
SSM_P = 64
SSM_N = 128
SSM_GROUPS = 2
SSM_DI = 1024
SSM_HEADS = SSM_DI // SSM_P
SSM_BC = SSM_GROUPS * SSM_N
SSM_CONV_CH = SSM_DI + 2 * SSM_BC
CONV_W = 4
CHUNK = 64
D_FF = 5632
EPS = 1e-6

OFF_QKV = 0
OFF_ZG = OFF_QKV + GDN_CONV_CH
OFF_ZS = OFF_ZG + GDN_V
OFF_XS = OFF_ZS + SSM_DI
OFF_BC = OFF_XS + SSM_DI
OFF_GATE = OFF_BC + 2 * SSM_BC
LANES = 128
GATE_B = 0
GATE_A = GATE_B + GDN_HEADS
GATE_DT = GATE_A + GDN_HEADS
GATE_DT2 = GATE_DT + SSM_HEADS
N_PROJ = 6912

VMEM_LIMIT = 52 * 1024 * 1024

_HIGHEST = lax.Precision.HIGHEST


def _silu(x):
    return x * jax.nn.sigmoid(x)


def _softplus(x):
    return jnp.maximum(x, 0.0) + jnp.log1p(jnp.exp(-jnp.abs(x)))


def _dot(a, b):
    return jnp.dot(a, b, preferred_element_type=F32)


def _dot_nt(a, b):
    return lax.dot_general(a, b, (((1,), (1,)), ((), ())), preferred_element_type=F32)


def _dot_tn(a, b):
    return lax.dot_general(a, b, (((0,), (0,)), ((), ())), preferred_element_type=F32)


def _transpose_rows(a):
    r = a.shape[0]
    if r < LANES:
        a = jnp.concatenate([a, jnp.zeros((LANES - r, LANES), a.dtype)], axis=0)
    return a.T


def _rms_rows(x, w):
    return x * lax.rsqrt(jnp.mean(x * x, axis=-1, keepdims=True) + EPS) * w


W_B = OFF_ZS
W_ZS = W_B + 2 * GDN_HEADS
W_DT = W_ZS + SSM_DI + SSM_CONV_CH
D_IN_PROJ = W_DT + SSM_HEADS


def _wprep_kernel(w_ref, o_ref):
    cols = o_ref.shape[1]
    o_ref[:W_B, :] = w_ref[0, :W_B, :].astype(BF16)
    o_ref[W_B:OFF_GATE, :] = w_ref[0, W_ZS:W_DT, :].astype(BF16)
    dt = w_ref[0, W_DT:D_IN_PROJ, :].astype(BF16)
    o_ref[OFF_GATE:OFF_GATE + GATE_DT, :] = w_ref[0, W_B:W_ZS, :].astype(BF16)
    o_ref[OFF_GATE + GATE_DT:OFF_GATE + GATE_DT2, :] = dt
    o_ref[OFF_GATE + GATE_DT2:OFF_GATE + GATE_DT2 + SSM_HEADS, :] = dt
    o_ref[OFF_GATE + GATE_DT2 + SSM_HEADS:, :] = jnp.zeros((N_PROJ - OFF_GATE - GATE_DT2 - SSM_HEADS, cols), BF16)


def _w_in_prep(w_in_t, *, tk):
    return pl.pallas_call(
        _wprep_kernel,
        grid=(D_MODEL // tk,),
        in_specs=[pl.BlockSpec((1, D_IN_PROJ, tk), lambda i: (0, 0, i))],
        out_specs=pl.BlockSpec((N_PROJ, tk), lambda i: (0, i)),
        out_shape=jax.ShapeDtypeStruct((N_PROJ, D_MODEL), BF16),
        compiler_params=pltpu.CompilerParams(
            dimension_semantics=("parallel",), vmem_limit_bytes=VMEM_LIMIT),
        name="w_in_prep",
    )(w_in_t)


INPROJ_SUB = 4


def _inproj_kernel(x_ref, nw_ref, wt_ref, o_ref, h_ref):
    j = pl.program_id(1)

    @pl.when(j == 0)
    def _first():
        rs = h_ref.shape[0] // INPROJ_SUB
        for r in range(INPROJ_SUB):
            rows = slice(r * rs, (r + 1) * rs)
            h = _rms_rows(x_ref[rows, :], nw_ref[...]).astype(BF16)
            h_ref[rows, :] = h
            o_ref[rows, :] = _dot_nt(h, wt_ref[...])

    @pl.when(j > 0)
    def _rest():
        o_ref[...] = _dot_nt(h_ref[...], wt_ref[...])


def _in_proj(x2d, norm_w, w_in_r, *, tm, tn):
    T = x2d.shape[0]
    return pl.pallas_call(
        _inproj_kernel,
        grid=(T // tm, N_PROJ // tn),
        in_specs=[
            pl.BlockSpec((tm, D_MODEL), lambda i, j: (i, 0)),
            pl.BlockSpec((1, D_MODEL), lambda i, j: (0, 0)),
            pl.BlockSpec((tn, D_MODEL), lambda i, j: (j, 0)),
        ],
        out_specs=pl.BlockSpec((tm, tn), lambda i, j: (i, j)),
        out_shape=jax.ShapeDtypeStruct((T, N_PROJ), F32),
        scratch_shapes=[pltpu.VMEM((tm, D_MODEL), BF16)],
        compiler_params=pltpu.CompilerParams(
            dimension_semantics=("parallel", "arbitrary"), vmem_limit_bytes=VMEM_LIMIT),
        name="in_proj",
    )(x2d, norm_w, w_in_r)


def _conv_block(x_ref, xpad_ref, cw_ref, cb_ref, dst_ref, Lb, C, post):
    xpad_ref[8:8 + Lb, :] = x_ref[...]
    rs = min(Lb, CHUNK)
    for sb in range(Lb // rs):
        r = sb * rs
        for s in range(C // LANES):
            cols = slice(s * LANES, (s + 1) * LANES)
            acc = xpad_ref[8 + r:8 + r + rs, cols] * cw_ref[3:4, cols]
            for i in range(CONV_W - 1):
                acc = acc + xpad_ref[5 + i + r:5 + i + r + rs, cols] * cw_ref[i:i + 1, cols]
            if cb_ref is not None:
                acc = acc + cb_ref[:, cols]
            dst_ref[r:r + rs, cols] = post(s, _silu(acc))
    last3 = xpad_ref[Lb + 5:Lb + 8, :]
    xpad_ref[5:8, :] = last3
    return last3


STACK = 128


def _gdn_qk_post(s, y):
    if s < 2 * GDN_HEADS:
        y = y * lax.rsqrt(jnp.sum(y * y, axis=-1, keepdims=True) + EPS)
        if s < GDN_HEADS:
            y = y * (GDN_DK ** -0.5)
    return y


def _gdn_local(items, glen):
    sh = glen.bit_length() - 1
    row = lax.broadcasted_iota(jnp.int32, (STACK, STACK), 0)
    col = lax.broadcasted_iota(jnp.int32, (STACK, STACK), 1)
    same = (row >> sh) == (col >> sh)
    incl = same & (row >= col)
    strict = same & (row > col)
    eye = (row == col).astype(F32)

    decay = [jnp.exp(jnp.where(incl, it["g"] - it["g"].T, -jnp.inf)) for it in items]
    kb = [it["k"] * it["beta"] for it in items]
    qkk = [_dot_nt(jnp.concatenate([it["q"], b], axis=0).astype(BF16), it["k"].astype(BF16))
           for it, b in zip(items, kb)]
    qk = [x[:STACK] * d for x, d in zip(qkk, decay)]
    nmat = [jnp.where(strict, -(x[STACK:] * d), 0.0) for x, d in zip(qkk, decay)]
    tinv = [eye + n for n in nmat]
    if sh >= 2:
        pw = [_dot(n.astype(BF16), n.astype(BF16)) for n in nmat]
        for _ in range(sh - 2):
            x = [_dot(jnp.concatenate([t, p], axis=0).astype(BF16), p.astype(BF16))
                 for t, p in zip(tinv, pw)]
            tinv = [t + y[:STACK] for t, y in zip(tinv, x)]
            pw = [y[STACK:] for y in x]
        tinv = [t + _dot(t.astype(BF16), p.astype(BF16)) for t, p in zip(tinv, pw)]
    uw = [_dot(t.astype(BF16),
               jnp.concatenate([it["v"] * it["beta"], b * it["eg"]], axis=1).astype(BF16))
          for t, it, b in zip(tinv, items, kb)]
    return [(x[:, :GDN_DV], x[:, GDN_DV:]) for x in uw], qk


GDN_LOCAL_CHUNKS = 2


def _gdn_prompt_kernel(qkv_ref, zg_ref, gate_ref, cw_ref, gbias_ref, galog_ref, nw_ref,
                       o_ref, cst_out_ref, sst_out_ref,
                       xpad_ref, qkvc_ref, s_ref, u_ref, wq16_ref, kd16_ref, qk16_ref, egl_ref, *, Lb):
    c = CHUNK
    l = pl.program_id(1)
    nl = pl.num_programs(1)

    @pl.when(l == 0)
    def _init():
        xpad_ref[0:8, :] = jnp.zeros((8, GDN_CONV_CH), F32)
        s_ref[...] = jnp.zeros(s_ref.shape, F32)

    _conv_block(qkv_ref, xpad_ref, cw_ref, None, qkvc_ref, Lb, GDN_CONV_CH, _gdn_qk_post)

    row_i = lax.broadcasted_iota(jnp.int32, (c, c), 0)
    col_i = lax.broadcasted_iota(jnp.int32, (c, c), 1)
    tril_f = (row_i >= col_i).astype(F32)
    gbias = gbias_ref[...]
    nega = -jnp.exp(galog_ref[...])
    nw = nw_ref[...]
    n_pairs = GDN_HEADS // 2

    def local_chunk_items(ci):
        rows = slice(ci * c, (ci + 1) * c)
        graw = gate_ref[rows, :]
        sp = _softplus(graw + gbias)
        beta_all = jax.nn.sigmoid(graw)
        G = jnp.dot(tril_f, nega * sp, precision=_HIGHEST, preferred_element_type=F32)
        eG = jnp.exp(G)
        glast = G[c - 1:c, :]
        eGrev = jnp.exp(glast - G)
        egl_ref[ci] = jnp.broadcast_to(jnp.exp(glast), (8, LANES))

        def heads(off, a, b):
            return jnp.concatenate([qkvc_ref[rows, off + a * LANES:off + (a + 1) * LANES],
                                    qkvc_ref[rows, off + b * LANES:off + (b + 1) * LANES]], axis=0)

        def colstack(m, a, b):
            return jnp.concatenate([jnp.broadcast_to(m[:, a:a + 1], (c, LANES)),
                                    jnp.broadcast_to(m[:, b:b + 1], (c, LANES))], axis=0)

        items = []
        for pr in range(n_pairs):
            a, b = 2 * pr, 2 * pr + 1
            items.append(dict(
                q=heads(0, a, b), k=heads(GDN_QK, a, b), v=heads(2 * GDN_QK, a, b),
                beta=colstack(beta_all, GATE_B + a, GATE_B + b),
                g=colstack(G, GATE_A + a, GATE_A + b),
                eg=colstack(eG, GATE_A + a, GATE_A + b),
                egrev=colstack(eGrev, GATE_A + a, GATE_A + b)))
        return items

    def local_group(cis):
        items = [it for ci in cis for it in local_chunk_items(ci)]
        uw, qk = _gdn_local(items, c)
        for n, it in enumerate(items):
            idx = cis[0] * n_pairs + n
            u, w = uw[n]
            qd = it["q"] * it["eg"]
            u_ref[idx] = u
            for hh in range(2):
                hs = slice(hh * c, (hh + 1) * c)
                wq16_ref[2 * idx + hh] = jnp.concatenate([w[hs], qd[hs]], axis=0).astype(BF16)
            kd16_ref[idx] = (it["k"] * it["egrev"]).astype(BF16)
            qk16_ref[idx] = qk[n].astype(BF16)

    def recurrent(ci):
        rows = slice(ci * c, (ci + 1) * c)
        egl = egl_ref[ci][0:1]
        r = [[_dot(wq16_ref[2 * (ci * n_pairs + pr) + hh], s_ref[2 * pr + hh].astype(BF16)) for hh in range(2)]
             for pr in range(n_pairs)]
        v16 = [(u_ref[ci * n_pairs + pr] - jnp.concatenate([r[pr][0][:c], r[pr][1][:c]], axis=0)).astype(BF16)
               for pr in range(n_pairs)]
        o = [jnp.concatenate([r[pr][0][c:], r[pr][1][c:]], axis=0) + _dot(qk16_ref[ci * n_pairs + pr], v16[pr])
             for pr in range(n_pairs)]
        for pr in range(n_pairs):
            kd16 = kd16_ref[ci * n_pairs + pr]
            for hh in range(2):
                h = 2 * pr + hh
                ga = GATE_A + h
                hs = slice(hh * c, (hh + 1) * c)
                s_ref[h] = s_ref[h] * egl[:, ga:ga + 1] + _dot_tn(kd16[hs], v16[pr][hs])
        for pr in range(n_pairs):
            for hh in range(2):
                h = 2 * pr + hh
                z = zg_ref[rows, h * GDN_DV:(h + 1) * GDN_DV]
                o_ref[rows, h * GDN_DV:(h + 1) * GDN_DV] = (
                    _rms_rows(o[pr][hh * c:(hh + 1) * c], nw) * _silu(z)).astype(o_ref.dtype)

    for g0 in range(0, Lb // c, GDN_LOCAL_CHUNKS):
        cis = list(range(g0, g0 + GDN_LOCAL_CHUNKS))
        local_group(cis)
        for ci in cis:
            recurrent(ci)

    @pl.when(l == nl - 1)
    def _final():
        cst_out_ref[0] = xpad_ref[5:8, :]
        sst_out_ref[0] = s_ref[...]


def _gdn_prompt(proj, cw, gbias, galog, nw, *, B, L, Lb):
    nl = L // Lb
    n_tiles = (Lb // CHUNK) * (GDN_HEADS // 2)
    row = lambda b, l: b * nl + l
    const = lambda shape: pl.BlockSpec(shape, lambda b, l: (0,) * len(shape))
    return pl.pallas_call(
        functools.partial(_gdn_prompt_kernel, Lb=Lb),
        grid=(B, nl),
        in_specs=[
            pl.BlockSpec((Lb, GDN_CONV_CH), lambda b, l: (row(b, l), OFF_QKV // GDN_CONV_CH)),
            pl.BlockSpec((Lb, GDN_V), lambda b, l: (row(b, l), OFF_ZG // GDN_V)),
            pl.BlockSpec((Lb, LANES), lambda b, l: (row(b, l), OFF_GATE // LANES)),
            const((CONV_W, GDN_CONV_CH)), const((1, LANES)), const((1, LANES)), const((1, GDN_DV)),
        ],
        out_specs=[
            pl.BlockSpec((Lb, GDN_V), lambda b, l: (row(b, l), 0)),
            pl.BlockSpec((1, CONV_W - 1, GDN_CONV_CH), lambda b, l: (b, 0, 0)),
            pl.BlockSpec((1, GDN_HEADS, GDN_DK, GDN_DV), lambda b, l: (b, 0, 0, 0)),
        ],
        out_shape=[
            jax.ShapeDtypeStruct((B * L, GDN_V), BF16),
            jax.ShapeDtypeStruct((B, CONV_W - 1, GDN_CONV_CH), F32),
            jax.ShapeDtypeStruct((B, GDN_HEADS, GDN_DK, GDN_DV), F32),
        ],
        scratch_shapes=[
            pltpu.VMEM((Lb + 8, GDN_CONV_CH), F32),
            pltpu.VMEM((Lb, GDN_CONV_CH), F32),
            pltpu.VMEM((GDN_HEADS, GDN_DK, GDN_DV), F32),
            pltpu.VMEM((n_tiles, STACK, GDN_DV), F32),
            pltpu.VMEM((2 * n_tiles, STACK, GDN_DK), BF16),
            pltpu.VMEM((n_tiles, STACK, GDN_DK), BF16),
            pltpu.VMEM((n_tiles, STACK, STACK), BF16),
            pltpu.VMEM((Lb // CHUNK, 8, LANES), F32),
        ],
        compiler_params=pltpu.CompilerParams(
            dimension_semantics=("parallel", "arbitrary"), vmem_limit_bytes=VMEM_LIMIT),
        name="gdn_prompt",
    )(proj, proj, proj, cw, gbias, galog, nw)


def _gdn_sample_kernel(qkv_ref, zg_ref, gate_ref, cst_ref, sst_ref, cw_ref, gbias_ref, galog_ref, nw_ref,
                       o_ref, cst_out_ref, sst_out_ref, xpad_ref, qkvc_ref, *, nb, L):
    R = nb * L
    sh = L.bit_length() - 1
    for bi in range(nb):
        xp = xpad_ref.at[bi]
        xp[5:8, :] = cst_ref[bi]
        cst_out_ref[bi] = _conv_block(
            qkv_ref.at[pl.ds(bi * L, L)], xp, cw_ref, None, qkvc_ref.at[pl.ds(bi * L, L)],
            L, GDN_CONV_CH, _gdn_qk_post)

    row_i = lax.broadcasted_iota(jnp.int32, (R, R), 0)
    col_i = lax.broadcasted_iota(jnp.int32, (R, R), 1)
    tril_f = (((row_i >> sh) == (col_i >> sh)) & (row_i >= col_i)).astype(F32)
    graw = gate_ref[...]
    sp = _softplus(graw + gbias_ref[...])
    beta_all = jax.nn.sigmoid(graw)
    G = jnp.dot(tril_f, -jnp.exp(galog_ref[...]) * sp, precision=_HIGHEST, preferred_element_type=F32)
    glast = [G[bi * L + L - 1:bi * L + L, :] for bi in range(nb)]
    eG = jnp.exp(G)
    eGrev = jnp.exp(jnp.concatenate([jnp.broadcast_to(x, (L, LANES)) for x in glast], axis=0) - G)
    egl = [jnp.exp(x) for x in glast]
    nw = nw_ref[...]

    n_st = R // (2 * L)

    def tiles(ref, st, off):
        return jnp.concatenate(
            [ref[st * 2 * L:(st + 1) * 2 * L, off + h * LANES:off + (h + 1) * LANES]
             for h in range(GDN_HEADS)], axis=0)

    def colstack(m, st, off):
        return jnp.concatenate(
            [jnp.broadcast_to(m[st * 2 * L:(st + 1) * 2 * L, off + h:off + h + 1], (2 * L, LANES))
             for h in range(GDN_HEADS)], axis=0)

    items = [dict(q=tiles(qkvc_ref, st, 0), k=tiles(qkvc_ref, st, GDN_QK), v=tiles(qkvc_ref, st, 2 * GDN_QK),
                  beta=colstack(beta_all, st, GATE_B), g=colstack(G, st, GATE_A),
                  eg=colstack(eG, st, GATE_A), egrev=colstack(eGrev, st, GATE_A))
             for st in range(n_st)]
    uw, qk = _gdn_local(items, L)

    groups = [(h, bi) for h in range(GDN_HEADS) for bi in range(2)]
    r = []
    for st, it in enumerate(items):
        w = uw[st][1]
        qd = it["q"] * it["eg"]
        r.append([
            _dot(jnp.concatenate([w[gi * L:(gi + 1) * L], qd[gi * L:(gi + 1) * L]], axis=0).astype(BF16),
                 sst_ref[2 * st + bi, h].astype(BF16))
            for gi, (h, bi) in enumerate(groups)])
    v_new = [uw[st][0] - jnp.concatenate([x[:L] for x in r[st]], axis=0) for st in range(n_st)]
    o = [jnp.concatenate([x[L:] for x in r[st]], axis=0)
         + _dot(qk[st].astype(BF16), v_new[st].astype(BF16)) for st in range(n_st)]
    for st, it in enumerate(items):
        kd = it["k"] * it["egrev"]
        for gi, (h, bi) in enumerate(groups):
            b = 2 * st + bi
            ga = GATE_A + h
            rs = slice(gi * L, (gi + 1) * L)
            sst_out_ref[b, h] = (sst_ref[b, h] * egl[b][:, ga:ga + 1]
                                 + _dot_tn(kd[rs].astype(BF16), v_new[st][rs].astype(BF16)))
    for st in range(n_st):
        out = (_rms_rows(o[st], nw) * _silu(tiles(zg_ref, st, 0))).astype(o_ref.dtype)
        for h in range(GDN_HEADS):
            o_ref[st * 2 * L:(st + 1) * 2 * L, h * GDN_DV:(h + 1) * GDN_DV] = out[h * 2 * L:(h + 1) * 2 * L]


def _gdn_sample(proj, conv_state, S_state, cw, gbias, galog, nw, *, B, L, nb):
    R = nb * L
    const = lambda shape: pl.BlockSpec(shape, lambda i: (0,) * len(shape))
    return pl.pallas_call(
        functools.partial(_gdn_sample_kernel, nb=nb, L=L),
        grid=(B // nb,),
        in_specs=[
            pl.BlockSpec((R, GDN_CONV_CH), lambda i: (i, OFF_QKV // GDN_CONV_CH)),
            pl.BlockSpec((R, GDN_V), lambda i: (i, OFF_ZG // GDN_V)),
            pl.BlockSpec((R, LANES), lambda i: (i, OFF_GATE // LANES)),
            pl.BlockSpec((nb, CONV_W - 1, GDN_CONV_CH), lambda i: (i, 0, 0)),
            pl.BlockSpec((nb, GDN_HEADS, GDN_DK, GDN_DV), lambda i: (i, 0, 0, 0)),
            const((CONV_W, GDN_CONV_CH)), const((1, LANES)), const((1, LANES)), const((1, GDN_DV)),
        ],
        out_specs=[
            pl.BlockSpec((R, GDN_V), lambda i: (i, 0)),
            pl.BlockSpec((nb, CONV_W - 1, GDN_CONV_CH), lambda i: (i, 0, 0)),
            pl.BlockSpec((nb, GDN_HEADS, GDN_DK, GDN_DV), lambda i: (i, 0, 0, 0)),
        ],
        out_shape=[
            jax.ShapeDtypeStruct((B * L, GDN_V), BF16),
            jax.ShapeDtypeStruct((B, CONV_W - 1, GDN_CONV_CH), F32),
            jax.ShapeDtypeStruct((B, GDN_HEADS, GDN_DK, GDN_DV), F32),
        ],
        scratch_shapes=[
            pltpu.VMEM((nb, L + 8, GDN_CONV_CH), F32),
            pltpu.VMEM((R, GDN_CONV_CH), F32),
        ],
        compiler_params=pltpu.CompilerParams(
            dimension_semantics=("parallel",), vmem_limit_bytes=VMEM_LIMIT),
        name="gdn_sample",
    )(proj, proj, proj, conv_state, S_state, cw, gbias, galog, nw)


N_PAIRS = SSM_HEADS // 2
PAIRS_PER_GROUP = N_PAIRS // SSM_GROUPS
GROUP_W = SSM_DI // SSM_GROUPS


def _ssm_tile(graw, gbias, nega, ld_x, ld_b, ld_c, ld_z, dcols_ref, nw_ref, st_o, get_h, set_h, glen):
    c = CHUNK
    P = SSM_P
    nseq = c // glen
    sh = glen.bit_length() - 1
    ri = lax.broadcasted_iota(jnp.int32, (c, c), 0)
    ci = lax.broadcasted_iota(jnp.int32, (c, c), 1)
    tril_f = (((ri >> sh) == (ci >> sh)) & (ri >= ci)).astype(F32)
    sp = _softplus(graw + gbias)
    acum = jnp.dot(tril_f, nega * sp, precision=_HIGHEST, preferred_element_type=F32)
    lasts = [acum[s * glen + glen - 1:(s + 1) * glen, :] for s in range(nseq)]
    alast = jnp.concatenate([jnp.broadcast_to(x, (glen, LANES)) for x in lasts], axis=0)
    dtrev = sp * jnp.exp(alast - acum)
    eal = [jnp.exp(x) for x in lasts]
    lane = lax.broadcasted_iota(jnp.int32, (c, LANES), 1)
    row = lax.broadcasted_iota(jnp.int32, (c, LANES), 0)
    m = jnp.where(lane < GATE_DT2, acum, sp)
    mt = jnp.concatenate([m, m], axis=0).T
    left = lane < P
    left_row = left[0:1]
    j = jnp.where(left, lane, lane - P)
    tril2 = ((row >> sh) == (j >> sh)) & (row >= j)
    rowh = lax.broadcasted_iota(jnp.int32, (2 * P, SSM_N), 0) < P

    def expand(mat, c0):
        return jnp.where(left, jnp.broadcast_to(mat[:, c0:c0 + 1], (c, LANES)),
                         jnp.broadcast_to(mat[:, c0 + 1:c0 + 2], (c, LANES)))

    def rowsel(base, e):
        return jnp.where(left_row, mt[base + 2 * e:base + 2 * e + 1, :], mt[base + 2 * e + 1:base + 2 * e + 2, :])

    for g in range(SSM_GROUPS):
        Bg = ld_b(g)
        Cg = ld_c(g)
        Bg16 = Bg.astype(BF16)
        Cg16 = Cg.astype(BF16)
        cb2 = _dot_nt(Cg16, jnp.concatenate([Bg16, Bg16], axis=0))
        pairs = [g * PAIRS_PER_GROUP + e4 for e4 in range(PAIRS_PER_GROUP)]
        acol = [expand(acum, GATE_DT + 2 * e) for e in pairs]
        scores16 = [
            (cb2 * jnp.exp(jnp.where(tril2, a - rowsel(GATE_DT, e), -jnp.inf)) * rowsel(GATE_DT2, e)).astype(BF16)
            for a, e in zip(acol, pairs)]
        xp = [ld_x(e) for e in pairs]
        bd16 = [jnp.concatenate([jnp.where(left, x, 0.0), jnp.where(left, 0.0, x)], axis=0).astype(BF16)
                for x in xp]
        ydiag = [_dot(s, b) for s, b in zip(scores16, bd16)]
        if nseq == 1:
            yoff = [_dot_nt(Cg16, get_h(0, e).astype(BF16)) for e in pairs]
        else:
            yoff = [jnp.concatenate(
                [_dot_nt(Cg[s * glen:(s + 1) * glen].astype(BF16), get_h(s, e).astype(BF16))
                 for s in range(nseq)], axis=0) for e in pairs]
        y = [yd + yo * jnp.exp(a) + dcols_ref[:, e * 2 * P:(e + 1) * 2 * P] * x
             for yd, yo, a, e, x in zip(ydiag, yoff, acol, pairs, xp)]
        xdr = [x * expand(dtrev, GATE_DT + 2 * e) for x, e in zip(xp, pairs)]
        for e, xd in zip(pairs, xdr):
            c0 = GATE_DT + 2 * e
            for s in range(nseq):
                rs = slice(s * glen, (s + 1) * glen)
                ealcol = jnp.where(rowh, eal[s][:, c0:c0 + 1], eal[s][:, c0 + 1:c0 + 2])
                set_h(s, e, get_h(s, e) * ealcol + _dot_tn(xd[rs].astype(BF16), Bg[rs].astype(BF16)))
        yg = jnp.concatenate(y, axis=1) * _silu(ld_z(g))
        gcols = slice(g * GROUP_W, (g + 1) * GROUP_W)
        st_o(g, _rms_rows(yg, nw_ref[:, gcols]))


def _ssm_prompt_kernel(xs_ref, bc_ref, zs_ref, gate_ref, cwx_ref, cbx_ref, cwbc_ref, cbbc_ref,
                       gbias_ref, galog_ref, dcols_ref, nw_ref,
                       o_ref, cst_out_ref, hst_out_ref, xpadx_ref, xpadbc_ref, xc_ref, bcc_ref, hh_ref, *, Lb):
    c = CHUNK
    l = pl.program_id(1)
    nl = pl.num_programs(1)

    @pl.when(l == 0)
    def _init():
        hh_ref[...] = jnp.zeros(hh_ref.shape, F32)
        xpadx_ref[0:8, :] = jnp.zeros((8, SSM_DI), F32)
        xpadbc_ref[0:8, :] = jnp.zeros((8, 2 * SSM_BC), F32)

    ident = lambda s, y: y
    _conv_block(xs_ref, xpadx_ref, cwx_ref, cbx_ref, xc_ref, Lb, SSM_DI, ident)
    _conv_block(bc_ref, xpadbc_ref, cwbc_ref, cbbc_ref, bcc_ref, Lb, 2 * SSM_BC, ident)

    gbias = gbias_ref[...]
    nega = -jnp.exp(galog_ref[...])

    def set_h(s, e, val):
        hh_ref[e] = val

    for ci in range(Lb // c):
        rows = slice(ci * c, (ci + 1) * c)

        def st_o(g, val, rows=rows):
            o_ref[rows, g * GROUP_W:(g + 1) * GROUP_W] = val.astype(o_ref.dtype)

        _ssm_tile(
            gate_ref[rows, :], gbias, nega,
            lambda e, rows=rows: xc_ref[rows, e * LANES:(e + 1) * LANES],
            lambda g, rows=rows: bcc_ref[rows, g * SSM_N:(g + 1) * SSM_N],
            lambda g, rows=rows: bcc_ref[rows, SSM_BC + g * SSM_N:SSM_BC + (g + 1) * SSM_N],
            lambda g, rows=rows: zs_ref[rows, g * GROUP_W:(g + 1) * GROUP_W],
            dcols_ref, nw_ref, st_o, lambda s, e: hh_ref[e], set_h, c)

    @pl.when(l == nl - 1)
    def _final():
        cst_out_ref[0, :, :SSM_DI] = xpadx_ref[5:8, :]
        cst_out_ref[0, :, SSM_DI:] = xpadbc_ref[5:8, :]
        hst_out_ref[0] = hh_ref[...]


def _ssm_sample_kernel(xs_ref, bc_ref, zs_ref, gate_ref, cst_ref, hst_ref, cwx_ref, cbx_ref, cwbc_ref, cbbc_ref,
                       gbias_ref, galog_ref, dcols_ref, nw_ref,
                       o_ref, cst_out_ref, hst_out_ref, xpadx_ref, xpadbc_ref, xc_ref, bcc_ref, *, L):
    ident = lambda s, y: y
    for bi in range(CHUNK // L):
        rs = pl.ds(bi * L, L)
        xpx = xpadx_ref.at[bi]
        xpb = xpadbc_ref.at[bi]
        xpx[5:8, :] = cst_ref[bi, :, :SSM_DI]
        xpb[5:8, :] = cst_ref[bi, :, SSM_DI:]
        cst_out_ref[bi, :, :SSM_DI] = _conv_block(
            xs_ref.at[rs], xpx, cwx_ref, cbx_ref, xc_ref.at[rs], L, SSM_DI, ident)
        cst_out_ref[bi, :, SSM_DI:] = _conv_block(
            bc_ref.at[rs], xpb, cwbc_ref, cbbc_ref, bcc_ref.at[rs], L, 2 * SSM_BC, ident)

    def st_o(g, val):
        o_ref[:, g * GROUP_W:(g + 1) * GROUP_W] = val.astype(o_ref.dtype)

    def set_h(s, e, val):
        hst_out_ref[s, e] = val

    _ssm_tile(
        gate_ref[...], gbias_ref[...], -jnp.exp(galog_ref[...]),
        lambda e: xc_ref[:, e * LANES:(e + 1) * LANES],
        lambda g: bcc_ref[:, g * SSM_N:(g + 1) * SSM_N],
        lambda g: bcc_ref[:, SSM_BC + g * SSM_N:SSM_BC + (g + 1) * SSM_N],
        lambda g: zs_ref[:, g * GROUP_W:(g + 1) * GROUP_W],
        dcols_ref, nw_ref, st_o, lambda s, e: hst_ref[s, e], set_h, L)


def _ssm_const_specs():
    const = lambda shape: pl.BlockSpec(shape, lambda *idx: (0,) * len(shape))
    return [
        const((CONV_W, SSM_DI)), const((1, SSM_DI)), const((CONV_W, 2 * SSM_BC)), const((1, 2 * SSM_BC)),
        const((1, LANES)), const((1, LANES)), const((1, SSM_DI)), const((1, SSM_DI)),
    ]


def _ssm_prompt(proj, cwx, cbx, cwbc, cbbc, gbias, galog, dcols, nw, *, B, L, Lb):
    nl = L // Lb
    row = lambda b, l: b * nl + l
    return pl.pallas_call(
        functools.partial(_ssm_prompt_kernel, Lb=Lb),
        grid=(B, nl),
        in_specs=[
            pl.BlockSpec((Lb, SSM_DI), lambda b, l: (row(b, l), OFF_XS // SSM_DI)),
            pl.BlockSpec((Lb, 2 * SSM_BC), lambda b, l: (row(b, l), OFF_BC // (2 * SSM_BC))),
            pl.BlockSpec((Lb, SSM_DI), lambda b, l: (row(b, l), OFF_ZS // SSM_DI)),
            pl.BlockSpec((Lb, LANES), lambda b, l: (row(b, l), OFF_GATE // LANES)),
        ] + _ssm_const_specs(),
        out_specs=[
            pl.BlockSpec((Lb, SSM_DI), lambda b, l: (row(b, l), 0)),
            pl.BlockSpec((1, CONV_W - 1, SSM_CONV_CH), lambda b, l: (b, 0, 0)),
            pl.BlockSpec((1, N_PAIRS, 2 * SSM_P, SSM_N), lambda b, l: (b, 0, 0, 0)),
        ],
        out_shape=[
            jax.ShapeDtypeStruct((B * L, SSM_DI), BF16),
            jax.ShapeDtypeStruct((B, CONV_W - 1, SSM_CONV_CH), F32),
            jax.ShapeDtypeStruct((B, N_PAIRS, 2 * SSM_P, SSM_N), F32),
        ],
        scratch_shapes=[
            pltpu.VMEM((Lb + 8, SSM_DI), F32),
            pltpu.VMEM((Lb + 8, 2 * SSM_BC), F32),
            pltpu.VMEM((Lb, SSM_DI), F32),
            pltpu.VMEM((Lb, 2 * SSM_BC), F32),
            pltpu.VMEM((N_PAIRS, 2 * SSM_P, SSM_N), F32),
        ],
        compiler_params=pltpu.CompilerParams(
            dimension_semantics=("parallel", "arbitrary"), vmem_limit_bytes=VMEM_LIMIT),
        name="ssm_prompt",
    )(proj, proj, proj, proj, cwx, cbx, cwbc, cbbc, gbias, galog, dcols, nw)


def _ssm_sample(proj, conv_state, h_pairs, cwx, cbx, cwbc, cbbc, gbias, galog, dcols, nw, *, B, L):
    nb = CHUNK // L
    return pl.pallas_call(
        functools.partial(_ssm_sample_kernel, L=L),
        grid=(B // nb,),
        in_specs=[
            pl.BlockSpec((CHUNK, SSM_DI), lambda i: (i, OFF_XS // SSM_DI)),
            pl.BlockSpec((CHUNK, 2 * SSM_BC), lambda i: (i, OFF_BC // (2 * SSM_BC))),
            pl.BlockSpec((CHUNK, SSM_DI), lambda i: (i, OFF_ZS // SSM_DI)),
            pl.BlockSpec((CHUNK, LANES), lambda i: (i, OFF_GATE // LANES)),
            pl.BlockSpec((nb, CONV_W - 1, SSM_CONV_CH), lambda i: (i, 0, 0)),
            pl.BlockSpec((nb, N_PAIRS, 2 * SSM_P, SSM_N), lambda i: (i, 0, 0, 0)),
        ] + _ssm_const_specs(),
        out_specs=[
            pl.BlockSpec((CHUNK, SSM_DI), lambda i: (i, 0)),
            pl.BlockSpec((nb, CONV_W - 1, SSM_CONV_CH), lambda i: (i, 0, 0)),
            pl.BlockSpec((nb, N_PAIRS, 2 * SSM_P, SSM_N), lambda i: (i, 0, 0, 0)),
        ],
        out_shape=[
            jax.ShapeDtypeStruct((B * L, SSM_DI), BF16),
            jax.ShapeDtypeStruct((B, CONV_W - 1, SSM_CONV_CH), F32),
            jax.ShapeDtypeStruct((B, N_PAIRS, 2 * SSM_P, SSM_N), F32),
        ],
        scratch_shapes=[
            pltpu.VMEM((nb, L + 8, SSM_DI), F32),
            pltpu.VMEM((nb, L + 8, 2 * SSM_BC), F32),
            pltpu.VMEM((CHUNK, SSM_DI), F32),
            pltpu.VMEM((CHUNK, 2 * SSM_BC), F32),
        ],
        compiler_params=pltpu.CompilerParams(
            dimension_semantics=("parallel",), vmem_limit_bytes=VMEM_LIMIT),
        name="ssm_sample",
    )(proj, proj, proj, proj, conv_state, h_pairs, cwx, cbx, cwbc, cbbc, gbias, galog, dcols, nw)


def _outproj_kernel(x_ref, mg_ref, ms_ref, w_ref, o_ref):
    acc = _dot(mg_ref[...].astype(BF16), w_ref[:GDN_V, :])
    acc = acc + _dot(ms_ref[...].astype(BF16), w_ref[GDN_V:, :])
    o_ref[...] = x_ref[...] + acc


def _out_proj(x2d, mix_g, mix_s, w_out16, *, tm):
    T = x2d.shape[0]
    return pl.pallas_call(
        _outproj_kernel,
        grid=(T // tm,),
        in_specs=[
            pl.BlockSpec((tm, D_MODEL), lambda i: (i, 0)),
            pl.BlockSpec((tm, GDN_V), lambda i: (i, 0)),
            pl.BlockSpec((tm, SSM_DI), lambda i: (i, 0)),
            pl.BlockSpec((D_MODEL, D_MODEL), lambda i: (0, 0)),
        ],
        out_specs=pl.BlockSpec((tm, D_MODEL), lambda i: (i, 0)),
        out_shape=jax.ShapeDtypeStruct((T, D_MODEL), F32),
        compiler_params=pltpu.CompilerParams(
            dimension_semantics=("parallel",), vmem_limit_bytes=VMEM_LIMIT),
        name="out_proj",
    )(x2d, mix_g, mix_s, w_out16)


FFN_SUB = 4


def _ffn_kernel(x_ref, nw_ref, wg_ref, wu_ref, wd_ref, fnw_ref, o_ref, h_ref):
    f = pl.program_id(1)
    nf = pl.num_programs(1)
    rs = h_ref.shape[0] // FFN_SUB

    def step(first, last):
        def gate_up(r):
            rows = slice(r * rs, (r + 1) * rs)
            if first:
                h = _rms_rows(x_ref[rows, :], nw_ref[...]).astype(BF16)
                h_ref[rows, :] = h
            else:
                h = h_ref[rows, :]
            return _dot(h, wg_ref[...]), _dot(h, wu_ref[...])

        def down(r, gu):
            rows = slice(r * rs, (r + 1) * rs)
            d = _dot((_silu(gu[0]) * gu[1]).astype(BF16), wd_ref[...])
            acc = d if first else o_ref[rows, :] + d
            if last:
                o_ref[rows, :] = _rms_rows(x_ref[rows, :] + acc, fnw_ref[...])
            else:
                o_ref[rows, :] = acc

        gu = gate_up(0)
        for r in range(1, FFN_SUB):
            gu_next = gate_up(r)
            down(r - 1, gu)
            gu = gu_next
        down(FFN_SUB - 1, gu)

    pl.when(f == 0)(lambda: step(True, False))
    pl.when((f > 0) & (f < nf - 1))(lambda: step(False, False))
    pl.when(f == nf - 1)(lambda: step(False, True))


def _ffn(x2d, norm_w, wg16, wu16, wd16, final_w, *, tm, tf):
    T = x2d.shape[0]
    return pl.pallas_call(
        _ffn_kernel,
        grid=(T // tm, D_FF // tf),
        in_specs=[
            pl.BlockSpec((tm, D_MODEL), lambda i, f: (i, 0)),
            pl.BlockSpec((1, D_MODEL), lambda i, f: (0, 0)),
            pl.BlockSpec((D_MODEL, tf), lambda i, f: (0, f)),
            pl.BlockSpec((D_MODEL, tf), lambda i, f: (0, f)),
            pl.BlockSpec((tf, D_MODEL), lambda i, f: (f, 0)),
            pl.BlockSpec((1, D_MODEL), lambda i, f: (0, 0)),
        ],
        out_specs=pl.BlockSpec((tm, D_MODEL), lambda i, f: (i, 0)),
        out_shape=jax.ShapeDtypeStruct((T, D_MODEL), F32),
        scratch_shapes=[pltpu.VMEM((tm, D_MODEL), BF16)],
        compiler_params=pltpu.CompilerParams(
            dimension_semantics=("parallel", "arbitrary"), vmem_limit_bytes=VMEM_LIMIT),
        name="ffn",
    )(x2d, norm_w, wg16, wu16, wd16, final_w)


PROMPT_ROWS = 512


def _trunk(x, states, p):
    B, L, _ = x.shape
    x2d = x.reshape(B * L, D_MODEL)
    proj = _in_proj(x2d, p["attn_norm_w"], p["w_in_r"], tm=1024, tn=768)
    gdn_w = (p["gdn_conv_w"], p["gbias"], p["galog"], p["gdn_norm_w"])
    ssm_w = (p["cwx"], p["cbx"], p["cwbc"], p["cbbc"], p["gbias"], p["galog"], p["dcols"], p["ssm_norm_w"])
    pair_shape = (B, N_PAIRS, 2 * SSM_P, SSM_N)
    if states is None:
        mix_g, gconv_new, gS_new = _gdn_prompt(proj, *gdn_w, B=B, L=L, Lb=PROMPT_ROWS)
        mix_s, sconv_new, sh_new = _ssm_prompt(proj, *ssm_w, B=B, L=L, Lb=PROMPT_ROWS // 2)
    else:
        gconv, gS, sconv, sh = states
        mix_g, gconv_new, gS_new = _gdn_sample(proj, gconv, gS, *gdn_w, B=B, L=L, nb=8)
        mix_s, sconv_new, sh_new = _ssm_sample(proj, sconv, sh.reshape(pair_shape), *ssm_w, B=B, L=L)
    sh_new = sh_new.reshape(B, SSM_HEADS, SSM_P, SSM_N)
    x1 = _out_proj(x2d, mix_g, mix_s, p["w_out16"], tm=512)
    y = _ffn(x1, p["ffn_norm_w"], p["wg16"], p["wu16"], p["wd16"], p["final_norm_w"], tm=1024, tf=512)
    return y.reshape(B, L, D_MODEL), (gconv_new[None], gS_new[None], sconv_new[None], sh_new[None])


def kernel(x_prompt, x_sample, state_gdn_conv, state_gdn, state_ssm_conv, state_ssm,
           attn_norm_w, w_in, gdn_conv_w, gdn_A_log, gdn_dt_bias, gdn_norm_w,
           ssm_conv_w, ssm_conv_b, ssm_A_log, ssm_dt_bias, ssm_D, ssm_norm_w,
           w_out, ffn_norm_w, w_gate, w_up, w_down, final_norm_w):
    assert w_in.shape[0] == 1, "single-layer trunk"
    assert x_prompt.shape[1] % PROMPT_ROWS == 0 and x_sample.shape[1] == 8 and x_sample.shape[0] % 8 == 0
    assert w_in.shape[2] == D_IN_PROJ
    w_in_r = _w_in_prep(jnp.swapaxes(w_in, 1, 2), tk=256)
    zeros8 = jnp.zeros((GDN_HEADS,), F32)
    tail = jnp.zeros((LANES - GATE_DT2 - SSM_HEADS,), F32)
    gbias = jnp.concatenate([zeros8, gdn_dt_bias[0], ssm_dt_bias[0], ssm_dt_bias[0], tail])[None]
    galog = jnp.concatenate([zeros8, gdn_A_log[0], ssm_A_log[0], ssm_A_log[0], tail])[None]
    p = dict(
        attn_norm_w=attn_norm_w, w_in_r=w_in_r, gdn_conv_w=gdn_conv_w[0], gbias=gbias, galog=galog,
        gdn_norm_w=gdn_norm_w,
        cwx=ssm_conv_w[0][:, :SSM_DI], cbx=ssm_conv_b[:, :SSM_DI],
        cwbc=ssm_conv_w[0][:, SSM_DI:], cbbc=ssm_conv_b[:, SSM_DI:],
        dcols=jnp.repeat(ssm_D[0], SSM_P)[None], ssm_norm_w=ssm_norm_w,
        w_out16=w_out[0].astype(BF16), ffn_norm_w=ffn_norm_w,
        wg16=w_gate[0].astype(BF16), wu16=w_up[0].astype(BF16), wd16=w_down[0].astype(BF16),
        final_norm_w=final_norm_w[None],
    )
    y_p, st_p = _trunk(x_prompt, None, p)
    y_s, st_s = _trunk(x_sample, (state_gdn_conv[0], state_gdn[0], state_ssm_conv[0], state_ssm[0]), p)
    return (y_p, y_s, st_p[0], st_p[1], st_p[2], st_p[3], st_s[0], st_s[1], st_s[2], st_s[3])
```

```python
import functools
import itertools

import jax
import jax.numpy as jnp
from jax import lax
from jax.experimental import pallas as pl
from jax.experimental.pallas import tpu as pltpu

F32 = jnp.float32
BF16 = jnp.bfloat16

D_MODEL = 2048
GDN_HEADS = 8
GDN_DK = 128
GDN_DV = 128
GDN_QK = GDN_HEADS * GDN_DK
GDN_V = GDN_HEADS * GDN_DV
GDN_CONV_CH = 2 * GDN_QK + GDN_V
SSM_P = 64
SSM_N = 128
SSM_GROUPS = 2
SSM_DI = 1024
SSM_HEADS = SSM_DI // SSM_P
SSM_BC = SSM_GROUPS * SSM_N
SSM_CONV_CH = SSM_DI + 2 * SSM_BC
CONV_W = 4
CHUNK = 64
D_FF = 5632
EPS = 1e-6

OFF_QKV = 0
OFF_ZG = OFF_QKV + GDN_CONV_CH
OFF_ZS = OFF_ZG + GDN_V
OFF_XS = OFF_ZS + SSM_DI
OFF_BC = OFF_XS + SSM_DI
OFF_GATE = OFF_BC + 2 * SSM_BC
LANES = 128
GATE_B = 0
GATE_A = GATE_B + GDN_HEADS
GATE_DT = GATE_A + GDN_HEADS
GATE_DT2 = GATE_DT + SSM_HEADS
N_PROJ = 6912

VMEM_LIMIT = 52 * 1024 * 1024

_HIGHEST = lax.Precision.HIGHEST


def _silu(x):
    return x * jax.nn.sigmoid(x)


def _softplus(x):
    return jnp.maximum(x, 0.0) + jnp.log1p(jnp.exp(-jnp.abs(x)))


def _dot(a, b):
    return jnp.dot(a, b, preferred_element_type=F32)


def _dot_nt(a, b):
    return lax.dot_general(a, b, (((1,), (1,)), ((), ())), preferred_element_type=F32)


def _dot_tn(a, b):
    return lax.dot_general(a, b, (((0,), (0,)), ((), ())), preferred_element_type=F32)


def _transpose_rows(a):
    r = a.shape[0]
    if r < LANES:
        a = jnp.concatenate([a, jnp.zeros((LANES - r, LANES), a.dtype)], axis=0)
    return a.T


def _rms_rows(x, w):
    return x * lax.rsqrt(jnp.mean(x * x, axis=-1, keepdims=True) + EPS) * w


W_B = OFF_ZS
W_ZS = W_B + 2 * GDN_HEADS
W_DT = W_ZS + SSM_DI + SSM_CONV_CH
D_IN_PROJ = W_DT + SSM_HEADS


def _wprep_kernel(w_ref, o_ref):
    cols = o_ref.shape[1]
    o_ref[:W_B, :] = w_ref[0, :W_B, :].astype(BF16)
    o_ref[W_B:OFF_GATE, :] = w_ref[0, W_ZS:W_DT, :].astype(BF16)
    dt = w_ref[0, W_DT:D_IN_PROJ, :].astype(BF16)
    o_ref[OFF_GATE:OFF_GATE + GATE_DT, :] = w_ref[0, W_B:W_ZS, :].astype(BF16)
    o_ref[OFF_GATE + GATE_DT:OFF_GATE + GATE_DT2, :] = dt
    o_ref[OFF_GATE + GATE_DT2:OFF_GATE + GATE_DT2 + SSM_HEADS, :] = dt
    o_ref[OFF_GATE + GATE_DT2 + SSM_HEADS:, :] = jnp.zeros((N_PROJ - OFF_GATE - GATE_DT2 - SSM_HEADS, cols), BF16)


def _w_in_prep(w_in_t, *, tk):
    return pl.pallas_call(
        _wprep_kernel,
        grid=(D_MODEL // tk,),
        in_specs=[pl.BlockSpec((1, D_IN_PROJ, tk), lambda i: (0, 0, i))],
        out_specs=pl.BlockSpec((N_PROJ, tk), lambda i: (0, i)),
        out_shape=jax.ShapeDtypeStruct((N_PROJ, D_MODEL), BF16),
        compiler_params=pltpu.CompilerParams(
            dimension_semantics=("parallel",), vmem_limit_bytes=VMEM_LIMIT),
        name="w_in_prep",
    )(w_in_t)


INPROJ_SUB = 4


def _inproj_kernel(x_ref, nw_ref, wt_ref, o_ref, h_ref):
    j = pl.program_id(1)

    @pl.when(j == 0)
    def _first():
        rs = h_ref.shape[0] // INPROJ_SUB
        for r in range(INPROJ_SUB):
            rows = slice(r * rs, (r + 1) * rs)
            h = _rms_rows(x_ref[rows, :], nw_ref[...]).astype(BF16)
            h_ref[rows, :] = h
            o_ref[rows, :] = _dot_nt(h, wt_ref[...])

    @pl.when(j > 0)
    def _rest():
        o_ref[...] = _dot_nt(h_ref[...], wt_ref[...])


def _in_proj(x2d, norm_w, w_in_r, *, tm, tn):
    T = x2d.shape[0]
    return pl.pallas_call(
        _inproj_kernel,
        grid=(T // tm, N_PROJ // tn),
        in_specs=[
            pl.BlockSpec((tm, D_MODEL), lambda i, j: (i, 0)),
            pl.BlockSpec((1, D_MODEL), lambda i, j: (0, 0)),
            pl.BlockSpec((tn, D_MODEL), lambda i, j: (j, 0)),
        ],
        out_specs=pl.BlockSpec((tm, tn), lambda i, j: (i, j)),
        out_shape=jax.ShapeDtypeStruct((T, N_PROJ), F32),
        scratch_shapes=[pltpu.VMEM((tm, D_MODEL), BF16)],
        compiler_params=pltpu.CompilerParams(
            dimension_semantics=("parallel", "arbitrary"), vmem_limit_bytes=VMEM_LIMIT),
        name="in_proj",
    )(x2d, norm_w, w_in_r)


def _conv_block(x_ref, xpad_ref, cw_ref, cb_ref, dst_ref, Lb, C, post):
    xpad_ref[8:8 + Lb, :] = x_ref[...]
    rs = min(Lb, CHUNK)
    for sb in range(Lb // rs):
        r = sb * rs
        for s in range(C // LANES):
            cols = slice(s * LANES, (s + 1) * LANES)
            acc = xpad_ref[8 + r:8 + r + rs, cols] * cw_ref[3:4, cols]
            for i in range(CONV_W - 1):
                acc = acc + xpad_ref[5 + i + r:5 + i + r + rs, cols] * cw_ref[i:i + 1, cols]
            if cb_ref is not None:
                acc = acc + cb_ref[:, cols]
            dst_ref[r:r + rs, cols] = post(s, _silu(acc))
    last3 = xpad_ref[Lb + 5:Lb + 8, :]
    xpad_ref[5:8, :] = last3
    return last3


STACK = 128


def _gdn_qk_post(s, y):
    if s < 2 * GDN_HEADS:
        y = y * lax.rsqrt(jnp.sum(y * y, axis=-1, keepdims=True) + EPS)
        if s < GDN_HEADS:
            y = y * (GDN_DK ** -0.5)
    return y


def _gdn_local(items, glen):
    sh = glen.bit_length() - 1
    row = lax.broadcasted_iota(jnp.int32, (STACK, STACK), 0)
    col = lax.broadcasted_iota(jnp.int32, (STACK, STACK), 1)
    same = (row >> sh) == (col >> sh)
    incl = same & (row >= col)
    strict = same & (row > col)
    eye = (row == col).astype(F32)

    decay = [jnp.exp(jnp.where(incl, it["g"] - it["g"].T, -jnp.inf)) for it in items]
    kb = [it["k"] * it["beta"] for it in items]
    qkk = [_dot_nt(jnp.concatenate([it["q"], b], axis=0).astype(BF16), it["k"].astype(BF16))
           for it, b in zip(items, kb)]
    qk = [x[:STACK] * d for x, d in zip(qkk, decay)]
    nmat = [jnp.where(strict, -(x[STACK:] * d), 0.0) for x, d in zip(qkk, decay)]
    tinv = [eye + n for n in nmat]
    if sh >= 2:
        pw = [_dot(n.astype(BF16), n.astype(BF16)) for n in nmat]
        for _ in range(sh - 2):
            x = [_dot(jnp.concatenate([t, p], axis=0).astype(BF16), p.astype(BF16))
                 for t, p in zip(tinv, pw)]
            tinv = [t + y[:STACK] for t, y in zip(tinv, x)]
            pw = [y[STACK:] for y in x]
        tinv = [t + _dot(t.astype(BF16), p.astype(BF16)) for t, p in zip(tinv, pw)]
    uw = [_dot(t.astype(BF16),
               jnp.concatenate([it["v"] * it["beta"], b * it["eg"]], axis=1).astype(BF16))
          for t, it, b in zip(tinv, items, kb)]
    return [(x[:, :GDN_DV], x[:, GDN_DV:]) for x in uw], qk


GDN_LOCAL_CHUNKS = 2


def _gdn_prompt_parts(qkv_ref, zg_ref, gate_ref, cw_ref, gbias_ref, galog_ref, nw_ref,
                      o_ref, cst_out_ref, sst_out_ref,
                      xpad_ref, qkvc_ref, s_ref, u_ref, wq16_ref, kd16_ref, qk16_ref, egl_ref, *, Lb):
    c = CHUNK

    def init():
        xpad_ref[0:8, :] = jnp.zeros((8, GDN_CONV_CH), F32)
        s_ref[...] = jnp.zeros(s_ref.shape, F32)

    def conv():
        _conv_block(qkv_ref, xpad_ref, cw_ref, None, qkvc_ref, Lb, GDN_CONV_CH, _gdn_qk_post)

    row_i = lax.broadcasted_iota(jnp.int32, (c, c), 0)
    col_i = lax.broadcasted_iota(jnp.int32, (c, c), 1)
    tril_f = (row_i >= col_i).astype(F32)
    gbias = gbias_ref[...]
    nega = -jnp.exp(galog_ref[...])
    nw = nw_ref[...]
    n_pairs = GDN_HEADS // 2

    def local_chunk_items(ci):
        rows = slice(ci * c, (ci + 1) * c)
        graw = gate_ref[rows, :]
        sp = _softplus(graw + gbias)
        beta_all = jax.nn.sigmoid(graw)
        G = jnp.dot(tril_f, nega * sp, precision=_HIGHEST, preferred_element_type=F32)
        eG = jnp.exp(G)
        glast = G[c - 1:c, :]
        eGrev = jnp.exp(glast - G)
        egl_ref[ci] = jnp.broadcast_to(jnp.exp(glast), (8, LANES))

        def heads(off, a, b):
            return jnp.concatenate([qkvc_ref[rows, off + a * LANES:off + (a + 1) * LANES],
                                    qkvc_ref[rows, off + b * LANES:off + (b + 1) * LANES]], axis=0)

        def colstack(m, a, b):
            return jnp.concatenate([jnp.broadcast_to(m[:, a:a + 1], (c, LANES)),
                                    jnp.broadcast_to(m[:, b:b + 1], (c, LANES))], axis=0)

        items = []
        for pr in range(n_pairs):
            a, b = 2 * pr, 2 * pr + 1
            items.append(dict(
                q=heads(0, a, b), k=heads(GDN_QK, a, b), v=heads(2 * GDN_QK, a, b),
                beta=colstack(beta_all, GATE_B + a, GATE_B + b),
                g=colstack(G, GATE_A + a, GATE_A + b),
                eg=colstack(eG, GATE_A + a, GATE_A + b),
                egrev=colstack(eGrev, GATE_A + a, GATE_A + b)))
        return items

    def local_group(cis):
        items = [it for ci in cis for it in local_chunk_items(ci)]
        uw, qk = _gdn_local(items, c)
        for n, it in enumerate(items):
            idx = cis[0] * n_pairs + n
            u, w = uw[n]
            qd = it["q"] * it["eg"]
            u_ref[idx] = u
            for hh in range(2):
                hs = slice(hh * c, (hh + 1) * c)
                wq16_ref[2 * idx + hh] = jnp.concatenate([w[hs], qd[hs]], axis=0).astype(BF16)
            kd16_ref[idx] = (it["k"] * it["egrev"]).astype(BF16)
            qk16_ref[idx] = qk[n].astype(BF16)

    def recurrent(ci):
        rows = slice(ci * c, (ci + 1) * c)
        egl = egl_ref[ci][0:1]
        r = [[_dot(wq16_ref[2 * (ci * n_pairs + pr) + hh], s_ref[2 * pr + hh].astype(BF16)) for hh in range(2)]
             for pr in range(n_pairs)]
        v16 = [(u_ref[ci * n_pairs + pr] - jnp.concatenate([r[pr][0][:c], r[pr][1][:c]], axis=0)).astype(BF16)
               for pr in range(n_pairs)]
        o = [jnp.concatenate([r[pr][0][c:], r[pr][1][c:]], axis=0) + _dot(qk16_ref[ci * n_pairs + pr], v16[pr])
             for pr in range(n_pairs)]
        for pr in range(n_pairs):
            kd16 = kd16_ref[ci * n_pairs + pr]
            for hh in range(2):
                h = 2 * pr + hh
                ga = GATE_A + h
                hs = slice(hh * c, (hh + 1) * c)
                s_ref[h] = s_ref[h] * egl[:, ga:ga + 1] + _dot_tn(kd16[hs], v16[pr][hs])
        for pr in range(n_pairs):
            for hh in range(2):
                h = 2 * pr + hh
                z = zg_ref[rows, h * GDN_DV:(h + 1) * GDN_DV]
                o_ref[rows, h * GDN_DV:(h + 1) * GDN_DV] = (
                    _rms_rows(o[pr][hh * c:(hh + 1) * c], nw) * _silu(z)).astype(o_ref.dtype)

    steps = [conv]
    for g0 in range(0, Lb // c, GDN_LOCAL_CHUNKS):
        cis = list(range(g0, g0 + GDN_LOCAL_CHUNKS))
        steps.append(functools.partial(local_group, cis))
        steps += [functools.partial(recurrent, ci) for ci in cis]

    def final():
        cst_out_ref[0] = xpad_ref[5:8, :]
        sst_out_ref[0] = s_ref[...]

    return init, steps, final


def _gdn_prompt_spec(proj, cw, gbias, galog, nw, *, B, L, Lb):
    nl = L // Lb
    n_tiles = (Lb // CHUNK) * (GDN_HEADS // 2)
    row = lambda b, l: b * nl + l
    const = lambda shape: pl.BlockSpec(shape, lambda b, l: (0,) * len(shape))
    return dict(
        in_specs=[
            pl.BlockSpec((Lb, GDN_CONV_CH), lambda b, l: (row(b, l), OFF_QKV // GDN_CONV_CH)),
            pl.BlockSpec((Lb, GDN_V), lambda b, l: (row(b, l), OFF_ZG // GDN_V)),
            pl.BlockSpec((Lb, LANES), lambda b, l: (row(b, l), OFF_GATE // LANES)),
            const((CONV_W, GDN_CONV_CH)), const((1, LANES)), const((1, LANES)), const((1, GDN_DV)),
        ],
        out_specs=[
            pl.BlockSpec((Lb, GDN_V), lambda b, l: (row(b, l), 0)),
            pl.BlockSpec((1, CONV_W - 1, GDN_CONV_CH), lambda b, l: (b, 0, 0)),
            pl.BlockSpec((1, GDN_HEADS, GDN_DK, GDN_DV), lambda b, l: (b, 0, 0, 0)),
        ],
        out_shape=[
            jax.ShapeDtypeStruct((B * L, GDN_V), BF16),
            jax.ShapeDtypeStruct((B, CONV_W - 1, GDN_CONV_CH), F32),
            jax.ShapeDtypeStruct((B, GDN_HEADS, GDN_DK, GDN_DV), F32),
        ],
        scratch=[
            pltpu.VMEM((Lb + 8, GDN_CONV_CH), F32),
            pltpu.VMEM((Lb, GDN_CONV_CH), F32),
            pltpu.VMEM((GDN_HEADS, GDN_DK, GDN_DV), F32),
            pltpu.VMEM((n_tiles, STACK, GDN_DV), F32),
            pltpu.VMEM((2 * n_tiles, STACK, GDN_DK), BF16),
            pltpu.VMEM((n_tiles, STACK, GDN_DK), BF16),
            pltpu.VMEM((n_tiles, STACK, STACK), BF16),
            pltpu.VMEM((Lb // CHUNK, 8, LANES), F32),
        ],
        args=[proj, proj, proj, cw, gbias, galog, nw])


def _gdn_sample_kernel(qkv_ref, zg_ref, gate_ref, cst_ref, sst_ref, cw_ref, gbias_ref, galog_ref, nw_ref,
                       o_ref, cst_out_ref, sst_out_ref, xpad_ref, qkvc_ref, *, nb, L):
    R = nb * L
    sh = L.bit_length() - 1
    for bi in range(nb):
        xp = xpad_ref.at[bi]
        xp[5:8, :] = cst_ref[bi]
        cst_out_ref[bi] = _conv_block(
            qkv_ref.at[pl.ds(bi * L, L)], xp, cw_ref, None, qkvc_ref.at[pl.ds(bi * L, L)],
            L, GDN_CONV_CH, _gdn_qk_post)

    row_i = lax.broadcasted_iota(jnp.int32, (R, R), 0)
    col_i = lax.broadcasted_iota(jnp.int32, (R, R), 1)
    tril_f = (((row_i >> sh) == (col_i >> sh)) & (row_i >= col_i)).astype(F32)
    graw = gate_ref[...]
    sp = _softplus(graw + gbias_ref[...])
    beta_all = jax.nn.sigmoid(graw)
    G = jnp.dot(tril_f, -jnp.exp(galog_ref[...]) * sp, precision=_HIGHEST, preferred_element_type=F32)
    glast = [G[bi * L + L - 1:bi * L + L, :] for bi in range(nb)]
    eG = jnp.exp(G)
    eGrev = jnp.exp(jnp.concatenate([jnp.broadcast_to(x, (L, LANES)) for x in glast], axis=0) - G)
    egl = [jnp.exp(x) for x in glast]
    nw = nw_ref[...]

    n_st = R // (2 * L)

    def tiles(ref, st, off):
        return jnp.concatenate(
            [ref[st * 2 * L:(st + 1) * 2 * L, off + h * LANES:off + (h + 1) * LANES]
             for h in range(GDN_HEADS)], axis=0)

    def colstack(m, st, off):
        return jnp.concatenate(
            [jnp.broadcast_to(m[st * 2 * L:(st + 1) * 2 * L, off + h:off + h + 1], (2 * L, LANES))
             for h in range(GDN_HEADS)], axis=0)

    items = [dict(q=tiles(qkvc_ref, st, 0), k=tiles(qkvc_ref, st, GDN_QK), v=tiles(qkvc_ref, st, 2 * GDN_QK),
                  beta=colstack(beta_all, st, GATE_B), g=colstack(G, st, GATE_A),
                  eg=colstack(eG, st, GATE_A), egrev=colstack(eGrev, st, GATE_A))
             for st in range(n_st)]
    uw, qk = _gdn_local(items, L)

    groups = [(h, bi) for h in range(GDN_HEADS) for bi in range(2)]
    r = []
    for st, it in enumerate(items):
        w = uw[st][1]
        qd = it["q"] * it["eg"]
        r.append([
            _dot(jnp.concatenate([w[gi * L:(gi + 1) * L], qd[gi * L:(gi + 1) * L]], axis=0).astype(BF16),
                 sst_ref[2 * st + bi, h].astype(BF16))
            for gi, (h, bi) in enumerate(groups)])
    v_new = [uw[st][0] - jnp.concatenate([x[:L] for x in r[st]], axis=0) for st in range(n_st)]
    o = [jnp.concatenate([x[L:] for x in r[st]], axis=0)
         + _dot(qk[st].astype(BF16), v_new[st].astype(BF16)) for st in range(n_st)]
    for st, it in enumerate(items):
        kd = it["k"] * it["egrev"]
        for gi, (h, bi) in enumerate(groups):
            b = 2 * st + bi
            ga = GATE_A + h
            rs = slice(gi * L, (gi + 1) * L)
            sst_out_ref[b, h] = (sst_ref[b, h] * egl[b][:, ga:ga + 1]
                                 + _dot_tn(kd[rs].astype(BF16), v_new[st][rs].astype(BF16)))
    for st in range(n_st):
        out = (_rms_rows(o[st], nw) * _silu(tiles(zg_ref, st, 0))).astype(o_ref.dtype)
        for h in range(GDN_HEADS):
            o_ref[st * 2 * L:(st + 1) * 2 * L, h * GDN_DV:(h + 1) * GDN_DV] = out[h * 2 * L:(h + 1) * 2 * L]


def _gdn_sample_spec(proj, conv_state, S_state, cw, gbias, galog, nw, *, B, L, nb):
    R = nb * L
    const = lambda shape: pl.BlockSpec(shape, lambda i: (0,) * len(shape))
    return dict(
        in_specs=[
            pl.BlockSpec((R, GDN_CONV_CH), lambda i: (i, OFF_QKV // GDN_CONV_CH)),
            pl.BlockSpec((R, GDN_V), lambda i: (i, OFF_ZG // GDN_V)),
            pl.BlockSpec((R, LANES), lambda i: (i, OFF_GATE // LANES)),
            pl.BlockSpec((nb, CONV_W - 1, GDN_CONV_CH), lambda i: (i, 0, 0)),
            pl.BlockSpec((nb, GDN_HEADS, GDN_DK, GDN_DV), lambda i: (i, 0, 0, 0)),
            const((CONV_W, GDN_CONV_CH)), const((1, LANES)), const((1, LANES)), const((1, GDN_DV)),
        ],
        out_specs=[
            pl.BlockSpec((R, GDN_V), lambda i: (i, 0)),
            pl.BlockSpec((nb, CONV_W - 1, GDN_CONV_CH), lambda i: (i, 0, 0)),
            pl.BlockSpec((nb, GDN_HEADS, GDN_DK, GDN_DV), lambda i: (i, 0, 0, 0)),
        ],
        out_shape=[
            jax.ShapeDtypeStruct((B * L, GDN_V), BF16),
            jax.ShapeDtypeStruct((B, CONV_W - 1, GDN_CONV_CH), F32),
            jax.ShapeDtypeStruct((B, GDN_HEADS, GDN_DK, GDN_DV), F32),
        ],
        scratch=[
            pltpu.VMEM((nb, L + 8, GDN_CONV_CH), F32),
            pltpu.VMEM((R, GDN_CONV_CH), F32),
        ],
        args=[proj, proj, proj, conv_state, S_state, cw, gbias, galog, nw])


N_PAIRS = SSM_HEADS // 2
PAIRS_PER_GROUP = N_PAIRS // SSM_GROUPS
GROUP_W = SSM_DI // SSM_GROUPS


def _ssm_tile(graw, gbias, nega, ld_x, ld_b, ld_c, ld_z, dcols_ref, nw_ref, st_o, get_h, set_h, glen):
    c = CHUNK
    P = SSM_P
    nseq = c // glen
    sh = glen.bit_length() - 1
    ri = lax.broadcasted_iota(jnp.int32, (c, c), 0)
    ci = lax.broadcasted_iota(jnp.int32, (c, c), 1)
    tril_f = (((ri >> sh) == (ci >> sh)) & (ri >= ci)).astype(F32)
    sp = _softplus(graw + gbias)
    acum = jnp.dot(tril_f, nega * sp, precision=_HIGHEST, preferred_element_type=F32)
    lasts = [acum[s * glen + glen - 1:(s + 1) * glen, :] for s in range(nseq)]
    alast = jnp.concatenate([jnp.broadcast_to(x, (glen, LANES)) for x in lasts], axis=0)
    dtrev = sp * jnp.exp(alast - acum)
    eal = [jnp.exp(x) for x in lasts]
    lane = lax.broadcasted_iota(jnp.int32, (c, LANES), 1)
    row = lax.broadcasted_iota(jnp.int32, (c, LANES), 0)
    m = jnp.where(lane < GATE_DT2, acum, sp)
    mt = jnp.concatenate([m, m], axis=0).T
    left = lane < P
    left_row = left[0:1]
    j = jnp.where(left, lane, lane - P)
    tril2 = ((row >> sh) == (j >> sh)) & (row >= j)
    rowh = lax.broadcasted_iota(jnp.int32, (2 * P, SSM_N), 0) < P

    def expand(mat, c0):
        return jnp.where(left, jnp.broadcast_to(mat[:, c0:c0 + 1], (c, LANES)),
                         jnp.broadcast_to(mat[:, c0 + 1:c0 + 2], (c, LANES)))

    def rowsel(base, e):
        return jnp.where(left_row, mt[base + 2 * e:base + 2 * e + 1, :], mt[base + 2 * e + 1:base + 2 * e + 2, :])

    for g in range(SSM_GROUPS):
        Bg = ld_b(g)
        Cg = ld_c(g)
        Bg16 = Bg.astype(BF16)
        Cg16 = Cg.astype(BF16)
        cb2 = _dot_nt(Cg16, jnp.concatenate([Bg16, Bg16], axis=0))
        pairs = [g * PAIRS_PER_GROUP + e4 for e4 in range(PAIRS_PER_GROUP)]
        acol = [expand(acum, GATE_DT + 2 * e) for e in pairs]
        scores16 = [
            (cb2 * jnp.exp(jnp.where(tril2, a - rowsel(GATE_DT, e), -jnp.inf)) * rowsel(GATE_DT2, e)).astype(BF16)
            for a, e in zip(acol, pairs)]
        xp = [ld_x(e) for e in pairs]
        bd16 = [jnp.concatenate([jnp.where(left, x, 0.0), jnp.where(left, 0.0, x)], axis=0).astype(BF16)
                for x in xp]
        ydiag = [_dot(s, b) for s, b in zip(scores16, bd16)]
        if nseq == 1:
            yoff = [_dot_nt(Cg16, get_h(0, e).astype(BF16)) for e in pairs]
        else:
            yoff = [jnp.concatenate(
                [_dot_nt(Cg[s * glen:(s + 1) * glen].astype(BF16), get_h(s, e).astype(BF16))
                 for s in range(nseq)], axis=0) for e in pairs]
        y = [yd + yo * jnp.exp(a) + dcols_ref[:, e * 2 * P:(e + 1) * 2 * P] * x
             for yd, yo, a, e, x in zip(ydiag, yoff, acol, pairs, xp)]
        xdr = [x * expand(dtrev, GATE_DT + 2 * e) for x, e in zip(xp, pairs)]
        for e, xd in zip(pairs, xdr):
            c0 = GATE_DT + 2 * e
            for s in range(nseq):
                rs = slice(s * glen, (s + 1) * glen)
                ealcol = jnp.where(rowh, eal[s][:, c0:c0 + 1], eal[s][:, c0 + 1:c0 + 2])
                set_h(s, e, get_h(s, e) * ealcol + _dot_tn(xd[rs].astype(BF16), Bg[rs].astype(BF16)))
        yg = jnp.concatenate(y, axis=1) * _silu(ld_z(g))
        gcols = slice(g * GROUP_W, (g + 1) * GROUP_W)
        st_o(g, _rms_rows(yg, nw_ref[:, gcols]))


def _ssm_prompt_parts(xs_ref, bc_ref, zs_ref, gate_ref, cwx_ref, cbx_ref, cwbc_ref, cbbc_ref,
                      gbias_ref, galog_ref, dcols_ref, nw_ref,
                      o_ref, cst_out_ref, hst_out_ref, xpadx_ref, xpadbc_ref, xc_ref, bcc_ref, hh_ref, *, Lb):
    c = CHUNK

    def init():
        hh_ref[...] = jnp.zeros(hh_ref.shape, F32)
        xpadx_ref[0:8, :] = jnp.zeros((8, SSM_DI), F32)
        xpadbc_ref[0:8, :] = jnp.zeros((8, 2 * SSM_BC), F32)

    def conv():
        ident = lambda s, y: y
        _conv_block(xs_ref, xpadx_ref, cwx_ref, cbx_ref, xc_ref, Lb, SSM_DI, ident)
        _conv_block(bc_ref, xpadbc_ref, cwbc_ref, cbbc_ref, bcc_ref, Lb, 2 * SSM_BC, ident)

    def set_h(s, e, val):
        hh_ref[e] = val

    def tile(ci):
        rows = slice(ci * c, (ci + 1) * c)

        def st_o(g, val):
            o_ref[rows, g * GROUP_W:(g + 1) * GROUP_W] = val.astype(o_ref.dtype)

        _ssm_tile(
            gate_ref[rows, :], gbias_ref[...], -jnp.exp(galog_ref[...]),
            lambda e: xc_ref[rows, e * LANES:(e + 1) * LANES],
            lambda g: bcc_ref[rows, g * SSM_N:(g + 1) * SSM_N],
            lambda g: bcc_ref[rows, SSM_BC + g * SSM_N:SSM_BC + (g + 1) * SSM_N],
            lambda g: zs_ref[rows, g * GROUP_W:(g + 1) * GROUP_W],
            dcols_ref, nw_ref, st_o, lambda s, e: hh_ref[e], set_h, c)

    steps = [conv] + [functools.partial(tile, ci) for ci in range(Lb // c)]

    def final():
        cst_out_ref[0, :, :SSM_DI] = xpadx_ref[5:8, :]
        cst_out_ref[0, :, SSM_DI:] = xpadbc_ref[5:8, :]
        hst_out_ref[0] = hh_ref[...]

    return init, steps, final


def _ssm_sample_kernel(xs_ref, bc_ref, zs_ref, gate_ref, cst_ref, hst_ref, cwx_ref, cbx_ref, cwbc_ref, cbbc_ref,
                       gbias_ref, galog_ref, dcols_ref, nw_ref,
                       o_ref, cst_out_ref, hst_out_ref, xpadx_ref, xpadbc_ref, xc_ref, bcc_ref, *, L):
    ident = lambda s, y: y
    for bi in range(CHUNK // L):
        rs = pl.ds(bi * L, L)
        xpx = xpadx_ref.at[bi]
        xpb = xpadbc_ref.at[bi]
        xpx[5:8, :] = cst_ref[bi, :, :SSM_DI]
        xpb[5:8, :] = cst_ref[bi, :, SSM_DI:]
        cst_out_ref[bi, :, :SSM_DI] = _conv_block(
            xs_ref.at[rs], xpx, cwx_ref, cbx_ref, xc_ref.at[rs], L, SSM_DI, ident)
        cst_out_ref[bi, :, SSM_DI:] = _conv_block(
            bc_ref.at[rs], xpb, cwbc_ref, cbbc_ref, bcc_ref.at[rs], L, 2 * SSM_BC, ident)

    def st_o(g, val):
        o_ref[:, g * GROUP_W:(g + 1) * GROUP_W] = val.astype(o_ref.dtype)

    def set_h(s, e, val):
        hst_out_ref[s, e] = val

    _ssm_tile(
        gate_ref[...], gbias_ref[...], -jnp.exp(galog_ref[...]),
        lambda e: xc_ref[:, e * LANES:(e + 1) * LANES],
        lambda g: bcc_ref[:, g * SSM_N:(g + 1) * SSM_N],
        lambda g: bcc_ref[:, SSM_BC + g * SSM_N:SSM_BC + (g + 1) * SSM_N],
        lambda g: zs_ref[:, g * GROUP_W:(g + 1) * GROUP_W],
        dcols_ref, nw_ref, st_o, lambda s, e: hst_ref[s, e], set_h, L)


def _ssm_const_specs():
    const = lambda shape: pl.BlockSpec(shape, lambda *idx: (0,) * len(shape))
    return [
        const((CONV_W, SSM_DI)), const((1, SSM_DI)), const((CONV_W, 2 * SSM_BC)), const((1, 2 * SSM_BC)),
        const((1, LANES)), const((1, LANES)), const((1, SSM_DI)), const((1, SSM_DI)),
    ]


def _ssm_prompt_spec(proj, cwx, cbx, cwbc, cbbc, gbias, galog, dcols, nw, *, B, L, Lb):
    nl = L // Lb
    row = lambda b, l: b * nl + l
    return dict(
        in_specs=[
            pl.BlockSpec((Lb, SSM_DI), lambda b, l: (row(b, l), OFF_XS // SSM_DI)),
            pl.BlockSpec((Lb, 2 * SSM_BC), lambda b, l: (row(b, l), OFF_BC // (2 * SSM_BC))),
            pl.BlockSpec((Lb, SSM_DI), lambda b, l: (row(b, l), OFF_ZS // SSM_DI)),
            pl.BlockSpec((Lb, LANES), lambda b, l: (row(b, l), OFF_GATE // LANES)),
        ] + _ssm_const_specs(),
        out_specs=[
            pl.BlockSpec((Lb, SSM_DI), lambda b, l: (row(b, l), 0)),
            pl.BlockSpec((1, CONV_W - 1, SSM_CONV_CH), lambda b, l: (b, 0, 0)),
            pl.BlockSpec((1, N_PAIRS, 2 * SSM_P, SSM_N), lambda b, l: (b, 0, 0, 0)),
        ],
        out_shape=[
            jax.ShapeDtypeStruct((B * L, SSM_DI), BF16),
            jax.ShapeDtypeStruct((B, CONV_W - 1, SSM_CONV_CH), F32),
            jax.ShapeDtypeStruct((B, N_PAIRS, 2 * SSM_P, SSM_N), F32),
        ],
        scratch=[
            pltpu.VMEM((Lb + 8, SSM_DI), F32),
            pltpu.VMEM((Lb + 8, 2 * SSM_BC), F32),
            pltpu.VMEM((Lb, SSM_DI), F32),
            pltpu.VMEM((Lb, 2 * SSM_BC), F32),
            pltpu.VMEM((N_PAIRS, 2 * SSM_P, SSM_N), F32),
        ],
        args=[proj, proj, proj, proj, cwx, cbx, cwbc, cbbc, gbias, galog, dcols, nw])


def _ssm_sample_spec(proj, conv_state, h_pairs, cwx, cbx, cwbc, cbbc, gbias, galog, dcols, nw, *, B, L):
    nb = CHUNK // L
    return dict(
        in_specs=[
            pl.BlockSpec((CHUNK, SSM_DI), lambda i: (i, OFF_XS // SSM_DI)),
            pl.BlockSpec((CHUNK, 2 * SSM_BC), lambda i: (i, OFF_BC // (2 * SSM_BC))),
            pl.BlockSpec((CHUNK, SSM_DI), lambda i: (i, OFF_ZS // SSM_DI)),
            pl.BlockSpec((CHUNK, LANES), lambda i: (i, OFF_GATE // LANES)),
            pl.BlockSpec((nb, CONV_W - 1, SSM_CONV_CH), lambda i: (i, 0, 0)),
            pl.BlockSpec((nb, N_PAIRS, 2 * SSM_P, SSM_N), lambda i: (i, 0, 0, 0)),
        ] + _ssm_const_specs(),
        out_specs=[
            pl.BlockSpec((CHUNK, SSM_DI), lambda i: (i, 0)),
            pl.BlockSpec((nb, CONV_W - 1, SSM_CONV_CH), lambda i: (i, 0, 0)),
            pl.BlockSpec((nb, N_PAIRS, 2 * SSM_P, SSM_N), lambda i: (i, 0, 0, 0)),
        ],
        out_shape=[
            jax.ShapeDtypeStruct((B * L, SSM_DI), BF16),
            jax.ShapeDtypeStruct((B, CONV_W - 1, SSM_CONV_CH), F32),
            jax.ShapeDtypeStruct((B, N_PAIRS, 2 * SSM_P, SSM_N), F32),
        ],
        scratch=[
            pltpu.VMEM((nb, L + 8, SSM_DI), F32),
            pltpu.VMEM((nb, L + 8, 2 * SSM_BC), F32),
            pltpu.VMEM((CHUNK, SSM_DI), F32),
            pltpu.VMEM((CHUNK, 2 * SSM_BC), F32),
        ],
        args=[proj, proj, proj, proj, conv_state, h_pairs, cwx, cbx, cwbc, cbbc, gbias, galog, dcols, nw])


def _split_refs(refs, g, s):
    it = iter(refs)
    take = lambda n: [next(it) for _ in range(n)]
    g_in, s_in = take(len(g["in_specs"])), take(len(s["in_specs"]))
    g_out, s_out = take(len(g["out_specs"])), take(len(s["out_specs"]))
    g_scr, s_scr = take(len(g["scratch"])), take(len(s["scratch"]))
    return g_in + g_out + g_scr, s_in + s_out + s_scr


def _mix_prompt_kernel(*refs, g, s, Lb):
    g_refs, s_refs = _split_refs(refs, g, s)
    g_init, g_steps, g_final = _gdn_prompt_parts(*g_refs, Lb=Lb)
    s_init, s_steps, s_final = _ssm_prompt_parts(*s_refs, Lb=Lb)
    l = pl.program_id(1)

    @pl.when(l == 0)
    def _init():
        g_init()
        s_init()

    for g_step, s_step in itertools.zip_longest(g_steps, s_steps):
        if g_step is not None:
            g_step()
        if s_step is not None:
            s_step()

    @pl.when(l == pl.num_programs(1) - 1)
    def _final():
        g_final()
        s_final()


def _mix_sample_kernel(*refs, g, s, nb, L):
    g_refs, s_refs = _split_refs(refs, g, s)
    _gdn_sample_kernel(*g_refs, nb=nb, L=L)
    _ssm_sample_kernel(*s_refs, L=L)


def _fused_call(kernel, g, s, grid, semantics, name):
    outs = pl.pallas_call(
        functools.partial(kernel, g={k: g[k] for k in ("in_specs", "out_specs", "scratch")},
                          s={k: s[k] for k in ("in_specs", "out_specs", "scratch")}),
        grid=grid,
        in_specs=g["in_specs"] + s["in_specs"],
        out_specs=g["out_specs"] + s["out_specs"],
        out_shape=g["out_shape"] + s["out_shape"],
        scratch_shapes=g["scratch"] + s["scratch"],
        compiler_params=pltpu.CompilerParams(dimension_semantics=semantics, vmem_limit_bytes=VMEM_LIMIT),
        name=name,
    )(*g["args"], *s["args"])
    n = len(g["out_specs"])
    return outs[:n], outs[n:]


def _outproj_kernel(x_ref, mg_ref, ms_ref, w_ref, o_ref):
    acc = _dot(mg_ref[...].astype(BF16), w_ref[:GDN_V, :])
    acc = acc + _dot(ms_ref[...].astype(BF16), w_ref[GDN_V:, :])
    o_ref[...] = x_ref[...] + acc


def _out_proj(x2d, mix_g, mix_s, w_out16, *, tm):
    T = x2d.shape[0]
    return pl.pallas_call(
        _outproj_kernel,
        grid=(T // tm,),
        in_specs=[
            pl.BlockSpec((tm, D_MODEL), lambda i: (i, 0)),
            pl.BlockSpec((tm, GDN_V), lambda i: (i, 0)),
            pl.BlockSpec((tm, SSM_DI), lambda i: (i, 0)),
            pl.BlockSpec((D_MODEL, D_MODEL), lambda i: (0, 0)),
        ],
        out_specs=pl.BlockSpec((tm, D_MODEL), lambda i: (i, 0)),
        out_shape=jax.ShapeDtypeStruct((T, D_MODEL), F32),
        compiler_params=pltpu.CompilerParams(
            dimension_semantics=("parallel",), vmem_limit_bytes=VMEM_LIMIT),
        name="out_proj",
    )(x2d, mix_g, mix_s, w_out16)


FFN_SUB = 4


def _ffn_kernel(x_ref, nw_ref, wg_ref, wu_ref, wd_ref, fnw_ref, o_ref, h_ref):
    f = pl.program_id(1)
    nf = pl.num_programs(1)
    rs = h_ref.shape[0] // FFN_SUB

    def step(first, last):
        def gate_up(r):
            rows = slice(r * rs, (r + 1) * rs)
            if first:
                h = _rms_rows(x_ref[rows, :], nw_ref[...]).astype(BF16)
                h_ref[rows, :] = h
            else:
                h = h_ref[rows, :]
            return _dot(h, wg_ref[...]), _dot(h, wu_ref[...])

        def down(r, gu):
            rows = slice(r * rs, (r + 1) * rs)
            d = _dot((_silu(gu[0]) * gu[1]).astype(BF16), wd_ref[...])
            acc = d if first else o_ref[rows, :] + d
            if last:
                o_ref[rows, :] = _rms_rows(x_ref[rows, :] + acc, fnw_ref[...])
            else:
                o_ref[rows, :] = acc

        gu = gate_up(0)
        for r in range(1, FFN_SUB):
            gu_next = gate_up(r)
            down(r - 1, gu)
            gu = gu_next
        down(FFN_SUB - 1, gu)

    pl.when(f == 0)(lambda: step(True, False))
    pl.when((f > 0) & (f < nf - 1))(lambda: step(False, False))
    pl.when(f == nf - 1)(lambda: step(False, True))


def _ffn(x2d, norm_w, wg16, wu16, wd16, final_w, *, tm, tf):
    T = x2d.shape[0]
    return pl.pallas_call(
        _ffn_kernel,
        grid=(T // tm, D_FF // tf),
        in_specs=[
            pl.BlockSpec((tm, D_MODEL), lambda i, f: (i, 0)),
            pl.BlockSpec((1, D_MODEL), lambda i, f: (0, 0)),
            pl.BlockSpec((D_MODEL, tf), lambda i, f: (0, f)),
            pl.BlockSpec((D_MODEL, tf), lambda i, f: (0, f)),
            pl.BlockSpec((tf, D_MODEL), lambda i, f: (f, 0)),
            pl.BlockSpec((1, D_MODEL), lambda i, f: (0, 0)),
        ],
        out_specs=pl.BlockSpec((tm, D_MODEL), lambda i, f: (i, 0)),
        out_shape=jax.ShapeDtypeStruct((T, D_MODEL), F32),
        scratch_shapes=[pltpu.VMEM((tm, D_MODEL), BF16)],
        compiler_params=pltpu.CompilerParams(
            dimension_semantics=("parallel", "arbitrary"), vmem_limit_bytes=VMEM_LIMIT),
        name="ffn",
    )(x2d, norm_w, wg16, wu16, wd16, final_w)


PROMPT_ROWS = 256


def _trunk(x, states, p):
    B, L, _ = x.shape
    x2d = x.reshape(B * L, D_MODEL)
    proj = _in_proj(x2d, p["attn_norm_w"], p["w_in_r"], tm=1024, tn=768)
    gdn_w = (p["gdn_conv_w"], p["gbias"], p["galog"], p["gdn_norm_w"])
    ssm_w = (p["cwx"], p["cbx"], p["cwbc"], p["cbbc"], p["gbias"], p["galog"], p["dcols"], p["ssm_norm_w"])
    pair_shape = (B, N_PAIRS, 2 * SSM_P, SSM_N)
    if states is None:
        Lb = PROMPT_ROWS
        g = _gdn_prompt_spec(proj, *gdn_w, B=B, L=L, Lb=Lb)
        s = _ssm_prompt_spec(proj, *ssm_w, B=B, L=L, Lb=Lb)
        kern = functools.partial(_mix_prompt_kernel, Lb=Lb)
        (mix_g, gconv_new, gS_new), (mix_s, sconv_new, sh_new) = _fused_call(
            kern, g, s, (B, L // Lb), ("parallel", "arbitrary"), "mix_prompt")
    else:
        gconv, gS, sconv, sh = states
        nb = CHUNK // L
        g = _gdn_sample_spec(proj, gconv, gS, *gdn_w, B=B, L=L, nb=nb)
        s = _ssm_sample_spec(proj, sconv, sh.reshape(pair_shape), *ssm_w, B=B, L=L)
        kern = functools.partial(_mix_sample_kernel, nb=nb, L=L)
        (mix_g, gconv_new, gS_new), (mix_s, sconv_new, sh_new) = _fused_call(
            kern, g, s, (B // nb,), ("parallel",), "mix_sample")
    sh_new = sh_new.reshape(B, SSM_HEADS, SSM_P, SSM_N)
    x1 = _out_proj(x2d, mix_g, mix_s, p["w_out16"], tm=512)
    y = _ffn(x1, p["ffn_norm_w"], p["wg16"], p["wu16"], p["wd16"], p["final_norm_w"], tm=1024, tf=512)
    return y.reshape(B, L, D_MODEL), (gconv_new[None], gS_new[None], sconv_new[None], sh_new[None])


def kernel(x_prompt, x_sample, state_gdn_conv, state_gdn, state_ssm_conv, state_ssm,
           attn_norm_w, w_in, gdn_conv_w, gdn_A_log, gdn_dt_bias, gdn_norm_w,
           ssm_conv_w, ssm_conv_b, ssm_A_log, ssm_dt_bias, ssm_D, ssm_norm_w,
           w_out, ffn_norm_w, w_gate, w_up, w_down, final_norm_w):
    assert w_in.shape[0] == 1, "single-layer trunk"
    assert x_prompt.shape[1] % PROMPT_ROWS == 0 and x_sample.shape[1] == 8 and x_sample.shape[0] % 8 == 0
    assert w_in.shape[2] == D_IN_PROJ
    w_in_r = _w_in_prep(jnp.swapaxes(w_in, 1, 2), tk=256)
    zeros8 = jnp.zeros((GDN_HEADS,), F32)
    tail = jnp.zeros((LANES - GATE_DT2 - SSM_HEADS,), F32)
    gbias = jnp.concatenate([zeros8, gdn_dt_bias[0], ssm_dt_bias[0], ssm_dt_bias[0], tail])[None]
    galog = jnp.concatenate([zeros8, gdn_A_log[0], ssm_A_log[0], ssm_A_log[0], tail])[None]
    p = dict(
        attn_norm_w=attn_norm_w, w_in_r=w_in_r, gdn_conv_w=gdn_conv_w[0], gbias=gbias, galog=galog,
        gdn_norm_w=gdn_norm_w,
        cwx=ssm_conv_w[0][:, :SSM_DI], cbx=ssm_conv_b[:, :SSM_DI],
        cwbc=ssm_conv_w[0][:, SSM_DI:], cbbc=ssm_conv_b[:, SSM_DI:],
        dcols=jnp.repeat(ssm_D[0], SSM_P)[None], ssm_norm_w=ssm_norm_w,
        w_out16=w_out[0].astype(BF16), ffn_norm_w=ffn_norm_w,
        wg16=w_gate[0].astype(BF16), wu16=w_up[0].astype(BF16), wd16=w_down[0].astype(BF16),
        final_norm_w=final_norm_w[None],
    )
    y_p, st_p = _trunk(x_prompt, None, p)
    y_s, st_s = _trunk(x_sample, (state_gdn_conv[0], state_gdn[0], state_ssm_conv[0], state_ssm[0]), p)
    return (y_p, y_s, st_p[0], st_p[1], st_p[2], st_p[3], st_s[0], st_s[1], st_s[2], st_s[3])
```

```python
import functools
import itertools

import jax
import jax.numpy as jnp
from jax import lax
from jax.experimental import pallas as pl
from jax.experimental.pallas import tpu as pltpu

F32 = jnp.float32
BF16 = jnp.bfloat16

D_MODEL = 2048
GDN_HEADS = 8
GDN_DK = 128
GDN_DV = 128
GDN_QK = GDN_HEADS * GDN_DK
GDN_V = GDN_HEADS * GDN_DV
GDN_CONV_CH = 2 * GDN_QK + GDN_V
SSM_P = 64
SSM_N = 128
SSM_GROUPS = 2
SSM_DI = 1024
SSM_HEADS = SSM_DI // SSM_P
SSM_BC = SSM_GROUPS * SSM_N
SSM_CONV_CH = SSM_DI + 2 * SSM_BC
CONV_W = 4
CHUNK = 64
D_FF = 5632
EPS = 1e-6

OFF_QKV = 0
OFF_ZG = OFF_QKV + GDN_CONV_CH
OFF_ZS = OFF_ZG + GDN_V
OFF_XS = OFF_ZS + SSM_DI
OFF_BC = OFF_XS + SSM_DI
OFF_GATE = OFF_BC + 2 * SSM_BC
LANES = 128
GATE_B = 0
GATE_A = GATE_B + GDN_HEADS
GATE_DT = GATE_A + GDN_HEADS
GATE_DT2 = GATE_DT + SSM_HEADS
N_PROJ = 6912

VMEM_LIMIT = 52 * 1024 * 1024

_HIGHEST = lax.Precision.HIGHEST


def _silu(x):
    return x * jax.nn.sigmoid(x)


def _softplus(x):
    return jnp.maximum(x, 0.0) + jnp.log1p(jnp.exp(-jnp.abs(x)))


def _dot(a, b):
    return jnp.dot(a, b, preferred_element_type=F32)


def _dot_nt(a, b):
    return lax.dot_general(a, b, (((1,), (1,)), ((), ())), preferred_element_type=F32)


def _dot_tn(a, b):
    return lax.dot_general(a, b, (((0,), (0,)), ((), ())), preferred_element_type=F32)


def _transpose_rows(a):
    r = a.shape[0]
    if r < LANES:
        a = jnp.concatenate([a, jnp.zeros((LANES - r, LANES), a.dtype)], axis=0)
    return a.T


def _rms_rows(x, w):
    return x * lax.rsqrt(jnp.mean(x * x, axis=-1, keepdims=True) + EPS) * w


W_B = OFF_ZS
W_ZS = W_B + 2 * GDN_HEADS
W_DT = W_ZS + SSM_DI + SSM_CONV_CH
D_IN_PROJ = W_DT + SSM_HEADS


def _wprep_kernel(w_ref, o_ref):
    cols = o_ref.shape[1]
    o_ref[:W_B, :] = w_ref[0, :W_B, :].astype(BF16)
    o_ref[W_B:OFF_GATE, :] = w_ref[0, W_ZS:W_DT, :].astype(BF16)
    dt = w_ref[0, W_DT:D_IN_PROJ, :].astype(BF16)
    o_ref[OFF_GATE:OFF_GATE + GATE_DT, :] = w_ref[0, W_B:W_ZS, :].astype(BF16)
    o_ref[OFF_GATE + GATE_DT:OFF_GATE + GATE_DT2, :] = dt
    o_ref[OFF_GATE + GATE_DT2:OFF_GATE + GATE_DT2 + SSM_HEADS, :] = dt
    o_ref[OFF_GATE + GATE_DT2 + SSM_HEADS:, :] = jnp.zeros((N_PROJ - OFF_GATE - GATE_DT2 - SSM_HEADS, cols), BF16)


def _w_in_prep(w_in_t, *, tk):
    return pl.pallas_call(
        _wprep_kernel,
        grid=(D_MODEL // tk,),
        in_specs=[pl.BlockSpec((1, D_IN_PROJ, tk), lambda i: (0, 0, i))],
        out_specs=pl.BlockSpec((N_PROJ, tk), lambda i: (0, i)),
        out_shape=jax.ShapeDtypeStruct((N_PROJ, D_MODEL), BF16),
        compiler_params=pltpu.CompilerParams(
            dimension_semantics=("parallel",), vmem_limit_bytes=VMEM_LIMIT),
        name="w_in_prep",
    )(w_in_t)


INPROJ_SUB = 4


def _inproj_kernel(x_ref, nw_ref, wt_ref, o_ref, h_ref):
    j = pl.program_id(1)

    @pl.when(j == 0)
    def _first():
        rs = h_ref.shape[0] // INPROJ_SUB
        for r in range(INPROJ_SUB):
            rows = slice(r * rs, (r + 1) * rs)
            h = _rms_rows(x_ref[rows, :], nw_ref[...]).astype(BF16)
            h_ref[rows, :] = h
            o_ref[rows, :] = _dot_nt(h, wt_ref[...])

    @pl.when(j > 0)
    def _rest():
        o_ref[...] = _dot_nt(h_ref[...], wt_ref[...])


def _in_proj(x2d, norm_w, w_in_r, *, tm, tn):
    T = x2d.shape[0]
    return pl.pallas_call(
        _inproj_kernel,
        grid=(T // tm, N_PROJ // tn),
        in_specs=[
            pl.BlockSpec((tm, D_MODEL), lambda i, j: (i, 0)),
            pl.BlockSpec((1, D_MODEL), lambda i, j: (0, 0)),
            pl.BlockSpec((tn, D_MODEL), lambda i, j: (j, 0)),
        ],
        out_specs=pl.BlockSpec((tm, tn), lambda i, j: (i, j)),
        out_shape=jax.ShapeDtypeStruct((T, N_PROJ), F32),
        scratch_shapes=[pltpu.VMEM((tm, D_MODEL), BF16)],
        compiler_params=pltpu.CompilerParams(
            dimension_semantics=("parallel", "arbitrary"), vmem_limit_bytes=VMEM_LIMIT),
        name="in_proj",
    )(x2d, norm_w, w_in_r)


def _conv_block(x_ref, xpad_ref, cw_ref, cb_ref, dst_ref, Lb, C, post):
    xpad_ref[8:8 + Lb, :] = x_ref[...]
    rs = min(Lb, CHUNK)
    for sb in range(Lb // rs):
        r = sb * rs
        for s in range(C // LANES):
            cols = slice(s * LANES, (s + 1) * LANES)
            acc = xpad_ref[8 + r:8 + r + rs, cols] * cw_ref[3:4, cols]
            for i in range(CONV_W - 1):
                acc = acc + xpad_ref[5 + i + r:5 + i + r + rs, cols] * cw_ref[i:i + 1, cols]
            if cb_ref is not None:
                acc = acc + cb_ref[:, cols]
            dst_ref[r:r + rs, cols] = post(s, _silu(acc))
    last3 = xpad_ref[Lb + 5:Lb + 8, :]
    xpad_ref[5:8, :] = last3
    return last3


STACK = 128


def _gdn_qk_post(s, y):
    if s < 2 * GDN_HEADS:
        y = y * lax.rsqrt(jnp.sum(y * y, axis=-1, keepdims=True) + EPS)
        if s < GDN_HEADS:
            y = y * (GDN_DK ** -0.5)
    return y


def _gdn_local(items, glen):
    sh = glen.bit_length() - 1
    row = lax.broadcasted_iota(jnp.int32, (STACK, STACK), 0)
    col = lax.broadcasted_iota(jnp.int32, (STACK, STACK), 1)
    same = (row >> sh) == (col >> sh)
    incl = same & (row >= col)
    strict = same & (row > col)
    eye = (row == col).astype(F32)

    decay = [jnp.exp(jnp.where(incl, it["g"] - it["g"].T, -jnp.inf)) for it in items]
    kb = [it["k"] * it["beta"] for it in items]
    qkk = [_dot_nt(jnp.concatenate([it["q"], b], axis=0).astype(BF16), it["k"].astype(BF16))
           for it, b in zip(items, kb)]
    qk = [x[:STACK] * d for x, d in zip(qkk, decay)]
    nmat = [jnp.where(strict, -(x[STACK:] * d), 0.0) for x, d in zip(qkk, decay)]
    tinv = [eye + n for n in nmat]
    if sh >= 2:
        pw = [_dot(n.astype(BF16), n.astype(BF16)) for n in nmat]
        for _ in range(sh - 2):
            x = [_dot(jnp.concatenate([t, p], axis=0).astype(BF16), p.astype(BF16))
                 for t, p in zip(tinv, pw)]
            tinv = [t + y[:STACK] for t, y in zip(tinv, x)]
            pw = [y[STACK:] for y in x]
        tinv = [t + _dot(t.astype(BF16), p.astype(BF16)) for t, p in zip(tinv, pw)]
    uw = [_dot(t.astype(BF16),
               jnp.concatenate([it["v"] * it["beta"], b * it["eg"]], axis=1).astype(BF16))
          for t, it, b in zip(tinv, items, kb)]
    return [(x[:, :GDN_DV], x[:, GDN_DV:]) for x in uw], qk


GDN_LOCAL_CHUNKS = 2


def _gdn_prompt_parts(qkv_ref, zg_ref, gate_ref, cw_ref, gbias_ref, galog_ref, nw_ref,
                      o_ref, cst_out_ref, sst_out_ref,
                      xpad_ref, qkvc_ref, s_ref, u_ref, wq16_ref, kd16_ref, qk16_ref, egl_ref, *, Lb):
    c = CHUNK

    def init():
        xpad_ref[0:8, :] = jnp.zeros((8, GDN_CONV_CH), F32)
        s_ref[...] = jnp.zeros(s_ref.shape, F32)

    def conv():
        _conv_block(qkv_ref, xpad_ref, cw_ref, None, qkvc_ref, Lb, GDN_CONV_CH, _gdn_qk_post)

    row_i = lax.broadcasted_iota(jnp.int32, (c, c), 0)
    col_i = lax.broadcasted_iota(jnp.int32, (c, c), 1)
    tril_f = (row_i >= col_i).astype(F32)
    gbias = gbias_ref[...]
    nega = -jnp.exp(galog_ref[...])
    nw = nw_ref[...]
    n_pairs = GDN_HEADS // 2

    def local_chunk_items(ci):
        rows = slice(ci * c, (ci + 1) * c)
        graw = gate_ref[rows, :]
        sp = _softplus(graw + gbias)
        beta_all = jax.nn.sigmoid(graw)
        G = jnp.dot(tril_f, nega * sp, precision=_HIGHEST, preferred_element_type=F32)
        eG = jnp.exp(G)
        glast = G[c - 1:c, :]
        eGrev = jnp.exp(glast - G)
        egl_ref[ci] = jnp.broadcast_to(jnp.exp(glast), (8, LANES))

        def heads(off, a, b):
            return jnp.concatenate([qkvc_ref[rows, off + a * LANES:off + (a + 1) * LANES],
                                    qkvc_ref[rows, off + b * LANES:off + (b + 1) * LANES]], axis=0)

        def colstack(m, a, b):
            return jnp.concatenate([jnp.broadcast_to(m[:, a:a + 1], (c, LANES)),
                                    jnp.broadcast_to(m[:, b:b + 1], (c, LANES))], axis=0)

        items = []
        for pr in range(n_pairs):
            a, b = 2 * pr, 2 * pr + 1
            items.append(dict(
                q=heads(0, a, b), k=heads(GDN_QK, a, b), v=heads(2 * GDN_QK, a, b),
                beta=colstack(beta_all, GATE_B + a, GATE_B + b),
                g=colstack(G, GATE_A + a, GATE_A + b),
                eg=colstack(eG, GATE_A + a, GATE_A + b),
                egrev=colstack(eGrev, GATE_A + a, GATE_A + b)))
        return items

    def local_group(cis):
        items = [it for ci in cis for it in local_chunk_items(ci)]
        uw, qk = _gdn_local(items, c)
        for n, it in enumerate(items):
            idx = cis[0] * n_pairs + n
            u, w = uw[n]
            qd = it["q"] * it["eg"]
            u_ref[idx] = u
            for hh in range(2):
                hs = slice(hh * c, (hh + 1) * c)
                wq16_ref[2 * idx + hh] = jnp.concatenate([w[hs], qd[hs]], axis=0).astype(BF16)
            kd16_ref[idx] = (it["k"] * it["egrev"]).astype(BF16)
            qk16_ref[idx] = qk[n].astype(BF16)

    def recurrent(ci):
        rows = slice(ci * c, (ci + 1) * c)
        egl = egl_ref[ci][0:1]
        r = [[_dot(wq16_ref[2 * (ci * n_pairs + pr) + hh], s_ref[2 * pr + hh].astype(BF16)) for hh in range(2)]
             for pr in range(n_pairs)]
        v16 = [(u_ref[ci * n_pairs + pr] - jnp.concatenate([r[pr][0][:c], r[pr][1][:c]], axis=0)).astype(BF16)
               for pr in range(n_pairs)]
        o = [jnp.concatenate([r[pr][0][c:], r[pr][1][c:]], axis=0) + _dot(qk16_ref[ci * n_pairs + pr], v16[pr])
             for pr in range(n_pairs)]
        for pr in range(n_pairs):
            kd16 = kd16_ref[ci * n_pairs + pr]
            for hh in range(2):
                h = 2 * pr + hh
                ga = GATE_A + h
                hs = slice(hh * c, (hh + 1) * c)
                s_ref[h] = s_ref[h] * egl[:, ga:ga + 1] + _dot_tn(kd16[hs], v16[pr][hs])
        for pr in range(n_pairs):
            for hh in range(2):
                h = 2 * pr + hh
                z = zg_ref[rows, h * GDN_DV:(h + 1) * GDN_DV]
                o_ref[rows, h * GDN_DV:(h + 1) * GDN_DV] = (
                    _rms_rows(o[pr][hh * c:(hh + 1) * c], nw) * _silu(z)).astype(o_ref.dtype)

    steps = [conv]
    for g0 in range(0, Lb // c, GDN_LOCAL_CHUNKS):
        cis = list(range(g0, g0 + GDN_LOCAL_CHUNKS))
        steps.append(functools.partial(local_group, cis))
        steps += [functools.partial(recurrent, ci) for ci in cis]

    def final():
        cst_out_ref[0] = xpad_ref[5:8, :]
        sst_out_ref[0] = s_ref[...]

    return init, steps, final


def _gdn_prompt_spec(proj, cw, gbias, galog, nw, *, B, L, Lb):
    nl = L // Lb
    n_tiles = (Lb // CHUNK) * (GDN_HEADS // 2)
    row = lambda b, l: b * nl + l
    const = lambda shape: pl.BlockSpec(shape, lambda b, l: (0,) * len(shape))
    return dict(
        in_specs=[
            pl.BlockSpec((Lb, GDN_CONV_CH), lambda b, l: (row(b, l), OFF_QKV // GDN_CONV_CH)),
            pl.BlockSpec((Lb, GDN_V), lambda b, l: (row(b, l), OFF_ZG // GDN_V)),
            pl.BlockSpec((Lb, LANES), lambda b, l: (row(b, l), OFF_GATE // LANES)),
            const((CONV_W, GDN_CONV_CH)), const((1, LANES)), const((1, LANES)), const((1, GDN_DV)),
        ],
        out_specs=[
            pl.BlockSpec((Lb, GDN_V), lambda b, l: (row(b, l), 0)),
            pl.BlockSpec((1, CONV_W - 1, GDN_CONV_CH), lambda b, l: (b, 0, 0)),
            pl.BlockSpec((1, GDN_HEADS, GDN_DK, GDN_DV), lambda b, l: (b, 0, 0, 0)),
        ],
        out_shape=[
            jax.ShapeDtypeStruct((B * L, GDN_V), BF16),
            jax.ShapeDtypeStruct((B, CONV_W - 1, GDN_CONV_CH), F32),
            jax.ShapeDtypeStruct((B, GDN_HEADS, GDN_DK, GDN_DV), F32),
        ],
        scratch=[
            pltpu.VMEM((Lb + 8, GDN_CONV_CH), F32),
            pltpu.VMEM((Lb, GDN_CONV_CH), F32),
            pltpu.VMEM((GDN_HEADS, GDN_DK, GDN_DV), F32),
            pltpu.VMEM((n_tiles, STACK, GDN_DV), F32),
            pltpu.VMEM((2 * n_tiles, STACK, GDN_DK), BF16),
            pltpu.VMEM((n_tiles, STACK, GDN_DK), BF16),
            pltpu.VMEM((n_tiles, STACK, STACK), BF16),
            pltpu.VMEM((Lb // CHUNK, 8, LANES), F32),
        ],
        args=[proj, proj, proj, cw, gbias, galog, nw])


def _gdn_sample_kernel(qkv_ref, zg_ref, gate_ref, cst_ref, sst_ref, cw_ref, gbias_ref, galog_ref, nw_ref,
                       o_ref, cst_out_ref, sst_out_ref, xpad_ref, qkvc_ref, *, nb, L):
    R = nb * L
    sh = L.bit_length() - 1
    for bi in range(nb):
        xp = xpad_ref.at[bi]
        xp[5:8, :] = cst_ref[:, bi, :]
        cst_out_ref[:, bi, :] = _conv_block(
            qkv_ref.at[pl.ds(bi * L, L)], xp, cw_ref, None, qkvc_ref.at[pl.ds(bi * L, L)],
            L, GDN_CONV_CH, _gdn_qk_post)

    row_i = lax.broadcasted_iota(jnp.int32, (R, R), 0)
    col_i = lax.broadcasted_iota(jnp.int32, (R, R), 1)
    tril_f = (((row_i >> sh) == (col_i >> sh)) & (row_i >= col_i)).astype(F32)
    graw = gate_ref[...]
    sp = _softplus(graw + gbias_ref[...])
    beta_all = jax.nn.sigmoid(graw)
    G = jnp.dot(tril_f, -jnp.exp(galog_ref[...]) * sp, precision=_HIGHEST, preferred_element_type=F32)
    glast = [G[bi * L + L - 1:bi * L + L, :] for bi in range(nb)]
    eG = jnp.exp(G)
    eGrev = jnp.exp(jnp.concatenate([jnp.broadcast_to(x, (L, LANES)) for x in glast], axis=0) - G)
    egl = [jnp.exp(x) for x in glast]
    nw = nw_ref[...]

    n_st = R // (2 * L)

    def tiles(ref, st, off):
        return jnp.concatenate(
            [ref[st * 2 * L:(st + 1) * 2 * L, off + h * LANES:off + (h + 1) * LANES]
             for h in range(GDN_HEADS)], axis=0)

    def colstack(m, st, off):
        return jnp.concatenate(
            [jnp.broadcast_to(m[st * 2 * L:(st + 1) * 2 * L, off + h:off + h + 1], (2 * L, LANES))
             for h in range(GDN_HEADS)], axis=0)

    items = [dict(q=tiles(qkvc_ref, st, 0), k=tiles(qkvc_ref, st, GDN_QK), v=tiles(qkvc_ref, st, 2 * GDN_QK),
                  beta=colstack(beta_all, st, GATE_B), g=colstack(G, st, GATE_A),
                  eg=colstack(eG, st, GATE_A), egrev=colstack(eGrev, st, GATE_A))
             for st in range(n_st)]
    uw, qk = _gdn_local(items, L)

    groups = [(h, bi) for h in range(GDN_HEADS) for bi in range(2)]
    r = []
    for st, it in enumerate(items):
        w = uw[st][1]
        qd = it["q"] * it["eg"]
        r.append([
            _dot(jnp.concatenate([w[gi * L:(gi + 1) * L], qd[gi * L:(gi + 1) * L]], axis=0).astype(BF16),
                 sst_ref[2 * st + bi, h].astype(BF16))
            for gi, (h, bi) in enumerate(groups)])
    v_new = [uw[st][0] - jnp.concatenate([x[:L] for x in r[st]], axis=0) for st in range(n_st)]
    o = [jnp.concatenate([x[L:] for x in r[st]], axis=0)
         + _dot(qk[st].astype(BF16), v_new[st].astype(BF16)) for st in range(n_st)]
    for st, it in enumerate(items):
        kd = it["k"] * it["egrev"]
        for gi, (h, bi) in enumerate(groups):
            b = 2 * st + bi
            ga = GATE_A + h
            rs = slice(gi * L, (gi + 1) * L)
            sst_out_ref[b, h] = (sst_ref[b, h] * egl[b][:, ga:ga + 1]
                                 + _dot_tn(kd[rs].astype(BF16), v_new[st][rs].astype(BF16)))
    for st in range(n_st):
        out = (_rms_rows(o[st], nw) * _silu(tiles(zg_ref, st, 0))).astype(o_ref.dtype)
        for h in range(GDN_HEADS):
            o_ref[st * 2 * L:(st + 1) * 2 * L, h * GDN_DV:(h + 1) * GDN_DV] = out[h * 2 * L:(h + 1) * 2 * L]


def _gdn_sample_spec(proj, conv_state, S_state, cw, gbias, galog, nw, *, B, L, nb):
    R = nb * L
    const = lambda shape: pl.BlockSpec(shape, lambda i: (0,) * len(shape))
    return dict(
        in_specs=[
            pl.BlockSpec((R, GDN_CONV_CH), lambda i: (i, OFF_QKV // GDN_CONV_CH)),
            pl.BlockSpec((R, GDN_V), lambda i: (i, OFF_ZG // GDN_V)),
            pl.BlockSpec((R, LANES), lambda i: (i, OFF_GATE // LANES)),
            pl.BlockSpec((CONV_W - 1, nb, GDN_CONV_CH), lambda i: (0, i, 0)),
            pl.BlockSpec((nb, GDN_HEADS, GDN_DK, GDN_DV), lambda i: (i, 0, 0, 0)),
            const((CONV_W, GDN_CONV_CH)), const((1, LANES)), const((1, LANES)), const((1, GDN_DV)),
        ],
        out_specs=[
            pl.BlockSpec((R, GDN_V), lambda i: (i, 0)),
            pl.BlockSpec((CONV_W - 1, nb, GDN_CONV_CH), lambda i: (0, i, 0)),
            pl.BlockSpec((nb, GDN_HEADS, GDN_DK, GDN_DV), lambda i: (i, 0, 0, 0)),
        ],
        out_shape=[
            jax.ShapeDtypeStruct((B * L, GDN_V), BF16),
            jax.ShapeDtypeStruct((CONV_W - 1, B, GDN_CONV_CH), F32),
            jax.ShapeDtypeStruct((B, GDN_HEADS, GDN_DK, GDN_DV), F32),
        ],
        scratch=[
            pltpu.VMEM((nb, L + 8, GDN_CONV_CH), F32),
            pltpu.VMEM((R, GDN_CONV_CH), F32),
        ],
        args=[proj, proj, proj, conv_state, S_state, cw, gbias, galog, nw])


N_PAIRS = SSM_HEADS // 2
PAIRS_PER_GROUP = N_PAIRS // SSM_GROUPS
GROUP_W = SSM_DI // SSM_GROUPS


def _ssm_tile(graw, gbias, nega, ld_x, ld_b, ld_c, ld_z, dcols_ref, nw_ref, st_o, get_h, set_h, glen):
    c = CHUNK
    P = SSM_P
    nseq = c // glen
    sh = glen.bit_length() - 1
    ri = lax.broadcasted_iota(jnp.int32, (c, c), 0)
    ci = lax.broadcasted_iota(jnp.int32, (c, c), 1)
    tril_f = (((ri >> sh) == (ci >> sh)) & (ri >= ci)).astype(F32)
    sp = _softplus(graw + gbias)
    acum = jnp.dot(tril_f, nega * sp, precision=_HIGHEST, preferred_element_type=F32)
    lasts = [acum[s * glen + glen - 1:(s + 1) * glen, :] for s in range(nseq)]
    alast = jnp.concatenate([jnp.broadcast_to(x, (glen, LANES)) for x in lasts], axis=0)
    dtrev = sp * jnp.exp(alast - acum)
    eal = [jnp.exp(x) for x in lasts]
    lane = lax.broadcasted_iota(jnp.int32, (c, LANES), 1)
    row = lax.broadcasted_iota(jnp.int32, (c, LANES), 0)
    m = jnp.where(lane < GATE_DT2, acum, sp)
    mt = jnp.concatenate([m, m], axis=0).T
    left = lane < P
    left_row = left[0:1]
    j = jnp.where(left, lane, lane - P)
    tril2 = ((row >> sh) == (j >> sh)) & (row >= j)
    rowh = lax.broadcasted_iota(jnp.int32, (2 * P, SSM_N), 0) < P

    def expand(mat, c0):
        return jnp.where(left, jnp.broadcast_to(mat[:, c0:c0 + 1], (c, LANES)),
                         jnp.broadcast_to(mat[:, c0 + 1:c0 + 2], (c, LANES)))

    def rowsel(base, e):
        return jnp.where(left_row, mt[base + 2 * e:base + 2 * e + 1, :], mt[base + 2 * e + 1:base + 2 * e + 2, :])

    for g in range(SSM_GROUPS):
        Bg = ld_b(g)
        Cg = ld_c(g)
        Bg16 = Bg.astype(BF16)
        Cg16 = Cg.astype(BF16)
        cb2 = _dot_nt(Cg16, jnp.concatenate([Bg16, Bg16], axis=0))
        pairs = [g * PAIRS_PER_GROUP + e4 for e4 in range(PAIRS_PER_GROUP)]
        acol = [expand(acum, GATE_DT + 2 * e) for e in pairs]
        scores16 = [
            (cb2 * jnp.exp(jnp.where(tril2, a - rowsel(GATE_DT, e), -jnp.inf)) * rowsel(GATE_DT2, e)).astype(BF16)
            for a, e in zip(acol, pairs)]
        xp = [ld_x(e) for e in pairs]
        bd16 = [jnp.concatenate([jnp.where(left, x, 0.0), jnp.where(left, 0.0, x)], axis=0).astype(BF16)
                for x in xp]
        ydiag = [_dot(s, b) for s, b in zip(scores16, bd16)]
        if nseq == 1:
            yoff = [_dot_nt(Cg16, get_h(0, e).astype(BF16)) for e in pairs]
        else:
            yoff = [jnp.concatenate(
                [_dot_nt(Cg[s * glen:(s + 1) * glen].astype(BF16), get_h(s, e).astype(BF16))
                 for s in range(nseq)], axis=0) for e in pairs]
        y = [yd + yo * jnp.exp(a) + dcols_ref[:, e * 2 * P:(e + 1) * 2 * P] * x
             for yd, yo, a, e, x in zip(ydiag, yoff, acol, pairs, xp)]
        xdr = [x * expand(dtrev, GATE_DT + 2 * e) for x, e in zip(xp, pairs)]
        for e, xd in zip(pairs, xdr):
            c0 = GATE_DT + 2 * e
            for s in range(nseq):
                rs = slice(s * glen, (s + 1) * glen)
                ealcol = jnp.where(rowh, eal[s][:, c0:c0 + 1], eal[s][:, c0 + 1:c0 + 2])
                set_h(s, e, get_h(s, e) * ealcol + _dot_tn(xd[rs].astype(BF16), Bg[rs].astype(BF16)))
        yg = jnp.concatenate(y, axis=1) * _silu(ld_z(g))
        gcols = slice(g * GROUP_W, (g + 1) * GROUP_W)
        st_o(g, _rms_rows(yg, nw_ref[:, gcols]))


def _ssm_prompt_parts(xs_ref, bc_ref, zs_ref, gate_ref, cwx_ref, cbx_ref, cwbc_ref, cbbc_ref,
                      gbias_ref, galog_ref, dcols_ref, nw_ref,
                      o_ref, cst_out_ref, hst_out_ref, xpadx_ref, xpadbc_ref, xc_ref, bcc_ref, hh_ref, *, Lb):
    c = CHUNK

    def init():
        hh_ref[...] = jnp.zeros(hh_ref.shape, F32)
        xpadx_ref[0:8, :] = jnp.zeros((8, SSM_DI), F32)
        xpadbc_ref[0:8, :] = jnp.zeros((8, 2 * SSM_BC), F32)

    def conv():
        ident = lambda s, y: y
        _conv_block(xs_ref, xpadx_ref, cwx_ref, cbx_ref, xc_ref, Lb, SSM_DI, ident)
        _conv_block(bc_ref, xpadbc_ref, cwbc_ref, cbbc_ref, bcc_ref, Lb, 2 * SSM_BC, ident)

    def set_h(s, e, val):
        hh_ref[e] = val

    def tile(ci):
        rows = slice(ci * c, (ci + 1) * c)

        def st_o(g, val):
            o_ref[rows, g * GROUP_W:(g + 1) * GROUP_W] = val.astype(o_ref.dtype)

        _ssm_tile(
            gate_ref[rows, :], gbias_ref[...], -jnp.exp(galog_ref[...]),
            lambda e: xc_ref[rows, e * LANES:(e + 1) * LANES],
            lambda g: bcc_ref[rows, g * SSM_N:(g + 1) * SSM_N],
            lambda g: bcc_ref[rows, SSM_BC + g * SSM_N:SSM_BC + (g + 1) * SSM_N],
            lambda g: zs_ref[rows, g * GROUP_W:(g + 1) * GROUP_W],
            dcols_ref, nw_ref, st_o, lambda s, e: hh_ref[e], set_h, c)

    steps = [conv] + [functools.partial(tile, ci) for ci in range(Lb // c)]

    def final():
        cst_out_ref[0, :, :SSM_DI] = xpadx_ref[5:8, :]
        cst_out_ref[0, :, SSM_DI:] = xpadbc_ref[5:8, :]
        hst_out_ref[0] = hh_ref[...]

    return init, steps, final


def _ssm_sample_kernel(xs_ref, bc_ref, zs_ref, gate_ref, cst_ref, hst_ref, cwx_ref, cbx_ref, cwbc_ref, cbbc_ref,
                       gbias_ref, galog_ref, dcols_ref, nw_ref,
                       o_ref, cst_out_ref, hst_out_ref, xpadx_ref, xpadbc_ref, xc_ref, bcc_ref, *, L):
    ident = lambda s, y: y
    for bi in range(CHUNK // L):
        rs = pl.ds(bi * L, L)
        xpx = xpadx_ref.at[bi]
        xpb = xpadbc_ref.at[bi]
        xpx[5:8, :] = cst_ref[:, bi, :SSM_DI]
        xpb[5:8, :] = cst_ref[:, bi, SSM_DI:]
        cst_out_ref[:, bi, :SSM_DI] = _conv_block(
            xs_ref.at[rs], xpx, cwx_ref, cbx_ref, xc_ref.at[rs], L, SSM_DI, ident)
        cst_out_ref[:, bi, SSM_DI:] = _conv_block(
            bc_ref.at[rs], xpb, cwbc_ref, cbbc_ref, bcc_ref.at[rs], L, 2 * SSM_BC, ident)

    def st_o(g, val):
        o_ref[:, g * GROUP_W:(g + 1) * GROUP_W] = val.astype(o_ref.dtype)

    def set_h(s, e, val):
        hst_out_ref[s, e] = val

    _ssm_tile(
        gate_ref[...], gbias_ref[...], -jnp.exp(galog_ref[...]),
        lambda e: xc_ref[:, e * LANES:(e + 1) * LANES],
        lambda g: bcc_ref[:, g * SSM_N:(g + 1) * SSM_N],
        lambda g: bcc_ref[:, SSM_BC + g * SSM_N:SSM_BC + (g + 1) * SSM_N],
        lambda g: zs_ref[:, g * GROUP_W:(g + 1) * GROUP_W],
        dcols_ref, nw_ref, st_o, lambda s, e: hst_ref[s, e], set_h, L)


def _ssm_const_specs():
    const = lambda shape: pl.BlockSpec(shape, lambda *idx: (0,) * len(shape))
    return [
        const((CONV_W, SSM_DI)), const((1, SSM_DI)), const((CONV_W, 2 * SSM_BC)), const((1, 2 * SSM_BC)),
        const((1, LANES)), const((1, LANES)), const((1, SSM_DI)), const((1, SSM_DI)),
    ]


def _ssm_prompt_spec(proj, cwx, cbx, cwbc, cbbc, gbias, galog, dcols, nw, *, B, L, Lb):
    nl = L // Lb
    row = lambda b, l: b * nl + l
    return dict(
        in_specs=[
            pl.BlockSpec((Lb, SSM_DI), lambda b, l: (row(b, l), OFF_XS // SSM_DI)),
            pl.BlockSpec((Lb, 2 * SSM_BC), lambda b, l: (row(b, l), OFF_BC // (2 * SSM_BC))),
            pl.BlockSpec((Lb, SSM_DI), lambda b, l: (row(b, l), OFF_ZS // SSM_DI)),
            pl.BlockSpec((Lb, LANES), lambda b, l: (row(b, l), OFF_GATE // LANES)),
        ] + _ssm_const_specs(),
        out_specs=[
            pl.BlockSpec((Lb, SSM_DI), lambda b, l: (row(b, l), 0)),
            pl.BlockSpec((1, CONV_W - 1, SSM_CONV_CH), lambda b, l: (b, 0, 0)),
            pl.BlockSpec((1, N_PAIRS, 2 * SSM_P, SSM_N), lambda b, l: (b, 0, 0, 0)),
        ],
        out_shape=[
            jax.ShapeDtypeStruct((B * L, SSM_DI), BF16),
            jax.ShapeDtypeStruct((B, CONV_W - 1, SSM_CONV_CH), F32),
            jax.ShapeDtypeStruct((B, N_PAIRS, 2 * SSM_P, SSM_N), F32),
        ],
        scratch=[
            pltpu.VMEM((Lb + 8, SSM_DI), F32),
            pltpu.VMEM((Lb + 8, 2 * SSM_BC), F32),
            pltpu.VMEM((Lb, SSM_DI), F32),
            pltpu.VMEM((Lb, 2 * SSM_BC), F32),
            pltpu.VMEM((N_PAIRS, 2 * SSM_P, SSM_N), F32),
        ],
        args=[proj, proj, proj, proj, cwx, cbx, cwbc, cbbc, gbias, galog, dcols, nw])


def _ssm_sample_spec(proj, conv_state, h_pairs, cwx, cbx, cwbc, cbbc, gbias, galog, dcols, nw, *, B, L):
    nb = CHUNK // L
    return dict(
        in_specs=[
            pl.BlockSpec((CHUNK, SSM_DI), lambda i: (i, OFF_XS // SSM_DI)),
            pl.BlockSpec((CHUNK, 2 * SSM_BC), lambda i: (i, OFF_BC // (2 * SSM_BC))),
            pl.BlockSpec((CHUNK, SSM_DI), lambda i: (i, OFF_ZS // SSM_DI)),
            pl.BlockSpec((CHUNK, LANES), lambda i: (i, OFF_GATE // LANES)),
            pl.BlockSpec((CONV_W - 1, nb, SSM_CONV_CH), lambda i: (0, i, 0)),
            pl.BlockSpec((nb, N_PAIRS, 2 * SSM_P, SSM_N), lambda i: (i, 0, 0, 0)),
        ] + _ssm_const_specs(),
        out_specs=[
            pl.BlockSpec((CHUNK, SSM_DI), lambda i: (i, 0)),
            pl.BlockSpec((CONV_W - 1, nb, SSM_CONV_CH), lambda i: (0, i, 0)),
            pl.BlockSpec((nb, N_PAIRS, 2 * SSM_P, SSM_N), lambda i: (i, 0, 0, 0)),
        ],
        out_shape=[
            jax.ShapeDtypeStruct((B * L, SSM_DI), BF16),
            jax.ShapeDtypeStruct((CONV_W - 1, B, SSM_CONV_CH), F32),
            jax.ShapeDtypeStruct((B, N_PAIRS, 2 * SSM_P, SSM_N), F32),
        ],
        scratch=[
            pltpu.VMEM((nb, L + 8, SSM_DI), F32),
            pltpu.VMEM((nb, L + 8, 2 * SSM_BC), F32),
            pltpu.VMEM((CHUNK, SSM_DI), F32),
            pltpu.VMEM((CHUNK, 2 * SSM_BC), F32),
        ],
        args=[proj, proj, proj, proj, conv_state, h_pairs, cwx, cbx, cwbc, cbbc, gbias, galog, dcols, nw])


def _split_refs(refs, g, s):
    it = iter(refs)
    take = lambda n: [next(it) for _ in range(n)]
    g_in, s_in = take(len(g["in_specs"])), take(len(s["in_specs"]))
    g_out, s_out = take(len(g["out_specs"])), take(len(s["out_specs"]))
    g_scr, s_scr = take(len(g["scratch"])), take(len(s["scratch"]))
    return g_in + g_out + g_scr, s_in + s_out + s_scr


def _mix_prompt_kernel(*refs, g, s, Lb):
    g_refs, s_refs = _split_refs(refs, g, s)
    g_init, g_steps, g_final = _gdn_prompt_parts(*g_refs, Lb=Lb)
    s_init, s_steps, s_final = _ssm_prompt_parts(*s_refs, Lb=Lb)
    l = pl.program_id(1)

    @pl.when(l == 0)
    def _init():
        g_init()
        s_init()

    for g_step, s_step in itertools.zip_longest(g_steps, s_steps):
        if g_step is not None:
            g_step()
        if s_step is not None:
            s_step()

    @pl.when(l == pl.num_programs(1) - 1)
    def _final():
        g_final()
        s_final()


def _mix_sample_kernel(*refs, g, s, nb, L):
    g_refs, s_refs = _split_refs(refs, g, s)
    _gdn_sample_kernel(*g_refs, nb=nb, L=L)
    _ssm_sample_kernel(*s_refs, L=L)


def _fused_call(kernel, g, s, grid, semantics, name):
    outs = pl.pallas_call(
        functools.partial(kernel, g={k: g[k] for k in ("in_specs", "out_specs", "scratch")},
                          s={k: s[k] for k in ("in_specs", "out_specs", "scratch")}),
        grid=grid,
        in_specs=g["in_specs"] + s["in_specs"],
        out_specs=g["out_specs"] + s["out_specs"],
        out_shape=g["out_shape"] + s["out_shape"],
        scratch_shapes=g["scratch"] + s["scratch"],
        compiler_params=pltpu.CompilerParams(dimension_semantics=semantics, vmem_limit_bytes=VMEM_LIMIT),
        name=name,
    )(*g["args"], *s["args"])
    n = len(g["out_specs"])
    return outs[:n], outs[n:]


def _outproj_kernel(x_ref, mg_ref, ms_ref, w_ref, o_ref):
    acc = _dot(mg_ref[...].astype(BF16), w_ref[:GDN_V, :])
    acc = acc + _dot(ms_ref[...].astype(BF16), w_ref[GDN_V:, :])
    o_ref[...] = x_ref[...] + acc


def _out_proj(x2d, mix_g, mix_s, w_out16, *, tm):
    T = x2d.shape[0]
    return pl.pallas_call(
        _outproj_kernel,
        grid=(T // tm,),
        in_specs=[
            pl.BlockSpec((tm, D_MODEL), lambda i: (i, 0)),
            pl.BlockSpec((tm, GDN_V), lambda i: (i, 0)),
            pl.BlockSpec((tm, SSM_DI), lambda i: (i, 0)),
            pl.BlockSpec((D_MODEL, D_MODEL), lambda i: (0, 0)),
        ],
        out_specs=pl.BlockSpec((tm, D_MODEL), lambda i: (i, 0)),
        out_shape=jax.ShapeDtypeStruct((T, D_MODEL), F32),
        compiler_params=pltpu.CompilerParams(
            dimension_semantics=("parallel",), vmem_limit_bytes=VMEM_LIMIT),
        name="out_proj",
    )(x2d, mix_g, mix_s, w_out16)


FFN_SUB = 4


def _ffn_kernel(x_ref, nw_ref, wg_ref, wu_ref, wd_ref, fnw_ref, o_ref, h_ref):
    f = pl.program_id(1)
    nf = pl.num_programs(1)
    rs = h_ref.shape[0] // FFN_SUB

    def step(first, last):
        def gate_up(r):
            rows = slice(r * rs, (r + 1) * rs)
            if first:
                h = _rms_rows(x_ref[rows, :], nw_ref[...]).astype(BF16)
                h_ref[rows, :] = h
            else:
                h = h_ref[rows, :]
            return _dot(h, wg_ref[...]), _dot(h, wu_ref[...])

        def down(r, gu):
            rows = slice(r * rs, (r + 1) * rs)
            d = _dot((_silu(gu[0]) * gu[1]).astype(BF16), wd_ref[...])
            acc = d if first else o_ref[rows, :] + d
            if last:
                o_ref[rows, :] = _rms_rows(x_ref[rows, :] + acc, fnw_ref[...])
            else:
                o_ref[rows, :] = acc

        gu = gate_up(0)
        for r in range(1, FFN_SUB):
            gu_next = gate_up(r)
            down(r - 1, gu)
            gu = gu_next
        down(FFN_SUB - 1, gu)

    pl.when(f == 0)(lambda: step(True, False))
    pl.when((f > 0) & (f < nf - 1))(lambda: step(False, False))
    pl.when(f == nf - 1)(lambda: step(False, True))


def _ffn(x2d, norm_w, wg16, wu16, wd16, final_w, *, tm, tf):
    T = x2d.shape[0]
    return pl.pallas_call(
        _ffn_kernel,
        grid=(T // tm, D_FF // tf),
        in_specs=[
            pl.BlockSpec((tm, D_MODEL), lambda i, f: (i, 0)),
            pl.BlockSpec((1, D_MODEL), lambda i, f: (0, 0)),
            pl.BlockSpec((D_MODEL, tf), lambda i, f: (0, f)),
            pl.BlockSpec((D_MODEL, tf), lambda i, f: (0, f)),
            pl.BlockSpec((tf, D_MODEL), lambda i, f: (f, 0)),
            pl.BlockSpec((1, D_MODEL), lambda i, f: (0, 0)),
        ],
        out_specs=pl.BlockSpec((tm, D_MODEL), lambda i, f: (i, 0)),
        out_shape=jax.ShapeDtypeStruct((T, D_MODEL), F32),
        scratch_shapes=[pltpu.VMEM((tm, D_MODEL), BF16)],
        compiler_params=pltpu.CompilerParams(
            dimension_semantics=("parallel", "arbitrary"), vmem_limit_bytes=VMEM_LIMIT),
        name="ffn",
    )(x2d, norm_w, wg16, wu16, wd16, final_w)


PROMPT_ROWS = 256


def _trunk(x, states, p):
    B, L, _ = x.shape
    x2d = x.reshape(B * L, D_MODEL)
    proj = _in_proj(x2d, p["attn_norm_w"], p["w_in_r"], tm=1024, tn=768)
    gdn_w = (p["gdn_conv_w"], p["gbias"], p["galog"], p["gdn_norm_w"])
    ssm_w = (p["cwx"], p["cbx"], p["cwbc"], p["cbbc"], p["gbias"], p["galog"], p["dcols"], p["ssm_norm_w"])
    pair_shape = (B, N_PAIRS, 2 * SSM_P, SSM_N)
    if states is None:
        Lb = PROMPT_ROWS
        g = _gdn_prompt_spec(proj, *gdn_w, B=B, L=L, Lb=Lb)
        s = _ssm_prompt_spec(proj, *ssm_w, B=B, L=L, Lb=Lb)
        kern = functools.partial(_mix_prompt_kernel, Lb=Lb)
        (mix_g, gconv_new, gS_new), (mix_s, sconv_new, sh_new) = _fused_call(
            kern, g, s, (B, L // Lb), ("parallel", "arbitrary"), "mix_prompt")
    else:
        gconv, gS, sconv, sh = states
        nb = CHUNK // L
        tap_major = lambda a: jnp.swapaxes(a, 0, 1)
        g = _gdn_sample_spec(proj, tap_major(gconv), gS, *gdn_w, B=B, L=L, nb=nb)
        s = _ssm_sample_spec(proj, tap_major(sconv), sh.reshape(pair_shape), *ssm_w, B=B, L=L)
        kern = functools.partial(_mix_sample_kernel, nb=nb, L=L)
        (mix_g, gconv_new, gS_new), (mix_s, sconv_new, sh_new) = _fused_call(
            kern, g, s, (B // nb,), ("parallel",), "mix_sample")
        gconv_new, sconv_new = tap_major(gconv_new), tap_major(sconv_new)
    sh_new = sh_new.reshape(B, SSM_HEADS, SSM_P, SSM_N)
    x1 = _out_proj(x2d, mix_g, mix_s, p["w_out16"], tm=512)
    y = _ffn(x1, p["ffn_norm_w"], p["wg16"], p["wu16"], p["wd16"], p["final_norm_w"], tm=1024, tf=512)
    return y.reshape(B, L, D_MODEL), (gconv_new[None], gS_new[None], sconv_new[None], sh_new[None])


def kernel(x_prompt, x_sample, state_gdn_conv, state_gdn, state_ssm_conv, state_ssm,
           attn_norm_w, w_in, gdn_conv_w, gdn_A_log, gdn_dt_bias, gdn_norm_w,
           ssm_conv_w, ssm_conv_b, ssm_A_log, ssm_dt_bias, ssm_D, ssm_norm_w,
           w_out, ffn_norm_w, w_gate, w_up, w_down, final_norm_w):
    assert w_in.shape[0] == 1, "single-layer trunk"
    assert x_prompt.shape[1] % PROMPT_ROWS == 0 and x_sample.shape[1] == 8 and x_sample.shape[0] % 8 == 0
    assert w_in.shape[2] == D_IN_PROJ
    w_in_r = _w_in_prep(jnp.swapaxes(w_in, 1, 2), tk=256)
    zeros8 = jnp.zeros((GDN_HEADS,), F32)
    tail = jnp.zeros((LANES - GATE_DT2 - SSM_HEADS,), F32)
    gbias = jnp.concatenate([zeros8, gdn_dt_bias[0], ssm_dt_bias[0], ssm_dt_bias[0], tail])[None]
    galog = jnp.concatenate([zeros8, gdn_A_log[0], ssm_A_log[0], ssm_A_log[0], tail])[None]
    p = dict(
        attn_norm_w=attn_norm_w, w_in_r=w_in_r, gdn_conv_w=gdn_conv_w[0], gbias=gbias, galog=galog,
        gdn_norm_w=gdn_norm_w,
        cwx=ssm_conv_w[0][:, :SSM_DI], cbx=ssm_conv_b[:, :SSM_DI],
        cwbc=ssm_conv_w[0][:, SSM_DI:], cbbc=ssm_conv_b[:, SSM_DI:],
        dcols=jnp.repeat(ssm_D[0], SSM_P)[None], ssm_norm_w=ssm_norm_w,
        w_out16=w_out[0].astype(BF16), ffn_norm_w=ffn_norm_w,
        wg16=w_gate[0].astype(BF16), wu16=w_up[0].astype(BF16), wd16=w_down[0].astype(BF16),
        final_norm_w=final_norm_w[None],
    )
    y_p, st_p = _trunk(x_prompt, None, p)
    y_s, st_s = _trunk(x_sample, (state_gdn_conv[0], state_gdn[0], state_ssm_conv[0], state_ssm[0]), p)
    return (y_p, y_s, st_p[0], st_p[1], st_p[2], st_p[3], st_s[0], st_s[1], st_s[2], st_s[3])
```

```python
import functools
import itertools

import jax
import jax.numpy as jnp
from jax import lax
from jax.experimental import pallas as pl
from jax.experimental.pallas import tpu as pltpu

F32 = jnp.float32
BF16 = jnp.bfloat16

D_MODEL = 2048
GDN_HEADS = 8
GDN_DK = 128
GDN_DV = 128
GDN_QK = GDN_HEADS * GDN_DK
GDN_V = GDN_HEADS * GDN_DV
GDN_CONV_CH = 2 * GDN_QK + GDN_V
SSM_P = 64
SSM_N = 128
SSM_GROUPS = 2
SSM_DI = 1024
SSM_HEADS = SSM_DI // SSM_P
SSM_BC = SSM_GROUPS * SSM_N
SSM_CONV_CH = SSM_DI + 2 * SSM_BC
CONV_W = 4
CHUNK = 64
D_FF = 5632
EPS = 1e-6

OFF_QKV = 0
OFF_ZG = OFF_QKV + GDN_CONV_CH
OFF_ZS = OFF_ZG + GDN_V
OFF_XS = OFF_ZS + SSM_DI
OFF_BC = OFF_XS + SSM_DI
OFF_GATE = OFF_BC + 2 * SSM_BC
LANES = 128
SUBLANES = 8
CONV_PAD = SUBLANES
CONV_HIST = CONV_PAD - (CONV_W - 1)
GATE_B = 0
GATE_A = GATE_B + GDN_HEADS
GATE_DT = GATE_A + GDN_HEADS
GATE_DT2 = GATE_DT + SSM_HEADS
N_PROJ = 6912

VMEM_LIMIT = 52 * 1024 * 1024
W_PREP_TK = 256
IN_PROJ_TM, IN_PROJ_TN = 1024, 768
OUT_PROJ_TM = 512
FFN_TM, FFN_TF = 1024, 512

_HIGHEST = lax.Precision.HIGHEST


def _silu(x):
    h = 0.5 * x
    return h + h * jnp.tanh(h)


def _softplus(x):
    return jnp.maximum(x, 0.0) + jnp.log1p(jnp.exp(-jnp.abs(x)))


def _dot(a, b):
    return jnp.dot(a, b, preferred_element_type=F32)


def _dot_nt(a, b):
    return lax.dot_general(a, b, (((1,), (1,)), ((), ())), preferred_element_type=F32)


def _dot_tn(a, b):
    return lax.dot_general(a, b, (((0,), (0,)), ((), ())), preferred_element_type=F32)


def _rms_rows(x, w):
    return x * lax.rsqrt(jnp.mean(x * x, axis=-1, keepdims=True) + EPS) * w


W_B = OFF_ZS
W_ZS = W_B + 2 * GDN_HEADS
W_DT = W_ZS + SSM_DI + SSM_CONV_CH
D_IN_PROJ = W_DT + SSM_HEADS


def _wprep_kernel(w_ref, o_ref):
    cols = o_ref.shape[1]
    o_ref[:W_B, :] = w_ref[0, :W_B, :].astype(BF16)
    o_ref[W_B:OFF_GATE, :] = w_ref[0, W_ZS:W_DT, :].astype(BF16)
    dt = w_ref[0, W_DT:D_IN_PROJ, :].astype(BF16)
    o_ref[OFF_GATE:OFF_GATE + GATE_DT, :] = w_ref[0, W_B:W_ZS, :].astype(BF16)
    o_ref[OFF_GATE + GATE_DT:OFF_GATE + GATE_DT2, :] = dt
    o_ref[OFF_GATE + GATE_DT2:OFF_GATE + GATE_DT2 + SSM_HEADS, :] = dt
    o_ref[OFF_GATE + GATE_DT2 + SSM_HEADS:, :] = jnp.zeros((N_PROJ - OFF_GATE - GATE_DT2 - SSM_HEADS, cols), BF16)


def _w_in_prep(w_in_t, *, tk):
    return pl.pallas_call(
        _wprep_kernel,
        grid=(D_MODEL // tk,),
        in_specs=[pl.BlockSpec((1, D_IN_PROJ, tk), lambda i: (0, 0, i))],
        out_specs=pl.BlockSpec((N_PROJ, tk), lambda i: (0, i)),
        out_shape=jax.ShapeDtypeStruct((N_PROJ, D_MODEL), BF16),
        compiler_params=pltpu.CompilerParams(
            dimension_semantics=("parallel",), vmem_limit_bytes=VMEM_LIMIT),
        name="w_in_prep",
    )(w_in_t)


INPROJ_SUB = 4


def _inproj_kernel(x_ref, nw_ref, wt_ref, o_ref, h_ref):
    j = pl.program_id(1)

    @pl.when(j == 0)
    def _first():
        rs = h_ref.shape[0] // INPROJ_SUB
        for r in range(INPROJ_SUB):
            rows = slice(r * rs, (r + 1) * rs)
            h = _rms_rows(x_ref[rows, :], nw_ref[...]).astype(BF16)
            h_ref[rows, :] = h
            o_ref[rows, :] = _dot_nt(h, wt_ref[...])

    @pl.when(j > 0)
    def _rest():
        o_ref[...] = _dot_nt(h_ref[...], wt_ref[...])


def _in_proj(x2d, norm_w, w_in_r, *, tm, tn):
    T = x2d.shape[0]
    return pl.pallas_call(
        _inproj_kernel,
        grid=(T // tm, N_PROJ // tn),
        in_specs=[
            pl.BlockSpec((tm, D_MODEL), lambda i, j: (i, 0)),
            pl.BlockSpec((1, D_MODEL), lambda i, j: (0, 0)),
            pl.BlockSpec((tn, D_MODEL), lambda i, j: (j, 0)),
        ],
        out_specs=pl.BlockSpec((tm, tn), lambda i, j: (i, j)),
        out_shape=jax.ShapeDtypeStruct((T, N_PROJ), F32),
        scratch_shapes=[pltpu.VMEM((tm, D_MODEL), BF16)],
        compiler_params=pltpu.CompilerParams(
            dimension_semantics=("parallel", "arbitrary"), vmem_limit_bytes=VMEM_LIMIT),
        name="in_proj",
    )(x2d, norm_w, w_in_r)


def _conv_block(x_ref, xpad_ref, cw_ref, cb_ref, dst_ref, Lb, C, post):
    xpad_ref[CONV_PAD:CONV_PAD + Lb, :] = x_ref[...]
    rs = min(Lb, CHUNK)
    for sb in range(Lb // rs):
        r = sb * rs
        for s in range(C // LANES):
            cols = slice(s * LANES, (s + 1) * LANES)
            acc = xpad_ref[CONV_PAD + r:CONV_PAD + r + rs, cols] * cw_ref[CONV_W - 1:CONV_W, cols]
            for i in range(CONV_W - 1):
                acc = acc + xpad_ref[CONV_HIST + i + r:CONV_HIST + i + r + rs, cols] * cw_ref[i:i + 1, cols]
            if cb_ref is not None:
                acc = acc + cb_ref[:, cols]
            dst_ref[r:r + rs, cols] = post(s, _silu(acc))
    hist = xpad_ref[Lb + CONV_HIST:Lb + CONV_PAD, :]
    xpad_ref[CONV_HIST:CONV_PAD, :] = hist
    return hist


STACK = 128


def _gdn_qk_post(s, y):
    if s < 2 * GDN_HEADS:
        y = y * lax.rsqrt(jnp.sum(y * y, axis=-1, keepdims=True) + EPS)
        if s < GDN_HEADS:
            y = y * (GDN_DK ** -0.5)
    return y


def _gdn_local(items, glen):
    sh = glen.bit_length() - 1
    row = lax.broadcasted_iota(jnp.int32, (STACK, STACK), 0)
    col = lax.broadcasted_iota(jnp.int32, (STACK, STACK), 1)
    same = (row >> sh) == (col >> sh)
    incl = same & (row >= col)
    strict = same & (row > col)
    eye = (row == col).astype(F32)

    decay = [jnp.exp(jnp.where(incl, it["g"] - it["g"].T, -jnp.inf)) for it in items]
    kb = [it["k"] * it["beta"] for it in items]
    qkk = [_dot_nt(jnp.concatenate([it["q"], b], axis=0).astype(BF16), it["k"].astype(BF16))
           for it, b in zip(items, kb)]
    qk = [x[:STACK] * d for x, d in zip(qkk, decay)]
    nmat = [jnp.where(strict, -(x[STACK:] * d), 0.0) for x, d in zip(qkk, decay)]
    tinv = [eye + n for n in nmat]
    if sh >= 2:
        pw = [_dot(n.astype(BF16), n.astype(BF16)) for n in nmat]
        for _ in range(sh - 2):
            x = [_dot(jnp.concatenate([t, p], axis=0).astype(BF16), p.astype(BF16))
                 for t, p in zip(tinv, pw)]
            tinv = [t + y[:STACK] for t, y in zip(tinv, x)]
            pw = [y[STACK:] for y in x]
        tinv = [t + _dot(t.astype(BF16), p.astype(BF16)) for t, p in zip(tinv, pw)]
    uw = [_dot(t.astype(BF16),
               jnp.concatenate([it["v"] * it["beta"], b * it["eg"]], axis=1).astype(BF16))
          for t, it, b in zip(tinv, items, kb)]
    return [(x[:, :GDN_DV], x[:, GDN_DV:]) for x in uw], qk


GDN_LOCAL_CHUNKS = 2


def _gdn_prompt_parts(qkv_ref, zg_ref, gate_ref, cw_ref, gbias_ref, galog_ref, nw_ref,
                      o_ref, cst_out_ref, sst_out_ref,
                      xpad_ref, qkvc_ref, s_ref, u_ref, wq16_ref, kd16_ref, qk16_ref, egl_ref, *, Lb):
    c = CHUNK

    def init():
        xpad_ref[0:CONV_PAD, :] = jnp.zeros((CONV_PAD, GDN_CONV_CH), F32)
        s_ref[...] = jnp.zeros(s_ref.shape, F32)

    def conv():
        _conv_block(qkv_ref, xpad_ref, cw_ref, None, qkvc_ref, Lb, GDN_CONV_CH, _gdn_qk_post)

    row_i = lax.broadcasted_iota(jnp.int32, (c, c), 0)
    col_i = lax.broadcasted_iota(jnp.int32, (c, c), 1)
    tril_f = (row_i >= col_i).astype(F32)
    gbias = gbias_ref[...]
    nega = -jnp.exp(galog_ref[...])
    nw = nw_ref[...]
    n_pairs = GDN_HEADS // 2

    def local_chunk_items(ci):
        rows = slice(ci * c, (ci + 1) * c)
        graw = gate_ref[rows, :]
        sp = _softplus(graw + gbias)
        beta_all = jax.nn.sigmoid(graw)
        G = jnp.dot(tril_f, nega * sp, precision=_HIGHEST, preferred_element_type=F32)
        eG = jnp.exp(G)
        glast = G[c - 1:c, :]
        eGrev = jnp.exp(glast - G)
        egl_ref[ci] = jnp.broadcast_to(jnp.exp(glast), (SUBLANES, LANES))

        def heads(off, a, b):
            return jnp.concatenate([qkvc_ref[rows, off + a * LANES:off + (a + 1) * LANES],
                                    qkvc_ref[rows, off + b * LANES:off + (b + 1) * LANES]], axis=0)

        def colstack(m, a, b):
            return jnp.concatenate([jnp.broadcast_to(m[:, a:a + 1], (c, LANES)),
                                    jnp.broadcast_to(m[:, b:b + 1], (c, LANES))], axis=0)

        items = []
        for pr in range(n_pairs):
            a, b = 2 * pr, 2 * pr + 1
            items.append(dict(
                q=heads(0, a, b), k=heads(GDN_QK, a, b), v=heads(2 * GDN_QK, a, b),
                beta=colstack(beta_all, GATE_B + a, GATE_B + b),
                g=colstack(G, GATE_A + a, GATE_A + b),
                eg=colstack(eG, GATE_A + a, GATE_A + b),
                egrev=colstack(eGrev, GATE_A + a, GATE_A + b)))
        return items

    def local_group(cis):
        items = [it for ci in cis for it in local_chunk_items(ci)]
        uw, qk = _gdn_local(items, c)
        for n, it in enumerate(items):
            idx = cis[0] * n_pairs + n
            u, w = uw[n]
            qd = it["q"] * it["eg"]
            u_ref[idx] = u
            for hh in range(2):
                hs = slice(hh * c, (hh + 1) * c)
                wq16_ref[2 * idx + hh] = jnp.concatenate([w[hs], qd[hs]], axis=0).astype(BF16)
            kd16_ref[idx] = (it["k"] * it["egrev"]).astype(BF16)
            qk16_ref[idx] = qk[n].astype(BF16)

    def recurrent(ci):
        rows = slice(ci * c, (ci + 1) * c)
        egl = egl_ref[ci][0:1]
        r = [[_dot(wq16_ref[2 * (ci * n_pairs + pr) + hh], s_ref[2 * pr + hh].astype(BF16)) for hh in range(2)]
             for pr in range(n_pairs)]
        v16 = [(u_ref[ci * n_pairs + pr] - jnp.concatenate([r[pr][0][:c], r[pr][1][:c]], axis=0)).astype(BF16)
               for pr in range(n_pairs)]
        o = [jnp.concatenate([r[pr][0][c:], r[pr][1][c:]], axis=0) + _dot(qk16_ref[ci * n_pairs + pr], v16[pr])
             for pr in range(n_pairs)]
        for pr in range(n_pairs):
            kd16 = kd16_ref[ci * n_pairs + pr]
            for hh in range(2):
                h = 2 * pr + hh
                ga = GATE_A + h
                hs = slice(hh * c, (hh + 1) * c)
                s_ref[h] = s_ref[h] * egl[:, ga:ga + 1] + _dot_tn(kd16[hs], v16[pr][hs])
        for pr in range(n_pairs):
            for hh in range(2):
                h = 2 * pr + hh
                z = zg_ref[rows, h * GDN_DV:(h + 1) * GDN_DV]
                o_ref[rows, h * GDN_DV:(h + 1) * GDN_DV] = (
                    _rms_rows(o[pr][hh * c:(hh + 1) * c], nw) * _silu(z)).astype(o_ref.dtype)

    steps = [conv]
    for g0 in range(0, Lb // c, GDN_LOCAL_CHUNKS):
        cis = list(range(g0, g0 + GDN_LOCAL_CHUNKS))
        steps.append(functools.partial(local_group, cis))
        steps += [functools.partial(recurrent, ci) for ci in cis]

    def final():
        cst_out_ref[0] = xpad_ref[CONV_HIST:CONV_PAD, :]
        sst_out_ref[0] = s_ref[...]

    return init, steps, final


def _gdn_prompt_spec(proj, cw, gbias, galog, nw, *, B, L, Lb):
    nl = L // Lb
    n_tiles = (Lb // CHUNK) * (GDN_HEADS // 2)
    row = lambda b, l: b * nl + l
    const = lambda shape: pl.BlockSpec(shape, lambda b, l: (0,) * len(shape))
    return dict(
        in_specs=[
            pl.BlockSpec((Lb, GDN_CONV_CH), lambda b, l: (row(b, l), OFF_QKV // GDN_CONV_CH)),
            pl.BlockSpec((Lb, GDN_V), lambda b, l: (row(b, l), OFF_ZG // GDN_V)),
            pl.BlockSpec((Lb, LANES), lambda b, l: (row(b, l), OFF_GATE // LANES)),
            const((CONV_W, GDN_CONV_CH)), const((1, LANES)), const((1, LANES)), const((1, GDN_DV)),
        ],
        out_specs=[
            pl.BlockSpec((Lb, GDN_V), lambda b, l: (row(b, l), 0)),
            pl.BlockSpec((1, CONV_W - 1, GDN_CONV_CH), lambda b, l: (b, 0, 0)),
            pl.BlockSpec((1, GDN_HEADS, GDN_DK, GDN_DV), lambda b, l: (b, 0, 0, 0)),
        ],
        out_shape=[
            jax.ShapeDtypeStruct((B * L, GDN_V), BF16),
            jax.ShapeDtypeStruct((B, CONV_W - 1, GDN_CONV_CH), F32),
            jax.ShapeDtypeStruct((B, GDN_HEADS, GDN_DK, GDN_DV), F32),
        ],
        scratch=[
            pltpu.VMEM((Lb + CONV_PAD, GDN_CONV_CH), F32),
            pltpu.VMEM((Lb, GDN_CONV_CH), F32),
            pltpu.VMEM((GDN_HEADS, GDN_DK, GDN_DV), F32),
            pltpu.VMEM((n_tiles, STACK, GDN_DV), F32),
            pltpu.VMEM((2 * n_tiles, STACK, GDN_DK), BF16),
            pltpu.VMEM((n_tiles, STACK, GDN_DK), BF16),
            pltpu.VMEM((n_tiles, STACK, STACK), BF16),
            pltpu.VMEM((Lb // CHUNK, SUBLANES, LANES), F32),
        ],
        args=[proj, proj, proj, cw, gbias, galog, nw])


def _gdn_sample_kernel(qkv_ref, zg_ref, gate_ref, cst_ref, sst_ref, cw_ref, gbias_ref, galog_ref, nw_ref,
                       o_ref, cst_out_ref, sst_out_ref, xpad_ref, qkvc_ref, *, nb, L):
    R = nb * L
    sh = L.bit_length() - 1
    for bi in range(nb):
        xp = xpad_ref.at[bi]
        xp[CONV_HIST:CONV_PAD, :] = cst_ref[:, bi, :]
        cst_out_ref[:, bi, :] = _conv_block(
            qkv_ref.at[pl.ds(bi * L, L)], xp, cw_ref, None, qkvc_ref.at[pl.ds(bi * L, L)],
            L, GDN_CONV_CH, _gdn_qk_post)

    row_i = lax.broadcasted_iota(jnp.int32, (R, R), 0)
    col_i = lax.broadcasted_iota(jnp.int32, (R, R), 1)
    tril_f = (((row_i >> sh) == (col_i >> sh)) & (row_i >= col_i)).astype(F32)
    graw = gate_ref[...]
    sp = _softplus(graw + gbias_ref[...])
    beta_all = jax.nn.sigmoid(graw)
    G = jnp.dot(tril_f, -jnp.exp(galog_ref[...]) * sp, precision=_HIGHEST, preferred_element_type=F32)
    glast = [G[bi * L + L - 1:bi * L + L, :] for bi in range(nb)]
    eG = jnp.exp(G)
    eGrev = jnp.exp(jnp.concatenate([jnp.broadcast_to(x, (L, LANES)) for x in glast], axis=0) - G)
    egl = [jnp.exp(x) for x in glast]
    nw = nw_ref[...]

    n_st = R // (2 * L)

    def tiles(ref, st, off):
        return jnp.concatenate(
            [ref[st * 2 * L:(st + 1) * 2 * L, off + h * LANES:off + (h + 1) * LANES]
             for h in range(GDN_HEADS)], axis=0)

    def colstack(m, st, off):
        return jnp.concatenate(
            [jnp.broadcast_to(m[st * 2 * L:(st + 1) * 2 * L, off + h:off + h + 1], (2 * L, LANES))
             for h in range(GDN_HEADS)], axis=0)

    items = [dict(q=tiles(qkvc_ref, st, 0), k=tiles(qkvc_ref, st, GDN_QK), v=tiles(qkvc_ref, st, 2 * GDN_QK),
                  beta=colstack(beta_all, st, GATE_B), g=colstack(G, st, GATE_A),
                  eg=colstack(eG, st, GATE_A), egrev=colstack(eGrev, st, GATE_A))
             for st in range(n_st)]
    uw, qk = _gdn_local(items, L)

    groups = [(h, bi) for h in range(GDN_HEADS) for bi in range(2)]
    r = []
    for st, it in enumerate(items):
        w = uw[st][1]
        qd = it["q"] * it["eg"]
        r.append([
            _dot(jnp.concatenate([w[gi * L:(gi + 1) * L], qd[gi * L:(gi + 1) * L]], axis=0).astype(BF16),
                 sst_ref[2 * st + bi, h].astype(BF16))
            for gi, (h, bi) in enumerate(groups)])
    v_new = [uw[st][0] - jnp.concatenate([x[:L] for x in r[st]], axis=0) for st in range(n_st)]
    o = [jnp.concatenate([x[L:] for x in r[st]], axis=0)
         + _dot(qk[st].astype(BF16), v_new[st].astype(BF16)) for st in range(n_st)]
    for st, it in enumerate(items):
        kd = it["k"] * it["egrev"]
        for gi, (h, bi) in enumerate(groups):
            b = 2 * st + bi
            ga = GATE_A + h
            rs = slice(gi * L, (gi + 1) * L)
            sst_out_ref[b, h] = (sst_ref[b, h] * egl[b][:, ga:ga + 1]
                                 + _dot_tn(kd[rs].astype(BF16), v_new[st][rs].astype(BF16)))
    for st in range(n_st):
        out = (_rms_rows(o[st], nw) * _silu(tiles(zg_ref, st, 0))).astype(o_ref.dtype)
        for h in range(GDN_HEADS):
            o_ref[st * 2 * L:(st + 1) * 2 * L, h * GDN_DV:(h + 1) * GDN_DV] = out[h * 2 * L:(h + 1) * 2 * L]


def _gdn_sample_spec(proj, conv_state, S_state, cw, gbias, galog, nw, *, B, L, nb):
    R = nb * L
    const = lambda shape: pl.BlockSpec(shape, lambda i: (0,) * len(shape))
    return dict(
        in_specs=[
            pl.BlockSpec((R, GDN_CONV_CH), lambda i: (i, OFF_QKV // GDN_CONV_CH)),
            pl.BlockSpec((R, GDN_V), lambda i: (i, OFF_ZG // GDN_V)),
            pl.BlockSpec((R, LANES), lambda i: (i, OFF_GATE // LANES)),
            pl.BlockSpec((CONV_W - 1, nb, GDN_CONV_CH), lambda i: (0, i, 0)),
            pl.BlockSpec((nb, GDN_HEADS, GDN_DK, GDN_DV), lambda i: (i, 0, 0, 0)),
            const((CONV_W, GDN_CONV_CH)), const((1, LANES)), const((1, LANES)), const((1, GDN_DV)),
        ],
        out_specs=[
            pl.BlockSpec((R, GDN_V), lambda i: (i, 0)),
            pl.BlockSpec((CONV_W - 1, nb, GDN_CONV_CH), lambda i: (0, i, 0)),
            pl.BlockSpec((nb, GDN_HEADS, GDN_DK, GDN_DV), lambda i: (i, 0, 0, 0)),
        ],
        out_shape=[
            jax.ShapeDtypeStruct((B * L, GDN_V), BF16),
            jax.ShapeDtypeStruct((CONV_W - 1, B, GDN_CONV_CH), F32),
            jax.ShapeDtypeStruct((B, GDN_HEADS, GDN_DK, GDN_DV), F32),
        ],
        scratch=[
            pltpu.VMEM((nb, L + CONV_PAD, GDN_CONV_CH), F32),
            pltpu.VMEM((R, GDN_CONV_CH), F32),
        ],
        args=[proj, proj, proj, conv_state, S_state, cw, gbias, galog, nw])


N_PAIRS = SSM_HEADS // 2
PAIRS_PER_GROUP = N_PAIRS // SSM_GROUPS
GROUP_W = SSM_DI // SSM_GROUPS


def _ssm_tile(graw, gbias, nega, ld_x, ld_b, ld_c, ld_z, dcols_ref, nw_ref, st_o, get_h, set_h, glen):
    c = CHUNK
    P = SSM_P
    nseq = c // glen
    sh = glen.bit_length() - 1
    ri = lax.broadcasted_iota(jnp.int32, (c, c), 0)
    ci = lax.broadcasted_iota(jnp.int32, (c, c), 1)
    tril_f = (((ri >> sh) == (ci >> sh)) & (ri >= ci)).astype(F32)
    sp = _softplus(graw + gbias)
    acum = jnp.dot(tril_f, nega * sp, precision=_HIGHEST, preferred_element_type=F32)
    lasts = [acum[s * glen + glen - 1:(s + 1) * glen, :] for s in range(nseq)]
    alast = jnp.concatenate([jnp.broadcast_to(x, (glen, LANES)) for x in lasts], axis=0)
    dtrev = sp * jnp.exp(alast - acum)
    eal = [jnp.exp(x) for x in lasts]
    lane = lax.broadcasted_iota(jnp.int32, (c, LANES), 1)
    row = lax.broadcasted_iota(jnp.int32, (c, LANES), 0)
    m = jnp.where(lane < GATE_DT2, acum, sp)
    mt = jnp.concatenate([m, m], axis=0).T
    left = lane < P
    left_row = left[0:1]
    j = jnp.where(left, lane, lane - P)
    tril2 = ((row >> sh) == (j >> sh)) & (row >= j)
    rowh = lax.broadcasted_iota(jnp.int32, (2 * P, SSM_N), 0) < P

    def expand(mat, c0):
        return jnp.where(left, jnp.broadcast_to(mat[:, c0:c0 + 1], (c, LANES)),
                         jnp.broadcast_to(mat[:, c0 + 1:c0 + 2], (c, LANES)))

    def rowsel(base, e):
        return jnp.where(left_row, mt[base + 2 * e:base + 2 * e + 1, :], mt[base + 2 * e + 1:base + 2 * e + 2, :])

    for g in range(SSM_GROUPS):
        Bg = ld_b(g)
        Cg = ld_c(g)
        Bg16 = Bg.astype(BF16)
        Cg16 = Cg.astype(BF16)
        cb2 = _dot_nt(Cg16, jnp.concatenate([Bg16, Bg16], axis=0))
        pairs = [g * PAIRS_PER_GROUP + e4 for e4 in range(PAIRS_PER_GROUP)]
        acol = [expand(acum, GATE_DT + 2 * e) for e in pairs]
        scores16 = [
            (cb2 * jnp.exp(jnp.where(tril2, a - rowsel(GATE_DT, e), -jnp.inf)) * rowsel(GATE_DT2, e)).astype(BF16)
            for a, e in zip(acol, pairs)]
        xp = [ld_x(e) for e in pairs]
        bd16 = [jnp.concatenate([jnp.where(left, x, 0.0), jnp.where(left, 0.0, x)], axis=0).astype(BF16)
                for x in xp]
        ydiag = [_dot(s, b) for s, b in zip(scores16, bd16)]
        if nseq == 1:
            yoff = [_dot_nt(Cg16, get_h(0, e).astype(BF16)) for e in pairs]
        else:
            yoff = [jnp.concatenate(
                [_dot_nt(Cg[s * glen:(s + 1) * glen].astype(BF16), get_h(s, e).astype(BF16))
                 for s in range(nseq)], axis=0) for e in pairs]
        y = [yd + yo * jnp.exp(a) + dcols_ref[:, e * 2 * P:(e + 1) * 2 * P] * x
             for yd, yo, a, e, x in zip(ydiag, yoff, acol, pairs, xp)]
        xdr = [x * expand(dtrev, GATE_DT + 2 * e) for x, e in zip(xp, pairs)]
        for e, xd in zip(pairs, xdr):
            c0 = GATE_DT + 2 * e
            for s in range(nseq):
                rs = slice(s * glen, (s + 1) * glen)
                ealcol = jnp.where(rowh, eal[s][:, c0:c0 + 1], eal[s][:, c0 + 1:c0 + 2])
                set_h(s, e, get_h(s, e) * ealcol + _dot_tn(xd[rs].astype(BF16), Bg[rs].astype(BF16)))
        yg = jnp.concatenate(y, axis=1) * _silu(ld_z(g))
        gcols = slice(g * GROUP_W, (g + 1) * GROUP_W)
        st_o(g, _rms_rows(yg, nw_ref[:, gcols]))


def _ssm_prompt_parts(xs_ref, bc_ref, zs_ref, gate_ref, cwx_ref, cbx_ref, cwbc_ref, cbbc_ref,
                      gbias_ref, galog_ref, dcols_ref, nw_ref,
                      o_ref, cst_out_ref, hst_out_ref, xpadx_ref, xpadbc_ref, xc_ref, bcc_ref, hh_ref, *, Lb):
    c = CHUNK

    def init():
        hh_ref[...] = jnp.zeros(hh_ref.shape, F32)
        xpadx_ref[0:CONV_PAD, :] = jnp.zeros((CONV_PAD, SSM_DI), F32)
        xpadbc_ref[0:CONV_PAD, :] = jnp.zeros((CONV_PAD, 2 * SSM_BC), F32)

    def conv():
        ident = lambda s, y: y
        _conv_block(xs_ref, xpadx_ref, cwx_ref, cbx_ref, xc_ref, Lb, SSM_DI, ident)
        _conv_block(bc_ref, xpadbc_ref, cwbc_ref, cbbc_ref, bcc_ref, Lb, 2 * SSM_BC, ident)

    def set_h(s, e, val):
        hh_ref[e] = val

    def tile(ci):
        rows = slice(ci * c, (ci + 1) * c)

        def st_o(g, val):
            o_ref[rows, g * GROUP_W:(g + 1) * GROUP_W] = val.astype(o_ref.dtype)

        _ssm_tile(
            gate_ref[rows, :], gbias_ref[...], -jnp.exp(galog_ref[...]),
            lambda e: xc_ref[rows, e * LANES:(e + 1) * LANES],
            lambda g: bcc_ref[rows, g * SSM_N:(g + 1) * SSM_N],
            lambda g: bcc_ref[rows, SSM_BC + g * SSM_N:SSM_BC + (g + 1) * SSM_N],
            lambda g: zs_ref[rows, g * GROUP_W:(g + 1) * GROUP_W],
            dcols_ref, nw_ref, st_o, lambda s, e: hh_ref[e], set_h, c)

    steps = [conv] + [functools.partial(tile, ci) for ci in range(Lb // c)]

    def final():
        cst_out_ref[0, :, :SSM_DI] = xpadx_ref[CONV_HIST:CONV_PAD, :]
        cst_out_ref[0, :, SSM_DI:] = xpadbc_ref[CONV_HIST:CONV_PAD, :]
        hst_out_ref[0] = hh_ref[...]

    return init, steps, final


def _ssm_sample_kernel(xs_ref, bc_ref, zs_ref, gate_ref, cst_ref, hst_ref, cwx_ref, cbx_ref, cwbc_ref, cbbc_ref,
                       gbias_ref, galog_ref, dcols_ref, nw_ref,
                       o_ref, cst_out_ref, hst_out_ref, xpadx_ref, xpadbc_ref, xc_ref, bcc_ref, *, L):
    ident = lambda s, y: y
    for bi in range(CHUNK // L):
        rs = pl.ds(bi * L, L)
        xpx = xpadx_ref.at[bi]
        xpb = xpadbc_ref.at[bi]
        xpx[CONV_HIST:CONV_PAD, :] = cst_ref[:, bi, :SSM_DI]
        xpb[CONV_HIST:CONV_PAD, :] = cst_ref[:, bi, SSM_DI:]
        cst_out_ref[:, bi, :SSM_DI] = _conv_block(
            xs_ref.at[rs], xpx, cwx_ref, cbx_ref, xc_ref.at[rs], L, SSM_DI, ident)
        cst_out_ref[:, bi, SSM_DI:] = _conv_block(
            bc_ref.at[rs], xpb, cwbc_ref, cbbc_ref, bcc_ref.at[rs], L, 2 * SSM_BC, ident)

    def st_o(g, val):
        o_ref[:, g * GROUP_W:(g + 1) * GROUP_W] = val.astype(o_ref.dtype)

    def set_h(s, e, val):
        hst_out_ref[s, e] = val

    _ssm_tile(
        gate_ref[...], gbias_ref[...], -jnp.exp(galog_ref[...]),
        lambda e: xc_ref[:, e * LANES:(e + 1) * LANES],
        lambda g: bcc_ref[:, g * SSM_N:(g + 1) * SSM_N],
        lambda g: bcc_ref[:, SSM_BC + g * SSM_N:SSM_BC + (g + 1) * SSM_N],
        lambda g: zs_ref[:, g * GROUP_W:(g + 1) * GROUP_W],
        dcols_ref, nw_ref, st_o, lambda s, e: hst_ref[s, e], set_h, L)


def _ssm_const_specs():
    const = lambda shape: pl.BlockSpec(shape, lambda *idx: (0,) * len(shape))
    return [
        const((CONV_W, SSM_DI)), const((1, SSM_DI)), const((CONV_W, 2 * SSM_BC)), const((1, 2 * SSM_BC)),
        const((1, LANES)), const((1, LANES)), const((1, SSM_DI)), const((1, SSM_DI)),
    ]


def _ssm_prompt_spec(proj, cwx, cbx, cwbc, cbbc, gbias, galog, dcols, nw, *, B, L, Lb):
    nl = L // Lb
    row = lambda b, l: b * nl + l
    return dict(
        in_specs=[
            pl.BlockSpec((Lb, SSM_DI), lambda b, l: (row(b, l), OFF_XS // SSM_DI)),
            pl.BlockSpec((Lb, 2 * SSM_BC), lambda b, l: (row(b, l), OFF_BC // (2 * SSM_BC))),
            pl.BlockSpec((Lb, SSM_DI), lambda b, l: (row(b, l), OFF_ZS // SSM_DI)),
            pl.BlockSpec((Lb, LANES), lambda b, l: (row(b, l), OFF_GATE // LANES)),
        ] + _ssm_const_specs(),
        out_specs=[
            pl.BlockSpec((Lb, SSM_DI), lambda b, l: (row(b, l), 0)),
            pl.BlockSpec((1, CONV_W - 1, SSM_CONV_CH), lambda b, l: (b, 0, 0)),
            pl.BlockSpec((1, N_PAIRS, 2 * SSM_P, SSM_N), lambda b, l: (b, 0, 0, 0)),
        ],
        out_shape=[
            jax.ShapeDtypeStruct((B * L, SSM_DI), BF16),
            jax.ShapeDtypeStruct((B, CONV_W - 1, SSM_CONV_CH), F32),
            jax.ShapeDtypeStruct((B, N_PAIRS, 2 * SSM_P, SSM_N), F32),
        ],
        scratch=[
            pltpu.VMEM((Lb + CONV_PAD, SSM_DI), F32),
            pltpu.VMEM((Lb + CONV_PAD, 2 * SSM_BC), F32),
            pltpu.VMEM((Lb, SSM_DI), F32),
            pltpu.VMEM((Lb, 2 * SSM_BC), F32),
            pltpu.VMEM((N_PAIRS, 2 * SSM_P, SSM_N), F32),
        ],
        args=[proj, proj, proj, proj, cwx, cbx, cwbc, cbbc, gbias, galog, dcols, nw])


def _ssm_sample_spec(proj, conv_state, h_pairs, cwx, cbx, cwbc, cbbc, gbias, galog, dcols, nw, *, B, L):
    nb = CHUNK // L
    return dict(
        in_specs=[
            pl.BlockSpec((CHUNK, SSM_DI), lambda i: (i, OFF_XS // SSM_DI)),
            pl.BlockSpec((CHUNK, 2 * SSM_BC), lambda i: (i, OFF_BC // (2 * SSM_BC))),
            pl.BlockSpec((CHUNK, SSM_DI), lambda i: (i, OFF_ZS // SSM_DI)),
            pl.BlockSpec((CHUNK, LANES), lambda i: (i, OFF_GATE // LANES)),
            pl.BlockSpec((CONV_W - 1, nb, SSM_CONV_CH), lambda i: (0, i, 0)),
            pl.BlockSpec((nb, N_PAIRS, 2 * SSM_P, SSM_N), lambda i: (i, 0, 0, 0)),
        ] + _ssm_const_specs(),
        out_specs=[
            pl.BlockSpec((CHUNK, SSM_DI), lambda i: (i, 0)),
            pl.BlockSpec((CONV_W - 1, nb, SSM_CONV_CH), lambda i: (0, i, 0)),
            pl.BlockSpec((nb, N_PAIRS, 2 * SSM_P, SSM_N), lambda i: (i, 0, 0, 0)),
        ],
        out_shape=[
            jax.ShapeDtypeStruct((B * L, SSM_DI), BF16),
            jax.ShapeDtypeStruct((CONV_W - 1, B, SSM_CONV_CH), F32),
            jax.ShapeDtypeStruct((B, N_PAIRS, 2 * SSM_P, SSM_N), F32),
        ],
        scratch=[
            pltpu.VMEM((nb, L + CONV_PAD, SSM_DI), F32),
            pltpu.VMEM((nb, L + CONV_PAD, 2 * SSM_BC), F32),
            pltpu.VMEM((CHUNK, SSM_DI), F32),
            pltpu.VMEM((CHUNK, 2 * SSM_BC), F32),
        ],
        args=[proj, proj, proj, proj, conv_state, h_pairs, cwx, cbx, cwbc, cbbc, gbias, galog, dcols, nw])


def _split_refs(refs, g, s):
    it = iter(refs)
    take = lambda n: [next(it) for _ in range(n)]
    g_in, s_in = take(len(g["in_specs"])), take(len(s["in_specs"]))
    g_out, s_out = take(len(g["out_specs"])), take(len(s["out_specs"]))
    g_scr, s_scr = take(len(g["scratch"])), take(len(s["scratch"]))
    return g_in + g_out + g_scr, s_in + s_out + s_scr


def _mix_prompt_kernel(*refs, g, s, Lb):
    g_refs, s_refs = _split_refs(refs, g, s)
    g_init, g_steps, g_final = _gdn_prompt_parts(*g_refs, Lb=Lb)
    s_init, s_steps, s_final = _ssm_prompt_parts(*s_refs, Lb=Lb)
    l = pl.program_id(1)

    @pl.when(l == 0)
    def _init():
        g_init()
        s_init()

    for g_step, s_step in itertools.zip_longest(g_steps, s_steps):
        if g_step is not None:
            g_step()
        if s_step is not None:
            s_step()

    @pl.when(l == pl.num_programs(1) - 1)
    def _final():
        g_final()
        s_final()


def _mix_sample_kernel(*refs, g, s, nb, L):
    g_refs, s_refs = _split_refs(refs, g, s)
    _gdn_sample_kernel(*g_refs, nb=nb, L=L)
    _ssm_sample_kernel(*s_refs, L=L)


def _fused_call(kernel, g, s, grid, semantics, name):
    outs = pl.pallas_call(
        functools.partial(kernel, g={k: g[k] for k in ("in_specs", "out_specs", "scratch")},
                          s={k: s[k] for k in ("in_specs", "out_specs", "scratch")}),
        grid=grid,
        in_specs=g["in_specs"] + s["in_specs"],
        out_specs=g["out_specs"] + s["out_specs"],
        out_shape=g["out_shape"] + s["out_shape"],
        scratch_shapes=g["scratch"] + s["scratch"],
        compiler_params=pltpu.CompilerParams(dimension_semantics=semantics, vmem_limit_bytes=VMEM_LIMIT),
        name=name,
    )(*g["args"], *s["args"])
    n = len(g["out_specs"])
    return outs[:n], outs[n:]


def _outproj_kernel(x_ref, mg_ref, ms_ref, w_ref, o_ref):
    acc = _dot(mg_ref[...].astype(BF16), w_ref[:GDN_V, :])
    acc = acc + _dot(ms_ref[...].astype(BF16), w_ref[GDN_V:, :])
    o_ref[...] = x_ref[...] + acc


def _out_proj(x2d, mix_g, mix_s, w_out16, *, tm):
    T = x2d.shape[0]
    return pl.pallas_call(
        _outproj_kernel,
        grid=(T // tm,),
        in_specs=[
            pl.BlockSpec((tm, D_MODEL), lambda i: (i, 0)),
            pl.BlockSpec((tm, GDN_V), lambda i: (i, 0)),
            pl.BlockSpec((tm, SSM_DI), lambda i: (i, 0)),
            pl.BlockSpec((D_MODEL, D_MODEL), lambda i: (0, 0)),
        ],
        out_specs=pl.BlockSpec((tm, D_MODEL), lambda i: (i, 0)),
        out_shape=jax.ShapeDtypeStruct((T, D_MODEL), F32),
        compiler_params=pltpu.CompilerParams(
            dimension_semantics=("parallel",), vmem_limit_bytes=VMEM_LIMIT),
        name="out_proj",
    )(x2d, mix_g, mix_s, w_out16)


FFN_SUB = 4


def _ffn_kernel(x_ref, nw_ref, wg_ref, wu_ref, wd_ref, fnw_ref, o_ref, h_ref):
    f = pl.program_id(1)
    nf = pl.num_programs(1)
    rs = h_ref.shape[0] // FFN_SUB

    def step(first, last):
        def gate_up(r):
            rows = slice(r * rs, (r + 1) * rs)
            if first:
                h = _rms_rows(x_ref[rows, :], nw_ref[...]).astype(BF16)
                h_ref[rows, :] = h
            else:
                h = h_ref[rows, :]
            return _dot(h, wg_ref[...]), _dot(h, wu_ref[...])

        def down(r, gu):
            rows = slice(r * rs, (r + 1) * rs)
            d = _dot((_silu(gu[0]) * gu[1]).astype(BF16), wd_ref[...])
            acc = d if first else o_ref[rows, :] + d
            if last:
                o_ref[rows, :] = _rms_rows(x_ref[rows, :] + acc, fnw_ref[...])
            else:
                o_ref[rows, :] = acc

        gu = gate_up(0)
        for r in range(1, FFN_SUB):
            gu_next = gate_up(r)
            down(r - 1, gu)
            gu = gu_next
        down(FFN_SUB - 1, gu)

    pl.when(f == 0)(lambda: step(True, False))
    pl.when((f > 0) & (f < nf - 1))(lambda: step(False, False))
    pl.when(f == nf - 1)(lambda: step(False, True))


def _ffn(x2d, norm_w, wg16, wu16, wd16, final_w, *, tm, tf):
    T = x2d.shape[0]
    return pl.pallas_call(
        _ffn_kernel,
        grid=(T // tm, D_FF // tf),
        in_specs=[
            pl.BlockSpec((tm, D_MODEL), lambda i, f: (i, 0)),
            pl.BlockSpec((1, D_MODEL), lambda i, f: (0, 0)),
            pl.BlockSpec((D_MODEL, tf), lambda i, f: (0, f)),
            pl.BlockSpec((D_MODEL, tf), lambda i, f: (0, f)),
            pl.BlockSpec((tf, D_MODEL), lambda i, f: (f, 0)),
            pl.BlockSpec((1, D_MODEL), lambda i, f: (0, 0)),
        ],
        out_specs=pl.BlockSpec((tm, D_MODEL), lambda i, f: (i, 0)),
        out_shape=jax.ShapeDtypeStruct((T, D_MODEL), F32),
        scratch_shapes=[pltpu.VMEM((tm, D_MODEL), BF16)],
        compiler_params=pltpu.CompilerParams(
            dimension_semantics=("parallel", "arbitrary"), vmem_limit_bytes=VMEM_LIMIT),
        name="ffn",
    )(x2d, norm_w, wg16, wu16, wd16, final_w)


PROMPT_ROWS = 256


def _trunk(x, states, p):
    B, L, _ = x.shape
    x2d = x.reshape(B * L, D_MODEL)
    proj = _in_proj(x2d, p["attn_norm_w"], p["w_in_r"], tm=IN_PROJ_TM, tn=IN_PROJ_TN)
    gdn_w = (p["gdn_conv_w"], p["gbias"], p["galog"], p["gdn_norm_w"])
    ssm_w = (p["cwx"], p["cbx"], p["cwbc"], p["cbbc"], p["gbias"], p["galog"], p["dcols"], p["ssm_norm_w"])
    pair_shape = (B, N_PAIRS, 2 * SSM_P, SSM_N)
    if states is None:
        Lb = PROMPT_ROWS
        g = _gdn_prompt_spec(proj, *gdn_w, B=B, L=L, Lb=Lb)
        s = _ssm_prompt_spec(proj, *ssm_w, B=B, L=L, Lb=Lb)
        kern = functools.partial(_mix_prompt_kernel, Lb=Lb)
        (mix_g, gconv_new, gS_new), (mix_s, sconv_new, sh_new) = _fused_call(
            kern, g, s, (B, L // Lb), ("parallel", "arbitrary"), "mix_prompt")
    else:
        gconv, gS, sconv, sh = states
        nb = CHUNK // L
        tap_major = lambda a: jnp.swapaxes(a, 0, 1)
        g = _gdn_sample_spec(proj, tap_major(gconv), gS, *gdn_w, B=B, L=L, nb=nb)
        s = _ssm_sample_spec(proj, tap_major(sconv), sh.reshape(pair_shape), *ssm_w, B=B, L=L)
        kern = functools.partial(_mix_sample_kernel, nb=nb, L=L)
        (mix_g, gconv_new, gS_new), (mix_s, sconv_new, sh_new) = _fused_call(
            kern, g, s, (B // nb,), ("parallel",), "mix_sample")
        gconv_new, sconv_new = tap_major(gconv_new), tap_major(sconv_new)
    sh_new = sh_new.reshape(B, SSM_HEADS, SSM_P, SSM_N)
    x1 = _out_proj(x2d, mix_g, mix_s, p["w_out16"], tm=OUT_PROJ_TM)
    y = _ffn(x1, p["ffn_norm_w"], p["wg16"], p["wu16"], p["wd16"], p["final_norm_w"], tm=FFN_TM, tf=FFN_TF)
    return y.reshape(B, L, D_MODEL), (gconv_new[None], gS_new[None], sconv_new[None], sh_new[None])


def kernel(x_prompt, x_sample, state_gdn_conv, state_gdn, state_ssm_conv, state_ssm,
           attn_norm_w, w_in, gdn_conv_w, gdn_A_log, gdn_dt_bias, gdn_norm_w,
           ssm_conv_w, ssm_conv_b, ssm_A_log, ssm_dt_bias, ssm_D, ssm_norm_w,
           w_out, ffn_norm_w, w_gate, w_up, w_down, final_norm_w):
    assert w_in.shape[0] == 1, "single-layer trunk"
    assert x_prompt.shape[1] % PROMPT_ROWS == 0 and CHUNK % x_sample.shape[1] == 0 and x_sample.shape[0] % (CHUNK // x_sample.shape[1]) == 0
    assert w_in.shape[2] == D_IN_PROJ
    w_in_r = _w_in_prep(jnp.swapaxes(w_in, 1, 2), tk=W_PREP_TK)
    zeros8 = jnp.zeros((GDN_HEADS,), F32)
    tail = jnp.zeros((LANES - GATE_DT2 - SSM_HEADS,), F32)
    gbias = jnp.concatenate([zeros8, gdn_dt_bias[0], ssm_dt_bias[0], ssm_dt_bias[0], tail])[None]
    galog = jnp.concatenate([zeros8, gdn_A_log[0], ssm_A_log[0], ssm_A_log[0], tail])[None]
    p = dict(
        attn_norm_w=attn_norm_w, w_in_r=w_in_r, gdn_conv_w=gdn_conv_w[0], gbias=gbias, galog=galog,
        gdn_norm_w=gdn_norm_w,
        cwx=ssm_conv_w[0][:, :SSM_DI], cbx=ssm_conv_b[:, :SSM_DI],
        cwbc=ssm_conv_w[0][:, SSM_DI:], cbbc=ssm_conv_b[:, SSM_DI:],
        dcols=jnp.repeat(ssm_D[0], SSM_P)[None], ssm_norm_w=ssm_norm_w,
        w_out16=w_out[0].astype(BF16), ffn_norm_w=ffn_norm_w,
        wg16=w_gate[0].astype(BF16), wu16=w_up[0].astype(BF16), wd16=w_down[0].astype(BF16),
        final_norm_w=final_norm_w[None],
    )
    y_p, st_p = _trunk(x_prompt, None, p)
    y_s, st_s = _trunk(x_sample, (state_gdn_conv[0], state_gdn[0], state_ssm_conv[0], state_ssm[0]), p)
    return (y_p, y_s, st_p[0], st_p[1], st_p[2], st_p[3], st_s[0], st_s[1], st_s[2], st_s[3])
```

```python
import functools
import itertools

import jax
import jax.numpy as jnp
from jax import lax
from jax.experimental import pallas as pl
from jax.experimental.pallas import tpu as pltpu

F32 = jnp.float32
BF16 = jnp.bfloat16

D_MODEL = 2048
GDN_HEADS = 8
GDN_DK = 128
GDN_DV = 128
GDN_QK = GDN_HEADS * GDN_DK
GDN_V = GDN_HEADS * GDN_DV
GDN_CONV_CH = 2 * GDN_QK + GDN_V
SSM_P = 64
SSM_N = 128
SSM_GROUPS = 2
SSM_DI = 1024
SSM_HEADS = SSM_DI // SSM_P
SSM_BC = SSM_GROUPS * SSM_N
SSM_CONV_CH = SSM_DI + 2 * SSM_BC
CONV_W = 4
CHUNK = 64
D_FF = 5632
EPS = 1e-6

OFF_QKV = 0
OFF_ZG = OFF_QKV + GDN_CONV_CH
OFF_ZS = OFF_ZG + GDN_V
OFF_XS = OFF_ZS + SSM_DI
OFF_BC = OFF_XS + SSM_DI
OFF_GATE = OFF_BC + 2 * SSM_BC
LANES = 128
SUBLANES = 8
CONV_PAD = SUBLANES
CONV_HIST = CONV_PAD - (CONV_W - 1)
GATE_B = 0
GATE_A = GATE_B + GDN_HEADS
GATE_DT = GATE_A + GDN_HEADS
GATE_DT2 = GATE_DT + SSM_HEADS
N_PROJ = 6912

VMEM_LIMIT = 52 * 1024 * 1024
W_PREP_TK = 256
IN_PROJ_TM, IN_PROJ_TN = 1024, 768
CAST_WD_ROWS, CAST_WO_ROWS = 176, 128
OUT_PROJ_TM = 512
FFN_TM, FFN_TF = 1024, 512

_HIGHEST = lax.Precision.HIGHEST


def _silu(x):
    h = 0.5 * x
    return h + h * jnp.tanh(h)


def _softplus(x):
    return jnp.maximum(x, 0.0) + jnp.log1p(jnp.exp(-jnp.abs(x)))


def _dot(a, b):
    return jnp.dot(a, b, preferred_element_type=F32)


def _dot_nt(a, b):
    return lax.dot_general(a, b, (((1,), (1,)), ((), ())), preferred_element_type=F32)


def _dot_tn(a, b):
    return lax.dot_general(a, b, (((0,), (0,)), ((), ())), preferred_element_type=F32)


def _rms_rows(x, w):
    return x * lax.rsqrt(jnp.mean(x * x, axis=-1, keepdims=True) + EPS) * w


W_B = OFF_ZS
W_ZS = W_B + 2 * GDN_HEADS
W_DT = W_ZS + SSM_DI + SSM_CONV_CH
D_IN_PROJ = W_DT + SSM_HEADS


def _wprep_kernel(w_ref, o_ref):
    cols = o_ref.shape[1]
    o_ref[:W_B, :] = w_ref[0, :W_B, :].astype(BF16)
    o_ref[W_B:OFF_GATE, :] = w_ref[0, W_ZS:W_DT, :].astype(BF16)
    dt = w_ref[0, W_DT:D_IN_PROJ, :].astype(BF16)
    o_ref[OFF_GATE:OFF_GATE + GATE_DT, :] = w_ref[0, W_B:W_ZS, :].astype(BF16)
    o_ref[OFF_GATE + GATE_DT:OFF_GATE + GATE_DT2, :] = dt
    o_ref[OFF_GATE + GATE_DT2:OFF_GATE + GATE_DT2 + SSM_HEADS, :] = dt
    o_ref[OFF_GATE + GATE_DT2 + SSM_HEADS:, :] = jnp.zeros((N_PROJ - OFF_GATE - GATE_DT2 - SSM_HEADS, cols), BF16)


def _w_in_prep(w_in_t, *, tk):
    return pl.pallas_call(
        _wprep_kernel,
        grid=(D_MODEL // tk,),
        in_specs=[pl.BlockSpec((1, D_IN_PROJ, tk), lambda i: (0, 0, i))],
        out_specs=pl.BlockSpec((N_PROJ, tk), lambda i: (0, i)),
        out_shape=jax.ShapeDtypeStruct((N_PROJ, D_MODEL), BF16),
        compiler_params=pltpu.CompilerParams(
            dimension_semantics=("parallel",), vmem_limit_bytes=VMEM_LIMIT),
        name="w_in_prep",
    )(w_in_t)


INPROJ_SUB = 4


def _inproj_kernel(x_ref, nw_ref, wt_ref, *rest, n_cast):
    j = pl.program_id(1)
    if n_cast is None:
        o_ref, h_ref = rest
    else:
        wd_ref, wo_ref, o_ref, wd16_ref, wo16_ref, h_ref = rest
        n_wd, n_wo = n_cast
        s = pl.program_id(0) * pl.num_programs(1) + j

        @pl.when(s < n_wd)
        def _cast_down():
            wd16_ref[...] = wd_ref[...].astype(BF16)

        @pl.when((s >= n_wd) & (s < n_wd + n_wo))
        def _cast_out():
            wo16_ref[...] = wo_ref[...].astype(BF16)

    @pl.when(j == 0)
    def _first():
        rs = h_ref.shape[0] // INPROJ_SUB
        for r in range(INPROJ_SUB):
            rows = slice(r * rs, (r + 1) * rs)
            h = _rms_rows(x_ref[rows, :], nw_ref[...]).astype(BF16)
            h_ref[rows, :] = h
            o_ref[rows, :] = _dot_nt(h, wt_ref[...])

    @pl.when(j > 0)
    def _rest():
        o_ref[...] = _dot_nt(h_ref[...], wt_ref[...])


def _in_proj(x2d, norm_w, w_in_r, cast=None, *, tm, tn):
    T = x2d.shape[0]
    nj = N_PROJ // tn
    in_specs = [
        pl.BlockSpec((tm, D_MODEL), lambda i, j: (i, 0)),
        pl.BlockSpec((1, D_MODEL), lambda i, j: (0, 0)),
        pl.BlockSpec((tn, D_MODEL), lambda i, j: (j, 0)),
    ]
    out_specs = [pl.BlockSpec((tm, tn), lambda i, j: (i, j))]
    out_shape = [jax.ShapeDtypeStruct((T, N_PROJ), F32)]
    args = [x2d, norm_w, w_in_r]
    n_cast = None
    if cast is not None:
        w_down, w_out = cast
        n_wd, n_wo = w_down.shape[0] // CAST_WD_ROWS, w_out.shape[0] // CAST_WO_ROWS
        assert n_wd + n_wo <= (T // tm) * nj
        n_cast = (n_wd, n_wo)
        wd_idx = lambda i, j: (jnp.minimum(i * nj + j, n_wd - 1), 0)
        wo_idx = lambda i, j: (jnp.clip(i * nj + j - n_wd, 0, n_wo - 1), 0)
        in_specs += [pl.BlockSpec((CAST_WD_ROWS, w_down.shape[1]), wd_idx),
                     pl.BlockSpec((CAST_WO_ROWS, w_out.shape[1]), wo_idx)]
        out_specs += [pl.BlockSpec((CAST_WD_ROWS, w_down.shape[1]), wd_idx),
                      pl.BlockSpec((CAST_WO_ROWS, w_out.shape[1]), wo_idx)]
        out_shape += [jax.ShapeDtypeStruct(w_down.shape, BF16), jax.ShapeDtypeStruct(w_out.shape, BF16)]
        args += [w_down, w_out]
    outs = pl.pallas_call(
        functools.partial(_inproj_kernel, n_cast=n_cast),
        grid=(T // tm, nj),
        in_specs=in_specs,
        out_specs=out_specs,
        out_shape=out_shape,
        scratch_shapes=[pltpu.VMEM((tm, D_MODEL), BF16)],
        compiler_params=pltpu.CompilerParams(
            dimension_semantics=("arbitrary", "arbitrary"), vmem_limit_bytes=VMEM_LIMIT),
        name="in_proj",
    )(*args)
    return outs[0] if cast is None else outs


def _conv_block(x_ref, xpad_ref, cw_ref, cb_ref, dst_ref, Lb, C, post):
    xpad_ref[CONV_PAD:CONV_PAD + Lb, :] = x_ref[...]
    rs = min(Lb, CHUNK)
    for sb in range(Lb // rs):
        r = sb * rs
        for s in range(C // LANES):
            cols = slice(s * LANES, (s + 1) * LANES)
            acc = xpad_ref[CONV_PAD + r:CONV_PAD + r + rs, cols] * cw_ref[CONV_W - 1:CONV_W, cols]
            for i in range(CONV_W - 1):
                acc = acc + xpad_ref[CONV_HIST + i + r:CONV_HIST + i + r + rs, cols] * cw_ref[i:i + 1, cols]
            if cb_ref is not None:
                acc = acc + cb_ref[:, cols]
            dst_ref[r:r + rs, cols] = post(s, _silu(acc))
    hist = xpad_ref[Lb + CONV_HIST:Lb + CONV_PAD, :]
    xpad_ref[CONV_HIST:CONV_PAD, :] = hist
    return hist


STACK = 128


def _gdn_qk_post(s, y):
    if s < 2 * GDN_HEADS:
        y = y * lax.rsqrt(jnp.sum(y * y, axis=-1, keepdims=True) + EPS)
        if s < GDN_HEADS:
            y = y * (GDN_DK ** -0.5)
    return y


def _gdn_local(items, glen):
    sh = glen.bit_length() - 1
    row = lax.broadcasted_iota(jnp.int32, (STACK, STACK), 0)
    col = lax.broadcasted_iota(jnp.int32, (STACK, STACK), 1)
    same = (row >> sh) == (col >> sh)
    incl = same & (row >= col)
    strict = same & (row > col)
    eye = (row == col).astype(F32)

    decay = [jnp.exp(jnp.where(incl, it["g"] - it["g"].T, -jnp.inf)) for it in items]
    kb = [it["k"] * it["beta"] for it in items]
    qkk = [_dot_nt(jnp.concatenate([it["q"], b], axis=0).astype(BF16), it["k"].astype(BF16))
           for it, b in zip(items, kb)]
    qk = [x[:STACK] * d for x, d in zip(qkk, decay)]
    nmat = [jnp.where(strict, -(x[STACK:] * d), 0.0) for x, d in zip(qkk, decay)]
    tinv = [eye + n for n in nmat]
    if sh >= 2:
        pw = [_dot(n.astype(BF16), n.astype(BF16)) for n in nmat]
        for _ in range(sh - 2):
            x = [_dot(jnp.concatenate([t, p], axis=0).astype(BF16), p.astype(BF16))
                 for t, p in zip(tinv, pw)]
            tinv = [t + y[:STACK] for t, y in zip(tinv, x)]
            pw = [y[STACK:] for y in x]
        tinv = [t + _dot(t.astype(BF16), p.astype(BF16)) for t, p in zip(tinv, pw)]
    uw = [_dot(t.astype(BF16),
               jnp.concatenate([it["v"] * it["beta"], b * it["eg"]], axis=1).astype(BF16))
          for t, it, b in zip(tinv, items, kb)]
    return [(x[:, :GDN_DV], x[:, GDN_DV:]) for x in uw], qk


GDN_LOCAL_CHUNKS = 2


def _gdn_prompt_parts(qkv_ref, zg_ref, gate_ref, cw_ref, gbias_ref, galog_ref, nw_ref,
                      o_ref, cst_out_ref, sst_out_ref,
                      xpad_ref, qkvc_ref, s_ref, u_ref, wq16_ref, kd16_ref, qk16_ref, egl_ref, *, Lb):
    c = CHUNK

    def init():
        xpad_ref[0:CONV_PAD, :] = jnp.zeros((CONV_PAD, GDN_CONV_CH), F32)
        s_ref[...] = jnp.zeros(s_ref.shape, F32)

    def conv():
        _conv_block(qkv_ref, xpad_ref, cw_ref, None, qkvc_ref, Lb, GDN_CONV_CH, _gdn_qk_post)

    row_i = lax.broadcasted_iota(jnp.int32, (c, c), 0)
    col_i = lax.broadcasted_iota(jnp.int32, (c, c), 1)
    tril_f = (row_i >= col_i).astype(F32)
    gbias = gbias_ref[...]
    nega = -jnp.exp(galog_ref[...])
    nw = nw_ref[...]
    n_pairs = GDN_HEADS // 2

    def local_chunk_items(ci):
        rows = slice(ci * c, (ci + 1) * c)
        graw = gate_ref[rows, :]
        sp = _softplus(graw + gbias)
        beta_all = jax.nn.sigmoid(graw)
        G = jnp.dot(tril_f, nega * sp, precision=_HIGHEST, preferred_element_type=F32)
        eG = jnp.exp(G)
        glast = G[c - 1:c, :]
        eGrev = jnp.exp(glast - G)
        egl_ref[ci] = jnp.broadcast_to(jnp.exp(glast), (SUBLANES, LANES))

        def heads(off, a, b):
            return jnp.concatenate([qkvc_ref[rows, off + a * LANES:off + (a + 1) * LANES],
                                    qkvc_ref[rows, off + b * LANES:off + (b + 1) * LANES]], axis=0)

        def colstack(m, a, b):
            return jnp.concatenate([jnp.broadcast_to(m[:, a:a + 1], (c, LANES)),
                                    jnp.broadcast_to(m[:, b:b + 1], (c, LANES))], axis=0)

        items = []
        for pr in range(n_pairs):
            a, b = 2 * pr, 2 * pr + 1
            items.append(dict(
                q=heads(0, a, b), k=heads(GDN_QK, a, b), v=heads(2 * GDN_QK, a, b),
                beta=colstack(beta_all, GATE_B + a, GATE_B + b),
                g=colstack(G, GATE_A + a, GATE_A + b),
                eg=colstack(eG, GATE_A + a, GATE_A + b),
                egrev=colstack(eGrev, GATE_A + a, GATE_A + b)))
        return items

    def local_group(cis):
        items = [it for ci in cis for it in local_chunk_items(ci)]
        uw, qk = _gdn_local(items, c)
        for n, it in enumerate(items):
            idx = cis[0] * n_pairs + n
            u, w = uw[n]
            qd = it["q"] * it["eg"]
            u_ref[idx] = u
            for hh in range(2):
                hs = slice(hh * c, (hh + 1) * c)
                wq16_ref[2 * idx + hh] = jnp.concatenate([w[hs], qd[hs]], axis=0).astype(BF16)
            kd16_ref[idx] = (it["k"] * it["egrev"]).astype(BF16)
            qk16_ref[idx] = qk[n].astype(BF16)

    def recurrent(ci):
        rows = slice(ci * c, (ci + 1) * c)
        egl = egl_ref[ci][0:1]
        r = [[_dot(wq16_ref[2 * (ci * n_pairs + pr) + hh], s_ref[2 * pr + hh].astype(BF16)) for hh in range(2)]
             for pr in range(n_pairs)]
        v16 = [(u_ref[ci * n_pairs + pr] - jnp.concatenate([r[pr][0][:c], r[pr][1][:c]], axis=0)).astype(BF16)
               for pr in range(n_pairs)]
        o = [jnp.concatenate([r[pr][0][c:], r[pr][1][c:]], axis=0) + _dot(qk16_ref[ci * n_pairs + pr], v16[pr])
             for pr in range(n_pairs)]
        for pr in range(n_pairs):
            kd16 = kd16_ref[ci * n_pairs + pr]
            for hh in range(2):
                h = 2 * pr + hh
                ga = GATE_A + h
                hs = slice(hh * c, (hh + 1) * c)
                s_ref[h] = s_ref[h] * egl[:, ga:ga + 1] + _dot_tn(kd16[hs], v16[pr][hs])
        for pr in range(n_pairs):
            for hh in range(2):
                h = 2 * pr + hh
                z = zg_ref[rows, h * GDN_DV:(h + 1) * GDN_DV]
                o_ref[rows, h * GDN_DV:(h + 1) * GDN_DV] = (
                    _rms_rows(o[pr][hh * c:(hh + 1) * c], nw) * _silu(z)).astype(o_ref.dtype)

    steps = [conv]
    for g0 in range(0, Lb // c, GDN_LOCAL_CHUNKS):
        cis = list(range(g0, g0 + GDN_LOCAL_CHUNKS))
        steps.append(functools.partial(local_group, cis))
        steps += [functools.partial(recurrent, ci) for ci in cis]

    def final():
        cst_out_ref[0] = xpad_ref[CONV_HIST:CONV_PAD, :]
        sst_out_ref[0] = s_ref[...]

    return init, steps, final


def _gdn_prompt_spec(proj, cw, gbias, galog, nw, *, B, L, Lb):
    nl = L // Lb
    n_tiles = (Lb // CHUNK) * (GDN_HEADS // 2)
    row = lambda b, l: b * nl + l
    const = lambda shape: pl.BlockSpec(shape, lambda b, l: (0,) * len(shape))
    return dict(
        in_specs=[
            pl.BlockSpec((Lb, GDN_CONV_CH), lambda b, l: (row(b, l), OFF_QKV // GDN_CONV_CH)),
            pl.BlockSpec((Lb, GDN_V), lambda b, l: (row(b, l), OFF_ZG // GDN_V)),
            pl.BlockSpec((Lb, LANES), lambda b, l: (row(b, l), OFF_GATE // LANES)),
            const((CONV_W, GDN_CONV_CH)), const((1, LANES)), const((1, LANES)), const((1, GDN_DV)),
        ],
        out_specs=[
            pl.BlockSpec((Lb, GDN_V), lambda b, l: (row(b, l), 0)),
            pl.BlockSpec((1, CONV_W - 1, GDN_CONV_CH), lambda b, l: (b, 0, 0)),
            pl.BlockSpec((1, GDN_HEADS, GDN_DK, GDN_DV), lambda b, l: (b, 0, 0, 0)),
        ],
        out_shape=[
            jax.ShapeDtypeStruct((B * L, GDN_V), BF16),
            jax.ShapeDtypeStruct((B, CONV_W - 1, GDN_CONV_CH), F32),
            jax.ShapeDtypeStruct((B, GDN_HEADS, GDN_DK, GDN_DV), F32),
        ],
        scratch=[
            pltpu.VMEM((Lb + CONV_PAD, GDN_CONV_CH), F32),
            pltpu.VMEM((Lb, GDN_CONV_CH), F32),
            pltpu.VMEM((GDN_HEADS, GDN_DK, GDN_DV), F32),
            pltpu.VMEM((n_tiles, STACK, GDN_DV), F32),
            pltpu.VMEM((2 * n_tiles, STACK, GDN_DK), BF16),
            pltpu.VMEM((n_tiles, STACK, GDN_DK), BF16),
            pltpu.VMEM((n_tiles, STACK, STACK), BF16),
            pltpu.VMEM((Lb // CHUNK, SUBLANES, LANES), F32),
        ],
        args=[proj, proj, proj, cw, gbias, galog, nw])


def _gdn_sample_kernel(qkv_ref, zg_ref, gate_ref, cst_ref, sst_ref, cw_ref, gbias_ref, galog_ref, nw_ref,
                       o_ref, cst_out_ref, sst_out_ref, xpad_ref, qkvc_ref, *, nb, L):
    R = nb * L
    sh = L.bit_length() - 1
    for bi in range(nb):
        xp = xpad_ref.at[bi]
        xp[CONV_HIST:CONV_PAD, :] = cst_ref[:, bi, :]
        cst_out_ref[:, bi, :] = _conv_block(
            qkv_ref.at[pl.ds(bi * L, L)], xp, cw_ref, None, qkvc_ref.at[pl.ds(bi * L, L)],
            L, GDN_CONV_CH, _gdn_qk_post)

    row_i = lax.broadcasted_iota(jnp.int32, (R, R), 0)
    col_i = lax.broadcasted_iota(jnp.int32, (R, R), 1)
    tril_f = (((row_i >> sh) == (col_i >> sh)) & (row_i >= col_i)).astype(F32)
    graw = gate_ref[...]
    sp = _softplus(graw + gbias_ref[...])
    beta_all = jax.nn.sigmoid(graw)
    G = jnp.dot(tril_f, -jnp.exp(galog_ref[...]) * sp, precision=_HIGHEST, preferred_element_type=F32)
    glast = [G[bi * L + L - 1:bi * L + L, :] for bi in range(nb)]
    eG = jnp.exp(G)
    eGrev = jnp.exp(jnp.concatenate([jnp.broadcast_to(x, (L, LANES)) for x in glast], axis=0) - G)
    egl = [jnp.exp(x) for x in glast]
    nw = nw_ref[...]

    n_st = R // (2 * L)

    def tiles(ref, st, off):
        return jnp.concatenate(
            [ref[st * 2 * L:(st + 1) * 2 * L, off + h * LANES:off + (h + 1) * LANES]
             for h in range(GDN_HEADS)], axis=0)

    def colstack(m, st, off):
        return jnp.concatenate(
            [jnp.broadcast_to(m[st * 2 * L:(st + 1) * 2 * L, off + h:off + h + 1], (2 * L, LANES))
             for h in range(GDN_HEADS)], axis=0)

    items = [dict(q=tiles(qkvc_ref, st, 0), k=tiles(qkvc_ref, st, GDN_QK), v=tiles(qkvc_ref, st, 2 * GDN_QK),
                  beta=colstack(beta_all, st, GATE_B), g=colstack(G, st, GATE_A),
                  eg=colstack(eG, st, GATE_A), egrev=colstack(eGrev, st, GATE_A))
             for st in range(n_st)]
    uw, qk = _gdn_local(items, L)

    groups = [(h, bi) for h in range(GDN_HEADS) for bi in range(2)]
    r = []
    for st, it in enumerate(items):
        w = uw[st][1]
        qd = it["q"] * it["eg"]
        r.append([
            _dot(jnp.concatenate([w[gi * L:(gi + 1) * L], qd[gi * L:(gi + 1) * L]], axis=0).astype(BF16),
                 sst_ref[2 * st + bi, h].astype(BF16))
            for gi, (h, bi) in enumerate(groups)])
    v_new = [uw[st][0] - jnp.concatenate([x[:L] for x in r[st]], axis=0) for st in range(n_st)]
    o = [jnp.concatenate([x[L:] for x in r[st]], axis=0)
         + _dot(qk[st].astype(BF16), v_new[st].astype(BF16)) for st in range(n_st)]
    for st, it in enumerate(items):
        kd = it["k"] * it["egrev"]
        for gi, (h, bi) in enumerate(groups):
            b = 2 * st + bi
            ga = GATE_A + h
            rs = slice(gi * L, (gi + 1) * L)
            sst_out_ref[b, h] = (sst_ref[b, h] * egl[b][:, ga:ga + 1]
                                 + _dot_tn(kd[rs].astype(BF16), v_new[st][rs].astype(BF16)))
    for st in range(n_st):
        out = (_rms_rows(o[st], nw) * _silu(tiles(zg_ref, st, 0))).astype(o_ref.dtype)
        for h in range(GDN_HEADS):
            o_ref[st * 2 * L:(st + 1) * 2 * L, h * GDN_DV:(h + 1) * GDN_DV] = out[h * 2 * L:(h + 1) * 2 * L]


def _gdn_sample_spec(proj, conv_state, S_state, cw, gbias, galog, nw, *, B, L, nb):
    R = nb * L
    const = lambda shape: pl.BlockSpec(shape, lambda i: (0,) * len(shape))
    return dict(
        in_specs=[
            pl.BlockSpec((R, GDN_CONV_CH), lambda i: (i, OFF_QKV // GDN_CONV_CH)),
            pl.BlockSpec((R, GDN_V), lambda i: (i, OFF_ZG // GDN_V)),
            pl.BlockSpec((R, LANES), lambda i: (i, OFF_GATE // LANES)),
            pl.BlockSpec((CONV_W - 1, nb, GDN_CONV_CH), lambda i: (0, i, 0)),
            pl.BlockSpec((nb, GDN_HEADS, GDN_DK, GDN_DV), lambda i: (i, 0, 0, 0)),
            const((CONV_W, GDN_CONV_CH)), const((1, LANES)), const((1, LANES)), const((1, GDN_DV)),
        ],
        out_specs=[
            pl.BlockSpec((R, GDN_V), lambda i: (i, 0)),
            pl.BlockSpec((CONV_W - 1, nb, GDN_CONV_CH), lambda i: (0, i, 0)),
            pl.BlockSpec((nb, GDN_HEADS, GDN_DK, GDN_DV), lambda i: (i, 0, 0, 0)),
        ],
        out_shape=[
            jax.ShapeDtypeStruct((B * L, GDN_V), BF16),
            jax.ShapeDtypeStruct((CONV_W - 1, B, GDN_CONV_CH), F32),
            jax.ShapeDtypeStruct((B, GDN_HEADS, GDN_DK, GDN_DV), F32),
        ],
        scratch=[
            pltpu.VMEM((nb, L + CONV_PAD, GDN_CONV_CH), F32),
            pltpu.VMEM((R, GDN_CONV_CH), F32),
        ],
        args=[proj, proj, proj, conv_state, S_state, cw, gbias, galog, nw])


N_PAIRS = SSM_HEADS // 2
PAIRS_PER_GROUP = N_PAIRS // SSM_GROUPS
GROUP_W = SSM_DI // SSM_GROUPS


def _ssm_tile(graw, gbias, nega, ld_x, ld_b, ld_c, ld_z, dcols_ref, nw_ref, st_o, get_h, set_h, glen):
    c = CHUNK
    P = SSM_P
    nseq = c // glen
    sh = glen.bit_length() - 1
    ri = lax.broadcasted_iota(jnp.int32, (c, c), 0)
    ci = lax.broadcasted_iota(jnp.int32, (c, c), 1)
    tril_f = (((ri >> sh) == (ci >> sh)) & (ri >= ci)).astype(F32)
    sp = _softplus(graw + gbias)
    acum = jnp.dot(tril_f, nega * sp, precision=_HIGHEST, preferred_element_type=F32)
    lasts = [acum[s * glen + glen - 1:(s + 1) * glen, :] for s in range(nseq)]
    alast = jnp.concatenate([jnp.broadcast_to(x, (glen, LANES)) for x in lasts], axis=0)
    dtrev = sp * jnp.exp(alast - acum)
    eal = [jnp.exp(x) for x in lasts]
    lane = lax.broadcasted_iota(jnp.int32, (c, LANES), 1)
    row = lax.broadcasted_iota(jnp.int32, (c, LANES), 0)
    m = jnp.where(lane < GATE_DT2, acum, sp)
    mt = jnp.concatenate([m, m], axis=0).T
    left = lane < P
    left_row = left[0:1]
    j = jnp.where(left, lane, lane - P)
    tril2 = ((row >> sh) == (j >> sh)) & (row >= j)
    rowh = lax.broadcasted_iota(jnp.int32, (2 * P, SSM_N), 0) < P

    def expand(mat, c0):
        return jnp.where(left, jnp.broadcast_to(mat[:, c0:c0 + 1], (c, LANES)),
                         jnp.broadcast_to(mat[:, c0 + 1:c0 + 2], (c, LANES)))

    def rowsel(base, e):
        return jnp.where(left_row, mt[base + 2 * e:base + 2 * e + 1, :], mt[base + 2 * e + 1:base + 2 * e + 2, :])

    for g in range(SSM_GROUPS):
        Bg = ld_b(g)
        Cg = ld_c(g)
        Bg16 = Bg.astype(BF16)
        Cg16 = Cg.astype(BF16)
        cb2 = _dot_nt(Cg16, jnp.concatenate([Bg16, Bg16], axis=0))
        pairs = [g * PAIRS_PER_GROUP + e4 for e4 in range(PAIRS_PER_GROUP)]
        acol = [expand(acum, GATE_DT + 2 * e) for e in pairs]
        scores16 = [
            (cb2 * jnp.exp(jnp.where(tril2, a - rowsel(GATE_DT, e), -jnp.inf)) * rowsel(GATE_DT2, e)).astype(BF16)
            for a, e in zip(acol, pairs)]
        xp = [ld_x(e) for e in pairs]
        bd16 = [jnp.concatenate([jnp.where(left, x, 0.0), jnp.where(left, 0.0, x)], axis=0).astype(BF16)
                for x in xp]
        ydiag = [_dot(s, b) for s, b in zip(scores16, bd16)]
        if nseq == 1:
            yoff = [_dot_nt(Cg16, get_h(0, e).astype(BF16)) for e in pairs]
        else:
            yoff = [jnp.concatenate(
                [_dot_nt(Cg[s * glen:(s + 1) * glen].astype(BF16), get_h(s, e).astype(BF16))
                 for s in range(nseq)], axis=0) for e in pairs]
        y = [yd + yo * jnp.exp(a) + dcols_ref[:, e * 2 * P:(e + 1) * 2 * P] * x
             for yd, yo, a, e, x in zip(ydiag, yoff, acol, pairs, xp)]
        xdr = [x * expand(dtrev, GATE_DT + 2 * e) for x, e in zip(xp, pairs)]
        for e, xd in zip(pairs, xdr):
            c0 = GATE_DT + 2 * e
            for s in range(nseq):
                rs = slice(s * glen, (s + 1) * glen)
                ealcol = jnp.where(rowh, eal[s][:, c0:c0 + 1], eal[s][:, c0 + 1:c0 + 2])
                set_h(s, e, get_h(s, e) * ealcol + _dot_tn(xd[rs].astype(BF16), Bg[rs].astype(BF16)))
        yg = jnp.concatenate(y, axis=1) * _silu(ld_z(g))
        gcols = slice(g * GROUP_W, (g + 1) * GROUP_W)
        st_o(g, _rms_rows(yg, nw_ref[:, gcols]))


def _ssm_prompt_parts(xs_ref, bc_ref, zs_ref, gate_ref, cwx_ref, cbx_ref, cwbc_ref, cbbc_ref,
                      gbias_ref, galog_ref, dcols_ref, nw_ref,
                      o_ref, cst_out_ref, hst_out_ref, xpadx_ref, xpadbc_ref, xc_ref, bcc_ref, hh_ref, *, Lb):
    c = CHUNK

    def init():
        hh_ref[...] = jnp.zeros(hh_ref.shape, F32)
        xpadx_ref[0:CONV_PAD, :] = jnp.zeros((CONV_PAD, SSM_DI), F32)
        xpadbc_ref[0:CONV_PAD, :] = jnp.zeros((CONV_PAD, 2 * SSM_BC), F32)

    def conv():
        ident = lambda s, y: y
        _conv_block(xs_ref, xpadx_ref, cwx_ref, cbx_ref, xc_ref, Lb, SSM_DI, ident)
        _conv_block(bc_ref, xpadbc_ref, cwbc_ref, cbbc_ref, bcc_ref, Lb, 2 * SSM_BC, ident)

    def set_h(s, e, val):
        hh_ref[e] = val

    def tile(ci):
        rows = slice(ci * c, (ci + 1) * c)

        def st_o(g, val):
            o_ref[rows, g * GROUP_W:(g + 1) * GROUP_W] = val.astype(o_ref.dtype)

        _ssm_tile(
            gate_ref[rows, :], gbias_ref[...], -jnp.exp(galog_ref[...]),
            lambda e: xc_ref[rows, e * LANES:(e + 1) * LANES],
            lambda g: bcc_ref[rows, g * SSM_N:(g + 1) * SSM_N],
            lambda g: bcc_ref[rows, SSM_BC + g * SSM_N:SSM_BC + (g + 1) * SSM_N],
            lambda g: zs_ref[rows, g * GROUP_W:(g + 1) * GROUP_W],
            dcols_ref, nw_ref, st_o, lambda s, e: hh_ref[e], set_h, c)

    steps = [conv] + [functools.partial(tile, ci) for ci in range(Lb // c)]

    def final():
        cst_out_ref[0, :, :SSM_DI] = xpadx_ref[CONV_HIST:CONV_PAD, :]
        cst_out_ref[0, :, SSM_DI:] = xpadbc_ref[CONV_HIST:CONV_PAD, :]
        hst_out_ref[0] = hh_ref[...]

    return init, steps, final


def _ssm_sample_kernel(xs_ref, bc_ref, zs_ref, gate_ref, cst_ref, hst_ref, cwx_ref, cbx_ref, cwbc_ref, cbbc_ref,
                       gbias_ref, galog_ref, dcols_ref, nw_ref,
                       o_ref, cst_out_ref, hst_out_ref, xpadx_ref, xpadbc_ref, xc_ref, bcc_ref, *, L):
    ident = lambda s, y: y
    for bi in range(CHUNK // L):
        rs = pl.ds(bi * L, L)
        xpx = xpadx_ref.at[bi]
        xpb = xpadbc_ref.at[bi]
        xpx[CONV_HIST:CONV_PAD, :] = cst_ref[:, bi, :SSM_DI]
        xpb[CONV_HIST:CONV_PAD, :] = cst_ref[:, bi, SSM_DI:]
        cst_out_ref[:, bi, :SSM_DI] = _conv_block(
            xs_ref.at[rs], xpx, cwx_ref, cbx_ref, xc_ref.at[rs], L, SSM_DI, ident)
        cst_out_ref[:, bi, SSM_DI:] = _conv_block(
            bc_ref.at[rs], xpb, cwbc_ref, cbbc_ref, bcc_ref.at[rs], L, 2 * SSM_BC, ident)

    def st_o(g, val):
        o_ref[:, g * GROUP_W:(g + 1) * GROUP_W] = val.astype(o_ref.dtype)

    def set_h(s, e, val):
        hst_out_ref[s, e] = val

    _ssm_tile(
        gate_ref[...], gbias_ref[...], -jnp.exp(galog_ref[...]),
        lambda e: xc_ref[:, e * LANES:(e + 1) * LANES],
        lambda g: bcc_ref[:, g * SSM_N:(g + 1) * SSM_N],
        lambda g: bcc_ref[:, SSM_BC + g * SSM_N:SSM_BC + (g + 1) * SSM_N],
        lambda g: zs_ref[:, g * GROUP_W:(g + 1) * GROUP_W],
        dcols_ref, nw_ref, st_o, lambda s, e: hst_ref[s, e], set_h, L)


def _ssm_const_specs():
    const = lambda shape: pl.BlockSpec(shape, lambda *idx: (0,) * len(shape))
    return [
        const((CONV_W, SSM_DI)), const((1, SSM_DI)), const((CONV_W, 2 * SSM_BC)), const((1, 2 * SSM_BC)),
        const((1, LANES)), const((1, LANES)), const((1, SSM_DI)), const((1, SSM_DI)),
    ]


def _ssm_prompt_spec(proj, cwx, cbx, cwbc, cbbc, gbias, galog, dcols, nw, *, B, L, Lb):
    nl = L // Lb
    row = lambda b, l: b * nl + l
    return dict(
        in_specs=[
            pl.BlockSpec((Lb, SSM_DI), lambda b, l: (row(b, l), OFF_XS // SSM_DI)),
            pl.BlockSpec((Lb, 2 * SSM_BC), lambda b, l: (row(b, l), OFF_BC // (2 * SSM_BC))),
            pl.BlockSpec((Lb, SSM_DI), lambda b, l: (row(b, l), OFF_ZS // SSM_DI)),
            pl.BlockSpec((Lb, LANES), lambda b, l: (row(b, l), OFF_GATE // LANES)),
        ] + _ssm_const_specs(),
        out_specs=[
            pl.BlockSpec((Lb, SSM_DI), lambda b, l: (row(b, l), 0)),
            pl.BlockSpec((1, CONV_W - 1, SSM_CONV_CH), lambda b, l: (b, 0, 0)),
            pl.BlockSpec((1, N_PAIRS, 2 * SSM_P, SSM_N), lambda b, l: (b, 0, 0, 0)),
        ],
        out_shape=[
            jax.ShapeDtypeStruct((B * L, SSM_DI), BF16),
            jax.ShapeDtypeStruct((B, CONV_W - 1, SSM_CONV_CH), F32),
            jax.ShapeDtypeStruct((B, N_PAIRS, 2 * SSM_P, SSM_N), F32),
        ],
        scratch=[
            pltpu.VMEM((Lb + CONV_PAD, SSM_DI), F32),
            pltpu.VMEM((Lb + CONV_PAD, 2 * SSM_BC), F32),
            pltpu.VMEM((Lb, SSM_DI), F32),
            pltpu.VMEM((Lb, 2 * SSM_BC), F32),
            pltpu.VMEM((N_PAIRS, 2 * SSM_P, SSM_N), F32),
        ],
        args=[proj, proj, proj, proj, cwx, cbx, cwbc, cbbc, gbias, galog, dcols, nw])


def _ssm_sample_spec(proj, conv_state, h_pairs, cwx, cbx, cwbc, cbbc, gbias, galog, dcols, nw, *, B, L):
    nb = CHUNK // L
    return dict(
        in_specs=[
            pl.BlockSpec((CHUNK, SSM_DI), lambda i: (i, OFF_XS // SSM_DI)),
            pl.BlockSpec((CHUNK, 2 * SSM_BC), lambda i: (i, OFF_BC // (2 * SSM_BC))),
            pl.BlockSpec((CHUNK, SSM_DI), lambda i: (i, OFF_ZS // SSM_DI)),
            pl.BlockSpec((CHUNK, LANES), lambda i: (i, OFF_GATE // LANES)),
            pl.BlockSpec((CONV_W - 1, nb, SSM_CONV_CH), lambda i: (0, i, 0)),
            pl.BlockSpec((nb, N_PAIRS, 2 * SSM_P, SSM_N), lambda i: (i, 0, 0, 0)),
        ] + _ssm_const_specs(),
        out_specs=[
            pl.BlockSpec((CHUNK, SSM_DI), lambda i: (i, 0)),
            pl.BlockSpec((CONV_W - 1, nb, SSM_CONV_CH), lambda i: (0, i, 0)),
            pl.BlockSpec((nb, N_PAIRS, 2 * SSM_P, SSM_N), lambda i: (i, 0, 0, 0)),
        ],
        out_shape=[
            jax.ShapeDtypeStruct((B * L, SSM_DI), BF16),
            jax.ShapeDtypeStruct((CONV_W - 1, B, SSM_CONV_CH), F32),
            jax.ShapeDtypeStruct((B, N_PAIRS, 2 * SSM_P, SSM_N), F32),
        ],
        scratch=[
            pltpu.VMEM((nb, L + CONV_PAD, SSM_DI), F32),
            pltpu.VMEM((nb, L + CONV_PAD, 2 * SSM_BC), F32),
            pltpu.VMEM((CHUNK, SSM_DI), F32),
            pltpu.VMEM((CHUNK, 2 * SSM_BC), F32),
        ],
        args=[proj, proj, proj, proj, conv_state, h_pairs, cwx, cbx, cwbc, cbbc, gbias, galog, dcols, nw])


def _split_refs(refs, g, s, n_cast=0):
    it = iter(refs)
    take = lambda n: [next(it) for _ in range(n)]
    g_in, s_in, c_in = take(len(g["in_specs"])), take(len(s["in_specs"])), take(n_cast)
    g_out, s_out, c_out = take(len(g["out_specs"])), take(len(s["out_specs"])), take(n_cast)
    g_scr, s_scr = take(len(g["scratch"])), take(len(s["scratch"]))
    return g_in + g_out + g_scr, s_in + s_out + s_scr, list(zip(c_in, c_out))


def _mix_prompt_kernel(*refs, g, s, n_cast, Lb):
    g_refs, s_refs, casts = _split_refs(refs, g, s, n_cast)
    for src_ref, dst_ref in casts:
        dst_ref[...] = src_ref[...].astype(BF16)
    g_init, g_steps, g_final = _gdn_prompt_parts(*g_refs, Lb=Lb)
    s_init, s_steps, s_final = _ssm_prompt_parts(*s_refs, Lb=Lb)
    l = pl.program_id(1)

    @pl.when(l == 0)
    def _init():
        g_init()
        s_init()

    for g_step, s_step in itertools.zip_longest(g_steps, s_steps):
        if g_step is not None:
            g_step()
        if s_step is not None:
            s_step()

    @pl.when(l == pl.num_programs(1) - 1)
    def _final():
        g_final()
        s_final()


def _mix_sample_kernel(*refs, g, s, nb, L):
    g_refs, s_refs, _ = _split_refs(refs, g, s)
    _gdn_sample_kernel(*g_refs, nb=nb, L=L)
    _ssm_sample_kernel(*s_refs, L=L)


def _fused_call(kernel, g, s, grid, semantics, name, cast=()):
    n_steps = functools.reduce(lambda a, b: a * b, grid)
    step = (lambda b, l: (b * grid[1] + l, 0)) if len(grid) == 2 else (lambda i: (i, 0))
    c_specs = [pl.BlockSpec((w.shape[0] // n_steps, w.shape[1]), step) for w in cast]
    c_shape = [jax.ShapeDtypeStruct(w.shape, BF16) for w in cast]
    kw = dict(n_cast=len(cast)) if cast else {}
    outs = pl.pallas_call(
        functools.partial(kernel, g={k: g[k] for k in ("in_specs", "out_specs", "scratch")},
                          s={k: s[k] for k in ("in_specs", "out_specs", "scratch")}, **kw),
        grid=grid,
        in_specs=g["in_specs"] + s["in_specs"] + c_specs,
        out_specs=g["out_specs"] + s["out_specs"] + c_specs,
        out_shape=g["out_shape"] + s["out_shape"] + c_shape,
        scratch_shapes=g["scratch"] + s["scratch"],
        compiler_params=pltpu.CompilerParams(dimension_semantics=semantics, vmem_limit_bytes=VMEM_LIMIT),
        name=name,
    )(*g["args"], *s["args"], *cast)
    n, m = len(g["out_specs"]), len(g["out_specs"]) + len(s["out_specs"])
    return (outs[:n], outs[n:m]) + ((outs[m:],) if cast else ())


def _outproj_kernel(x_ref, mg_ref, ms_ref, w_ref, o_ref):
    acc = _dot(mg_ref[...].astype(BF16), w_ref[:GDN_V, :])
    acc = acc + _dot(ms_ref[...].astype(BF16), w_ref[GDN_V:, :])
    o_ref[...] = x_ref[...] + acc


def _out_proj(x2d, mix_g, mix_s, w_out16, *, tm):
    T = x2d.shape[0]
    return pl.pallas_call(
        _outproj_kernel,
        grid=(T // tm,),
        in_specs=[
            pl.BlockSpec((tm, D_MODEL), lambda i: (i, 0)),
            pl.BlockSpec((tm, GDN_V), lambda i: (i, 0)),
            pl.BlockSpec((tm, SSM_DI), lambda i: (i, 0)),
            pl.BlockSpec((D_MODEL, D_MODEL), lambda i: (0, 0)),
        ],
        out_specs=pl.BlockSpec((tm, D_MODEL), lambda i: (i, 0)),
        out_shape=jax.ShapeDtypeStruct((T, D_MODEL), F32),
        compiler_params=pltpu.CompilerParams(
            dimension_semantics=("parallel",), vmem_limit_bytes=VMEM_LIMIT),
        name="out_proj",
    )(x2d, mix_g, mix_s, w_out16)


FFN_SUB = 4


def _ffn_kernel(x_ref, nw_ref, wg_ref, wu_ref, wd_ref, fnw_ref, o_ref, h_ref):
    f = pl.program_id(1)
    nf = pl.num_programs(1)
    rs = h_ref.shape[0] // FFN_SUB

    def step(first, last):
        def gate_up(r):
            rows = slice(r * rs, (r + 1) * rs)
            if first:
                h = _rms_rows(x_ref[rows, :], nw_ref[...]).astype(BF16)
                h_ref[rows, :] = h
            else:
                h = h_ref[rows, :]
            return _dot(h, wg_ref[...]), _dot(h, wu_ref[...])

        def down(r, gu):
            rows = slice(r * rs, (r + 1) * rs)
            d = _dot((_silu(gu[0]) * gu[1]).astype(BF16), wd_ref[...])
            acc = d if first else o_ref[rows, :] + d
            if last:
                o_ref[rows, :] = _rms_rows(x_ref[rows, :] + acc, fnw_ref[...])
            else:
                o_ref[rows, :] = acc

        gu = gate_up(0)
        for r in range(1, FFN_SUB):
            gu_next = gate_up(r)
            down(r - 1, gu)
            gu = gu_next
        down(FFN_SUB - 1, gu)

    pl.when(f == 0)(lambda: step(True, False))
    pl.when((f > 0) & (f < nf - 1))(lambda: step(False, False))
    pl.when(f == nf - 1)(lambda: step(False, True))


def _ffn(x2d, norm_w, wg16, wu16, wd16, final_w, *, tm, tf):
    T = x2d.shape[0]
    return pl.pallas_call(
        _ffn_kernel,
        grid=(T // tm, D_FF // tf),
        in_specs=[
            pl.BlockSpec((tm, D_MODEL), lambda i, f: (i, 0)),
            pl.BlockSpec((1, D_MODEL), lambda i, f: (0, 0)),
            pl.BlockSpec((D_MODEL, tf), lambda i, f: (0, f)),
            pl.BlockSpec((D_MODEL, tf), lambda i, f: (0, f)),
            pl.BlockSpec((tf, D_MODEL), lambda i, f: (f, 0)),
            pl.BlockSpec((1, D_MODEL), lambda i, f: (0, 0)),
        ],
        out_specs=pl.BlockSpec((tm, D_MODEL), lambda i, f: (i, 0)),
        out_shape=jax.ShapeDtypeStruct((T, D_MODEL), F32),
        scratch_shapes=[pltpu.VMEM((tm, D_MODEL), BF16)],
        compiler_params=pltpu.CompilerParams(
            dimension_semantics=("parallel", "arbitrary"), vmem_limit_bytes=VMEM_LIMIT),
        name="ffn",
    )(x2d, norm_w, wg16, wu16, wd16, final_w)


PROMPT_ROWS = 256


def _trunk(x, states, p):
    B, L, _ = x.shape
    x2d = x.reshape(B * L, D_MODEL)
    if "wd16" in p:
        proj = _in_proj(x2d, p["attn_norm_w"], p["w_in_r"], tm=IN_PROJ_TM, tn=IN_PROJ_TN)
    else:
        proj, p["wd16"], p["w_out16"] = _in_proj(
            x2d, p["attn_norm_w"], p["w_in_r"], (p["w_down"], p["w_out"]), tm=IN_PROJ_TM, tn=IN_PROJ_TN)
    gdn_w = (p["gdn_conv_w"], p["gbias"], p["galog"], p["gdn_norm_w"])
    ssm_w = (p["cwx"], p["cbx"], p["cwbc"], p["cbbc"], p["gbias"], p["galog"], p["dcols"], p["ssm_norm_w"])
    pair_shape = (B, N_PAIRS, 2 * SSM_P, SSM_N)
    if states is None:
        Lb = PROMPT_ROWS
        g = _gdn_prompt_spec(proj, *gdn_w, B=B, L=L, Lb=Lb)
        s = _ssm_prompt_spec(proj, *ssm_w, B=B, L=L, Lb=Lb)
        kern = functools.partial(_mix_prompt_kernel, Lb=Lb)
        (mix_g, gconv_new, gS_new), (mix_s, sconv_new, sh_new), (p["wg16"], p["wu16"]) = _fused_call(
            kern, g, s, (B, L // Lb), ("parallel", "arbitrary"), "mix_prompt", cast=(p["w_gate"], p["w_up"]))
    else:
        gconv, gS, sconv, sh = states
        nb = CHUNK // L
        tap_major = lambda a: jnp.swapaxes(a, 0, 1)
        g = _gdn_sample_spec(proj, tap_major(gconv), gS, *gdn_w, B=B, L=L, nb=nb)
        s = _ssm_sample_spec(proj, tap_major(sconv), sh.reshape(pair_shape), *ssm_w, B=B, L=L)
        kern = functools.partial(_mix_sample_kernel, nb=nb, L=L)
        (mix_g, gconv_new, gS_new), (mix_s, sconv_new, sh_new) = _fused_call(
            kern, g, s, (B // nb,), ("parallel",), "mix_sample")
        gconv_new, sconv_new = tap_major(gconv_new), tap_major(sconv_new)
    sh_new = sh_new.reshape(B, SSM_HEADS, SSM_P, SSM_N)
    x1 = _out_proj(x2d, mix_g, mix_s, p["w_out16"], tm=OUT_PROJ_TM)
    y = _ffn(x1, p["ffn_norm_w"], p["wg16"], p["wu16"], p["wd16"], p["final_norm_w"], tm=FFN_TM, tf=FFN_TF)
    return y.reshape(B, L, D_MODEL), (gconv_new[None], gS_new[None], sconv_new[None], sh_new[None])


def kernel(x_prompt, x_sample, state_gdn_conv, state_gdn, state_ssm_conv, state_ssm,
           attn_norm_w, w_in, gdn_conv_w, gdn_A_log, gdn_dt_bias, gdn_norm_w,
           ssm_conv_w, ssm_conv_b, ssm_A_log, ssm_dt_bias, ssm_D, ssm_norm_w,
           w_out, ffn_norm_w, w_gate, w_up, w_down, final_norm_w):
    assert w_in.shape[0] == 1, "single-layer trunk"
    assert x_prompt.shape[1] % PROMPT_ROWS == 0 and CHUNK % x_sample.shape[1] == 0 and x_sample.shape[0] % (CHUNK // x_sample.shape[1]) == 0
    assert w_in.shape[2] == D_IN_PROJ
    w_in_r = _w_in_prep(jnp.swapaxes(w_in, 1, 2), tk=W_PREP_TK)
    zeros8 = jnp.zeros((GDN_HEADS,), F32)
    tail = jnp.zeros((LANES - GATE_DT2 - SSM_HEADS,), F32)
    gbias = jnp.concatenate([zeros8, gdn_dt_bias[0], ssm_dt_bias[0], ssm_dt_bias[0], tail])[None]
    galog = jnp.concatenate([zeros8, gdn_A_log[0], ssm_A_log[0], ssm_A_log[0], tail])[None]
    p = dict(
        attn_norm_w=attn_norm_w, w_in_r=w_in_r, gdn_conv_w=gdn_conv_w[0], gbias=gbias, galog=galog,
        gdn_norm_w=gdn_norm_w,
        cwx=ssm_conv_w[0][:, :SSM_DI], cbx=ssm_conv_b[:, :SSM_DI],
        cwbc=ssm_conv_w[0][:, SSM_DI:], cbbc=ssm_conv_b[:, SSM_DI:],
        dcols=jnp.repeat(ssm_D[0], SSM_P)[None], ssm_norm_w=ssm_norm_w,
        w_out=w_out[0], w_down=w_down[0], ffn_norm_w=ffn_norm_w,
        w_gate=w_gate[0], w_up=w_up[0],
        final_norm_w=final_norm_w[None],
    )
    y_p, st_p = _trunk(x_prompt, None, p)
    y_s, st_s = _trunk(x_sample, (state_gdn_conv[0], state_gdn[0], state_ssm_conv[0], state_ssm[0]), p)
    return (y_p, y_s, st_p[0], st_p[1], st_p[2], st_p[3], st_s[0], st_s[1], st_s[2], st_s[3])
```

```python
import functools
import itertools

import jax
import jax.numpy as jnp
from jax import lax
from jax.experimental import pallas as pl
from jax.experimental.pallas import tpu as pltpu

F32 = jnp.float32
BF16 = jnp.bfloat16

D_MODEL = 2048
GDN_HEADS = 8
GDN_DK = 128
GDN_DV = 128
GDN_QK = GDN_HEADS * GDN_DK
GDN_V = GDN_HEADS * GDN_DV
GDN_CONV_CH = 2 * GDN_QK + GDN_V
SSM_P = 64
SSM_N = 128
SSM_GROUPS = 2
SSM_DI = 1024
SSM_HEADS = SSM_DI // SSM_P
SSM_BC = SSM_GROUPS * SSM_N
SSM_CONV_CH = SSM_DI + 2 * SSM_BC
CONV_W = 4
CHUNK = 64
D_FF = 5632
EPS = 1e-6

OFF_QKV = 0
OFF_ZG = OFF_QKV + GDN_CONV_CH
OFF_ZS = OFF_ZG + GDN_V
OFF_XS = OFF_ZS + SSM_DI
OFF_BC = OFF_XS + SSM_DI
OFF_GATE = OFF_BC + 2 * SSM_BC
LANES = 128
SUBLANES = 8
CONV_PAD = SUBLANES
CONV_HIST = CONV_PAD - (CONV_W - 1)
GATE_B = 0
GATE_A = GATE_B + GDN_HEADS
GATE_DT = GATE_A + GDN_HEADS
GATE_DT2 = GATE_DT + SSM_HEADS
N_PROJ = 6912

VMEM_LIMIT = 52 * 1024 * 1024
W_PREP_TK = 256
IN_PROJ_TM, IN_PROJ_TN = 1024, 768
OUT_PROJ_TM = 512
FFN_TM, FFN_TF = 1024, 512

_HIGHEST = lax.Precision.HIGHEST


def _silu(x):
    h = 0.5 * x
    return h + h * jnp.tanh(h)


def _softplus(x):
    return jnp.maximum(x, 0.0) + jnp.log1p(jnp.exp(-jnp.abs(x)))


def _dot(a, b):
    return jnp.dot(a, b, preferred_element_type=F32)


def _dot_nt(a, b):
    return lax.dot_general(a, b, (((1,), (1,)), ((), ())), preferred_element_type=F32)


def _dot_tn(a, b):
    return lax.dot_general(a, b, (((0,), (0,)), ((), ())), preferred_element_type=F32)


def _rms_rows(x, w):
    return x * lax.rsqrt(jnp.mean(x * x, axis=-1, keepdims=True) + EPS) * w


W_B = OFF_ZS
W_ZS = W_B + 2 * GDN_HEADS
W_DT = W_ZS + SSM_DI + SSM_CONV_CH
D_IN_PROJ = W_DT + SSM_HEADS


def _wprep_kernel(w_ref, o_ref):
    cols = o_ref.shape[1]
    o_ref[:W_B, :] = w_ref[0, :W_B, :].astype(BF16)
    o_ref[W_B:OFF_GATE, :] = w_ref[0, W_ZS:W_DT, :].astype(BF16)
    dt = w_ref[0, W_DT:D_IN_PROJ, :].astype(BF16)
    o_ref[OFF_GATE:OFF_GATE + GATE_DT, :] = w_ref[0, W_B:W_ZS, :].astype(BF16)
    o_ref[OFF_GATE + GATE_DT:OFF_GATE + GATE_DT2, :] = dt
    o_ref[OFF_GATE + GATE_DT2:OFF_GATE + GATE_DT2 + SSM_HEADS, :] = dt
    o_ref[OFF_GATE + GATE_DT2 + SSM_HEADS:, :] = jnp.zeros((N_PROJ - OFF_GATE - GATE_DT2 - SSM_HEADS, cols), BF16)


def _w_in_prep(w_in_t, *, tk):
    return pl.pallas_call(
        _wprep_kernel,
        grid=(D_MODEL // tk,),
        in_specs=[pl.BlockSpec((1, D_IN_PROJ, tk), lambda i: (0, 0, i))],
        out_specs=pl.BlockSpec((N_PROJ, tk), lambda i: (0, i)),
        out_shape=jax.ShapeDtypeStruct((N_PROJ, D_MODEL), BF16),
        compiler_params=pltpu.CompilerParams(
            dimension_semantics=("parallel",), vmem_limit_bytes=VMEM_LIMIT),
        name="w_in_prep",
    )(w_in_t)


INPROJ_SUB = 4


def _inproj_kernel(x_ref, nw_ref, wt_ref, o_ref, h_ref):
    j = pl.program_id(1)

    @pl.when(j == 0)
    def _first():
        rs = h_ref.shape[0] // INPROJ_SUB
        for r in range(INPROJ_SUB):
            rows = slice(r * rs, (r + 1) * rs)
            h = _rms_rows(x_ref[rows, :], nw_ref[...]).astype(BF16)
            h_ref[rows, :] = h
            o_ref[rows, :] = _dot_nt(h, wt_ref[...])

    @pl.when(j > 0)
    def _rest():
        o_ref[...] = _dot_nt(h_ref[...], wt_ref[...])


def _in_proj(x2d, norm_w, w_in_r, *, tm, tn):
    T = x2d.shape[0]
    return pl.pallas_call(
        _inproj_kernel,
        grid=(T // tm, N_PROJ // tn),
        in_specs=[
            pl.BlockSpec((tm, D_MODEL), lambda i, j: (i, 0)),
            pl.BlockSpec((1, D_MODEL), lambda i, j: (0, 0)),
            pl.BlockSpec((tn, D_MODEL), lambda i, j: (j, 0)),
        ],
        out_specs=pl.BlockSpec((tm, tn), lambda i, j: (i, j)),
        out_shape=jax.ShapeDtypeStruct((T, N_PROJ), F32),
        scratch_shapes=[pltpu.VMEM((tm, D_MODEL), BF16)],
        compiler_params=pltpu.CompilerParams(
            dimension_semantics=("parallel", "arbitrary"), vmem_limit_bytes=VMEM_LIMIT),
        name="in_proj",
    )(x2d, norm_w, w_in_r)


def _conv_block(x_ref, xpad_ref, cw_ref, cb_ref, dst_ref, Lb, C, post):
    xpad_ref[CONV_PAD:CONV_PAD + Lb, :] = x_ref[...]
    rs = min(Lb, CHUNK)
    for sb in range(Lb // rs):
        r = sb * rs
        for s in range(C // LANES):
            cols = slice(s * LANES, (s + 1) * LANES)
            acc = xpad_ref[CONV_PAD + r:CONV_PAD + r + rs, cols] * cw_ref[CONV_W - 1:CONV_W, cols]
            for i in range(CONV_W - 1):
                acc = acc + xpad_ref[CONV_HIST + i + r:CONV_HIST + i + r + rs, cols] * cw_ref[i:i + 1, cols]
            if cb_ref is not None:
                acc = acc + cb_ref[:, cols]
            dst_ref[r:r + rs, cols] = post(s, _silu(acc))
    hist = xpad_ref[Lb + CONV_HIST:Lb + CONV_PAD, :]
    xpad_ref[CONV_HIST:CONV_PAD, :] = hist
    return hist


STACK = 128


def _gdn_qk_post(s, y):
    if s < 2 * GDN_HEADS:
        y = y * lax.rsqrt(jnp.sum(y * y, axis=-1, keepdims=True) + EPS)
        if s < GDN_HEADS:
            y = y * (GDN_DK ** -0.5)
    return y


def _gdn_local(items, glen):
    sh = glen.bit_length() - 1
    row = lax.broadcasted_iota(jnp.int32, (STACK, STACK), 0)
    col = lax.broadcasted_iota(jnp.int32, (STACK, STACK), 1)
    same = (row >> sh) == (col >> sh)
    incl = same & (row >= col)
    strict = same & (row > col)
    eye = (row == col).astype(F32)

    decay = [jnp.exp(jnp.where(incl, it["g"] - it["g"].T, -jnp.inf)) for it in items]
    kb = [it["k"] * it["beta"] for it in items]
    qkk = [_dot_nt(jnp.concatenate([it["q"], b], axis=0).astype(BF16), it["k"].astype(BF16))
           for it, b in zip(items, kb)]
    qk = [x[:STACK] * d for x, d in zip(qkk, decay)]
    nmat = [jnp.where(strict, -(x[STACK:] * d), 0.0) for x, d in zip(qkk, decay)]
    tinv = [eye + n for n in nmat]
    if sh >= 2:
        pw = [_dot(n.astype(BF16), n.astype(BF16)) for n in nmat]
        for _ in range(sh - 2):
            x = [_dot(jnp.concatenate([t, p], axis=0).astype(BF16), p.astype(BF16))
                 for t, p in zip(tinv, pw)]
            tinv = [t + y[:STACK] for t, y in zip(tinv, x)]
            pw = [y[STACK:] for y in x]
        tinv = [t + _dot(t.astype(BF16), p.astype(BF16)) for t, p in zip(tinv, pw)]
    uw = [_dot(t.astype(BF16),
               jnp.concatenate([it["v"] * it["beta"], b * it["eg"]], axis=1).astype(BF16))
          for t, it, b in zip(tinv, items, kb)]
    return [(x[:, :GDN_DV], x[:, GDN_DV:]) for x in uw], qk


GDN_LOCAL_CHUNKS = 2


def _gdn_prompt_parts(qkv_ref, zg_ref, gate_ref, cw_ref, gbias_ref, galog_ref, nw_ref,
                      o_ref, cst_out_ref, sst_out_ref,
                      xpad_ref, qkvc_ref, s_ref, u_ref, wq16_ref, kd16_ref, qk16_ref, egl_ref, *, Lb):
    c = CHUNK

    def init():
        xpad_ref[0:CONV_PAD, :] = jnp.zeros((CONV_PAD, GDN_CONV_CH), F32)
        s_ref[...] = jnp.zeros(s_ref.shape, F32)

    def conv():
        _conv_block(qkv_ref, xpad_ref, cw_ref, None, qkvc_ref, Lb, GDN_CONV_CH, _gdn_qk_post)

    row_i = lax.broadcasted_iota(jnp.int32, (c, c), 0)
    col_i = lax.broadcasted_iota(jnp.int32, (c, c), 1)
    tril_f = (row_i >= col_i).astype(F32)
    gbias = gbias_ref[...]
    nega = -jnp.exp(galog_ref[...])
    nw = nw_ref[...]
    n_pairs = GDN_HEADS // 2

    def local_chunk_items(ci):
        rows = slice(ci * c, (ci + 1) * c)
        graw = gate_ref[rows, :]
        sp = _softplus(graw + gbias)
        beta_all = jax.nn.sigmoid(graw)
        G = jnp.dot(tril_f, nega * sp, precision=_HIGHEST, preferred_element_type=F32)
        eG = jnp.exp(G)
        glast = G[c - 1:c, :]
        eGrev = jnp.exp(glast - G)
        egl_ref[ci] = jnp.broadcast_to(jnp.exp(glast), (SUBLANES, LANES))

        def heads(off, a, b):
            return jnp.concatenate([qkvc_ref[rows, off + a * LANES:off + (a + 1) * LANES],
                                    qkvc_ref[rows, off + b * LANES:off + (b + 1) * LANES]], axis=0)

        def colstack(m, a, b):
            return jnp.concatenate([jnp.broadcast_to(m[:, a:a + 1], (c, LANES)),
                                    jnp.broadcast_to(m[:, b:b + 1], (c, LANES))], axis=0)

        items = []
        for pr in range(n_pairs):
            a, b = 2 * pr, 2 * pr + 1
            items.append(dict(
                q=heads(0, a, b), k=heads(GDN_QK, a, b), v=heads(2 * GDN_QK, a, b),
                beta=colstack(beta_all, GATE_B + a, GATE_B + b),
                g=colstack(G, GATE_A + a, GATE_A + b),
                eg=colstack(eG, GATE_A + a, GATE_A + b),
                egrev=colstack(eGrev, GATE_A + a, GATE_A + b)))
        return items

    def local_group(cis):
        items = [it for ci in cis for it in local_chunk_items(ci)]
        uw, qk = _gdn_local(items, c)
        for n, it in enumerate(items):
            idx = cis[0] * n_pairs + n
            u, w = uw[n]
            qd = it["q"] * it["eg"]
            u_ref[idx] = u
            for hh in range(2):
                hs = slice(hh * c, (hh + 1) * c)
                wq16_ref[2 * idx + hh] = jnp.concatenate([w[hs], qd[hs]], axis=0).astype(BF16)
            kd16_ref[idx] = (it["k"] * it["egrev"]).astype(BF16)
            qk16_ref[idx] = qk[n].astype(BF16)

    def recurrent(ci):
        rows = slice(ci * c, (ci + 1) * c)
        egl = egl_ref[ci][0:1]
        r = [[_dot(wq16_ref[2 * (ci * n_pairs + pr) + hh], s_ref[2 * pr + hh].astype(BF16)) for hh in range(2)]
             for pr in range(n_pairs)]
        v16 = [(u_ref[ci * n_pairs + pr] - jnp.concatenate([r[pr][0][:c], r[pr][1][:c]], axis=0)).astype(BF16)
               for pr in range(n_pairs)]
        o = [jnp.concatenate([r[pr][0][c:], r[pr][1][c:]], axis=0) + _dot(qk16_ref[ci * n_pairs + pr], v16[pr])
             for pr in range(n_pairs)]
        for pr in range(n_pairs):
            kd16 = kd16_ref[ci * n_pairs + pr]
            for hh in range(2):
                h = 2 * pr + hh
                ga = GATE_A + h
                hs = slice(hh * c, (hh + 1) * c)
                s_ref[h] = s_ref[h] * egl[:, ga:ga + 1] + _dot_tn(kd16[hs], v16[pr][hs])
        for pr in range(n_pairs):
            for hh in range(2):
                h = 2 * pr + hh
                z = zg_ref[rows, h * GDN_DV:(h + 1) * GDN_DV]
                o_ref[rows, h * GDN_DV:(h + 1) * GDN_DV] = (
                    _rms_rows(o[pr][hh * c:(hh + 1) * c], nw) * _silu(z)).astype(o_ref.dtype)

    steps = [conv]
    for g0 in range(0, Lb // c, GDN_LOCAL_CHUNKS):
        cis = list(range(g0, g0 + GDN_LOCAL_CHUNKS))
        steps.append(functools.partial(local_group, cis))
        steps += [functools.partial(recurrent, ci) for ci in cis]

    def final():
        cst_out_ref[0] = xpad_ref[CONV_HIST:CONV_PAD, :]
        sst_out_ref[0] = s_ref[...]

    return init, steps, final


def _gdn_prompt_spec(proj, cw, gbias, galog, nw, *, B, L, Lb):
    nl = L // Lb
    n_tiles = (Lb // CHUNK) * (GDN_HEADS // 2)
    row = lambda b, l: b * nl + l
    const = lambda shape: pl.BlockSpec(shape, lambda b, l: (0,) * len(shape))
    return dict(
        in_specs=[
            pl.BlockSpec((Lb, GDN_CONV_CH), lambda b, l: (row(b, l), OFF_QKV // GDN_CONV_CH)),
            pl.BlockSpec((Lb, GDN_V), lambda b, l: (row(b, l), OFF_ZG // GDN_V)),
            pl.BlockSpec((Lb, LANES), lambda b, l: (row(b, l), OFF_GATE // LANES)),
            const((CONV_W, GDN_CONV_CH)), const((1, LANES)), const((1, LANES)), const((1, GDN_DV)),
        ],
        out_specs=[
            pl.BlockSpec((Lb, GDN_V), lambda b, l: (row(b, l), 0)),
            pl.BlockSpec((1, CONV_W - 1, GDN_CONV_CH), lambda b, l: (b, 0, 0)),
            pl.BlockSpec((1, GDN_HEADS, GDN_DK, GDN_DV), lambda b, l: (b, 0, 0, 0)),
        ],
        out_shape=[
            jax.ShapeDtypeStruct((B * L, GDN_V), BF16),
            jax.ShapeDtypeStruct((B, CONV_W - 1, GDN_CONV_CH), F32),
            jax.ShapeDtypeStruct((B, GDN_HEADS, GDN_DK, GDN_DV), F32),
        ],
        scratch=[
            pltpu.VMEM((Lb + CONV_PAD, GDN_CONV_CH), F32),
            pltpu.VMEM((Lb, GDN_CONV_CH), F32),
            pltpu.VMEM((GDN_HEADS, GDN_DK, GDN_DV), F32),
            pltpu.VMEM((n_tiles, STACK, GDN_DV), F32),
            pltpu.VMEM((2 * n_tiles, STACK, GDN_DK), BF16),
            pltpu.VMEM((n_tiles, STACK, GDN_DK), BF16),
            pltpu.VMEM((n_tiles, STACK, STACK), BF16),
            pltpu.VMEM((Lb // CHUNK, SUBLANES, LANES), F32),
        ],
        args=[proj, proj, proj, cw, gbias, galog, nw])


def _gdn_sample_kernel(qkv_ref, zg_ref, gate_ref, cst_ref, sst_ref, cw_ref, gbias_ref, galog_ref, nw_ref,
                       o_ref, cst_out_ref, sst_out_ref, xpad_ref, qkvc_ref, *, nb, L):
    R = nb * L
    sh = L.bit_length() - 1
    for bi in range(nb):
        xp = xpad_ref.at[bi]
        xp[CONV_HIST:CONV_PAD, :] = cst_ref[:, bi, :]
        cst_out_ref[:, bi, :] = _conv_block(
            qkv_ref.at[pl.ds(bi * L, L)], xp, cw_ref, None, qkvc_ref.at[pl.ds(bi * L, L)],
            L, GDN_CONV_CH, _gdn_qk_post)

    row_i = lax.broadcasted_iota(jnp.int32, (R, R), 0)
    col_i = lax.broadcasted_iota(jnp.int32, (R, R), 1)
    tril_f = (((row_i >> sh) == (col_i >> sh)) & (row_i >= col_i)).astype(F32)
    graw = gate_ref[...]
    sp = _softplus(graw + gbias_ref[...])
    beta_all = jax.nn.sigmoid(graw)
    G = jnp.dot(tril_f, -jnp.exp(galog_ref[...]) * sp, precision=_HIGHEST, preferred_element_type=F32)
    glast = [G[bi * L + L - 1:bi * L + L, :] for bi in range(nb)]
    eG = jnp.exp(G)
    eGrev = jnp.exp(jnp.concatenate([jnp.broadcast_to(x, (L, LANES)) for x in glast], axis=0) - G)
    egl = [jnp.exp(x) for x in glast]
    nw = nw_ref[...]

    n_st = R // (2 * L)

    def tiles(ref, st, off):
        return jnp.concatenate(
            [ref[st * 2 * L:(st + 1) * 2 * L, off + h * LANES:off + (h + 1) * LANES]
             for h in range(GDN_HEADS)], axis=0)

    def colstack(m, st, off):
        return jnp.concatenate(
            [jnp.broadcast_to(m[st * 2 * L:(st + 1) * 2 * L, off + h:off + h + 1], (2 * L, LANES))
             for h in range(GDN_HEADS)], axis=0)

    items = [dict(q=tiles(qkvc_ref, st, 0), k=tiles(qkvc_ref, st, GDN_QK), v=tiles(qkvc_ref, st, 2 * GDN_QK),
                  beta=colstack(beta_all, st, GATE_B), g=colstack(G, st, GATE_A),
                  eg=colstack(eG, st, GATE_A), egrev=colstack(eGrev, st, GATE_A))
             for st in range(n_st)]
    uw, qk = _gdn_local(items, L)

    groups = [(h, bi) for h in range(GDN_HEADS) for bi in range(2)]
    r = []
    for st, it in enumerate(items):
        w = uw[st][1]
        qd = it["q"] * it["eg"]
        r.append([
            _dot(jnp.concatenate([w[gi * L:(gi + 1) * L], qd[gi * L:(gi + 1) * L]], axis=0).astype(BF16),
                 sst_ref[2 * st + bi, h].astype(BF16))
            for gi, (h, bi) in enumerate(groups)])
    v_new = [uw[st][0] - jnp.concatenate([x[:L] for x in r[st]], axis=0) for st in range(n_st)]
    o = [jnp.concatenate([x[L:] for x in r[st]], axis=0)
         + _dot(qk[st].astype(BF16), v_new[st].astype(BF16)) for st in range(n_st)]
    for st, it in enumerate(items):
        kd = it["k"] * it["egrev"]
        for gi, (h, bi) in enumerate(groups):
            b = 2 * st + bi
            ga = GATE_A + h
            rs = slice(gi * L, (gi + 1) * L)
            sst_out_ref[b, h] = (sst_ref[b, h] * egl[b][:, ga:ga + 1]
                                 + _dot_tn(kd[rs].astype(BF16), v_new[st][rs].astype(BF16)))
    for st in range(n_st):
        out = (_rms_rows(o[st], nw) * _silu(tiles(zg_ref, st, 0))).astype(o_ref.dtype)
        for h in range(GDN_HEADS):
            o_ref[st * 2 * L:(st + 1) * 2 * L, h * GDN_DV:(h + 1) * GDN_DV] = out[h * 2 * L:(h + 1) * 2 * L]


def _gdn_sample_spec(proj, conv_state, S_state, cw, gbias, galog, nw, *, B, L, nb):
    R = nb * L
    const = lambda shape: pl.BlockSpec(shape, lambda i: (0,) * len(shape))
    return dict(
        in_specs=[
            pl.BlockSpec((R, GDN_CONV_CH), lambda i: (i, OFF_QKV // GDN_CONV_CH)),
            pl.BlockSpec((R, GDN_V), lambda i: (i, OFF_ZG // GDN_V)),
            pl.BlockSpec((R, LANES), lambda i: (i, OFF_GATE // LANES)),
            pl.BlockSpec((CONV_W - 1, nb, GDN_CONV_CH), lambda i: (0, i, 0)),
            pl.BlockSpec((nb, GDN_HEADS, GDN_DK, GDN_DV), lambda i: (i, 0, 0, 0)),
            const((CONV_W, GDN_CONV_CH)), const((1, LANES)), const((1, LANES)), const((1, GDN_DV)),
        ],
        out_specs=[
            pl.BlockSpec((R, GDN_V), lambda i: (i, 0)),
            pl.BlockSpec((CONV_W - 1, nb, GDN_CONV_CH), lambda i: (0, i, 0)),
            pl.BlockSpec((nb, GDN_HEADS, GDN_DK, GDN_DV), lambda i: (i, 0, 0, 0)),
        ],
        out_shape=[
            jax.ShapeDtypeStruct((B * L, GDN_V), BF16),
            jax.ShapeDtypeStruct((CONV_W - 1, B, GDN_CONV_CH), F32),
            jax.ShapeDtypeStruct((B, GDN_HEADS, GDN_DK, GDN_DV), F32),
        ],
        scratch=[
            pltpu.VMEM((nb, L + CONV_PAD, GDN_CONV_CH), F32),
            pltpu.VMEM((R, GDN_CONV_CH), F32),
        ],
        args=[proj, proj, proj, conv_state, S_state, cw, gbias, galog, nw])


N_PAIRS = SSM_HEADS // 2
PAIRS_PER_GROUP = N_PAIRS // SSM_GROUPS
GROUP_W = SSM_DI // SSM_GROUPS


def _ssm_tile(graw, gbias, nega, ld_x, ld_b, ld_c, ld_z, dcols_ref, nw_ref, st_o, get_h, set_h, glen):
    c = CHUNK
    P = SSM_P
    nseq = c // glen
    sh = glen.bit_length() - 1
    ri = lax.broadcasted_iota(jnp.int32, (c, c), 0)
    ci = lax.broadcasted_iota(jnp.int32, (c, c), 1)
    tril_f = (((ri >> sh) == (ci >> sh)) & (ri >= ci)).astype(F32)
    sp = _softplus(graw + gbias)
    acum = jnp.dot(tril_f, nega * sp, precision=_HIGHEST, preferred_element_type=F32)
    lasts = [acum[s * glen + glen - 1:(s + 1) * glen, :] for s in range(nseq)]
    alast = jnp.concatenate([jnp.broadcast_to(x, (glen, LANES)) for x in lasts], axis=0)
    dtrev = sp * jnp.exp(alast - acum)
    eal = [jnp.exp(x) for x in lasts]
    lane = lax.broadcasted_iota(jnp.int32, (c, LANES), 1)
    row = lax.broadcasted_iota(jnp.int32, (c, LANES), 0)
    m = jnp.where(lane < GATE_DT2, acum, sp)
    mt = jnp.concatenate([m, m], axis=0).T
    left = lane < P
    left_row = left[0:1]
    j = jnp.where(left, lane, lane - P)
    tril2 = ((row >> sh) == (j >> sh)) & (row >= j)
    rowh = lax.broadcasted_iota(jnp.int32, (2 * P, SSM_N), 0) < P

    def expand(mat, c0):
        return jnp.where(left, jnp.broadcast_to(mat[:, c0:c0 + 1], (c, LANES)),
                         jnp.broadcast_to(mat[:, c0 + 1:c0 + 2], (c, LANES)))

    def rowsel(base, e):
        return jnp.where(left_row, mt[base + 2 * e:base + 2 * e + 1, :], mt[base + 2 * e + 1:base + 2 * e + 2, :])

    for g in range(SSM_GROUPS):
        Bg = ld_b(g)
        Cg = ld_c(g)
        Bg16 = Bg.astype(BF16)
        Cg16 = Cg.astype(BF16)
        cb2 = _dot_nt(Cg16, jnp.concatenate([Bg16, Bg16], axis=0))
        pairs = [g * PAIRS_PER_GROUP + e4 for e4 in range(PAIRS_PER_GROUP)]
        acol = [expand(acum, GATE_DT + 2 * e) for e in pairs]
        scores16 = [
            (cb2 * jnp.exp(jnp.where(tril2, a - rowsel(GATE_DT, e), -jnp.inf)) * rowsel(GATE_DT2, e)).astype(BF16)
            for a, e in zip(acol, pairs)]
        xp = [ld_x(e) for e in pairs]
        bd16 = [jnp.concatenate([jnp.where(left, x, 0.0), jnp.where(left, 0.0, x)], axis=0).astype(BF16)
                for x in xp]
        ydiag = [_dot(s, b) for s, b in zip(scores16, bd16)]
        if nseq == 1:
            yoff = [_dot_nt(Cg16, get_h(0, e).astype(BF16)) for e in pairs]
        else:
            yoff = [jnp.concatenate(
                [_dot_nt(Cg[s * glen:(s + 1) * glen].astype(BF16), get_h(s, e).astype(BF16))
                 for s in range(nseq)], axis=0) for e in pairs]
        y = [yd + yo * jnp.exp(a) + dcols_ref[:, e * 2 * P:(e + 1) * 2 * P] * x
             for yd, yo, a, e, x in zip(ydiag, yoff, acol, pairs, xp)]
        xdr = [x * expand(dtrev, GATE_DT + 2 * e) for x, e in zip(xp, pairs)]
        for e, xd in zip(pairs, xdr):
            c0 = GATE_DT + 2 * e
            for s in range(nseq):
                rs = slice(s * glen, (s + 1) * glen)
                ealcol = jnp.where(rowh, eal[s][:, c0:c0 + 1], eal[s][:, c0 + 1:c0 + 2])
                set_h(s, e, get_h(s, e) * ealcol + _dot_tn(xd[rs].astype(BF16), Bg[rs].astype(BF16)))
        yg = jnp.concatenate(y, axis=1) * _silu(ld_z(g))
        gcols = slice(g * GROUP_W, (g + 1) * GROUP_W)
        st_o(g, _rms_rows(yg, nw_ref[:, gcols]))


def _ssm_prompt_parts(xs_ref, bc_ref, zs_ref, gate_ref, cwx_ref, cbx_ref, cwbc_ref, cbbc_ref,
                      gbias_ref, galog_ref, dcols_ref, nw_ref,
                      o_ref, cst_out_ref, hst_out_ref, xpadx_ref, xpadbc_ref, xc_ref, bcc_ref, hh_ref, *, Lb):
    c = CHUNK

    def init():
        hh_ref[...] = jnp.zeros(hh_ref.shape, F32)
        xpadx_ref[0:CONV_PAD, :] = jnp.zeros((CONV_PAD, SSM_DI), F32)
        xpadbc_ref[0:CONV_PAD, :] = jnp.zeros((CONV_PAD, 2 * SSM_BC), F32)

    def conv():
        ident = lambda s, y: y
        _conv_block(xs_ref, xpadx_ref, cwx_ref, cbx_ref, xc_ref, Lb, SSM_DI, ident)
        _conv_block(bc_ref, xpadbc_ref, cwbc_ref, cbbc_ref, bcc_ref, Lb, 2 * SSM_BC, ident)

    def set_h(s, e, val):
        hh_ref[e] = val

    def tile(ci):
        rows = slice(ci * c, (ci + 1) * c)

        def st_o(g, val):
            o_ref[rows, g * GROUP_W:(g + 1) * GROUP_W] = val.astype(o_ref.dtype)

        _ssm_tile(
            gate_ref[rows, :], gbias_ref[...], -jnp.exp(galog_ref[...]),
            lambda e: xc_ref[rows, e * LANES:(e + 1) * LANES],
            lambda g: bcc_ref[rows, g * SSM_N:(g + 1) * SSM_N],
            lambda g: bcc_ref[rows, SSM_BC + g * SSM_N:SSM_BC + (g + 1) * SSM_N],
            lambda g: zs_ref[rows, g * GROUP_W:(g + 1) * GROUP_W],
            dcols_ref, nw_ref, st_o, lambda s, e: hh_ref[e], set_h, c)

    steps = [conv] + [functools.partial(tile, ci) for ci in range(Lb // c)]

    def final():
        cst_out_ref[0, :, :SSM_DI] = xpadx_ref[CONV_HIST:CONV_PAD, :]
        cst_out_ref[0, :, SSM_DI:] = xpadbc_ref[CONV_HIST:CONV_PAD, :]
        hst_out_ref[0] = hh_ref[...]

    return init, steps, final


def _ssm_sample_kernel(xs_ref, bc_ref, zs_ref, gate_ref, cst_ref, hst_ref, cwx_ref, cbx_ref, cwbc_ref, cbbc_ref,
                       gbias_ref, galog_ref, dcols_ref, nw_ref,
                       o_ref, cst_out_ref, hst_out_ref, xpadx_ref, xpadbc_ref, xc_ref, bcc_ref, *, L):
    ident = lambda s, y: y
    for bi in range(CHUNK // L):
        rs = pl.ds(bi * L, L)
        xpx = xpadx_ref.at[bi]
        xpb = xpadbc_ref.at[bi]
        xpx[CONV_HIST:CONV_PAD, :] = cst_ref[:, bi, :SSM_DI]
        xpb[CONV_HIST:CONV_PAD, :] = cst_ref[:, bi, SSM_DI:]
        cst_out_ref[:, bi, :SSM_DI] = _conv_block(
            xs_ref.at[rs], xpx, cwx_ref, cbx_ref, xc_ref.at[rs], L, SSM_DI, ident)
        cst_out_ref[:, bi, SSM_DI:] = _conv_block(
            bc_ref.at[rs], xpb, cwbc_ref, cbbc_ref, bcc_ref.at[rs], L, 2 * SSM_BC, ident)

    def st_o(g, val):
        o_ref[:, g * GROUP_W:(g + 1) * GROUP_W] = val.astype(o_ref.dtype)

    def set_h(s, e, val):
        hst_out_ref[s, e] = val

    _ssm_tile(
        gate_ref[...], gbias_ref[...], -jnp.exp(galog_ref[...]),
        lambda e: xc_ref[:, e * LANES:(e + 1) * LANES],
        lambda g: bcc_ref[:, g * SSM_N:(g + 1) * SSM_N],
        lambda g: bcc_ref[:, SSM_BC + g * SSM_N:SSM_BC + (g + 1) * SSM_N],
        lambda g: zs_ref[:, g * GROUP_W:(g + 1) * GROUP_W],
        dcols_ref, nw_ref, st_o, lambda s, e: hst_ref[s, e], set_h, L)


def _ssm_const_specs():
    const = lambda shape: pl.BlockSpec(shape, lambda *idx: (0,) * len(shape))
    return [
        const((CONV_W, SSM_DI)), const((1, SSM_DI)), const((CONV_W, 2 * SSM_BC)), const((1, 2 * SSM_BC)),
        const((1, LANES)), const((1, LANES)), const((1, SSM_DI)), const((1, SSM_DI)),
    ]


def _ssm_prompt_spec(proj, cwx, cbx, cwbc, cbbc, gbias, galog, dcols, nw, *, B, L, Lb):
    nl = L // Lb
    row = lambda b, l: b * nl + l
    return dict(
        in_specs=[
            pl.BlockSpec((Lb, SSM_DI), lambda b, l: (row(b, l), OFF_XS // SSM_DI)),
            pl.BlockSpec((Lb, 2 * SSM_BC), lambda b, l: (row(b, l), OFF_BC // (2 * SSM_BC))),
            pl.BlockSpec((Lb, SSM_DI), lambda b, l: (row(b, l), OFF_ZS // SSM_DI)),
            pl.BlockSpec((Lb, LANES), lambda b, l: (row(b, l), OFF_GATE // LANES)),
        ] + _ssm_const_specs(),
        out_specs=[
            pl.BlockSpec((Lb, SSM_DI), lambda b, l: (row(b, l), 0)),
            pl.BlockSpec((1, CONV_W - 1, SSM_CONV_CH), lambda b, l: (b, 0, 0)),
            pl.BlockSpec((1, N_PAIRS, 2 * SSM_P, SSM_N), lambda b, l: (b, 0, 0, 0)),
        ],
        out_shape=[
            jax.ShapeDtypeStruct((B * L, SSM_DI), BF16),
            jax.ShapeDtypeStruct((B, CONV_W - 1, SSM_CONV_CH), F32),
            jax.ShapeDtypeStruct((B, N_PAIRS, 2 * SSM_P, SSM_N), F32),
        ],
        scratch=[
            pltpu.VMEM((Lb + CONV_PAD, SSM_DI), F32),
            pltpu.VMEM((Lb + CONV_PAD, 2 * SSM_BC), F32),
            pltpu.VMEM((Lb, SSM_DI), F32),
            pltpu.VMEM((Lb, 2 * SSM_BC), F32),
            pltpu.VMEM((N_PAIRS, 2 * SSM_P, SSM_N), F32),
        ],
        args=[proj, proj, proj, proj, cwx, cbx, cwbc, cbbc, gbias, galog, dcols, nw])


def _ssm_sample_spec(proj, conv_state, h_pairs, cwx, cbx, cwbc, cbbc, gbias, galog, dcols, nw, *, B, L):
    nb = CHUNK // L
    return dict(
        in_specs=[
            pl.BlockSpec((CHUNK, SSM_DI), lambda i: (i, OFF_XS // SSM_DI)),
            pl.BlockSpec((CHUNK, 2 * SSM_BC), lambda i: (i, OFF_BC // (2 * SSM_BC))),
            pl.BlockSpec((CHUNK, SSM_DI), lambda i: (i, OFF_ZS // SSM_DI)),
            pl.BlockSpec((CHUNK, LANES), lambda i: (i, OFF_GATE // LANES)),
            pl.BlockSpec((CONV_W - 1, nb, SSM_CONV_CH), lambda i: (0, i, 0)),
            pl.BlockSpec((nb, N_PAIRS, 2 * SSM_P, SSM_N), lambda i: (i, 0, 0, 0)),
        ] + _ssm_const_specs(),
        out_specs=[
            pl.BlockSpec((CHUNK, SSM_DI), lambda i: (i, 0)),
            pl.BlockSpec((CONV_W - 1, nb, SSM_CONV_CH), lambda i: (0, i, 0)),
            pl.BlockSpec((nb, N_PAIRS, 2 * SSM_P, SSM_N), lambda i: (i, 0, 0, 0)),
        ],
        out_shape=[
            jax.ShapeDtypeStruct((B * L, SSM_DI), BF16),
            jax.ShapeDtypeStruct((CONV_W - 1, B, SSM_CONV_CH), F32),
            jax.ShapeDtypeStruct((B, N_PAIRS, 2 * SSM_P, SSM_N), F32),
        ],
        scratch=[
            pltpu.VMEM((nb, L + CONV_PAD, SSM_DI), F32),
            pltpu.VMEM((nb, L + CONV_PAD, 2 * SSM_BC), F32),
            pltpu.VMEM((CHUNK, SSM_DI), F32),
            pltpu.VMEM((CHUNK, 2 * SSM_BC), F32),
        ],
        args=[proj, proj, proj, proj, conv_state, h_pairs, cwx, cbx, cwbc, cbbc, gbias, galog, dcols, nw])


def _split_refs(refs, g, s, n_cast=0):
    it = iter(refs)
    take = lambda n: [next(it) for _ in range(n)]
    g_in, s_in, c_in = take(len(g["in_specs"])), take(len(s["in_specs"])), take(n_cast)
    g_out, s_out, c_out = take(len(g["out_specs"])), take(len(s["out_specs"])), take(n_cast)
    g_scr, s_scr = take(len(g["scratch"])), take(len(s["scratch"]))
    return g_in + g_out + g_scr, s_in + s_out + s_scr, list(zip(c_in, c_out))


def _mix_prompt_kernel(*refs, g, s, n_cast, Lb):
    g_refs, s_refs, casts = _split_refs(refs, g, s, n_cast)
    for src_ref, dst_ref in casts:
        dst_ref[...] = src_ref[...].astype(BF16)
    g_init, g_steps, g_final = _gdn_prompt_parts(*g_refs, Lb=Lb)
    s_init, s_steps, s_final = _ssm_prompt_parts(*s_refs, Lb=Lb)
    l = pl.program_id(1)

    @pl.when(l == 0)
    def _init():
        g_init()
        s_init()

    for g_step, s_step in itertools.zip_longest(g_steps, s_steps):
        if g_step is not None:
            g_step()
        if s_step is not None:
            s_step()

    @pl.when(l == pl.num_programs(1) - 1)
    def _final():
        g_final()
        s_final()


def _mix_sample_kernel(*refs, g, s, nb, L):
    g_refs, s_refs, _ = _split_refs(refs, g, s)
    _gdn_sample_kernel(*g_refs, nb=nb, L=L)
    _ssm_sample_kernel(*s_refs, L=L)


def _fused_call(kernel, g, s, grid, semantics, name, cast=()):
    n_steps = functools.reduce(lambda a, b: a * b, grid)
    step = (lambda b, l: (b * grid[1] + l, 0)) if len(grid) == 2 else (lambda i: (i, 0))
    c_specs = [pl.BlockSpec((w.shape[0] // n_steps, w.shape[1]), step) for w in cast]
    c_shape = [jax.ShapeDtypeStruct(w.shape, BF16) for w in cast]
    kw = dict(n_cast=len(cast)) if cast else {}
    outs = pl.pallas_call(
        functools.partial(kernel, g={k: g[k] for k in ("in_specs", "out_specs", "scratch")},
                          s={k: s[k] for k in ("in_specs", "out_specs", "scratch")}, **kw),
        grid=grid,
        in_specs=g["in_specs"] + s["in_specs"] + c_specs,
        out_specs=g["out_specs"] + s["out_specs"] + c_specs,
        out_shape=g["out_shape"] + s["out_shape"] + c_shape,
        scratch_shapes=g["scratch"] + s["scratch"],
        compiler_params=pltpu.CompilerParams(dimension_semantics=semantics, vmem_limit_bytes=VMEM_LIMIT),
        name=name,
    )(*g["args"], *s["args"], *cast)
    n, m = len(g["out_specs"]), len(g["out_specs"]) + len(s["out_specs"])
    return (outs[:n], outs[n:m]) + ((outs[m:],) if cast else ())


def _outproj_kernel(x_ref, mg_ref, ms_ref, w_ref, o_ref):
    acc = _dot(mg_ref[...].astype(BF16), w_ref[:GDN_V, :])
    acc = acc + _dot(ms_ref[...].astype(BF16), w_ref[GDN_V:, :])
    o_ref[...] = x_ref[...] + acc


def _out_proj(x2d, mix_g, mix_s, w_out16, *, tm):
    T = x2d.shape[0]
    return pl.pallas_call(
        _outproj_kernel,
        grid=(T // tm,),
        in_specs=[
            pl.BlockSpec((tm, D_MODEL), lambda i: (i, 0)),
            pl.BlockSpec((tm, GDN_V), lambda i: (i, 0)),
            pl.BlockSpec((tm, SSM_DI), lambda i: (i, 0)),
            pl.BlockSpec((D_MODEL, D_MODEL), lambda i: (0, 0)),
        ],
        out_specs=pl.BlockSpec((tm, D_MODEL), lambda i: (i, 0)),
        out_shape=jax.ShapeDtypeStruct((T, D_MODEL), F32),
        compiler_params=pltpu.CompilerParams(
            dimension_semantics=("parallel",), vmem_limit_bytes=VMEM_LIMIT),
        name="out_proj",
    )(x2d, mix_g, mix_s, w_out16)


FFN_SUB = 4


def _ffn_kernel(x_ref, nw_ref, wg_ref, wu_ref, wd_ref, fnw_ref, o_ref, h_ref):
    f = pl.program_id(1)
    nf = pl.num_programs(1)
    rs = h_ref.shape[0] // FFN_SUB

    def step(first, last):
        def gate_up(r):
            rows = slice(r * rs, (r + 1) * rs)
            if first:
                h = _rms_rows(x_ref[rows, :], nw_ref[...]).astype(BF16)
                h_ref[rows, :] = h
            else:
                h = h_ref[rows, :]
            return _dot(h, wg_ref[...]), _dot(h, wu_ref[...])

        def down(r, gu):
            rows = slice(r * rs, (r + 1) * rs)
            d = _dot((_silu(gu[0]) * gu[1]).astype(BF16), wd_ref[...])
            acc = d if first else o_ref[rows, :] + d
            if last:
                o_ref[rows, :] = _rms_rows(x_ref[rows, :] + acc, fnw_ref[...])
            else:
                o_ref[rows, :] = acc

        gu = gate_up(0)
        for r in range(1, FFN_SUB):
            gu_next = gate_up(r)
            down(r - 1, gu)
            gu = gu_next
        down(FFN_SUB - 1, gu)

    pl.when(f == 0)(lambda: step(True, False))
    pl.when((f > 0) & (f < nf - 1))(lambda: step(False, False))
    pl.when(f == nf - 1)(lambda: step(False, True))


def _ffn(x2d, norm_w, wg16, wu16, wd16, final_w, *, tm, tf):
    T = x2d.shape[0]
    return pl.pallas_call(
        _ffn_kernel,
        grid=(T // tm, D_FF // tf),
        in_specs=[
            pl.BlockSpec((tm, D_MODEL), lambda i, f: (i, 0)),
            pl.BlockSpec((1, D_MODEL), lambda i, f: (0, 0)),
            pl.BlockSpec((D_MODEL, tf), lambda i, f: (0, f)),
            pl.BlockSpec((D_MODEL, tf), lambda i, f: (0, f)),
            pl.BlockSpec((tf, D_MODEL), lambda i, f: (f, 0)),
            pl.BlockSpec((1, D_MODEL), lambda i, f: (0, 0)),
        ],
        out_specs=pl.BlockSpec((tm, D_MODEL), lambda i, f: (i, 0)),
        out_shape=jax.ShapeDtypeStruct((T, D_MODEL), F32),
        scratch_shapes=[pltpu.VMEM((tm, D_MODEL), BF16)],
        compiler_params=pltpu.CompilerParams(
            dimension_semantics=("parallel", "arbitrary"), vmem_limit_bytes=VMEM_LIMIT),
        name="ffn",
    )(x2d, norm_w, wg16, wu16, wd16, final_w)


PROMPT_ROWS = 256


def _trunk(x, states, p):
    B, L, _ = x.shape
    x2d = x.reshape(B * L, D_MODEL)
    proj = _in_proj(x2d, p["attn_norm_w"], p["w_in_r"], tm=IN_PROJ_TM, tn=IN_PROJ_TN)
    gdn_w = (p["gdn_conv_w"], p["gbias"], p["galog"], p["gdn_norm_w"])
    ssm_w = (p["cwx"], p["cbx"], p["cwbc"], p["cbbc"], p["gbias"], p["galog"], p["dcols"], p["ssm_norm_w"])
    pair_shape = (B, N_PAIRS, 2 * SSM_P, SSM_N)
    if states is None:
        Lb = PROMPT_ROWS
        g = _gdn_prompt_spec(proj, *gdn_w, B=B, L=L, Lb=Lb)
        s = _ssm_prompt_spec(proj, *ssm_w, B=B, L=L, Lb=Lb)
        kern = functools.partial(_mix_prompt_kernel, Lb=Lb)
        f32_weights = (p["w_gate"], p["w_up"], p["w_down"], p["w_out"])
        (mix_g, gconv_new, gS_new), (mix_s, sconv_new, sh_new), bf16_weights = _fused_call(
            kern, g, s, (B, L // Lb), ("parallel", "arbitrary"), "mix_prompt", cast=f32_weights)
        p["wg16"], p["wu16"], p["wd16"], p["w_out16"] = bf16_weights
    else:
        gconv, gS, sconv, sh = states
        nb = CHUNK // L
        tap_major = lambda a: jnp.swapaxes(a, 0, 1)
        g = _gdn_sample_spec(proj, tap_major(gconv), gS, *gdn_w, B=B, L=L, nb=nb)
        s = _ssm_sample_spec(proj, tap_major(sconv), sh.reshape(pair_shape), *ssm_w, B=B, L=L)
        kern = functools.partial(_mix_sample_kernel, nb=nb, L=L)
        (mix_g, gconv_new, gS_new), (mix_s, sconv_new, sh_new) = _fused_call(
            kern, g, s, (B // nb,), ("parallel",), "mix_sample")
        gconv_new, sconv_new = tap_major(gconv_new), tap_major(sconv_new)
    sh_new = sh_new.reshape(B, SSM_HEADS, SSM_P, SSM_N)
    x1 = _out_proj(x2d, mix_g, mix_s, p["w_out16"], tm=OUT_PROJ_TM)
    y = _ffn(x1, p["ffn_norm_w"], p["wg16"], p["wu16"], p["wd16"], p["final_norm_w"], tm=FFN_TM, tf=FFN_TF)
    return y.reshape(B, L, D_MODEL), (gconv_new[None], gS_new[None], sconv_new[None], sh_new[None])


def kernel(x_prompt, x_sample, state_gdn_conv, state_gdn, state_ssm_conv, state_ssm,
           attn_norm_w, w_in, gdn_conv_w, gdn_A_log, gdn_dt_bias, gdn_norm_w,
           ssm_conv_w, ssm_conv_b, ssm_A_log, ssm_dt_bias, ssm_D, ssm_norm_w,
           w_out, ffn_norm_w, w_gate, w_up, w_down, final_norm_w):
    assert w_in.shape[0] == 1, "single-layer trunk"
    assert x_prompt.shape[1] % PROMPT_ROWS == 0 and CHUNK % x_sample.shape[1] == 0 and x_sample.shape[0] % (CHUNK // x_sample.shape[1]) == 0
    assert w_in.shape[2] == D_IN_PROJ
    w_in_r = _w_in_prep(jnp.swapaxes(w_in, 1, 2), tk=W_PREP_TK)
    zeros8 = jnp.zeros((GDN_HEADS,), F32)
    tail = jnp.zeros((LANES - GATE_DT2 - SSM_HEADS,), F32)
    gbias = jnp.concatenate([zeros8, gdn_dt_bias[0], ssm_dt_bias[0], ssm_dt_bias[0], tail])[None]
    galog = jnp.concatenate([zeros8, gdn_A_log[0], ssm_A_log[0], ssm_A_log[0], tail])[None]
    p = dict(
        attn_norm_w=attn_norm_w, w_in_r=w_in_r, gdn_conv_w=gdn_conv_w[0], gbias=gbias, galog=galog,
        gdn_norm_w=gdn_norm_w,
        cwx=ssm_conv_w[0][:, :SSM_DI], cbx=ssm_conv_b[:, :SSM_DI],
        cwbc=ssm_conv_w[0][:, SSM_DI:], cbbc=ssm_conv_b[:, SSM_DI:],
        dcols=jnp.repeat(ssm_D[0], SSM_P)[None], ssm_norm_w=ssm_norm_w,
        w_out=w_out[0], w_down=w_down[0], ffn_norm_w=ffn_norm_w,
        w_gate=w_gate[0], w_up=w_up[0],
        final_norm_w=final_norm_w[None],
    )
    y_p, st_p = _trunk(x_prompt, None, p)
    y_s, st_s = _trunk(x_sample, (state_gdn_conv[0], state_gdn[0], state_ssm_conv[0], state_ssm[0]), p)
    return (y_p, y_s, st_p[0], st_p[1], st_p[2], st_p[3], st_s[0], st_s[1], st_s[2], st_s[3])
```

```python
import functools
import itertools

import jax
import jax.numpy as jnp
from jax import lax
from jax.experimental import pallas as pl
from jax.experimental.pallas import tpu as pltpu

F32 = jnp.float32
BF16 = jnp.bfloat16

D_MODEL = 2048
GDN_HEADS = 8
GDN_DK = 128
GDN_DV = 128
GDN_QK = GDN_HEADS * GDN_DK
GDN_V = GDN_HEADS * GDN_DV
GDN_CONV_CH = 2 * GDN_QK + GDN_V
SSM_P = 64
SSM_N = 128
SSM_GROUPS = 2
SSM_DI = 1024
SSM_HEADS = SSM_DI // SSM_P
SSM_BC = SSM_GROUPS * SSM_N
SSM_CONV_CH = SSM_DI + 2 * SSM_BC
CONV_W = 4
CHUNK = 64
D_FF = 5632
EPS = 1e-6

OFF_QKV = 0
OFF_ZG = OFF_QKV + GDN_CONV_CH
OFF_ZS = OFF_ZG + GDN_V
OFF_XS = OFF_ZS + SSM_DI
OFF_BC = OFF_XS + SSM_DI
OFF_GATE = OFF_BC + 2 * SSM_BC
LANES = 128
SUBLANES = 8
CONV_PAD = SUBLANES
CONV_HIST = CONV_PAD - (CONV_W - 1)
GATE_B = 0
GATE_A = GATE_B + GDN_HEADS
GATE_DT = GATE_A + GDN_HEADS
GATE_DT2 = GATE_DT + SSM_HEADS
N_PROJ = 6912

VMEM_LIMIT = 52 * 1024 * 1024
W_PREP_TK = 256
IN_PROJ_TM, IN_PROJ_TN = 1024, 768
OUT_PROJ_TM = 512
FFN_TM, FFN_TF = 1024, 512

_HIGHEST = lax.Precision.HIGHEST


def _silu(x):
    h = 0.5 * x
    return h + h * jnp.tanh(h)


def _softplus(x):
    return jnp.maximum(x, 0.0) + jnp.log1p(jnp.exp(-jnp.abs(x)))


def _dot(a, b):
    return jnp.dot(a, b, preferred_element_type=F32)


def _dot_nt(a, b):
    return lax.dot_general(a, b, (((1,), (1,)), ((), ())), preferred_element_type=F32)


def _dot_tn(a, b):
    return lax.dot_general(a, b, (((0,), (0,)), ((), ())), preferred_element_type=F32)


def _rms_rows(x, w):
    return x * lax.rsqrt(jnp.mean(x * x, axis=-1, keepdims=True) + EPS) * w


W_B = OFF_ZS
W_ZS = W_B + 2 * GDN_HEADS
W_DT = W_ZS + SSM_DI + SSM_CONV_CH
D_IN_PROJ = W_DT + SSM_HEADS


def _wprep_kernel(w_ref, o_ref):
    cols = o_ref.shape[1]
    o_ref[:W_B, :] = w_ref[0, :W_B, :].astype(BF16)
    o_ref[W_B:OFF_GATE, :] = w_ref[0, W_ZS:W_DT, :].astype(BF16)
    dt = w_ref[0, W_DT:D_IN_PROJ, :].astype(BF16)
    o_ref[OFF_GATE:OFF_GATE + GATE_DT, :] = w_ref[0, W_B:W_ZS, :].astype(BF16)
    o_ref[OFF_GATE + GATE_DT:OFF_GATE + GATE_DT2, :] = dt
    o_ref[OFF_GATE + GATE_DT2:OFF_GATE + GATE_DT2 + SSM_HEADS, :] = dt
    o_ref[OFF_GATE + GATE_DT2 + SSM_HEADS:, :] = jnp.zeros((N_PROJ - OFF_GATE - GATE_DT2 - SSM_HEADS, cols), BF16)


def _w_in_prep(w_in_t, *, tk):
    return pl.pallas_call(
        _wprep_kernel,
        grid=(D_MODEL // tk,),
        in_specs=[pl.BlockSpec((1, D_IN_PROJ, tk), lambda i: (0, 0, i))],
        out_specs=pl.BlockSpec((N_PROJ, tk), lambda i: (0, i)),
        out_shape=jax.ShapeDtypeStruct((N_PROJ, D_MODEL), BF16),
        compiler_params=pltpu.CompilerParams(
            dimension_semantics=("parallel",), vmem_limit_bytes=VMEM_LIMIT),
        name="w_in_prep",
    )(w_in_t)


INPROJ_SUB = 4


def _inproj_kernel(x_ref, nw_ref, wt_ref, o_ref, h_ref):
    j = pl.program_id(1)

    @pl.when(j == 0)
    def _first():
        rs = h_ref.shape[0] // INPROJ_SUB
        for r in range(INPROJ_SUB):
            rows = slice(r * rs, (r + 1) * rs)
            h = _rms_rows(x_ref[rows, :], nw_ref[...]).astype(BF16)
            h_ref[rows, :] = h
            o_ref[rows, :] = _dot_nt(h, wt_ref[...])

    @pl.when(j > 0)
    def _rest():
        o_ref[...] = _dot_nt(h_ref[...], wt_ref[...])


def _in_proj(x2d, norm_w, w_in_r, *, tm, tn):
    T = x2d.shape[0]
    return pl.pallas_call(
        _inproj_kernel,
        grid=(T // tm, N_PROJ // tn),
        in_specs=[
            pl.BlockSpec((tm, D_MODEL), lambda i, j: (i, 0)),
            pl.BlockSpec((1, D_MODEL), lambda i, j: (0, 0)),
            pl.BlockSpec((tn, D_MODEL), lambda i, j: (j, 0)),
        ],
        out_specs=pl.BlockSpec((tm, tn), lambda i, j: (i, j)),
        out_shape=jax.ShapeDtypeStruct((T, N_PROJ), F32),
        scratch_shapes=[pltpu.VMEM((tm, D_MODEL), BF16)],
        compiler_params=pltpu.CompilerParams(
            dimension_semantics=("parallel", "arbitrary"), vmem_limit_bytes=VMEM_LIMIT),
        name="in_proj",
    )(x2d, norm_w, w_in_r)


def _conv_block(x_ref, xpad_ref, cw_ref, cb_ref, dst_ref, Lb, C, post):
    xpad_ref[CONV_PAD:CONV_PAD + Lb, :] = x_ref[...]
    rs = min(Lb, CHUNK)
    for sb in range(Lb // rs):
        r = sb * rs
        for s in range(C // LANES):
            cols = slice(s * LANES, (s + 1) * LANES)
            acc = xpad_ref[CONV_PAD + r:CONV_PAD + r + rs, cols] * cw_ref[CONV_W - 1:CONV_W, cols]
            for i in range(CONV_W - 1):
                acc = acc + xpad_ref[CONV_HIST + i + r:CONV_HIST + i + r + rs, cols] * cw_ref[i:i + 1, cols]
            if cb_ref is not None:
                acc = acc + cb_ref[:, cols]
            dst_ref[r:r + rs, cols] = post(s, _silu(acc))
    hist = xpad_ref[Lb + CONV_HIST:Lb + CONV_PAD, :]
    xpad_ref[CONV_HIST:CONV_PAD, :] = hist
    return hist


STACK = 128


def _gdn_qk_post(s, y):
    if s < 2 * GDN_HEADS:
        y = y * lax.rsqrt(jnp.sum(y * y, axis=-1, keepdims=True) + EPS)
        if s < GDN_HEADS:
            y = y * (GDN_DK ** -0.5)
    return y


def _gdn_local(items, glen):
    sh = glen.bit_length() - 1
    row = lax.broadcasted_iota(jnp.int32, (STACK, STACK), 0)
    col = lax.broadcasted_iota(jnp.int32, (STACK, STACK), 1)
    same = (row >> sh) == (col >> sh)
    incl = same & (row >= col)
    strict = same & (row > col)
    eye = (row == col).astype(F32)

    decay = [jnp.exp(jnp.where(incl, it["g"] - it["g"].T, -jnp.inf)) for it in items]
    kb = [it["k"] * it["beta"] for it in items]
    qkk = [_dot_nt(jnp.concatenate([it["q"], b], axis=0).astype(BF16), it["k"].astype(BF16))
           for it, b in zip(items, kb)]
    qk = [x[:STACK] * d for x, d in zip(qkk, decay)]
    nmat = [jnp.where(strict, -(x[STACK:] * d), 0.0) for x, d in zip(qkk, decay)]
    tinv = [eye + n for n in nmat]
    if sh >= 2:
        pw = [_dot(n.astype(BF16), n.astype(BF16)) for n in nmat]
        for _ in range(sh - 2):
            x = [_dot(jnp.concatenate([t, p], axis=0).astype(BF16), p.astype(BF16))
                 for t, p in zip(tinv, pw)]
            tinv = [t + y[:STACK] for t, y in zip(tinv, x)]
            pw = [y[STACK:] for y in x]
        tinv = [t + _dot(t.astype(BF16), p.astype(BF16)) for t, p in zip(tinv, pw)]
    uw = [_dot(t.astype(BF16),
               jnp.concatenate([it["v"] * it["beta"], b * it["eg"]], axis=1).astype(BF16))
          for t, it, b in zip(tinv, items, kb)]
    return [(x[:, :GDN_DV], x[:, GDN_DV:]) for x in uw], qk


GDN_LOCAL_CHUNKS = 2


def _gdn_prompt_parts(qkv_ref, zg_ref, gate_ref, cw_ref, gbias_ref, galog_ref, nw_ref,
                      o_ref, cst_out_ref, sst_out_ref,
                      xpad_ref, qkvc_ref, s_ref, u_ref, wq16_ref, kd16_ref, qk16_ref, egl_ref, *, Lb):
    c = CHUNK

    def init():
        xpad_ref[0:CONV_PAD, :] = jnp.zeros((CONV_PAD, GDN_CONV_CH), F32)
        s_ref[...] = jnp.zeros(s_ref.shape, F32)

    def conv():
        _conv_block(qkv_ref, xpad_ref, cw_ref, None, qkvc_ref, Lb, GDN_CONV_CH, _gdn_qk_post)

    row_i = lax.broadcasted_iota(jnp.int32, (c, c), 0)
    col_i = lax.broadcasted_iota(jnp.int32, (c, c), 1)
    tril_f = (row_i >= col_i).astype(F32)
    gbias = gbias_ref[...]
    nega = -jnp.exp(galog_ref[...])
    nw = nw_ref[...]
    n_pairs = GDN_HEADS // 2

    def local_chunk_items(ci):
        rows = slice(ci * c, (ci + 1) * c)
        graw = gate_ref[rows, :]
        sp = _softplus(graw + gbias)
        beta_all = jax.nn.sigmoid(graw)
        G = jnp.dot(tril_f, nega * sp, precision=_HIGHEST, preferred_element_type=F32)
        glast = G[c - 1:c, :]
        egl_ref[ci] = jnp.broadcast_to(jnp.exp(glast), (SUBLANES, LANES))

        def heads(off, a, b):
            return jnp.concatenate([qkvc_ref[rows, off + a * LANES:off + (a + 1) * LANES],
                                    qkvc_ref[rows, off + b * LANES:off + (b + 1) * LANES]], axis=0)

        def colstack(m, a, b):
            return jnp.concatenate([jnp.broadcast_to(m[:, a:a + 1], (c, LANES)),
                                    jnp.broadcast_to(m[:, b:b + 1], (c, LANES))], axis=0)

        items = []
        for pr in range(n_pairs):
            a, b = 2 * pr, 2 * pr + 1
            g_tile = colstack(G, GATE_A + a, GATE_A + b)
            items.append(dict(
                q=heads(0, a, b), k=heads(GDN_QK, a, b), v=heads(2 * GDN_QK, a, b),
                beta=colstack(beta_all, GATE_B + a, GATE_B + b),
                g=g_tile, eg=jnp.exp(g_tile),
                egrev=jnp.exp(colstack(glast, GATE_A + a, GATE_A + b) - g_tile)))
        return items

    def local_group(cis):
        items = [it for ci in cis for it in local_chunk_items(ci)]
        uw, qk = _gdn_local(items, c)
        for n, it in enumerate(items):
            idx = cis[0] * n_pairs + n
            u, w = uw[n]
            qd = it["q"] * it["eg"]
            u_ref[idx] = u
            for hh in range(2):
                hs = slice(hh * c, (hh + 1) * c)
                wq16_ref[2 * idx + hh] = jnp.concatenate([w[hs], qd[hs]], axis=0).astype(BF16)
            kd16_ref[idx] = (it["k"] * it["egrev"]).astype(BF16)
            qk16_ref[idx] = qk[n].astype(BF16)

    def recurrent(ci):
        rows = slice(ci * c, (ci + 1) * c)
        egl = egl_ref[ci][0:1]
        r = [[_dot(wq16_ref[2 * (ci * n_pairs + pr) + hh], s_ref[2 * pr + hh].astype(BF16)) for hh in range(2)]
             for pr in range(n_pairs)]
        v16 = [(u_ref[ci * n_pairs + pr] - jnp.concatenate([r[pr][0][:c], r[pr][1][:c]], axis=0)).astype(BF16)
               for pr in range(n_pairs)]
        o = [jnp.concatenate([r[pr][0][c:], r[pr][1][c:]], axis=0) + _dot(qk16_ref[ci * n_pairs + pr], v16[pr])
             for pr in range(n_pairs)]
        for pr in range(n_pairs):
            kd16 = kd16_ref[ci * n_pairs + pr]
            for hh in range(2):
                h = 2 * pr + hh
                ga = GATE_A + h
                hs = slice(hh * c, (hh + 1) * c)
                s_ref[h] = s_ref[h] * egl[:, ga:ga + 1] + _dot_tn(kd16[hs], v16[pr][hs])
        for pr in range(n_pairs):
            for hh in range(2):
                h = 2 * pr + hh
                z = zg_ref[rows, h * GDN_DV:(h + 1) * GDN_DV]
                o_ref[rows, h * GDN_DV:(h + 1) * GDN_DV] = (
                    _rms_rows(o[pr][hh * c:(hh + 1) * c], nw) * _silu(z)).astype(o_ref.dtype)

    steps = [conv]
    for g0 in range(0, Lb // c, GDN_LOCAL_CHUNKS):
        cis = list(range(g0, g0 + GDN_LOCAL_CHUNKS))
        steps.append(functools.partial(local_group, cis))
        steps += [functools.partial(recurrent, ci) for ci in cis]

    def final():
        cst_out_ref[0] = xpad_ref[CONV_HIST:CONV_PAD, :]
        sst_out_ref[0] = s_ref[...]

    return init, steps, final


def _gdn_prompt_spec(proj, cw, gbias, galog, nw, *, B, L, Lb):
    nl = L // Lb
    n_tiles = (Lb // CHUNK) * (GDN_HEADS // 2)
    row = lambda b, l: b * nl + l
    const = lambda shape: pl.BlockSpec(shape, lambda b, l: (0,) * len(shape))
    return dict(
        in_specs=[
            pl.BlockSpec((Lb, GDN_CONV_CH), lambda b, l: (row(b, l), OFF_QKV // GDN_CONV_CH)),
            pl.BlockSpec((Lb, GDN_V), lambda b, l: (row(b, l), OFF_ZG // GDN_V)),
            pl.BlockSpec((Lb, LANES), lambda b, l: (row(b, l), OFF_GATE // LANES)),
            const((CONV_W, GDN_CONV_CH)), const((1, LANES)), const((1, LANES)), const((1, GDN_DV)),
        ],
        out_specs=[
            pl.BlockSpec((Lb, GDN_V), lambda b, l: (row(b, l), 0)),
            pl.BlockSpec((1, CONV_W - 1, GDN_CONV_CH), lambda b, l: (b, 0, 0)),
            pl.BlockSpec((1, GDN_HEADS, GDN_DK, GDN_DV), lambda b, l: (b, 0, 0, 0)),
        ],
        out_shape=[
            jax.ShapeDtypeStruct((B * L, GDN_V), BF16),
            jax.ShapeDtypeStruct((B, CONV_W - 1, GDN_CONV_CH), F32),
            jax.ShapeDtypeStruct((B, GDN_HEADS, GDN_DK, GDN_DV), F32),
        ],
        scratch=[
            pltpu.VMEM((Lb + CONV_PAD, GDN_CONV_CH), F32),
            pltpu.VMEM((Lb, GDN_CONV_CH), F32),
            pltpu.VMEM((GDN_HEADS, GDN_DK, GDN_DV), F32),
            pltpu.VMEM((n_tiles, STACK, GDN_DV), F32),
            pltpu.VMEM((2 * n_tiles, STACK, GDN_DK), BF16),
            pltpu.VMEM((n_tiles, STACK, GDN_DK), BF16),
            pltpu.VMEM((n_tiles, STACK, STACK), BF16),
            pltpu.VMEM((Lb // CHUNK, SUBLANES, LANES), F32),
        ],
        args=[proj, proj, proj, cw, gbias, galog, nw])


def _gdn_sample_kernel(qkv_ref, zg_ref, gate_ref, cst_ref, sst_ref, cw_ref, gbias_ref, galog_ref, nw_ref,
                       o_ref, cst_out_ref, sst_out_ref, xpad_ref, qkvc_ref, *, nb, L):
    R = nb * L
    sh = L.bit_length() - 1
    for bi in range(nb):
        xp = xpad_ref.at[bi]
        xp[CONV_HIST:CONV_PAD, :] = cst_ref[:, bi, :]
        cst_out_ref[:, bi, :] = _conv_block(
            qkv_ref.at[pl.ds(bi * L, L)], xp, cw_ref, None, qkvc_ref.at[pl.ds(bi * L, L)],
            L, GDN_CONV_CH, _gdn_qk_post)

    row_i = lax.broadcasted_iota(jnp.int32, (R, R), 0)
    col_i = lax.broadcasted_iota(jnp.int32, (R, R), 1)
    tril_f = (((row_i >> sh) == (col_i >> sh)) & (row_i >= col_i)).astype(F32)
    graw = gate_ref[...]
    sp = _softplus(graw + gbias_ref[...])
    beta_all = jax.nn.sigmoid(graw)
    G = jnp.dot(tril_f, -jnp.exp(galog_ref[...]) * sp, precision=_HIGHEST, preferred_element_type=F32)
    glast = [G[bi * L + L - 1:bi * L + L, :] for bi in range(nb)]
    glast_rows = jnp.concatenate([jnp.broadcast_to(x, (L, LANES)) for x in glast], axis=0)
    egl = [jnp.exp(x) for x in glast]
    nw = nw_ref[...]

    n_st = R // (2 * L)

    def tiles(ref, st, off):
        return jnp.concatenate(
            [ref[st * 2 * L:(st + 1) * 2 * L, off + h * LANES:off + (h + 1) * LANES]
             for h in range(GDN_HEADS)], axis=0)

    def colstack(m, st, off):
        return jnp.concatenate(
            [jnp.broadcast_to(m[st * 2 * L:(st + 1) * 2 * L, off + h:off + h + 1], (2 * L, LANES))
             for h in range(GDN_HEADS)], axis=0)

    items = []
    for st in range(n_st):
        g_tile = colstack(G, st, GATE_A)
        items.append(dict(
            q=tiles(qkvc_ref, st, 0), k=tiles(qkvc_ref, st, GDN_QK), v=tiles(qkvc_ref, st, 2 * GDN_QK),
            beta=colstack(beta_all, st, GATE_B), g=g_tile, eg=jnp.exp(g_tile),
            egrev=jnp.exp(colstack(glast_rows, st, GATE_A) - g_tile)))
    uw, qk = _gdn_local(items, L)

    groups = [(h, bi) for h in range(GDN_HEADS) for bi in range(2)]
    r = []
    for st, it in enumerate(items):
        w = uw[st][1]
        qd = it["q"] * it["eg"]
        r.append([
            _dot(jnp.concatenate([w[gi * L:(gi + 1) * L], qd[gi * L:(gi + 1) * L]], axis=0).astype(BF16),
                 sst_ref[2 * st + bi, h].astype(BF16))
            for gi, (h, bi) in enumerate(groups)])
    v_new = [uw[st][0] - jnp.concatenate([x[:L] for x in r[st]], axis=0) for st in range(n_st)]
    o = [jnp.concatenate([x[L:] for x in r[st]], axis=0)
         + _dot(qk[st].astype(BF16), v_new[st].astype(BF16)) for st in range(n_st)]
    for st, it in enumerate(items):
        kd = it["k"] * it["egrev"]
        for gi, (h, bi) in enumerate(groups):
            b = 2 * st + bi
            ga = GATE_A + h
            rs = slice(gi * L, (gi + 1) * L)
            sst_out_ref[b, h] = (sst_ref[b, h] * egl[b][:, ga:ga + 1]
                                 + _dot_tn(kd[rs].astype(BF16), v_new[st][rs].astype(BF16)))
    for st in range(n_st):
        out = (_rms_rows(o[st], nw) * _silu(tiles(zg_ref, st, 0))).astype(o_ref.dtype)
        for h in range(GDN_HEADS):
            o_ref[st * 2 * L:(st + 1) * 2 * L, h * GDN_DV:(h + 1) * GDN_DV] = out[h * 2 * L:(h + 1) * 2 * L]


def _gdn_sample_spec(proj, conv_state, S_state, cw, gbias, galog, nw, *, B, L, nb):
    R = nb * L
    const = lambda shape: pl.BlockSpec(shape, lambda i: (0,) * len(shape))
    return dict(
        in_specs=[
            pl.BlockSpec((R, GDN_CONV_CH), lambda i: (i, OFF_QKV // GDN_CONV_CH)),
            pl.BlockSpec((R, GDN_V), lambda i: (i, OFF_ZG // GDN_V)),
            pl.BlockSpec((R, LANES), lambda i: (i, OFF_GATE // LANES)),
            pl.BlockSpec((CONV_W - 1, nb, GDN_CONV_CH), lambda i: (0, i, 0)),
            pl.BlockSpec((nb, GDN_HEADS, GDN_DK, GDN_DV), lambda i: (i, 0, 0, 0)),
            const((CONV_W, GDN_CONV_CH)), const((1, LANES)), const((1, LANES)), const((1, GDN_DV)),
        ],
        out_specs=[
            pl.BlockSpec((R, GDN_V), lambda i: (i, 0)),
            pl.BlockSpec((CONV_W - 1, nb, GDN_CONV_CH), lambda i: (0, i, 0)),
            pl.BlockSpec((nb, GDN_HEADS, GDN_DK, GDN_DV), lambda i: (i, 0, 0, 0)),
        ],
        out_shape=[
            jax.ShapeDtypeStruct((B * L, GDN_V), BF16),
            jax.ShapeDtypeStruct((CONV_W - 1, B, GDN_CONV_CH), F32),
            jax.ShapeDtypeStruct((B, GDN_HEADS, GDN_DK, GDN_DV), F32),
        ],
        scratch=[
            pltpu.VMEM((nb, L + CONV_PAD, GDN_CONV_CH), F32),
            pltpu.VMEM((R, GDN_CONV_CH), F32),
        ],
        args=[proj, proj, proj, conv_state, S_state, cw, gbias, galog, nw])


N_PAIRS = SSM_HEADS // 2
PAIRS_PER_GROUP = N_PAIRS // SSM_GROUPS
GROUP_W = SSM_DI // SSM_GROUPS


def _ssm_tile(graw, gbias, nega, ld_x, ld_b, ld_c, ld_z, dcols_ref, nw_ref, st_o, get_h, set_h, glen):
    c = CHUNK
    P = SSM_P
    nseq = c // glen
    sh = glen.bit_length() - 1
    ri = lax.broadcasted_iota(jnp.int32, (c, c), 0)
    ci = lax.broadcasted_iota(jnp.int32, (c, c), 1)
    tril_f = (((ri >> sh) == (ci >> sh)) & (ri >= ci)).astype(F32)
    sp = _softplus(graw + gbias)
    acum = jnp.dot(tril_f, nega * sp, precision=_HIGHEST, preferred_element_type=F32)
    lasts = [acum[s * glen + glen - 1:(s + 1) * glen, :] for s in range(nseq)]
    alast = jnp.concatenate([jnp.broadcast_to(x, (glen, LANES)) for x in lasts], axis=0)
    dtrev = sp * jnp.exp(alast - acum)
    eal = [jnp.exp(x) for x in lasts]
    lane = lax.broadcasted_iota(jnp.int32, (c, LANES), 1)
    row = lax.broadcasted_iota(jnp.int32, (c, LANES), 0)
    m = jnp.where(lane < GATE_DT2, acum, sp)
    mt = jnp.concatenate([m, m], axis=0).T
    left = lane < P
    left_row = left[0:1]
    j = jnp.where(left, lane, lane - P)
    tril2 = ((row >> sh) == (j >> sh)) & (row >= j)
    rowh = lax.broadcasted_iota(jnp.int32, (2 * P, SSM_N), 0) < P

    def expand(mat, c0):
        return jnp.where(left, jnp.broadcast_to(mat[:, c0:c0 + 1], (c, LANES)),
                         jnp.broadcast_to(mat[:, c0 + 1:c0 + 2], (c, LANES)))

    def rowsel(base, e):
        return jnp.where(left_row, mt[base + 2 * e:base + 2 * e + 1, :], mt[base + 2 * e + 1:base + 2 * e + 2, :])

    for g in range(SSM_GROUPS):
        Bg = ld_b(g)
        Cg = ld_c(g)
        Bg16 = Bg.astype(BF16)
        Cg16 = Cg.astype(BF16)
        cb2 = _dot_nt(Cg16, jnp.concatenate([Bg16, Bg16], axis=0))
        pairs = [g * PAIRS_PER_GROUP + e4 for e4 in range(PAIRS_PER_GROUP)]
        acol = [expand(acum, GATE_DT + 2 * e) for e in pairs]
        scores16 = [
            (cb2 * jnp.exp(jnp.where(tril2, a - rowsel(GATE_DT, e), -jnp.inf)) * rowsel(GATE_DT2, e)).astype(BF16)
            for a, e in zip(acol, pairs)]
        xp = [ld_x(e) for e in pairs]
        bd16 = [jnp.concatenate([jnp.where(left, x, 0.0), jnp.where(left, 0.0, x)], axis=0).astype(BF16)
                for x in xp]
        ydiag = [_dot(s, b) for s, b in zip(scores16, bd16)]
        if nseq == 1:
            yoff = [_dot_nt(Cg16, get_h(0, e).astype(BF16)) for e in pairs]
        else:
            yoff = [jnp.concatenate(
                [_dot_nt(Cg[s * glen:(s + 1) * glen].astype(BF16), get_h(s, e).astype(BF16))
                 for s in range(nseq)], axis=0) for e in pairs]
        y = [yd + yo * jnp.exp(a) + dcols_ref[:, e * 2 * P:(e + 1) * 2 * P] * x
             for yd, yo, a, e, x in zip(ydiag, yoff, acol, pairs, xp)]
        xdr = [x * expand(dtrev, GATE_DT + 2 * e) for x, e in zip(xp, pairs)]
        for e, xd in zip(pairs, xdr):
            c0 = GATE_DT + 2 * e
            for s in range(nseq):
                rs = slice(s * glen, (s + 1) * glen)
                ealcol = jnp.where(rowh, eal[s][:, c0:c0 + 1], eal[s][:, c0 + 1:c0 + 2])
                set_h(s, e, get_h(s, e) * ealcol + _dot_tn(xd[rs].astype(BF16), Bg[rs].astype(BF16)))
        yg = jnp.concatenate(y, axis=1) * _silu(ld_z(g))
        gcols = slice(g * GROUP_W, (g + 1) * GROUP_W)
        st_o(g, _rms_rows(yg, nw_ref[:, gcols]))


def _ssm_prompt_parts(xs_ref, bc_ref, zs_ref, gate_ref, cwx_ref, cbx_ref, cwbc_ref, cbbc_ref,
                      gbias_ref, galog_ref, dcols_ref, nw_ref,
                      o_ref, cst_out_ref, hst_out_ref, xpadx_ref, xpadbc_ref, xc_ref, bcc_ref, hh_ref, *, Lb):
    c = CHUNK

    def init():
        hh_ref[...] = jnp.zeros(hh_ref.shape, F32)
        xpadx_ref[0:CONV_PAD, :] = jnp.zeros((CONV_PAD, SSM_DI), F32)
        xpadbc_ref[0:CONV_PAD, :] = jnp.zeros((CONV_PAD, 2 * SSM_BC), F32)

    def conv():
        ident = lambda s, y: y
        _conv_block(xs_ref, xpadx_ref, cwx_ref, cbx_ref, xc_ref, Lb, SSM_DI, ident)
        _conv_block(bc_ref, xpadbc_ref, cwbc_ref, cbbc_ref, bcc_ref, Lb, 2 * SSM_BC, ident)

    def set_h(s, e, val):
        hh_ref[e] = val

    def tile(ci):
        rows = slice(ci * c, (ci + 1) * c)

        def st_o(g, val):
            o_ref[rows, g * GROUP_W:(g + 1) * GROUP_W] = val.astype(o_ref.dtype)

        _ssm_tile(
            gate_ref[rows, :], gbias_ref[...], -jnp.exp(galog_ref[...]),
            lambda e: xc_ref[rows, e * LANES:(e + 1) * LANES],
            lambda g: bcc_ref[rows, g * SSM_N:(g + 1) * SSM_N],
            lambda g: bcc_ref[rows, SSM_BC + g * SSM_N:SSM_BC + (g + 1) * SSM_N],
            lambda g: zs_ref[rows, g * GROUP_W:(g + 1) * GROUP_W],
            dcols_ref, nw_ref, st_o, lambda s, e: hh_ref[e], set_h, c)

    steps = [conv] + [functools.partial(tile, ci) for ci in range(Lb // c)]

    def final():
        cst_out_ref[0, :, :SSM_DI] = xpadx_ref[CONV_HIST:CONV_PAD, :]
        cst_out_ref[0, :, SSM_DI:] = xpadbc_ref[CONV_HIST:CONV_PAD, :]
        hst_out_ref[0] = hh_ref[...]

    return init, steps, final


def _ssm_sample_kernel(xs_ref, bc_ref, zs_ref, gate_ref, cst_ref, hst_ref, cwx_ref, cbx_ref, cwbc_ref, cbbc_ref,
                       gbias_ref, galog_ref, dcols_ref, nw_ref,
                       o_ref, cst_out_ref, hst_out_ref, xpadx_ref, xpadbc_ref, xc_ref, bcc_ref, *, L):
    ident = lambda s, y: y
    for bi in range(CHUNK // L):
        rs = pl.ds(bi * L, L)
        xpx = xpadx_ref.at[bi]
        xpb = xpadbc_ref.at[bi]
        xpx[CONV_HIST:CONV_PAD, :] = cst_ref[:, bi, :SSM_DI]
        xpb[CONV_HIST:CONV_PAD, :] = cst_ref[:, bi, SSM_DI:]
        cst_out_ref[:, bi, :SSM_DI] = _conv_block(
            xs_ref.at[rs], xpx, cwx_ref, cbx_ref, xc_ref.at[rs], L, SSM_DI, ident)
        cst_out_ref[:, bi, SSM_DI:] = _conv_block(
            bc_ref.at[rs], xpb, cwbc_ref, cbbc_ref, bcc_ref.at[rs], L, 2 * SSM_BC, ident)

    def st_o(g, val):
        o_ref[:, g * GROUP_W:(g + 1) * GROUP_W] = val.astype(o_ref.dtype)

    def set_h(s, e, val):
        hst_out_ref[s, e] = val

    _ssm_tile(
        gate_ref[...], gbias_ref[...], -jnp.exp(galog_ref[...]),
        lambda e: xc_ref[:, e * LANES:(e + 1) * LANES],
        lambda g: bcc_ref[:, g * SSM_N:(g + 1) * SSM_N],
        lambda g: bcc_ref[:, SSM_BC + g * SSM_N:SSM_BC + (g + 1) * SSM_N],
        lambda g: zs_ref[:, g * GROUP_W:(g + 1) * GROUP_W],
        dcols_ref, nw_ref, st_o, lambda s, e: hst_ref[s, e], set_h, L)


def _ssm_const_specs():
    const = lambda shape: pl.BlockSpec(shape, lambda *idx: (0,) * len(shape))
    return [
        const((CONV_W, SSM_DI)), const((1, SSM_DI)), const((CONV_W, 2 * SSM_BC)), const((1, 2 * SSM_BC)),
        const((1, LANES)), const((1, LANES)), const((1, SSM_DI)), const((1, SSM_DI)),
    ]


def _ssm_prompt_spec(proj, cwx, cbx, cwbc, cbbc, gbias, galog, dcols, nw, *, B, L, Lb):
    nl = L // Lb
    row = lambda b, l: b * nl + l
    return dict(
        in_specs=[
            pl.BlockSpec((Lb, SSM_DI), lambda b, l: (row(b, l), OFF_XS // SSM_DI)),
            pl.BlockSpec((Lb, 2 * SSM_BC), lambda b, l: (row(b, l), OFF_BC // (2 * SSM_BC))),
            pl.BlockSpec((Lb, SSM_DI), lambda b, l: (row(b, l), OFF_ZS // SSM_DI)),
            pl.BlockSpec((Lb, LANES), lambda b, l: (row(b, l), OFF_GATE // LANES)),
        ] + _ssm_const_specs(),
        out_specs=[
            pl.BlockSpec((Lb, SSM_DI), lambda b, l: (row(b, l), 0)),
            pl.BlockSpec((1, CONV_W - 1, SSM_CONV_CH), lambda b, l: (b, 0, 0)),
            pl.BlockSpec((1, N_PAIRS, 2 * SSM_P, SSM_N), lambda b, l: (b, 0, 0, 0)),
        ],
        out_shape=[
            jax.ShapeDtypeStruct((B * L, SSM_DI), BF16),
            jax.ShapeDtypeStruct((B, CONV_W - 1, SSM_CONV_CH), F32),
            jax.ShapeDtypeStruct((B, N_PAIRS, 2 * SSM_P, SSM_N), F32),
        ],
        scratch=[
            pltpu.VMEM((Lb + CONV_PAD, SSM_DI), F32),
            pltpu.VMEM((Lb + CONV_PAD, 2 * SSM_BC), F32),
            pltpu.VMEM((Lb, SSM_DI), F32),
            pltpu.VMEM((Lb, 2 * SSM_BC), F32),
            pltpu.VMEM((N_PAIRS, 2 * SSM_P, SSM_N), F32),
        ],
        args=[proj, proj, proj, proj, cwx, cbx, cwbc, cbbc, gbias, galog, dcols, nw])


def _ssm_sample_spec(proj, conv_state, h_pairs, cwx, cbx, cwbc, cbbc, gbias, galog, dcols, nw, *, B, L):
    nb = CHUNK // L
    return dict(
        in_specs=[
            pl.BlockSpec((CHUNK, SSM_DI), lambda i: (i, OFF_XS // SSM_DI)),
            pl.BlockSpec((CHUNK, 2 * SSM_BC), lambda i: (i, OFF_BC // (2 * SSM_BC))),
            pl.BlockSpec((CHUNK, SSM_DI), lambda i: (i, OFF_ZS // SSM_DI)),
            pl.BlockSpec((CHUNK, LANES), lambda i: (i, OFF_GATE // LANES)),
            pl.BlockSpec((CONV_W - 1, nb, SSM_CONV_CH), lambda i: (0, i, 0)),
            pl.BlockSpec((nb, N_PAIRS, 2 * SSM_P, SSM_N), lambda i: (i, 0, 0, 0)),
        ] + _ssm_const_specs(),
        out_specs=[
            pl.BlockSpec((CHUNK, SSM_DI), lambda i: (i, 0)),
            pl.BlockSpec((CONV_W - 1, nb, SSM_CONV_CH), lambda i: (0, i, 0)),
            pl.BlockSpec((nb, N_PAIRS, 2 * SSM_P, SSM_N), lambda i: (i, 0, 0, 0)),
        ],
        out_shape=[
            jax.ShapeDtypeStruct((B * L, SSM_DI), BF16),
            jax.ShapeDtypeStruct((CONV_W - 1, B, SSM_CONV_CH), F32),
            jax.ShapeDtypeStruct((B, N_PAIRS, 2 * SSM_P, SSM_N), F32),
        ],
        scratch=[
            pltpu.VMEM((nb, L + CONV_PAD, SSM_DI), F32),
            pltpu.VMEM((nb, L + CONV_PAD, 2 * SSM_BC), F32),
            pltpu.VMEM((CHUNK, SSM_DI), F32),
            pltpu.VMEM((CHUNK, 2 * SSM_BC), F32),
        ],
        args=[proj, proj, proj, proj, conv_state, h_pairs, cwx, cbx, cwbc, cbbc, gbias, galog, dcols, nw])


def _split_refs(refs, g, s, n_cast=0):
    it = iter(refs)
    take = lambda n: [next(it) for _ in range(n)]
    g_in, s_in, c_in = take(len(g["in_specs"])), take(len(s["in_specs"])), take(n_cast)
    g_out, s_out, c_out = take(len(g["out_specs"])), take(len(s["out_specs"])), take(n_cast)
    g_scr, s_scr = take(len(g["scratch"])), take(len(s["scratch"]))
    return g_in + g_out + g_scr, s_in + s_out + s_scr, list(zip(c_in, c_out))


def _mix_prompt_kernel(*refs, g, s, n_cast, Lb):
    g_refs, s_refs, casts = _split_refs(refs, g, s, n_cast)
    def cast_step(src_ref, dst_ref):
        dst_ref[...] = src_ref[...].astype(BF16)

    c_steps = [functools.partial(cast_step, a, b) for a, b in casts]
    g_init, g_steps, g_final = _gdn_prompt_parts(*g_refs, Lb=Lb)
    s_init, s_steps, s_final = _ssm_prompt_parts(*s_refs, Lb=Lb)
    l = pl.program_id(1)

    @pl.when(l == 0)
    def _init():
        g_init()
        s_init()

    for g_step, s_step, c_step in itertools.zip_longest(g_steps, s_steps, [None, None] + c_steps):
        for step in (g_step, c_step, s_step):
            if step is not None:
                step()

    @pl.when(l == pl.num_programs(1) - 1)
    def _final():
        g_final()
        s_final()


def _mix_sample_kernel(*refs, g, s, nb, L):
    g_refs, s_refs, _ = _split_refs(refs, g, s)
    _gdn_sample_kernel(*g_refs, nb=nb, L=L)
    _ssm_sample_kernel(*s_refs, L=L)


def _fused_call(kernel, g, s, grid, semantics, name, cast=()):
    n_steps = functools.reduce(lambda a, b: a * b, grid)
    step = (lambda b, l: (b * grid[1] + l, 0)) if len(grid) == 2 else (lambda i: (i, 0))
    assert all(w.shape[0] % (16 * n_steps) == 0 for w in cast), "cast row blocks must be whole bf16 tiles"
    c_specs = [pl.BlockSpec((w.shape[0] // n_steps, w.shape[1]), step) for w in cast]
    c_shape = [jax.ShapeDtypeStruct(w.shape, BF16) for w in cast]
    kw = dict(n_cast=len(cast)) if cast else {}
    outs = pl.pallas_call(
        functools.partial(kernel, g={k: g[k] for k in ("in_specs", "out_specs", "scratch")},
                          s={k: s[k] for k in ("in_specs", "out_specs", "scratch")}, **kw),
        grid=grid,
        in_specs=g["in_specs"] + s["in_specs"] + c_specs,
        out_specs=g["out_specs"] + s["out_specs"] + c_specs,
        out_shape=g["out_shape"] + s["out_shape"] + c_shape,
        scratch_shapes=g["scratch"] + s["scratch"],
        compiler_params=pltpu.CompilerParams(dimension_semantics=semantics, vmem_limit_bytes=VMEM_LIMIT),
        name=name,
    )(*g["args"], *s["args"], *cast)
    n, m = len(g["out_specs"]), len(g["out_specs"]) + len(s["out_specs"])
    return (outs[:n], outs[n:m]) + ((outs[m:],) if cast else ())


def _outproj_kernel(x_ref, mg_ref, ms_ref, w_ref, o_ref):
    acc = _dot(mg_ref[...].astype(BF16), w_ref[:GDN_V, :])
    acc = acc + _dot(ms_ref[...].astype(BF16), w_ref[GDN_V:, :])
    o_ref[...] = x_ref[...] + acc


def _out_proj(x2d, mix_g, mix_s, w_out16, *, tm):
    T = x2d.shape[0]
    return pl.pallas_call(
        _outproj_kernel,
        grid=(T // tm,),
        in_specs=[
            pl.BlockSpec((tm, D_MODEL), lambda i: (i, 0)),
            pl.BlockSpec((tm, GDN_V), lambda i: (i, 0)),
            pl.BlockSpec((tm, SSM_DI), lambda i: (i, 0)),
            pl.BlockSpec((D_MODEL, D_MODEL), lambda i: (0, 0)),
        ],
        out_specs=pl.BlockSpec((tm, D_MODEL), lambda i: (i, 0)),
        out_shape=jax.ShapeDtypeStruct((T, D_MODEL), F32),
        compiler_params=pltpu.CompilerParams(
            dimension_semantics=("parallel",), vmem_limit_bytes=VMEM_LIMIT),
        name="out_proj",
    )(x2d, mix_g, mix_s, w_out16)


FFN_SUB = 4


def _ffn_kernel(x_ref, nw_ref, wg_ref, wu_ref, wd_ref, fnw_ref, o_ref, h_ref):
    f = pl.program_id(1)
    nf = pl.num_programs(1)
    rs = h_ref.shape[0] // FFN_SUB

    def step(first, last):
        def gate_up(r):
            rows = slice(r * rs, (r + 1) * rs)
            if first:
                h = _rms_rows(x_ref[rows, :], nw_ref[...]).astype(BF16)
                h_ref[rows, :] = h
            else:
                h = h_ref[rows, :]
            return _dot(h, wg_ref[...]), _dot(h, wu_ref[...])

        def down(r, gu):
            rows = slice(r * rs, (r + 1) * rs)
            d = _dot((_silu(gu[0]) * gu[1]).astype(BF16), wd_ref[...])
            acc = d if first else o_ref[rows, :] + d
            if last:
                o_ref[rows, :] = _rms_rows(x_ref[rows, :] + acc, fnw_ref[...])
            else:
                o_ref[rows, :] = acc

        gu = gate_up(0)
        for r in range(1, FFN_SUB):
            gu_next = gate_up(r)
            down(r - 1, gu)
            gu = gu_next
        down(FFN_SUB - 1, gu)

    pl.when(f == 0)(lambda: step(True, False))
    pl.when((f > 0) & (f < nf - 1))(lambda: step(False, False))
    pl.when(f == nf - 1)(lambda: step(False, True))


def _ffn(x2d, norm_w, wg16, wu16, wd16, final_w, *, tm, tf):
    T = x2d.shape[0]
    return pl.pallas_call(
        _ffn_kernel,
        grid=(T // tm, D_FF // tf),
        in_specs=[
            pl.BlockSpec((tm, D_MODEL), lambda i, f: (i, 0)),
            pl.BlockSpec((1, D_MODEL), lambda i, f: (0, 0)),
            pl.BlockSpec((D_MODEL, tf), lambda i, f: (0, f)),
            pl.BlockSpec((D_MODEL, tf), lambda i, f: (0, f)),
            pl.BlockSpec((tf, D_MODEL), lambda i, f: (f, 0)),
            pl.BlockSpec((1, D_MODEL), lambda i, f: (0, 0)),
        ],
        out_specs=pl.BlockSpec((tm, D_MODEL), lambda i, f: (i, 0)),
        out_shape=jax.ShapeDtypeStruct((T, D_MODEL), F32),
        scratch_shapes=[pltpu.VMEM((tm, D_MODEL), BF16)],
        compiler_params=pltpu.CompilerParams(
            dimension_semantics=("parallel", "arbitrary"), vmem_limit_bytes=VMEM_LIMIT),
        name="ffn",
    )(x2d, norm_w, wg16, wu16, wd16, final_w)


PROMPT_ROWS = 256


def _trunk(x, states, p):
    B, L, _ = x.shape
    x2d = x.reshape(B * L, D_MODEL)
    proj = _in_proj(x2d, p["attn_norm_w"], p["w_in_r"], tm=IN_PROJ_TM, tn=IN_PROJ_TN)
    gdn_w = (p["gdn_conv_w"], p["gbias"], p["galog"], p["gdn_norm_w"])
    ssm_w = (p["cwx"], p["cbx"], p["cwbc"], p["cbbc"], p["gbias"], p["galog"], p["dcols"], p["ssm_norm_w"])
    pair_shape = (B, N_PAIRS, 2 * SSM_P, SSM_N)
    if states is None:
        Lb = PROMPT_ROWS
        g = _gdn_prompt_spec(proj, *gdn_w, B=B, L=L, Lb=Lb)
        s = _ssm_prompt_spec(proj, *ssm_w, B=B, L=L, Lb=Lb)
        kern = functools.partial(_mix_prompt_kernel, Lb=Lb)
        f32_weights = (p["w_gate"], p["w_up"], p["w_down"], p["w_out"])
        (mix_g, gconv_new, gS_new), (mix_s, sconv_new, sh_new), bf16_weights = _fused_call(
            kern, g, s, (B, L // Lb), ("parallel", "arbitrary"), "mix_prompt", cast=f32_weights)
        p["wg16"], p["wu16"], p["wd16"], p["w_out16"] = bf16_weights
    else:
        gconv, gS, sconv, sh = states
        nb = CHUNK // L
        tap_major = lambda a: jnp.swapaxes(a, 0, 1)
        g = _gdn_sample_spec(proj, tap_major(gconv), gS, *gdn_w, B=B, L=L, nb=nb)
        s = _ssm_sample_spec(proj, tap_major(sconv), sh.reshape(pair_shape), *ssm_w, B=B, L=L)
        kern = functools.partial(_mix_sample_kernel, nb=nb, L=L)
        (mix_g, gconv_new, gS_new), (mix_s, sconv_new, sh_new) = _fused_call(
            kern, g, s, (B // nb,), ("parallel",), "mix_sample")
        gconv_new, sconv_new = tap_major(gconv_new), tap_major(sconv_new)
    sh_new = sh_new.reshape(B, SSM_HEADS, SSM_P, SSM_N)
    x1 = _out_proj(x2d, mix_g, mix_s, p["w_out16"], tm=OUT_PROJ_TM)
    y = _ffn(x1, p["ffn_norm_w"], p["wg16"], p["wu16"], p["wd16"], p["final_norm_w"], tm=FFN_TM, tf=FFN_TF)
    return y.reshape(B, L, D_MODEL), (gconv_new[None], gS_new[None], sconv_new[None], sh_new[None])


def kernel(x_prompt, x_sample, state_gdn_conv, state_gdn, state_ssm_conv, state_ssm,
           attn_norm_w, w_in, gdn_conv_w, gdn_A_log, gdn_dt_bias, gdn_norm_w,
           ssm_conv_w, ssm_conv_b, ssm_A_log, ssm_dt_bias, ssm_D, ssm_norm_w,
           w_out, ffn_norm_w, w_gate, w_up, w_down, final_norm_w):
    assert w_in.shape[0] == 1, "single-layer trunk"
    assert x_prompt.shape[1] % PROMPT_ROWS == 0 and CHUNK % x_sample.shape[1] == 0 and x_sample.shape[0] % (CHUNK // x_sample.shape[1]) == 0
    assert w_in.shape[2] == D_IN_PROJ
    w_in_r = _w_in_prep(jnp.swapaxes(w_in, 1, 2), tk=W_PREP_TK)
    zeros8 = jnp.zeros((GDN_HEADS,), F32)
    tail = jnp.zeros((LANES - GATE_DT2 - SSM_HEADS,), F32)
    gbias = jnp.concatenate([zeros8, gdn_dt_bias[0], ssm_dt_bias[0], ssm_dt_bias[0], tail])[None]
    galog = jnp.concatenate([zeros8, gdn_A_log[0], ssm_A_log[0], ssm_A_log[0], tail])[None]
    p = dict(
        attn_norm_w=attn_norm_w, w_in_r=w_in_r, gdn_conv_w=gdn_conv_w[0], gbias=gbias, galog=galog,
        gdn_norm_w=gdn_norm_w,
        cwx=ssm_conv_w[0][:, :SSM_DI], cbx=ssm_conv_b[:, :SSM_DI],
        cwbc=ssm_conv_w[0][:, SSM_DI:], cbbc=ssm_conv_b[:, SSM_DI:],
        dcols=jnp.repeat(ssm_D[0], SSM_P)[None], ssm_norm_w=ssm_norm_w,
        w_out=w_out[0], w_down=w_down[0], ffn_norm_w=ffn_norm_w,
        w_gate=w_gate[0], w_up=w_up[0],
        final_norm_w=final_norm_w[None],
    )
    y_p, st_p = _trunk(x_prompt, None, p)
    y_s, st_s = _trunk(x_sample, (state_gdn_conv[0], state_gdn[0], state_ssm_conv[0], state_ssm[0]), p)
    return (y_p, y_s, st_p[0], st_p[1], st_p[2], st_p[3], st_s[0], st_s[1], st_s[2], st_s[3])
```

```python
import functools
import itertools

import jax
import jax.numpy as jnp
from jax import lax
from jax.experimental import pallas as pl
from jax.experimental.pallas import tpu as pltpu

F32 = jnp.float32
BF16 = jnp.bfloat16

D_MODEL = 2048
GDN_HEADS = 8
GDN_DK = 128
GDN_DV = 128
GDN_QK = GDN_HEADS * GDN_DK
GDN_V = GDN_HEADS * GDN_DV
GDN_CONV_CH = 2 * GDN_QK + GDN_V
SSM_P = 64
SSM_N = 128
SSM_GROUPS = 2
SSM_DI = 1024
SSM_HEADS = SSM_DI // SSM_P
SSM_BC = SSM_GROUPS * SSM_N
SSM_CONV_CH = SSM_DI + 2 * SSM_BC
CONV_W = 4
CHUNK = 64
D_FF = 5632
EPS = 1e-6

OFF_QKV = 0
OFF_ZG = OFF_QKV + GDN_CONV_CH
OFF_ZS = OFF_ZG + GDN_V
OFF_XS = OFF_ZS + SSM_DI
OFF_BC = OFF_XS + SSM_DI
OFF_GATE = OFF_BC + 2 * SSM_BC
LANES = 128
SUBLANES = 8
CONV_PAD = SUBLANES
CONV_HIST = CONV_PAD - (CONV_W - 1)
GATE_B = 0
GATE_A = GATE_B + GDN_HEADS
GATE_DT = GATE_A + GDN_HEADS
GATE_DT2 = GATE_DT + SSM_HEADS
N_PROJ = 6912

VMEM_LIMIT = 52 * 1024 * 1024
W_PREP_TK = 256
IN_PROJ_TM, IN_PROJ_TN = 1024, 768
OUT_PROJ_TM = 512
FFN_TM, FFN_TF = 1024, 512

_HIGHEST = lax.Precision.HIGHEST


def _silu(x):
    h = 0.5 * x
    return h + h * jnp.tanh(h)


def _softplus(x):
    return jnp.maximum(x, 0.0) + jnp.log1p(jnp.exp(-jnp.abs(x)))


def _dot(a, b):
    return jnp.dot(a, b, preferred_element_type=F32)


def _dot_nt(a, b):
    return lax.dot_general(a, b, (((1,), (1,)), ((), ())), preferred_element_type=F32)


def _dot_tn(a, b):
    return lax.dot_general(a, b, (((0,), (0,)), ((), ())), preferred_element_type=F32)


def _rms_rows(x, w):
    return x * lax.rsqrt(jnp.mean(x * x, axis=-1, keepdims=True) + EPS) * w


W_B = OFF_ZS
W_ZS = W_B + 2 * GDN_HEADS
W_DT = W_ZS + SSM_DI + SSM_CONV_CH
D_IN_PROJ = W_DT + SSM_HEADS


def _wprep_kernel(w_ref, o_ref):
    cols = o_ref.shape[1]
    o_ref[:W_B, :] = w_ref[0, :W_B, :].astype(BF16)
    o_ref[W_B:OFF_GATE, :] = w_ref[0, W_ZS:W_DT, :].astype(BF16)
    dt = w_ref[0, W_DT:D_IN_PROJ, :].astype(BF16)
    o_ref[OFF_GATE:OFF_GATE + GATE_DT, :] = w_ref[0, W_B:W_ZS, :].astype(BF16)
    o_ref[OFF_GATE + GATE_DT:OFF_GATE + GATE_DT2, :] = dt
    o_ref[OFF_GATE + GATE_DT2:OFF_GATE + GATE_DT2 + SSM_HEADS, :] = dt
    o_ref[OFF_GATE + GATE_DT2 + SSM_HEADS:, :] = jnp.zeros((N_PROJ - OFF_GATE - GATE_DT2 - SSM_HEADS, cols), BF16)


def _w_in_prep(w_in_t, *, tk):
    return pl.pallas_call(
        _wprep_kernel,
        grid=(D_MODEL // tk,),
        in_specs=[pl.BlockSpec((1, D_IN_PROJ, tk), lambda i: (0, 0, i))],
        out_specs=pl.BlockSpec((N_PROJ, tk), lambda i: (0, i)),
        out_shape=jax.ShapeDtypeStruct((N_PROJ, D_MODEL), BF16),
        compiler_params=pltpu.CompilerParams(
            dimension_semantics=("parallel",), vmem_limit_bytes=VMEM_LIMIT),
        name="w_in_prep",
    )(w_in_t)


INPROJ_SUB = 4


def _inproj_kernel(x_ref, nw_ref, wt_ref, o_ref, h_ref):
    j = pl.program_id(1)

    @pl.when(j == 0)
    def _first():
        rs = h_ref.shape[0] // INPROJ_SUB
        for r in range(INPROJ_SUB):
            rows = slice(r * rs, (r + 1) * rs)
            h = _rms_rows(x_ref[rows, :], nw_ref[...]).astype(BF16)
            h_ref[rows, :] = h
            o_ref[rows, :] = _dot_nt(h, wt_ref[...])

    @pl.when(j > 0)
    def _rest():
        o_ref[...] = _dot_nt(h_ref[...], wt_ref[...])


def _in_proj(x2d, norm_w, w_in_r, *, tm, tn):
    T = x2d.shape[0]
    return pl.pallas_call(
        _inproj_kernel,
        grid=(T // tm, N_PROJ // tn),
        in_specs=[
            pl.BlockSpec((tm, D_MODEL), lambda i, j: (i, 0)),
            pl.BlockSpec((1, D_MODEL), lambda i, j: (0, 0)),
            pl.BlockSpec((tn, D_MODEL), lambda i, j: (j, 0)),
        ],
        out_specs=pl.BlockSpec((tm, tn), lambda i, j: (i, j)),
        out_shape=jax.ShapeDtypeStruct((T, N_PROJ), F32),
        scratch_shapes=[pltpu.VMEM((tm, D_MODEL), BF16)],
        compiler_params=pltpu.CompilerParams(
            dimension_semantics=("parallel", "arbitrary"), vmem_limit_bytes=VMEM_LIMIT),
        name="in_proj",
    )(x2d, norm_w, w_in_r)


def _conv_block(x_ref, xpad_ref, cw_ref, cb_ref, dst_ref, Lb, C, post):
    xpad_ref[CONV_PAD:CONV_PAD + Lb, :] = x_ref[...]
    rs = min(Lb, CHUNK)
    for sb in range(Lb // rs):
        r = sb * rs
        for s in range(C // LANES):
            cols = slice(s * LANES, (s + 1) * LANES)
            acc = xpad_ref[CONV_PAD + r:CONV_PAD + r + rs, cols] * cw_ref[CONV_W - 1:CONV_W, cols]
            for i in range(CONV_W - 1):
                acc = acc + xpad_ref[CONV_HIST + i + r:CONV_HIST + i + r + rs, cols] * cw_ref[i:i + 1, cols]
            if cb_ref is not None:
                acc = acc + cb_ref[:, cols]
            dst_ref[r:r + rs, cols] = post(s, _silu(acc))
    hist = xpad_ref[Lb + CONV_HIST:Lb + CONV_PAD, :]
    xpad_ref[CONV_HIST:CONV_PAD, :] = hist
    return hist


STACK = 128


def _gdn_qk_post(s, y):
    if s < 2 * GDN_HEADS:
        y = y * lax.rsqrt(jnp.sum(y * y, axis=-1, keepdims=True) + EPS)
        if s < GDN_HEADS:
            y = y * (GDN_DK ** -0.5)
    return y


def _gdn_local(items, glen):
    sh = glen.bit_length() - 1
    row = lax.broadcasted_iota(jnp.int32, (STACK, STACK), 0)
    col = lax.broadcasted_iota(jnp.int32, (STACK, STACK), 1)
    same = (row >> sh) == (col >> sh)
    incl = same & (row >= col)
    strict = same & (row > col)
    eye = (row == col).astype(F32)

    g_rows = [it["g_row"] if "g_row" in it else it["g"].T for it in items]
    decay = [jnp.exp(jnp.where(incl, it["g"] - gr, -jnp.inf)) for it, gr in zip(items, g_rows)]
    kb = [it["k"] * it["beta"] for it in items]
    qkk = [_dot_nt(jnp.concatenate([it["q"], b], axis=0).astype(BF16), it["k"].astype(BF16))
           for it, b in zip(items, kb)]
    qk = [x[:STACK] * d for x, d in zip(qkk, decay)]
    nmat = [jnp.where(strict, -(x[STACK:] * d), 0.0) for x, d in zip(qkk, decay)]
    tinv = [eye + n for n in nmat]
    if sh >= 2:
        pw = [_dot(n.astype(BF16), n.astype(BF16)) for n in nmat]
        for _ in range(sh - 2):
            x = [_dot(jnp.concatenate([t, p], axis=0).astype(BF16), p.astype(BF16))
                 for t, p in zip(tinv, pw)]
            tinv = [t + y[:STACK] for t, y in zip(tinv, x)]
            pw = [y[STACK:] for y in x]
        tinv = [t + _dot(t.astype(BF16), p.astype(BF16)) for t, p in zip(tinv, pw)]
    uw = [_dot(t.astype(BF16),
               jnp.concatenate([it["v"] * it["beta"], b * it["eg"]], axis=1).astype(BF16))
          for t, it, b in zip(tinv, items, kb)]
    return [(x[:, :GDN_DV], x[:, GDN_DV:]) for x in uw], qk


GDN_LOCAL_CHUNKS = 2


def _gdn_prompt_parts(qkv_ref, zg_ref, gate_ref, cw_ref, gbias_ref, galog_ref, nw_ref,
                      o_ref, cst_out_ref, sst_out_ref,
                      xpad_ref, qkvc_ref, s_ref, u_ref, wq16_ref, kd16_ref, qk16_ref, egl_ref, *, Lb):
    c = CHUNK

    def init():
        xpad_ref[0:CONV_PAD, :] = jnp.zeros((CONV_PAD, GDN_CONV_CH), F32)
        s_ref[...] = jnp.zeros(s_ref.shape, F32)

    def conv():
        _conv_block(qkv_ref, xpad_ref, cw_ref, None, qkvc_ref, Lb, GDN_CONV_CH, _gdn_qk_post)

    row_i = lax.broadcasted_iota(jnp.int32, (c, c), 0)
    col_i = lax.broadcasted_iota(jnp.int32, (c, c), 1)
    tril_f = (row_i >= col_i).astype(F32)
    gbias = gbias_ref[...]
    nega = -jnp.exp(galog_ref[...])
    nw = nw_ref[...]
    n_pairs = GDN_HEADS // 2

    def local_chunk_items(ci):
        rows = slice(ci * c, (ci + 1) * c)
        graw = gate_ref[rows, :]
        sp = _softplus(graw + gbias)
        beta_all = jax.nn.sigmoid(graw)
        G = jnp.dot(tril_f, nega * sp, precision=_HIGHEST, preferred_element_type=F32)
        glast = G[c - 1:c, :]
        egl_ref[ci] = jnp.broadcast_to(jnp.exp(glast), (SUBLANES, LANES))

        def heads(off, a, b):
            return jnp.concatenate([qkvc_ref[rows, off + a * LANES:off + (a + 1) * LANES],
                                    qkvc_ref[rows, off + b * LANES:off + (b + 1) * LANES]], axis=0)

        def colstack(m, a, b):
            return jnp.concatenate([jnp.broadcast_to(m[:, a:a + 1], (c, LANES)),
                                    jnp.broadcast_to(m[:, b:b + 1], (c, LANES))], axis=0)

        gt2 = jnp.concatenate([G, G], axis=0).T
        first_head = lax.broadcasted_iota(jnp.int32, (1, STACK), 1) < c
        items = []
        for pr in range(n_pairs):
            a, b = 2 * pr, 2 * pr + 1
            g_tile = colstack(G, GATE_A + a, GATE_A + b)
            g_row = jnp.where(first_head, gt2[GATE_A + a:GATE_A + a + 1, :], gt2[GATE_A + b:GATE_A + b + 1, :])
            items.append(dict(
                q=heads(0, a, b), k=heads(GDN_QK, a, b), v=heads(2 * GDN_QK, a, b),
                beta=colstack(beta_all, GATE_B + a, GATE_B + b),
                g=g_tile, g_row=g_row, eg=jnp.exp(g_tile),
                egrev=jnp.exp(colstack(glast, GATE_A + a, GATE_A + b) - g_tile)))
        return items

    def local_group(cis):
        items = [it for ci in cis for it in local_chunk_items(ci)]
        uw, qk = _gdn_local(items, c)
        for n, it in enumerate(items):
            idx = cis[0] * n_pairs + n
            u, w = uw[n]
            qd = it["q"] * it["eg"]
            u_ref[idx] = u
            for hh in range(2):
                hs = slice(hh * c, (hh + 1) * c)
                wq16_ref[2 * idx + hh] = jnp.concatenate([w[hs], qd[hs]], axis=0).astype(BF16)
            kd16_ref[idx] = (it["k"] * it["egrev"]).astype(BF16)
            qk16_ref[idx] = qk[n].astype(BF16)

    def recurrent(ci):
        rows = slice(ci * c, (ci + 1) * c)
        egl = egl_ref[ci][0:1]
        r = [[_dot(wq16_ref[2 * (ci * n_pairs + pr) + hh], s_ref[2 * pr + hh].astype(BF16)) for hh in range(2)]
             for pr in range(n_pairs)]
        v16 = [(u_ref[ci * n_pairs + pr] - jnp.concatenate([r[pr][0][:c], r[pr][1][:c]], axis=0)).astype(BF16)
               for pr in range(n_pairs)]
        o = [jnp.concatenate([r[pr][0][c:], r[pr][1][c:]], axis=0) + _dot(qk16_ref[ci * n_pairs + pr], v16[pr])
             for pr in range(n_pairs)]
        for pr in range(n_pairs):
            kd16 = kd16_ref[ci * n_pairs + pr]
            for hh in range(2):
                h = 2 * pr + hh
                ga = GATE_A + h
                hs = slice(hh * c, (hh + 1) * c)
                s_ref[h] = s_ref[h] * egl[:, ga:ga + 1] + _dot_tn(kd16[hs], v16[pr][hs])
        for pr in range(n_pairs):
            for hh in range(2):
                h = 2 * pr + hh
                z = zg_ref[rows, h * GDN_DV:(h + 1) * GDN_DV]
                o_ref[rows, h * GDN_DV:(h + 1) * GDN_DV] = (
                    _rms_rows(o[pr][hh * c:(hh + 1) * c], nw) * _silu(z)).astype(o_ref.dtype)

    steps = [conv]
    for g0 in range(0, Lb // c, GDN_LOCAL_CHUNKS):
        cis = list(range(g0, g0 + GDN_LOCAL_CHUNKS))
        steps.append(functools.partial(local_group, cis))
        steps += [functools.partial(recurrent, ci) for ci in cis]

    def final():
        cst_out_ref[0] = xpad_ref[CONV_HIST:CONV_PAD, :]
        sst_out_ref[0] = s_ref[...]

    return init, steps, final


def _gdn_prompt_spec(proj, cw, gbias, galog, nw, *, B, L, Lb):
    nl = L // Lb
    n_tiles = (Lb // CHUNK) * (GDN_HEADS // 2)
    row = lambda b, l: b * nl + l
    const = lambda shape: pl.BlockSpec(shape, lambda b, l: (0,) * len(shape))
    return dict(
        in_specs=[
            pl.BlockSpec((Lb, GDN_CONV_CH), lambda b, l: (row(b, l), OFF_QKV // GDN_CONV_CH)),
            pl.BlockSpec((Lb, GDN_V), lambda b, l: (row(b, l), OFF_ZG // GDN_V)),
            pl.BlockSpec((Lb, LANES), lambda b, l: (row(b, l), OFF_GATE // LANES)),
            const((CONV_W, GDN_CONV_CH)), const((1, LANES)), const((1, LANES)), const((1, GDN_DV)),
        ],
        out_specs=[
            pl.BlockSpec((Lb, GDN_V), lambda b, l: (row(b, l), 0)),
            pl.BlockSpec((1, CONV_W - 1, GDN_CONV_CH), lambda b, l: (b, 0, 0)),
            pl.BlockSpec((1, GDN_HEADS, GDN_DK, GDN_DV), lambda b, l: (b, 0, 0, 0)),
        ],
        out_shape=[
            jax.ShapeDtypeStruct((B * L, GDN_V), BF16),
            jax.ShapeDtypeStruct((B, CONV_W - 1, GDN_CONV_CH), F32),
            jax.ShapeDtypeStruct((B, GDN_HEADS, GDN_DK, GDN_DV), F32),
        ],
        scratch=[
            pltpu.VMEM((Lb + CONV_PAD, GDN_CONV_CH), F32),
            pltpu.VMEM((Lb, GDN_CONV_CH), F32),
            pltpu.VMEM((GDN_HEADS, GDN_DK, GDN_DV), F32),
            pltpu.VMEM((n_tiles, STACK, GDN_DV), F32),
            pltpu.VMEM((2 * n_tiles, STACK, GDN_DK), BF16),
            pltpu.VMEM((n_tiles, STACK, GDN_DK), BF16),
            pltpu.VMEM((n_tiles, STACK, STACK), BF16),
            pltpu.VMEM((Lb // CHUNK, SUBLANES, LANES), F32),
        ],
        args=[proj, proj, proj, cw, gbias, galog, nw])


def _gdn_sample_kernel(qkv_ref, zg_ref, gate_ref, cst_ref, sst_ref, cw_ref, gbias_ref, galog_ref, nw_ref,
                       o_ref, cst_out_ref, sst_out_ref, xpad_ref, qkvc_ref, *, nb, L):
    R = nb * L
    sh = L.bit_length() - 1
    for bi in range(nb):
        xp = xpad_ref.at[bi]
        xp[CONV_HIST:CONV_PAD, :] = cst_ref[:, bi, :]
        cst_out_ref[:, bi, :] = _conv_block(
            qkv_ref.at[pl.ds(bi * L, L)], xp, cw_ref, None, qkvc_ref.at[pl.ds(bi * L, L)],
            L, GDN_CONV_CH, _gdn_qk_post)

    row_i = lax.broadcasted_iota(jnp.int32, (R, R), 0)
    col_i = lax.broadcasted_iota(jnp.int32, (R, R), 1)
    tril_f = (((row_i >> sh) == (col_i >> sh)) & (row_i >= col_i)).astype(F32)
    graw = gate_ref[...]
    sp = _softplus(graw + gbias_ref[...])
    beta_all = jax.nn.sigmoid(graw)
    G = jnp.dot(tril_f, -jnp.exp(galog_ref[...]) * sp, precision=_HIGHEST, preferred_element_type=F32)
    glast = [G[bi * L + L - 1:bi * L + L, :] for bi in range(nb)]
    glast_rows = jnp.concatenate([jnp.broadcast_to(x, (L, LANES)) for x in glast], axis=0)
    egl = [jnp.exp(x) for x in glast]
    nw = nw_ref[...]

    n_st = R // (2 * L)

    def tiles(ref, st, off):
        return jnp.concatenate(
            [ref[st * 2 * L:(st + 1) * 2 * L, off + h * LANES:off + (h + 1) * LANES]
             for h in range(GDN_HEADS)], axis=0)

    def colstack(m, st, off):
        return jnp.concatenate(
            [jnp.broadcast_to(m[st * 2 * L:(st + 1) * 2 * L, off + h:off + h + 1], (2 * L, LANES))
             for h in range(GDN_HEADS)], axis=0)

    items = []
    for st in range(n_st):
        g_tile = colstack(G, st, GATE_A)
        items.append(dict(
            q=tiles(qkvc_ref, st, 0), k=tiles(qkvc_ref, st, GDN_QK), v=tiles(qkvc_ref, st, 2 * GDN_QK),
            beta=colstack(beta_all, st, GATE_B), g=g_tile, eg=jnp.exp(g_tile),
            egrev=jnp.exp(colstack(glast_rows, st, GATE_A) - g_tile)))
    uw, qk = _gdn_local(items, L)

    groups = [(h, bi) for h in range(GDN_HEADS) for bi in range(2)]
    r = []
    for st, it in enumerate(items):
        w = uw[st][1]
        qd = it["q"] * it["eg"]
        r.append([
            _dot(jnp.concatenate([w[gi * L:(gi + 1) * L], qd[gi * L:(gi + 1) * L]], axis=0).astype(BF16),
                 sst_ref[2 * st + bi, h].astype(BF16))
            for gi, (h, bi) in enumerate(groups)])
    v_new = [uw[st][0] - jnp.concatenate([x[:L] for x in r[st]], axis=0) for st in range(n_st)]
    o = [jnp.concatenate([x[L:] for x in r[st]], axis=0)
         + _dot(qk[st].astype(BF16), v_new[st].astype(BF16)) for st in range(n_st)]
    for st, it in enumerate(items):
        kd = it["k"] * it["egrev"]
        for gi, (h, bi) in enumerate(groups):
            b = 2 * st + bi
            ga = GATE_A + h
            rs = slice(gi * L, (gi + 1) * L)
            sst_out_ref[b, h] = (sst_ref[b, h] * egl[b][:, ga:ga + 1]
                                 + _dot_tn(kd[rs].astype(BF16), v_new[st][rs].astype(BF16)))
    for st in range(n_st):
        out = (_rms_rows(o[st], nw) * _silu(tiles(zg_ref, st, 0))).astype(o_ref.dtype)
        for h in range(GDN_HEADS):
            o_ref[st * 2 * L:(st + 1) * 2 * L, h * GDN_DV:(h + 1) * GDN_DV] = out[h * 2 * L:(h + 1) * 2 * L]


def _gdn_sample_spec(proj, conv_state, S_state, cw, gbias, galog, nw, *, B, L, nb):
    R = nb * L
    const = lambda shape: pl.BlockSpec(shape, lambda i: (0,) * len(shape))
    return dict(
        in_specs=[
            pl.BlockSpec((R, GDN_CONV_CH), lambda i: (i, OFF_QKV // GDN_CONV_CH)),
            pl.BlockSpec((R, GDN_V), lambda i: (i, OFF_ZG // GDN_V)),
            pl.BlockSpec((R, LANES), lambda i: (i, OFF_GATE // LANES)),
            pl.BlockSpec((CONV_W - 1, nb, GDN_CONV_CH), lambda i: (0, i, 0)),
            pl.BlockSpec((nb, GDN_HEADS, GDN_DK, GDN_DV), lambda i: (i, 0, 0, 0)),
            const((CONV_W, GDN_CONV_CH)), const((1, LANES)), const((1, LANES)), const((1, GDN_DV)),
        ],
        out_specs=[
            pl.BlockSpec((R, GDN_V), lambda i: (i, 0)),
            pl.BlockSpec((CONV_W - 1, nb, GDN_CONV_CH), lambda i: (0, i, 0)),
            pl.BlockSpec((nb, GDN_HEADS, GDN_DK, GDN_DV), lambda i: (i, 0, 0, 0)),
        ],
        out_shape=[
            jax.ShapeDtypeStruct((B * L, GDN_V), BF16),
            jax.ShapeDtypeStruct((CONV_W - 1, B, GDN_CONV_CH), F32),
            jax.ShapeDtypeStruct((B, GDN_HEADS, GDN_DK, GDN_DV), F32),
        ],
        scratch=[
            pltpu.VMEM((nb, L + CONV_PAD, GDN_CONV_CH), F32),
            pltpu.VMEM((R, GDN_CONV_CH), F32),
        ],
        args=[proj, proj, proj, conv_state, S_state, cw, gbias, galog, nw])


N_PAIRS = SSM_HEADS // 2
PAIRS_PER_GROUP = N_PAIRS // SSM_GROUPS
GROUP_W = SSM_DI // SSM_GROUPS


def _ssm_tile(graw, gbias, nega, ld_x, ld_b, ld_c, ld_z, dcols_ref, nw_ref, st_o, get_h, set_h, glen):
    c = CHUNK
    P = SSM_P
    nseq = c // glen
    sh = glen.bit_length() - 1
    ri = lax.broadcasted_iota(jnp.int32, (c, c), 0)
    ci = lax.broadcasted_iota(jnp.int32, (c, c), 1)
    tril_f = (((ri >> sh) == (ci >> sh)) & (ri >= ci)).astype(F32)
    sp = _softplus(graw + gbias)
    acum = jnp.dot(tril_f, nega * sp, precision=_HIGHEST, preferred_element_type=F32)
    lasts = [acum[s * glen + glen - 1:(s + 1) * glen, :] for s in range(nseq)]
    alast = jnp.concatenate([jnp.broadcast_to(x, (glen, LANES)) for x in lasts], axis=0)
    dtrev = sp * jnp.exp(alast - acum)
    eal = [jnp.exp(x) for x in lasts]
    lane = lax.broadcasted_iota(jnp.int32, (c, LANES), 1)
    row = lax.broadcasted_iota(jnp.int32, (c, LANES), 0)
    m = jnp.where(lane < GATE_DT2, acum, sp)
    mt = jnp.concatenate([m, m], axis=0).T
    left = lane < P
    left_row = left[0:1]
    j = jnp.where(left, lane, lane - P)
    tril2 = ((row >> sh) == (j >> sh)) & (row >= j)
    rowh = lax.broadcasted_iota(jnp.int32, (2 * P, SSM_N), 0) < P

    def expand(mat, c0):
        return jnp.where(left, jnp.broadcast_to(mat[:, c0:c0 + 1], (c, LANES)),
                         jnp.broadcast_to(mat[:, c0 + 1:c0 + 2], (c, LANES)))

    def rowsel(base, e):
        return jnp.where(left_row, mt[base + 2 * e:base + 2 * e + 1, :], mt[base + 2 * e + 1:base + 2 * e + 2, :])

    for g in range(SSM_GROUPS):
        Bg = ld_b(g)
        Cg = ld_c(g)
        Bg16 = Bg.astype(BF16)
        Cg16 = Cg.astype(BF16)
        cb2 = _dot_nt(Cg16, jnp.concatenate([Bg16, Bg16], axis=0))
        pairs = [g * PAIRS_PER_GROUP + e4 for e4 in range(PAIRS_PER_GROUP)]
        acol = [expand(acum, GATE_DT + 2 * e) for e in pairs]
        scores16 = [
            (cb2 * jnp.exp(jnp.where(tril2, a - rowsel(GATE_DT, e), -jnp.inf)) * rowsel(GATE_DT2, e)).astype(BF16)
            for a, e in zip(acol, pairs)]
        xp = [ld_x(e) for e in pairs]
        bd16 = [jnp.concatenate([jnp.where(left, x, 0.0), jnp.where(left, 0.0, x)], axis=0).astype(BF16)
                for x in xp]
        ydiag = [_dot(s, b) for s, b in zip(scores16, bd16)]
        if nseq == 1:
            yoff = [_dot_nt(Cg16, get_h(0, e).astype(BF16)) for e in pairs]
        else:
            yoff = [jnp.concatenate(
                [_dot_nt(Cg[s * glen:(s + 1) * glen].astype(BF16), get_h(s, e).astype(BF16))
                 for s in range(nseq)], axis=0) for e in pairs]
        y = [yd + yo * jnp.exp(a) + dcols_ref[:, e * 2 * P:(e + 1) * 2 * P] * x
             for yd, yo, a, e, x in zip(ydiag, yoff, acol, pairs, xp)]
        xdr = [x * expand(dtrev, GATE_DT + 2 * e) for x, e in zip(xp, pairs)]
        for e, xd in zip(pairs, xdr):
            c0 = GATE_DT + 2 * e
            for s in range(nseq):
                rs = slice(s * glen, (s + 1) * glen)
                ealcol = jnp.where(rowh, eal[s][:, c0:c0 + 1], eal[s][:, c0 + 1:c0 + 2])
                set_h(s, e, get_h(s, e) * ealcol + _dot_tn(xd[rs].astype(BF16), Bg[rs].astype(BF16)))
        yg = jnp.concatenate(y, axis=1) * _silu(ld_z(g))
        gcols = slice(g * GROUP_W, (g + 1) * GROUP_W)
        st_o(g, _rms_rows(yg, nw_ref[:, gcols]))


def _ssm_prompt_parts(xs_ref, bc_ref, zs_ref, gate_ref, cwx_ref, cbx_ref, cwbc_ref, cbbc_ref,
                      gbias_ref, galog_ref, dcols_ref, nw_ref,
                      o_ref, cst_out_ref, hst_out_ref, xpadx_ref, xpadbc_ref, xc_ref, bcc_ref, hh_ref, *, Lb):
    c = CHUNK

    def init():
        hh_ref[...] = jnp.zeros(hh_ref.shape, F32)
        xpadx_ref[0:CONV_PAD, :] = jnp.zeros((CONV_PAD, SSM_DI), F32)
        xpadbc_ref[0:CONV_PAD, :] = jnp.zeros((CONV_PAD, 2 * SSM_BC), F32)

    def conv():
        ident = lambda s, y: y
        _conv_block(xs_ref, xpadx_ref, cwx_ref, cbx_ref, xc_ref, Lb, SSM_DI, ident)
        _conv_block(bc_ref, xpadbc_ref, cwbc_ref, cbbc_ref, bcc_ref, Lb, 2 * SSM_BC, ident)

    def set_h(s, e, val):
        hh_ref[e] = val

    def tile(ci):
        rows = slice(ci * c, (ci + 1) * c)

        def st_o(g, val):
            o_ref[rows, g * GROUP_W:(g + 1) * GROUP_W] = val.astype(o_ref.dtype)

        _ssm_tile(
            gate_ref[rows, :], gbias_ref[...], -jnp.exp(galog_ref[...]),
            lambda e: xc_ref[rows, e * LANES:(e + 1) * LANES],
            lambda g: bcc_ref[rows, g * SSM_N:(g + 1) * SSM_N],
            lambda g: bcc_ref[rows, SSM_BC + g * SSM_N:SSM_BC + (g + 1) * SSM_N],
            lambda g: zs_ref[rows, g * GROUP_W:(g + 1) * GROUP_W],
            dcols_ref, nw_ref, st_o, lambda s, e: hh_ref[e], set_h, c)

    steps = [conv] + [functools.partial(tile, ci) for ci in range(Lb // c)]

    def final():
        cst_out_ref[0, :, :SSM_DI] = xpadx_ref[CONV_HIST:CONV_PAD, :]
        cst_out_ref[0, :, SSM_DI:] = xpadbc_ref[CONV_HIST:CONV_PAD, :]
        hst_out_ref[0] = hh_ref[...]

    return init, steps, final


def _ssm_sample_kernel(xs_ref, bc_ref, zs_ref, gate_ref, cst_ref, hst_ref, cwx_ref, cbx_ref, cwbc_ref, cbbc_ref,
                       gbias_ref, galog_ref, dcols_ref, nw_ref,
                       o_ref, cst_out_ref, hst_out_ref, xpadx_ref, xpadbc_ref, xc_ref, bcc_ref, *, L):
    ident = lambda s, y: y
    for bi in range(CHUNK // L):
        rs = pl.ds(bi * L, L)
        xpx = xpadx_ref.at[bi]
        xpb = xpadbc_ref.at[bi]
        xpx[CONV_HIST:CONV_PAD, :] = cst_ref[:, bi, :SSM_DI]
        xpb[CONV_HIST:CONV_PAD, :] = cst_ref[:, bi, SSM_DI:]
        cst_out_ref[:, bi, :SSM_DI] = _conv_block(
            xs_ref.at[rs], xpx, cwx_ref, cbx_ref, xc_ref.at[rs], L, SSM_DI, ident)
        cst_out_ref[:, bi, SSM_DI:] = _conv_block(
            bc_ref.at[rs], xpb, cwbc_ref, cbbc_ref, bcc_ref.at[rs], L, 2 * SSM_BC, ident)

    def st_o(g, val):
        o_ref[:, g * GROUP_W:(g + 1) * GROUP_W] = val.astype(o_ref.dtype)

    def set_h(s, e, val):
        hst_out_ref[s, e] = val

    _ssm_tile(
        gate_ref[...], gbias_ref[...], -jnp.exp(galog_ref[...]),
        lambda e: xc_ref[:, e * LANES:(e + 1) * LANES],
        lambda g: bcc_ref[:, g * SSM_N:(g + 1) * SSM_N],
        lambda g: bcc_ref[:, SSM_BC + g * SSM_N:SSM_BC + (g + 1) * SSM_N],
        lambda g: zs_ref[:, g * GROUP_W:(g + 1) * GROUP_W],
        dcols_ref, nw_ref, st_o, lambda s, e: hst_ref[s, e], set_h, L)


def _ssm_const_specs():
    const = lambda shape: pl.BlockSpec(shape, lambda *idx: (0,) * len(shape))
    return [
        const((CONV_W, SSM_DI)), const((1, SSM_DI)), const((CONV_W, 2 * SSM_BC)), const((1, 2 * SSM_BC)),
        const((1, LANES)), const((1, LANES)), const((1, SSM_DI)), const((1, SSM_DI)),
    ]


def _ssm_prompt_spec(proj, cwx, cbx, cwbc, cbbc, gbias, galog, dcols, nw, *, B, L, Lb):
    nl = L // Lb
    row = lambda b, l: b * nl + l
    return dict(
        in_specs=[
            pl.BlockSpec((Lb, SSM_DI), lambda b, l: (row(b, l), OFF_XS // SSM_DI)),
            pl.BlockSpec((Lb, 2 * SSM_BC), lambda b, l: (row(b, l), OFF_BC // (2 * SSM_BC))),
            pl.BlockSpec((Lb, SSM_DI), lambda b, l: (row(b, l), OFF_ZS // SSM_DI)),
            pl.BlockSpec((Lb, LANES), lambda b, l: (row(b, l), OFF_GATE // LANES)),
        ] + _ssm_const_specs(),
        out_specs=[
            pl.BlockSpec((Lb, SSM_DI), lambda b, l: (row(b, l), 0)),
            pl.BlockSpec((1, CONV_W - 1, SSM_CONV_CH), lambda b, l: (b, 0, 0)),
            pl.BlockSpec((1, N_PAIRS, 2 * SSM_P, SSM_N), lambda b, l: (b, 0, 0, 0)),
        ],
        out_shape=[
            jax.ShapeDtypeStruct((B * L, SSM_DI), BF16),
            jax.ShapeDtypeStruct((B, CONV_W - 1, SSM_CONV_CH), F32),
            jax.ShapeDtypeStruct((B, N_PAIRS, 2 * SSM_P, SSM_N), F32),
        ],
        scratch=[
            pltpu.VMEM((Lb + CONV_PAD, SSM_DI), F32),
            pltpu.VMEM((Lb + CONV_PAD, 2 * SSM_BC), F32),
            pltpu.VMEM((Lb, SSM_DI), F32),
            pltpu.VMEM((Lb, 2 * SSM_BC), F32),
            pltpu.VMEM((N_PAIRS, 2 * SSM_P, SSM_N), F32),
        ],
        args=[proj, proj, proj, proj, cwx, cbx, cwbc, cbbc, gbias, galog, dcols, nw])


def _ssm_sample_spec(proj, conv_state, h_pairs, cwx, cbx, cwbc, cbbc, gbias, galog, dcols, nw, *, B, L):
    nb = CHUNK // L
    return dict(
        in_specs=[
            pl.BlockSpec((CHUNK, SSM_DI), lambda i: (i, OFF_XS // SSM_DI)),
            pl.BlockSpec((CHUNK, 2 * SSM_BC), lambda i: (i, OFF_BC // (2 * SSM_BC))),
            pl.BlockSpec((CHUNK, SSM_DI), lambda i: (i, OFF_ZS // SSM_DI)),
            pl.BlockSpec((CHUNK, LANES), lambda i: (i, OFF_GATE // LANES)),
            pl.BlockSpec((CONV_W - 1, nb, SSM_CONV_CH), lambda i: (0, i, 0)),
            pl.BlockSpec((nb, N_PAIRS, 2 * SSM_P, SSM_N), lambda i: (i, 0, 0, 0)),
        ] + _ssm_const_specs(),
        out_specs=[
            pl.BlockSpec((CHUNK, SSM_DI), lambda i: (i, 0)),
            pl.BlockSpec((CONV_W - 1, nb, SSM_CONV_CH), lambda i: (0, i, 0)),
            pl.BlockSpec((nb, N_PAIRS, 2 * SSM_P, SSM_N), lambda i: (i, 0, 0, 0)),
        ],
        out_shape=[
            jax.ShapeDtypeStruct((B * L, SSM_DI), BF16),
            jax.ShapeDtypeStruct((CONV_W - 1, B, SSM_CONV_CH), F32),
            jax.ShapeDtypeStruct((B, N_PAIRS, 2 * SSM_P, SSM_N), F32),
        ],
        scratch=[
            pltpu.VMEM((nb, L + CONV_PAD, SSM_DI), F32),
            pltpu.VMEM((nb, L + CONV_PAD, 2 * SSM_BC), F32),
            pltpu.VMEM((CHUNK, SSM_DI), F32),
            pltpu.VMEM((CHUNK, 2 * SSM_BC), F32),
        ],
        args=[proj, proj, proj, proj, conv_state, h_pairs, cwx, cbx, cwbc, cbbc, gbias, galog, dcols, nw])


def _split_refs(refs, g, s, n_cast=0):
    it = iter(refs)
    take = lambda n: [next(it) for _ in range(n)]
    g_in, s_in, c_in = take(len(g["in_specs"])), take(len(s["in_specs"])), take(n_cast)
    g_out, s_out, c_out = take(len(g["out_specs"])), take(len(s["out_specs"])), take(n_cast)
    g_scr, s_scr = take(len(g["scratch"])), take(len(s["scratch"]))
    return g_in + g_out + g_scr, s_in + s_out + s_scr, list(zip(c_in, c_out))


def _mix_prompt_kernel(*refs, g, s, n_cast, Lb):
    g_refs, s_refs, casts = _split_refs(refs, g, s, n_cast)
    def cast_step(src_ref, dst_ref):
        dst_ref[...] = src_ref[...].astype(BF16)

    c_steps = [functools.partial(cast_step, a, b) for a, b in casts]
    g_init, g_steps, g_final = _gdn_prompt_parts(*g_refs, Lb=Lb)
    s_init, s_steps, s_final = _ssm_prompt_parts(*s_refs, Lb=Lb)
    l = pl.program_id(1)

    @pl.when(l == 0)
    def _init():
        g_init()
        s_init()

    for g_step, s_step, c_step in itertools.zip_longest(g_steps, s_steps, [None, None] + c_steps):
        for step in (g_step, c_step, s_step):
            if step is not None:
                step()

    @pl.when(l == pl.num_programs(1) - 1)
    def _final():
        g_final()
        s_final()


def _mix_sample_kernel(*refs, g, s, nb, L):
    g_refs, s_refs, _ = _split_refs(refs, g, s)
    _gdn_sample_kernel(*g_refs, nb=nb, L=L)
    _ssm_sample_kernel(*s_refs, L=L)


def _fused_call(kernel, g, s, grid, semantics, name, cast=()):
    n_steps = functools.reduce(lambda a, b: a * b, grid)
    step = (lambda b, l: (b * grid[1] + l, 0)) if len(grid) == 2 else (lambda i: (i, 0))
    assert all(w.shape[0] % (16 * n_steps) == 0 for w in cast), "cast row blocks must be whole bf16 tiles"
    c_specs = [pl.BlockSpec((w.shape[0] // n_steps, w.shape[1]), step) for w in cast]
    c_shape = [jax.ShapeDtypeStruct(w.shape, BF16) for w in cast]
    kw = dict(n_cast=len(cast)) if cast else {}
    outs = pl.pallas_call(
        functools.partial(kernel, g={k: g[k] for k in ("in_specs", "out_specs", "scratch")},
                          s={k: s[k] for k in ("in_specs", "out_specs", "scratch")}, **kw),
        grid=grid,
        in_specs=g["in_specs"] + s["in_specs"] + c_specs,
        out_specs=g["out_specs"] + s["out_specs"] + c_specs,
        out_shape=g["out_shape"] + s["out_shape"] + c_shape,
        scratch_shapes=g["scratch"] + s["scratch"],
        compiler_params=pltpu.CompilerParams(dimension_semantics=semantics, vmem_limit_bytes=VMEM_LIMIT),
        name=name,
    )(*g["args"], *s["args"], *cast)
    n, m = len(g["out_specs"]), len(g["out_specs"]) + len(s["out_specs"])
    return (outs[:n], outs[n:m]) + ((outs[m:],) if cast else ())


def _outproj_kernel(x_ref, mg_ref, ms_ref, w_ref, o_ref):
    acc = _dot(mg_ref[...].astype(BF16), w_ref[:GDN_V, :])
    acc = acc + _dot(ms_ref[...].astype(BF16), w_ref[GDN_V:, :])
    o_ref[...] = x_ref[...] + acc


def _out_proj(x2d, mix_g, mix_s, w_out16, *, tm):
    T = x2d.shape[0]
    return pl.pallas_call(
        _outproj_kernel,
        grid=(T // tm,),
        in_specs=[
            pl.BlockSpec((tm, D_MODEL), lambda i: (i, 0)),
            pl.BlockSpec((tm, GDN_V), lambda i: (i, 0)),
            pl.BlockSpec((tm, SSM_DI), lambda i: (i, 0)),
            pl.BlockSpec((D_MODEL, D_MODEL), lambda i: (0, 0)),
        ],
        out_specs=pl.BlockSpec((tm, D_MODEL), lambda i: (i, 0)),
        out_shape=jax.ShapeDtypeStruct((T, D_MODEL), F32),
        compiler_params=pltpu.CompilerParams(
            dimension_semantics=("parallel",), vmem_limit_bytes=VMEM_LIMIT),
        name="out_proj",
    )(x2d, mix_g, mix_s, w_out16)


FFN_SUB = 4


def _ffn_kernel(x_ref, nw_ref, wg_ref, wu_ref, wd_ref, fnw_ref, o_ref, h_ref):
    f = pl.program_id(1)
    nf = pl.num_programs(1)
    rs = h_ref.shape[0] // FFN_SUB

    def step(first, last):
        def gate_up(r):
            rows = slice(r * rs, (r + 1) * rs)
            if first:
                h = _rms_rows(x_ref[rows, :], nw_ref[...]).astype(BF16)
                h_ref[rows, :] = h
            else:
                h = h_ref[rows, :]
            return _dot(h, wg_ref[...]), _dot(h, wu_ref[...])

        def down(r, gu):
            rows = slice(r * rs, (r + 1) * rs)
            d = _dot((_silu(gu[0]) * gu[1]).astype(BF16), wd_ref[...])
            acc = d if first else o_ref[rows, :] + d
            if last:
                o_ref[rows, :] = _rms_rows(x_ref[rows, :] + acc, fnw_ref[...])
            else:
                o_ref[rows, :] = acc

        gu = gate_up(0)
        for r in range(1, FFN_SUB):
            gu_next = gate_up(r)
            down(r - 1, gu)
            gu = gu_next
        down(FFN_SUB - 1, gu)

    pl.when(f == 0)(lambda: step(True, False))
    pl.when((f > 0) & (f < nf - 1))(lambda: step(False, False))
    pl.when(f == nf - 1)(lambda: step(False, True))


def _ffn(x2d, norm_w, wg16, wu16, wd16, final_w, *, tm, tf):
    T = x2d.shape[0]
    return pl.pallas_call(
        _ffn_kernel,
        grid=(T // tm, D_FF // tf),
        in_specs=[
            pl.BlockSpec((tm, D_MODEL), lambda i, f: (i, 0)),
            pl.BlockSpec((1, D_MODEL), lambda i, f: (0, 0)),
            pl.BlockSpec((D_MODEL, tf), lambda i, f: (0, f)),
            pl.BlockSpec((D_MODEL, tf), lambda i, f: (0, f)),
            pl.BlockSpec((tf, D_MODEL), lambda i, f: (f, 0)),
            pl.BlockSpec((1, D_MODEL), lambda i, f: (0, 0)),
        ],
        out_specs=pl.BlockSpec((tm, D_MODEL), lambda i, f: (i, 0)),
        out_shape=jax.ShapeDtypeStruct((T, D_MODEL), F32),
        scratch_shapes=[pltpu.VMEM((tm, D_MODEL), BF16)],
        compiler_params=pltpu.CompilerParams(
            dimension_semantics=("parallel", "arbitrary"), vmem_limit_bytes=VMEM_LIMIT),
        name="ffn",
    )(x2d, norm_w, wg16, wu16, wd16, final_w)


PROMPT_ROWS = 256


def _trunk(x, states, p):
    B, L, _ = x.shape
    x2d = x.reshape(B * L, D_MODEL)
    proj = _in_proj(x2d, p["attn_norm_w"], p["w_in_r"], tm=IN_PROJ_TM, tn=IN_PROJ_TN)
    gdn_w = (p["gdn_conv_w"], p["gbias"], p["galog"], p["gdn_norm_w"])
    ssm_w = (p["cwx"], p["cbx"], p["cwbc"], p["cbbc"], p["gbias"], p["galog"], p["dcols"], p["ssm_norm_w"])
    pair_shape = (B, N_PAIRS, 2 * SSM_P, SSM_N)
    if states is None:
        Lb = PROMPT_ROWS
        g = _gdn_prompt_spec(proj, *gdn_w, B=B, L=L, Lb=Lb)
        s = _ssm_prompt_spec(proj, *ssm_w, B=B, L=L, Lb=Lb)
        kern = functools.partial(_mix_prompt_kernel, Lb=Lb)
        f32_weights = (p["w_gate"], p["w_up"], p["w_down"], p["w_out"])
        (mix_g, gconv_new, gS_new), (mix_s, sconv_new, sh_new), bf16_weights = _fused_call(
            kern, g, s, (B, L // Lb), ("parallel", "arbitrary"), "mix_prompt", cast=f32_weights)
        p["wg16"], p["wu16"], p["wd16"], p["w_out16"] = bf16_weights
    else:
        gconv, gS, sconv, sh = states
        nb = CHUNK // L
        tap_major = lambda a: jnp.swapaxes(a, 0, 1)
        g = _gdn_sample_spec(proj, tap_major(gconv), gS, *gdn_w, B=B, L=L, nb=nb)
        s = _ssm_sample_spec(proj, tap_major(sconv), sh.reshape(pair_shape), *ssm_w, B=B, L=L)
        kern = functools.partial(_mix_sample_kernel, nb=nb, L=L)
        (mix_g, gconv_new, gS_new), (mix_s, sconv_new, sh_new) = _fused_call(
            kern, g, s, (B // nb,), ("parallel",), "mix_sample")
        gconv_new, sconv_new = tap_major(gconv_new), tap_major(sconv_new)
    sh_new = sh_new.reshape(B, SSM_HEADS, SSM_P, SSM_N)
    x1 = _out_proj(x2d, mix_g, mix_s, p["w_out16"], tm=OUT_PROJ_TM)
    y = _ffn(x1, p["ffn_norm_w"], p["wg16"], p["wu16"], p["wd16"], p["final_norm_w"], tm=FFN_TM, tf=FFN_TF)
    return y.reshape(B, L, D_MODEL), (gconv_new[None], gS_new[None], sconv_new[None], sh_new[None])


def kernel(x_prompt, x_sample, state_gdn_conv, state_gdn, state_ssm_conv, state_ssm,
           attn_norm_w, w_in, gdn_conv_w, gdn_A_log, gdn_dt_bias, gdn_norm_w,
           ssm_conv_w, ssm_conv_b, ssm_A_log, ssm_dt_bias, ssm_D, ssm_norm_w,
           w_out, ffn_norm_w, w_gate, w_up, w_down, final_norm_w):
    assert w_in.shape[0] == 1, "single-layer trunk"
    assert x_prompt.shape[1] % PROMPT_ROWS == 0 and CHUNK % x_sample.shape[1] == 0 and x_sample.shape[0] % (CHUNK // x_sample.shape[1]) == 0
    assert w_in.shape[2] == D_IN_PROJ
    w_in_r = _w_in_prep(jnp.swapaxes(w_in, 1, 2), tk=W_PREP_TK)
    zeros8 = jnp.zeros((GDN_HEADS,), F32)
    tail = jnp.zeros((LANES - GATE_DT2 - SSM_HEADS,), F32)
    gbias = jnp.concatenate([zeros8, gdn_dt_bias[0], ssm_dt_bias[0], ssm_dt_bias[0], tail])[None]
    galog = jnp.concatenate([zeros8, gdn_A_log[0], ssm_A_log[0], ssm_A_log[0], tail])[None]
    p = dict(
        attn_norm_w=attn_norm_w, w_in_r=w_in_r, gdn_conv_w=gdn_conv_w[0], gbias=gbias, galog=galog,
        gdn_norm_w=gdn_norm_w,
        cwx=ssm_conv_w[0][:, :SSM_DI], cbx=ssm_conv_b[:, :SSM_DI],
        cwbc=ssm_conv_w[0][:, SSM_DI:], cbbc=ssm_conv_b[:, SSM_DI:],
        dcols=jnp.repeat(ssm_D[0], SSM_P)[None], ssm_norm_w=ssm_norm_w,
        w_out=w_out[0], w_down=w_down[0], ffn_norm_w=ffn_norm_w,
        w_gate=w_gate[0], w_up=w_up[0],
        final_norm_w=final_norm_w[None],
    )
    y_p, st_p = _trunk(x_prompt, None, p)
    y_s, st_s = _trunk(x_sample, (state_gdn_conv[0], state_gdn[0], state_ssm_conv[0], state_ssm[0]), p)
    return (y_p, y_s, st_p[0], st_p[1], st_p[2], st_p[3], st_s[0], st_s[1], st_s[2], st_s[3])
```

```python
import functools
import itertools

import jax
import jax.numpy as jnp
from jax import lax
from jax.experimental import pallas as pl
from jax.experimental.pallas import tpu as pltpu

F32 = jnp.float32
BF16 = jnp.bfloat16

D_MODEL = 2048
GDN_HEADS = 8
GDN_DK = 128
GDN_DV = 128
GDN_QK = GDN_HEADS * GDN_DK
GDN_V = GDN_HEADS * GDN_DV
GDN_CONV_CH = 2 * GDN_QK + GDN_V
SSM_P = 64
SSM_N = 128
SSM_GROUPS = 2
SSM_DI = 1024
SSM_HEADS = SSM_DI // SSM_P
SSM_BC = SSM_GROUPS * SSM_N
SSM_CONV_CH = SSM_DI + 2 * SSM_BC
CONV_W = 4
CHUNK = 64
D_FF = 5632
EPS = 1e-6

OFF_QKV = 0
OFF_ZG = OFF_QKV + GDN_CONV_CH
OFF_ZS = OFF_ZG + GDN_V
OFF_XS = OFF_ZS + SSM_DI
OFF_BC = OFF_XS + SSM_DI
OFF_GATE = OFF_BC + 2 * SSM_BC
LANES = 128
SUBLANES = 8
CONV_PAD = SUBLANES
CONV_HIST = CONV_PAD - (CONV_W - 1)
GATE_B = 0
GATE_A = GATE_B + GDN_HEADS
GATE_DT = GATE_A + GDN_HEADS
GATE_DT2 = GATE_DT + SSM_HEADS
N_PROJ = 6912

VMEM_LIMIT = 52 * 1024 * 1024
W_PREP_TK = 256
IN_PROJ_TM, IN_PROJ_TN = 1024, 768
OUT_PROJ_TM = 512
FFN_TM, FFN_TF = 1024, 512

_HIGHEST = lax.Precision.HIGHEST


def _silu(x):
    h = 0.5 * x
    return h + h * jnp.tanh(h)


def _softplus(x):
    return jnp.maximum(x, 0.0) + jnp.log1p(jnp.exp(-jnp.abs(x)))


def _dot(a, b):
    return jnp.dot(a, b, preferred_element_type=F32)


def _dot_nt(a, b):
    return lax.dot_general(a, b, (((1,), (1,)), ((), ())), preferred_element_type=F32)


def _dot_tn(a, b):
    return lax.dot_general(a, b, (((0,), (0,)), ((), ())), preferred_element_type=F32)


def _rms_rows(x, w):
    return x * lax.rsqrt(jnp.mean(x * x, axis=-1, keepdims=True) + EPS) * w


W_B = OFF_ZS
W_ZS = W_B + 2 * GDN_HEADS
W_DT = W_ZS + SSM_DI + SSM_CONV_CH
D_IN_PROJ = W_DT + SSM_HEADS


def _wprep_kernel(w_ref, o_ref):
    cols = o_ref.shape[1]
    o_ref[:W_B, :] = w_ref[0, :W_B, :].astype(BF16)
    o_ref[W_B:OFF_GATE, :] = w_ref[0, W_ZS:W_DT, :].astype(BF16)
    dt = w_ref[0, W_DT:D_IN_PROJ, :].astype(BF16)
    o_ref[OFF_GATE:OFF_GATE + GATE_DT, :] = w_ref[0, W_B:W_ZS, :].astype(BF16)
    o_ref[OFF_GATE + GATE_DT:OFF_GATE + GATE_DT2, :] = dt
    o_ref[OFF_GATE + GATE_DT2:OFF_GATE + GATE_DT2 + SSM_HEADS, :] = dt
    o_ref[OFF_GATE + GATE_DT2 + SSM_HEADS:, :] = jnp.zeros((N_PROJ - OFF_GATE - GATE_DT2 - SSM_HEADS, cols), BF16)


def _w_in_prep(w_in_t, *, tk):
    return pl.pallas_call(
        _wprep_kernel,
        grid=(D_MODEL // tk,),
        in_specs=[pl.BlockSpec((1, D_IN_PROJ, tk), lambda i: (0, 0, i))],
        out_specs=pl.BlockSpec((N_PROJ, tk), lambda i: (0, i)),
        out_shape=jax.ShapeDtypeStruct((N_PROJ, D_MODEL), BF16),
        compiler_params=pltpu.CompilerParams(
            dimension_semantics=("parallel",), vmem_limit_bytes=VMEM_LIMIT),
        name="w_in_prep",
    )(w_in_t)


INPROJ_SUB = 4


def _inproj_kernel(x_ref, nw_ref, wt_ref, o_ref, h_ref):
    j = pl.program_id(1)

    @pl.when(j == 0)
    def _first():
        rs = h_ref.shape[0] // INPROJ_SUB
        for r in range(INPROJ_SUB):
            rows = slice(r * rs, (r + 1) * rs)
            h = _rms_rows(x_ref[rows, :], nw_ref[...]).astype(BF16)
            h_ref[rows, :] = h
            o_ref[rows, :] = _dot_nt(h, wt_ref[...])

    @pl.when(j > 0)
    def _rest():
        o_ref[...] = _dot_nt(h_ref[...], wt_ref[...])


def _in_proj(x2d, norm_w, w_in_r, *, tm, tn):
    T = x2d.shape[0]
    return pl.pallas_call(
        _inproj_kernel,
        grid=(T // tm, N_PROJ // tn),
        in_specs=[
            pl.BlockSpec((tm, D_MODEL), lambda i, j: (i, 0)),
            pl.BlockSpec((1, D_MODEL), lambda i, j: (0, 0)),
            pl.BlockSpec((tn, D_MODEL), lambda i, j: (j, 0)),
        ],
        out_specs=pl.BlockSpec((tm, tn), lambda i, j: (i, j)),
        out_shape=jax.ShapeDtypeStruct((T, N_PROJ), F32),
        scratch_shapes=[pltpu.VMEM((tm, D_MODEL), BF16)],
        compiler_params=pltpu.CompilerParams(
            dimension_semantics=("parallel", "arbitrary"), vmem_limit_bytes=VMEM_LIMIT),
        name="in_proj",
    )(x2d, norm_w, w_in_r)


def _conv_block(x_ref, xpad_ref, cw_ref, cb_ref, dst_ref, Lb, C, post):
    xpad_ref[CONV_PAD:CONV_PAD + Lb, :] = x_ref[...]
    rs = min(Lb, CHUNK)
    for sb in range(Lb // rs):
        r = sb * rs
        for s in range(C // LANES):
            cols = slice(s * LANES, (s + 1) * LANES)
            acc = xpad_ref[CONV_PAD + r:CONV_PAD + r + rs, cols] * cw_ref[CONV_W - 1:CONV_W, cols]
            for i in range(CONV_W - 1):
                acc = acc + xpad_ref[CONV_HIST + i + r:CONV_HIST + i + r + rs, cols] * cw_ref[i:i + 1, cols]
            if cb_ref is not None:
                acc = acc + cb_ref[:, cols]
            dst_ref[r:r + rs, cols] = post(s, _silu(acc))
    hist = xpad_ref[Lb + CONV_HIST:Lb + CONV_PAD, :]
    xpad_ref[CONV_HIST:CONV_PAD, :] = hist
    return hist


STACK = 128


def _gdn_qk_post(s, y):
    if s < 2 * GDN_HEADS:
        y = y * lax.rsqrt(jnp.sum(y * y, axis=-1, keepdims=True) + EPS)
        if s < GDN_HEADS:
            y = y * (GDN_DK ** -0.5)
    return y


def _gdn_local(items, glen):
    sh = glen.bit_length() - 1
    row = lax.broadcasted_iota(jnp.int32, (STACK, STACK), 0)
    col = lax.broadcasted_iota(jnp.int32, (STACK, STACK), 1)
    same = (row >> sh) == (col >> sh)
    incl = same & (row >= col)
    strict = same & (row > col)
    eye = (row == col).astype(F32)

    g_rows = [it["g_row"] if "g_row" in it else it["g"].T for it in items]
    decay = [jnp.exp(jnp.where(incl, it["g"] - gr, -jnp.inf)) for it, gr in zip(items, g_rows)]
    kb = [it["k"] * it["beta"] for it in items]
    qkk = [_dot_nt(jnp.concatenate([it["q"], b], axis=0).astype(BF16), it["k"].astype(BF16))
           for it, b in zip(items, kb)]
    qk = [x[:STACK] * d for x, d in zip(qkk, decay)]
    nmat = [jnp.where(strict, -(x[STACK:] * d), 0.0) for x, d in zip(qkk, decay)]
    tinv = [eye + n for n in nmat]
    if sh >= 2:
        pw = [_dot(n.astype(BF16), n.astype(BF16)) for n in nmat]
        for _ in range(sh - 2):
            x = [_dot(jnp.concatenate([t, p], axis=0).astype(BF16), p.astype(BF16))
                 for t, p in zip(tinv, pw)]
            tinv = [t + y[:STACK] for t, y in zip(tinv, x)]
            pw = [y[STACK:] for y in x]
        tinv = [t + _dot(t.astype(BF16), p.astype(BF16)) for t, p in zip(tinv, pw)]
    uw = [_dot(t.astype(BF16),
               jnp.concatenate([it["v"] * it["beta"], b * it["eg"]], axis=1).astype(BF16))
          for t, it, b in zip(tinv, items, kb)]
    return [(x[:, :GDN_DV], x[:, GDN_DV:]) for x in uw], qk


GDN_LOCAL_CHUNKS = 2


def _gdn_prompt_parts(qkv_ref, zg_ref, gate_ref, cw_ref, gbias_ref, galog_ref, nw_ref,
                      o_ref, cst_out_ref, sst_out_ref,
                      xpad_ref, qkvc_ref, s_ref, u_ref, wq16_ref, kd16_ref, qk16_ref, egl_ref, *, Lb, gates):
    c = CHUNK

    def init():
        xpad_ref[0:CONV_PAD, :] = jnp.zeros((CONV_PAD, GDN_CONV_CH), F32)
        s_ref[...] = jnp.zeros(s_ref.shape, F32)

    def conv():
        _conv_block(qkv_ref, xpad_ref, cw_ref, None, qkvc_ref, Lb, GDN_CONV_CH, _gdn_qk_post)

    nw = nw_ref[...]
    n_pairs = GDN_HEADS // 2

    def local_chunk_items(ci):
        rows = slice(ci * c, (ci + 1) * c)
        graw, _, G = gates(ci)
        beta_all = jax.nn.sigmoid(graw)
        glast = G[c - 1:c, :]
        egl_ref[ci] = jnp.broadcast_to(jnp.exp(glast), (SUBLANES, LANES))

        def heads(off, a, b):
            return jnp.concatenate([qkvc_ref[rows, off + a * LANES:off + (a + 1) * LANES],
                                    qkvc_ref[rows, off + b * LANES:off + (b + 1) * LANES]], axis=0)

        def colstack(m, a, b):
            return jnp.concatenate([jnp.broadcast_to(m[:, a:a + 1], (c, LANES)),
                                    jnp.broadcast_to(m[:, b:b + 1], (c, LANES))], axis=0)

        gt2 = jnp.concatenate([G, G], axis=0).T
        first_head = lax.broadcasted_iota(jnp.int32, (1, STACK), 1) < c
        items = []
        for pr in range(n_pairs):
            a, b = 2 * pr, 2 * pr + 1
            g_tile = colstack(G, GATE_A + a, GATE_A + b)
            g_row = jnp.where(first_head, gt2[GATE_A + a:GATE_A + a + 1, :], gt2[GATE_A + b:GATE_A + b + 1, :])
            items.append(dict(
                q=heads(0, a, b), k=heads(GDN_QK, a, b), v=heads(2 * GDN_QK, a, b),
                beta=colstack(beta_all, GATE_B + a, GATE_B + b),
                g=g_tile, g_row=g_row, eg=jnp.exp(g_tile),
                egrev=jnp.exp(colstack(glast, GATE_A + a, GATE_A + b) - g_tile)))
        return items

    def local_group(cis):
        items = [it for ci in cis for it in local_chunk_items(ci)]
        uw, qk = _gdn_local(items, c)
        for n, it in enumerate(items):
            idx = cis[0] * n_pairs + n
            u, w = uw[n]
            qd = it["q"] * it["eg"]
            u_ref[idx] = u
            for hh in range(2):
                hs = slice(hh * c, (hh + 1) * c)
                wq16_ref[2 * idx + hh] = jnp.concatenate([w[hs], qd[hs]], axis=0).astype(BF16)
            kd16_ref[idx] = (it["k"] * it["egrev"]).astype(BF16)
            qk16_ref[idx] = qk[n].astype(BF16)

    def recurrent(ci):
        rows = slice(ci * c, (ci + 1) * c)
        egl = egl_ref[ci][0:1]
        r = [[_dot(wq16_ref[2 * (ci * n_pairs + pr) + hh], s_ref[2 * pr + hh].astype(BF16)) for hh in range(2)]
             for pr in range(n_pairs)]
        v16 = [(u_ref[ci * n_pairs + pr] - jnp.concatenate([r[pr][0][:c], r[pr][1][:c]], axis=0)).astype(BF16)
               for pr in range(n_pairs)]
        o = [jnp.concatenate([r[pr][0][c:], r[pr][1][c:]], axis=0) + _dot(qk16_ref[ci * n_pairs + pr], v16[pr])
             for pr in range(n_pairs)]
        for pr in range(n_pairs):
            kd16 = kd16_ref[ci * n_pairs + pr]
            for hh in range(2):
                h = 2 * pr + hh
                ga = GATE_A + h
                hs = slice(hh * c, (hh + 1) * c)
                s_ref[h] = s_ref[h] * egl[:, ga:ga + 1] + _dot_tn(kd16[hs], v16[pr][hs])
        for pr in range(n_pairs):
            for hh in range(2):
                h = 2 * pr + hh
                z = zg_ref[rows, h * GDN_DV:(h + 1) * GDN_DV]
                o_ref[rows, h * GDN_DV:(h + 1) * GDN_DV] = (
                    _rms_rows(o[pr][hh * c:(hh + 1) * c], nw) * _silu(z)).astype(o_ref.dtype)

    steps = [conv]
    for g0 in range(0, Lb // c, GDN_LOCAL_CHUNKS):
        cis = list(range(g0, g0 + GDN_LOCAL_CHUNKS))
        steps.append(functools.partial(local_group, cis))
        steps += [functools.partial(recurrent, ci) for ci in cis]

    def final():
        cst_out_ref[0] = xpad_ref[CONV_HIST:CONV_PAD, :]
        sst_out_ref[0] = s_ref[...]

    return init, steps, final


def _gdn_prompt_spec(proj, cw, gbias, galog, nw, *, B, L, Lb):
    nl = L // Lb
    n_tiles = (Lb // CHUNK) * (GDN_HEADS // 2)
    row = lambda b, l: b * nl + l
    const = lambda shape: pl.BlockSpec(shape, lambda b, l: (0,) * len(shape))
    return dict(
        in_specs=[
            pl.BlockSpec((Lb, GDN_CONV_CH), lambda b, l: (row(b, l), OFF_QKV // GDN_CONV_CH)),
            pl.BlockSpec((Lb, GDN_V), lambda b, l: (row(b, l), OFF_ZG // GDN_V)),
            pl.BlockSpec((Lb, LANES), lambda b, l: (row(b, l), OFF_GATE // LANES)),
            const((CONV_W, GDN_CONV_CH)), const((1, LANES)), const((1, LANES)), const((1, GDN_DV)),
        ],
        out_specs=[
            pl.BlockSpec((Lb, GDN_V), lambda b, l: (row(b, l), 0)),
            pl.BlockSpec((1, CONV_W - 1, GDN_CONV_CH), lambda b, l: (b, 0, 0)),
            pl.BlockSpec((1, GDN_HEADS, GDN_DK, GDN_DV), lambda b, l: (b, 0, 0, 0)),
        ],
        out_shape=[
            jax.ShapeDtypeStruct((B * L, GDN_V), BF16),
            jax.ShapeDtypeStruct((B, CONV_W - 1, GDN_CONV_CH), F32),
            jax.ShapeDtypeStruct((B, GDN_HEADS, GDN_DK, GDN_DV), F32),
        ],
        scratch=[
            pltpu.VMEM((Lb + CONV_PAD, GDN_CONV_CH), F32),
            pltpu.VMEM((Lb, GDN_CONV_CH), F32),
            pltpu.VMEM((GDN_HEADS, GDN_DK, GDN_DV), F32),
            pltpu.VMEM((n_tiles, STACK, GDN_DV), F32),
            pltpu.VMEM((2 * n_tiles, STACK, GDN_DK), BF16),
            pltpu.VMEM((n_tiles, STACK, GDN_DK), BF16),
            pltpu.VMEM((n_tiles, STACK, STACK), BF16),
            pltpu.VMEM((Lb // CHUNK, SUBLANES, LANES), F32),
        ],
        args=[proj, proj, proj, cw, gbias, galog, nw])


def _gdn_sample_kernel(qkv_ref, zg_ref, gate_ref, cst_ref, sst_ref, cw_ref, gbias_ref, galog_ref, nw_ref,
                       o_ref, cst_out_ref, sst_out_ref, xpad_ref, qkvc_ref, *, nb, L):
    R = nb * L
    sh = L.bit_length() - 1
    for bi in range(nb):
        xp = xpad_ref.at[bi]
        xp[CONV_HIST:CONV_PAD, :] = cst_ref[:, bi, :]
        cst_out_ref[:, bi, :] = _conv_block(
            qkv_ref.at[pl.ds(bi * L, L)], xp, cw_ref, None, qkvc_ref.at[pl.ds(bi * L, L)],
            L, GDN_CONV_CH, _gdn_qk_post)

    row_i = lax.broadcasted_iota(jnp.int32, (R, R), 0)
    col_i = lax.broadcasted_iota(jnp.int32, (R, R), 1)
    tril_f = (((row_i >> sh) == (col_i >> sh)) & (row_i >= col_i)).astype(F32)
    graw = gate_ref[...]
    sp = _softplus(graw + gbias_ref[...])
    beta_all = jax.nn.sigmoid(graw)
    G = jnp.dot(tril_f, -jnp.exp(galog_ref[...]) * sp, precision=_HIGHEST, preferred_element_type=F32)
    glast = [G[bi * L + L - 1:bi * L + L, :] for bi in range(nb)]
    glast_rows = jnp.concatenate([jnp.broadcast_to(x, (L, LANES)) for x in glast], axis=0)
    egl = [jnp.exp(x) for x in glast]
    nw = nw_ref[...]

    n_st = R // (2 * L)

    def tiles(ref, st, off):
        return jnp.concatenate(
            [ref[st * 2 * L:(st + 1) * 2 * L, off + h * LANES:off + (h + 1) * LANES]
             for h in range(GDN_HEADS)], axis=0)

    def colstack(m, st, off):
        return jnp.concatenate(
            [jnp.broadcast_to(m[st * 2 * L:(st + 1) * 2 * L, off + h:off + h + 1], (2 * L, LANES))
             for h in range(GDN_HEADS)], axis=0)

    items = []
    for st in range(n_st):
        g_tile = colstack(G, st, GATE_A)
        items.append(dict(
            q=tiles(qkvc_ref, st, 0), k=tiles(qkvc_ref, st, GDN_QK), v=tiles(qkvc_ref, st, 2 * GDN_QK),
            beta=colstack(beta_all, st, GATE_B), g=g_tile, eg=jnp.exp(g_tile),
            egrev=jnp.exp(colstack(glast_rows, st, GATE_A) - g_tile)))
    uw, qk = _gdn_local(items, L)

    groups = [(h, bi) for h in range(GDN_HEADS) for bi in range(2)]
    r = []
    for st, it in enumerate(items):
        w = uw[st][1]
        qd = it["q"] * it["eg"]
        r.append([
            _dot(jnp.concatenate([w[gi * L:(gi + 1) * L], qd[gi * L:(gi + 1) * L]], axis=0).astype(BF16),
                 sst_ref[2 * st + bi, h].astype(BF16))
            for gi, (h, bi) in enumerate(groups)])
    v_new = [uw[st][0] - jnp.concatenate([x[:L] for x in r[st]], axis=0) for st in range(n_st)]
    o = [jnp.concatenate([x[L:] for x in r[st]], axis=0)
         + _dot(qk[st].astype(BF16), v_new[st].astype(BF16)) for st in range(n_st)]
    for st, it in enumerate(items):
        kd = it["k"] * it["egrev"]
        for gi, (h, bi) in enumerate(groups):
            b = 2 * st + bi
            ga = GATE_A + h
            rs = slice(gi * L, (gi + 1) * L)
            sst_out_ref[b, h] = (sst_ref[b, h] * egl[b][:, ga:ga + 1]
                                 + _dot_tn(kd[rs].astype(BF16), v_new[st][rs].astype(BF16)))
    for st in range(n_st):
        out = (_rms_rows(o[st], nw) * _silu(tiles(zg_ref, st, 0))).astype(o_ref.dtype)
        for h in range(GDN_HEADS):
            o_ref[st * 2 * L:(st + 1) * 2 * L, h * GDN_DV:(h + 1) * GDN_DV] = out[h * 2 * L:(h + 1) * 2 * L]


def _gdn_sample_spec(proj, conv_state, S_state, cw, gbias, galog, nw, *, B, L, nb):
    R = nb * L
    const = lambda shape: pl.BlockSpec(shape, lambda i: (0,) * len(shape))
    return dict(
        in_specs=[
            pl.BlockSpec((R, GDN_CONV_CH), lambda i: (i, OFF_QKV // GDN_CONV_CH)),
            pl.BlockSpec((R, GDN_V), lambda i: (i, OFF_ZG // GDN_V)),
            pl.BlockSpec((R, LANES), lambda i: (i, OFF_GATE // LANES)),
            pl.BlockSpec((CONV_W - 1, nb, GDN_CONV_CH), lambda i: (0, i, 0)),
            pl.BlockSpec((nb, GDN_HEADS, GDN_DK, GDN_DV), lambda i: (i, 0, 0, 0)),
            const((CONV_W, GDN_CONV_CH)), const((1, LANES)), const((1, LANES)), const((1, GDN_DV)),
        ],
        out_specs=[
            pl.BlockSpec((R, GDN_V), lambda i: (i, 0)),
            pl.BlockSpec((CONV_W - 1, nb, GDN_CONV_CH), lambda i: (0, i, 0)),
            pl.BlockSpec((nb, GDN_HEADS, GDN_DK, GDN_DV), lambda i: (i, 0, 0, 0)),
        ],
        out_shape=[
            jax.ShapeDtypeStruct((B * L, GDN_V), BF16),
            jax.ShapeDtypeStruct((CONV_W - 1, B, GDN_CONV_CH), F32),
            jax.ShapeDtypeStruct((B, GDN_HEADS, GDN_DK, GDN_DV), F32),
        ],
        scratch=[
            pltpu.VMEM((nb, L + CONV_PAD, GDN_CONV_CH), F32),
            pltpu.VMEM((R, GDN_CONV_CH), F32),
        ],
        args=[proj, proj, proj, conv_state, S_state, cw, gbias, galog, nw])


N_PAIRS = SSM_HEADS // 2
PAIRS_PER_GROUP = N_PAIRS // SSM_GROUPS
GROUP_W = SSM_DI // SSM_GROUPS


def _gate_cumsum(graw, gbias, nega, glen):
    c = graw.shape[0]
    sh = glen.bit_length() - 1
    ri = lax.broadcasted_iota(jnp.int32, (c, c), 0)
    ci = lax.broadcasted_iota(jnp.int32, (c, c), 1)
    tril_f = (((ri >> sh) == (ci >> sh)) & (ri >= ci)).astype(F32)
    sp = _softplus(graw + gbias)
    return sp, jnp.dot(tril_f, nega * sp, precision=_HIGHEST, preferred_element_type=F32)


def _ssm_tile(graw, gbias, nega, ld_x, ld_b, ld_c, ld_z, dcols_ref, nw_ref, st_o, get_h, set_h, glen,
              sp_acum=None):
    c = CHUNK
    P = SSM_P
    nseq = c // glen
    sh = glen.bit_length() - 1
    sp, acum = sp_acum if sp_acum is not None else _gate_cumsum(graw, gbias, nega, glen)
    lasts = [acum[s * glen + glen - 1:(s + 1) * glen, :] for s in range(nseq)]
    alast = jnp.concatenate([jnp.broadcast_to(x, (glen, LANES)) for x in lasts], axis=0)
    dtrev = sp * jnp.exp(alast - acum)
    eal = [jnp.exp(x) for x in lasts]
    lane = lax.broadcasted_iota(jnp.int32, (c, LANES), 1)
    row = lax.broadcasted_iota(jnp.int32, (c, LANES), 0)
    m = jnp.where(lane < GATE_DT2, acum, sp)
    mt = jnp.concatenate([m, m], axis=0).T
    left = lane < P
    left_row = left[0:1]
    j = jnp.where(left, lane, lane - P)
    tril2 = ((row >> sh) == (j >> sh)) & (row >= j)
    rowh = lax.broadcasted_iota(jnp.int32, (2 * P, SSM_N), 0) < P

    def expand(mat, c0):
        return jnp.where(left, jnp.broadcast_to(mat[:, c0:c0 + 1], (c, LANES)),
                         jnp.broadcast_to(mat[:, c0 + 1:c0 + 2], (c, LANES)))

    def rowsel(base, e):
        return jnp.where(left_row, mt[base + 2 * e:base + 2 * e + 1, :], mt[base + 2 * e + 1:base + 2 * e + 2, :])

    for g in range(SSM_GROUPS):
        Bg = ld_b(g)
        Cg = ld_c(g)
        Bg16 = Bg.astype(BF16)
        Cg16 = Cg.astype(BF16)
        cb2 = _dot_nt(Cg16, jnp.concatenate([Bg16, Bg16], axis=0))
        pairs = [g * PAIRS_PER_GROUP + e4 for e4 in range(PAIRS_PER_GROUP)]
        acol = [expand(acum, GATE_DT + 2 * e) for e in pairs]
        scores16 = [
            (cb2 * jnp.exp(jnp.where(tril2, a - rowsel(GATE_DT, e), -jnp.inf)) * rowsel(GATE_DT2, e)).astype(BF16)
            for a, e in zip(acol, pairs)]
        xp = [ld_x(e) for e in pairs]
        bd16 = [jnp.concatenate([jnp.where(left, x, 0.0), jnp.where(left, 0.0, x)], axis=0).astype(BF16)
                for x in xp]
        ydiag = [_dot(s, b) for s, b in zip(scores16, bd16)]
        if nseq == 1:
            yoff = [_dot_nt(Cg16, get_h(0, e).astype(BF16)) for e in pairs]
        else:
            yoff = [jnp.concatenate(
                [_dot_nt(Cg[s * glen:(s + 1) * glen].astype(BF16), get_h(s, e).astype(BF16))
                 for s in range(nseq)], axis=0) for e in pairs]
        y = [yd + yo * jnp.exp(a) + dcols_ref[:, e * 2 * P:(e + 1) * 2 * P] * x
             for yd, yo, a, e, x in zip(ydiag, yoff, acol, pairs, xp)]
        xdr = [x * expand(dtrev, GATE_DT + 2 * e) for x, e in zip(xp, pairs)]
        for e, xd in zip(pairs, xdr):
            c0 = GATE_DT + 2 * e
            for s in range(nseq):
                rs = slice(s * glen, (s + 1) * glen)
                ealcol = jnp.where(rowh, eal[s][:, c0:c0 + 1], eal[s][:, c0 + 1:c0 + 2])
                set_h(s, e, get_h(s, e) * ealcol + _dot_tn(xd[rs].astype(BF16), Bg[rs].astype(BF16)))
        yg = jnp.concatenate(y, axis=1) * _silu(ld_z(g))
        gcols = slice(g * GROUP_W, (g + 1) * GROUP_W)
        st_o(g, _rms_rows(yg, nw_ref[:, gcols]))


def _ssm_prompt_parts(xs_ref, bc_ref, zs_ref, gate_ref, cwx_ref, cbx_ref, cwbc_ref, cbbc_ref,
                      gbias_ref, galog_ref, dcols_ref, nw_ref,
                      o_ref, cst_out_ref, hst_out_ref, xpadx_ref, xpadbc_ref, xc_ref, bcc_ref, hh_ref, *, Lb, gates):
    c = CHUNK

    def init():
        hh_ref[...] = jnp.zeros(hh_ref.shape, F32)
        xpadx_ref[0:CONV_PAD, :] = jnp.zeros((CONV_PAD, SSM_DI), F32)
        xpadbc_ref[0:CONV_PAD, :] = jnp.zeros((CONV_PAD, 2 * SSM_BC), F32)

    def conv():
        ident = lambda s, y: y
        _conv_block(xs_ref, xpadx_ref, cwx_ref, cbx_ref, xc_ref, Lb, SSM_DI, ident)
        _conv_block(bc_ref, xpadbc_ref, cwbc_ref, cbbc_ref, bcc_ref, Lb, 2 * SSM_BC, ident)

    def set_h(s, e, val):
        hh_ref[e] = val

    def tile(ci):
        rows = slice(ci * c, (ci + 1) * c)

        def st_o(g, val):
            o_ref[rows, g * GROUP_W:(g + 1) * GROUP_W] = val.astype(o_ref.dtype)

        graw, sp, acum = gates(ci)
        _ssm_tile(
            graw, None, None,
            lambda e: xc_ref[rows, e * LANES:(e + 1) * LANES],
            lambda g: bcc_ref[rows, g * SSM_N:(g + 1) * SSM_N],
            lambda g: bcc_ref[rows, SSM_BC + g * SSM_N:SSM_BC + (g + 1) * SSM_N],
            lambda g: zs_ref[rows, g * GROUP_W:(g + 1) * GROUP_W],
            dcols_ref, nw_ref, st_o, lambda s, e: hh_ref[e], set_h, c, sp_acum=(sp, acum))

    steps = [conv] + [functools.partial(tile, ci) for ci in range(Lb // c)]

    def final():
        cst_out_ref[0, :, :SSM_DI] = xpadx_ref[CONV_HIST:CONV_PAD, :]
        cst_out_ref[0, :, SSM_DI:] = xpadbc_ref[CONV_HIST:CONV_PAD, :]
        hst_out_ref[0] = hh_ref[...]

    return init, steps, final


def _ssm_sample_kernel(xs_ref, bc_ref, zs_ref, gate_ref, cst_ref, hst_ref, cwx_ref, cbx_ref, cwbc_ref, cbbc_ref,
                       gbias_ref, galog_ref, dcols_ref, nw_ref,
                       o_ref, cst_out_ref, hst_out_ref, xpadx_ref, xpadbc_ref, xc_ref, bcc_ref, *, L):
    ident = lambda s, y: y
    for bi in range(CHUNK // L):
        rs = pl.ds(bi * L, L)
        xpx = xpadx_ref.at[bi]
        xpb = xpadbc_ref.at[bi]
        xpx[CONV_HIST:CONV_PAD, :] = cst_ref[:, bi, :SSM_DI]
        xpb[CONV_HIST:CONV_PAD, :] = cst_ref[:, bi, SSM_DI:]
        cst_out_ref[:, bi, :SSM_DI] = _conv_block(
            xs_ref.at[rs], xpx, cwx_ref, cbx_ref, xc_ref.at[rs], L, SSM_DI, ident)
        cst_out_ref[:, bi, SSM_DI:] = _conv_block(
            bc_ref.at[rs], xpb, cwbc_ref, cbbc_ref, bcc_ref.at[rs], L, 2 * SSM_BC, ident)

    def st_o(g, val):
        o_ref[:, g * GROUP_W:(g + 1) * GROUP_W] = val.astype(o_ref.dtype)

    def set_h(s, e, val):
        hst_out_ref[s, e] = val

    _ssm_tile(
        gate_ref[...], gbias_ref[...], -jnp.exp(galog_ref[...]),
        lambda e: xc_ref[:, e * LANES:(e + 1) * LANES],
        lambda g: bcc_ref[:, g * SSM_N:(g + 1) * SSM_N],
        lambda g: bcc_ref[:, SSM_BC + g * SSM_N:SSM_BC + (g + 1) * SSM_N],
        lambda g: zs_ref[:, g * GROUP_W:(g + 1) * GROUP_W],
        dcols_ref, nw_ref, st_o, lambda s, e: hst_ref[s, e], set_h, L)


def _ssm_const_specs():
    const = lambda shape: pl.BlockSpec(shape, lambda *idx: (0,) * len(shape))
    return [
        const((CONV_W, SSM_DI)), const((1, SSM_DI)), const((CONV_W, 2 * SSM_BC)), const((1, 2 * SSM_BC)),
        const((1, LANES)), const((1, LANES)), const((1, SSM_DI)), const((1, SSM_DI)),
    ]


def _ssm_prompt_spec(proj, cwx, cbx, cwbc, cbbc, gbias, galog, dcols, nw, *, B, L, Lb):
    nl = L // Lb
    row = lambda b, l: b * nl + l
    return dict(
        in_specs=[
            pl.BlockSpec((Lb, SSM_DI), lambda b, l: (row(b, l), OFF_XS // SSM_DI)),
            pl.BlockSpec((Lb, 2 * SSM_BC), lambda b, l: (row(b, l), OFF_BC // (2 * SSM_BC))),
            pl.BlockSpec((Lb, SSM_DI), lambda b, l: (row(b, l), OFF_ZS // SSM_DI)),
            pl.BlockSpec((Lb, LANES), lambda b, l: (row(b, l), OFF_GATE // LANES)),
        ] + _ssm_const_specs(),
        out_specs=[
            pl.BlockSpec((Lb, SSM_DI), lambda b, l: (row(b, l), 0)),
            pl.BlockSpec((1, CONV_W - 1, SSM_CONV_CH), lambda b, l: (b, 0, 0)),
            pl.BlockSpec((1, N_PAIRS, 2 * SSM_P, SSM_N), lambda b, l: (b, 0, 0, 0)),
        ],
        out_shape=[
            jax.ShapeDtypeStruct((B * L, SSM_DI), BF16),
            jax.ShapeDtypeStruct((B, CONV_W - 1, SSM_CONV_CH), F32),
            jax.ShapeDtypeStruct((B, N_PAIRS, 2 * SSM_P, SSM_N), F32),
        ],
        scratch=[
            pltpu.VMEM((Lb + CONV_PAD, SSM_DI), F32),
            pltpu.VMEM((Lb + CONV_PAD, 2 * SSM_BC), F32),
            pltpu.VMEM((Lb, SSM_DI), F32),
            pltpu.VMEM((Lb, 2 * SSM_BC), F32),
            pltpu.VMEM((N_PAIRS, 2 * SSM_P, SSM_N), F32),
        ],
        args=[proj, proj, proj, proj, cwx, cbx, cwbc, cbbc, gbias, galog, dcols, nw])


def _ssm_sample_spec(proj, conv_state, h_pairs, cwx, cbx, cwbc, cbbc, gbias, galog, dcols, nw, *, B, L):
    nb = CHUNK // L
    return dict(
        in_specs=[
            pl.BlockSpec((CHUNK, SSM_DI), lambda i: (i, OFF_XS // SSM_DI)),
            pl.BlockSpec((CHUNK, 2 * SSM_BC), lambda i: (i, OFF_BC // (2 * SSM_BC))),
            pl.BlockSpec((CHUNK, SSM_DI), lambda i: (i, OFF_ZS // SSM_DI)),
            pl.BlockSpec((CHUNK, LANES), lambda i: (i, OFF_GATE // LANES)),
            pl.BlockSpec((CONV_W - 1, nb, SSM_CONV_CH), lambda i: (0, i, 0)),
            pl.BlockSpec((nb, N_PAIRS, 2 * SSM_P, SSM_N), lambda i: (i, 0, 0, 0)),
        ] + _ssm_const_specs(),
        out_specs=[
            pl.BlockSpec((CHUNK, SSM_DI), lambda i: (i, 0)),
            pl.BlockSpec((CONV_W - 1, nb, SSM_CONV_CH), lambda i: (0, i, 0)),
            pl.BlockSpec((nb, N_PAIRS, 2 * SSM_P, SSM_N), lambda i: (i, 0, 0, 0)),
        ],
        out_shape=[
            jax.ShapeDtypeStruct((B * L, SSM_DI), BF16),
            jax.ShapeDtypeStruct((CONV_W - 1, B, SSM_CONV_CH), F32),
            jax.ShapeDtypeStruct((B, N_PAIRS, 2 * SSM_P, SSM_N), F32),
        ],
        scratch=[
            pltpu.VMEM((nb, L + CONV_PAD, SSM_DI), F32),
            pltpu.VMEM((nb, L + CONV_PAD, 2 * SSM_BC), F32),
            pltpu.VMEM((CHUNK, SSM_DI), F32),
            pltpu.VMEM((CHUNK, 2 * SSM_BC), F32),
        ],
        args=[proj, proj, proj, proj, conv_state, h_pairs, cwx, cbx, cwbc, cbbc, gbias, galog, dcols, nw])


def _split_refs(refs, g, s, n_cast=0):
    it = iter(refs)
    take = lambda n: [next(it) for _ in range(n)]
    g_in, s_in, c_in = take(len(g["in_specs"])), take(len(s["in_specs"])), take(n_cast)
    g_out, s_out, c_out = take(len(g["out_specs"])), take(len(s["out_specs"])), take(n_cast)
    g_scr, s_scr = take(len(g["scratch"])), take(len(s["scratch"]))
    return g_in + g_out + g_scr, s_in + s_out + s_scr, list(zip(c_in, c_out))


def _mix_prompt_kernel(*refs, g, s, n_cast, Lb):
    g_refs, s_refs, casts = _split_refs(refs, g, s, n_cast)
    def cast_step(src_ref, dst_ref):
        dst_ref[...] = src_ref[...].astype(BF16)

    c_steps = [functools.partial(cast_step, a, b) for a, b in casts]
    gate_ref, gbias_ref, galog_ref = g_refs[2], g_refs[4], g_refs[5]
    memo = {}

    def gates(ci):
        if ci not in memo:
            graw = gate_ref[ci * CHUNK:(ci + 1) * CHUNK, :]
            memo[ci] = (graw,) + _gate_cumsum(graw, gbias_ref[...], -jnp.exp(galog_ref[...]), CHUNK)
        return memo[ci]

    g_init, g_steps, g_final = _gdn_prompt_parts(*g_refs, Lb=Lb, gates=gates)
    s_init, s_steps, s_final = _ssm_prompt_parts(*s_refs, Lb=Lb, gates=gates)
    l = pl.program_id(1)

    @pl.when(l == 0)
    def _init():
        g_init()
        s_init()

    for g_step, s_step, c_step in itertools.zip_longest(g_steps, s_steps, [None, None] + c_steps):
        for step in (g_step, c_step, s_step):
            if step is not None:
                step()

    @pl.when(l == pl.num_programs(1) - 1)
    def _final():
        g_final()
        s_final()


def _mix_sample_kernel(*refs, g, s, nb, L):
    g_refs, s_refs, _ = _split_refs(refs, g, s)
    _gdn_sample_kernel(*g_refs, nb=nb, L=L)
    _ssm_sample_kernel(*s_refs, L=L)


def _fused_call(kernel, g, s, grid, semantics, name, cast=()):
    n_steps = functools.reduce(lambda a, b: a * b, grid)
    step = (lambda b, l: (b * grid[1] + l, 0)) if len(grid) == 2 else (lambda i: (i, 0))
    assert all(w.shape[0] % (16 * n_steps) == 0 for w in cast), "cast row blocks must be whole bf16 tiles"
    c_specs = [pl.BlockSpec((w.shape[0] // n_steps, w.shape[1]), step) for w in cast]
    c_shape = [jax.ShapeDtypeStruct(w.shape, BF16) for w in cast]
    kw = dict(n_cast=len(cast)) if cast else {}
    outs = pl.pallas_call(
        functools.partial(kernel, g={k: g[k] for k in ("in_specs", "out_specs", "scratch")},
                          s={k: s[k] for k in ("in_specs", "out_specs", "scratch")}, **kw),
        grid=grid,
        in_specs=g["in_specs"] + s["in_specs"] + c_specs,
        out_specs=g["out_specs"] + s["out_specs"] + c_specs,
        out_shape=g["out_shape"] + s["out_shape"] + c_shape,
        scratch_shapes=g["scratch"] + s["scratch"],
        compiler_params=pltpu.CompilerParams(dimension_semantics=semantics, vmem_limit_bytes=VMEM_LIMIT),
        name=name,
    )(*g["args"], *s["args"], *cast)
    n, m = len(g["out_specs"]), len(g["out_specs"]) + len(s["out_specs"])
    return (outs[:n], outs[n:m]) + ((outs[m:],) if cast else ())


def _outproj_kernel(x_ref, mg_ref, ms_ref, w_ref, o_ref):
    acc = _dot(mg_ref[...].astype(BF16), w_ref[:GDN_V, :])
    acc = acc + _dot(ms_ref[...].astype(BF16), w_ref[GDN_V:, :])
    o_ref[...] = x_ref[...] + acc


def _out_proj(x2d, mix_g, mix_s, w_out16, *, tm):
    T = x2d.shape[0]
    return pl.pallas_call(
        _outproj_kernel,
        grid=(T // tm,),
        in_specs=[
            pl.BlockSpec((tm, D_MODEL), lambda i: (i, 0)),
            pl.BlockSpec((tm, GDN_V), lambda i: (i, 0)),
            pl.BlockSpec((tm, SSM_DI), lambda i: (i, 0)),
            pl.BlockSpec((D_MODEL, D_MODEL), lambda i: (0, 0)),
        ],
        out_specs=pl.BlockSpec((tm, D_MODEL), lambda i: (i, 0)),
        out_shape=jax.ShapeDtypeStruct((T, D_MODEL), F32),
        compiler_params=pltpu.CompilerParams(
            dimension_semantics=("parallel",), vmem_limit_bytes=VMEM_LIMIT),
        name="out_proj",
    )(x2d, mix_g, mix_s, w_out16)


FFN_SUB = 4


def _ffn_kernel(x_ref, nw_ref, wg_ref, wu_ref, wd_ref, fnw_ref, o_ref, h_ref):
    f = pl.program_id(1)
    nf = pl.num_programs(1)
    rs = h_ref.shape[0] // FFN_SUB

    def step(first, last):
        def gate_up(r):
            rows = slice(r * rs, (r + 1) * rs)
            if first:
                h = _rms_rows(x_ref[rows, :], nw_ref[...]).astype(BF16)
                h_ref[rows, :] = h
            else:
                h = h_ref[rows, :]
            return _dot(h, wg_ref[...]), _dot(h, wu_ref[...])

        def down(r, gu):
            rows = slice(r * rs, (r + 1) * rs)
            d = _dot((_silu(gu[0]) * gu[1]).astype(BF16), wd_ref[...])
            acc = d if first else o_ref[rows, :] + d
            if last:
                o_ref[rows, :] = _rms_rows(x_ref[rows, :] + acc, fnw_ref[...])
            else:
                o_ref[rows, :] = acc

        gu = gate_up(0)
        for r in range(1, FFN_SUB):
            gu_next = gate_up(r)
            down(r - 1, gu)
            gu = gu_next
        down(FFN_SUB - 1, gu)

    pl.when(f == 0)(lambda: step(True, False))
    pl.when((f > 0) & (f < nf - 1))(lambda: step(False, False))
    pl.when(f == nf - 1)(lambda: step(False, True))


def _ffn(x2d, norm_w, wg16, wu16, wd16, final_w, *, tm, tf):
    T = x2d.shape[0]
    return pl.pallas_call(
        _ffn_kernel,
        grid=(T // tm, D_FF // tf),
        in_specs=[
            pl.BlockSpec((tm, D_MODEL), lambda i, f: (i, 0)),
            pl.BlockSpec((1, D_MODEL), lambda i, f: (0, 0)),
            pl.BlockSpec((D_MODEL, tf), lambda i, f: (0, f)),
            pl.BlockSpec((D_MODEL, tf), lambda i, f: (0, f)),
            pl.BlockSpec((tf, D_MODEL), lambda i, f: (f, 0)),
            pl.BlockSpec((1, D_MODEL), lambda i, f: (0, 0)),
        ],
        out_specs=pl.BlockSpec((tm, D_MODEL), lambda i, f: (i, 0)),
        out_shape=jax.ShapeDtypeStruct((T, D_MODEL), F32),
        scratch_shapes=[pltpu.VMEM((tm, D_MODEL), BF16)],
        compiler_params=pltpu.CompilerParams(
            dimension_semantics=("parallel", "arbitrary"), vmem_limit_bytes=VMEM_LIMIT),
        name="ffn",
    )(x2d, norm_w, wg16, wu16, wd16, final_w)


PROMPT_ROWS = 256


def _trunk(x, states, p):
    B, L, _ = x.shape
    x2d = x.reshape(B * L, D_MODEL)
    proj = _in_proj(x2d, p["attn_norm_w"], p["w_in_r"], tm=IN_PROJ_TM, tn=IN_PROJ_TN)
    gdn_w = (p["gdn_conv_w"], p["gbias"], p["galog"], p["gdn_norm_w"])
    ssm_w = (p["cwx"], p["cbx"], p["cwbc"], p["cbbc"], p["gbias"], p["galog"], p["dcols"], p["ssm_norm_w"])
    pair_shape = (B, N_PAIRS, 2 * SSM_P, SSM_N)
    if states is None:
        Lb = PROMPT_ROWS
        g = _gdn_prompt_spec(proj, *gdn_w, B=B, L=L, Lb=Lb)
        s = _ssm_prompt_spec(proj, *ssm_w, B=B, L=L, Lb=Lb)
        kern = functools.partial(_mix_prompt_kernel, Lb=Lb)
        f32_weights = (p["w_gate"], p["w_up"], p["w_down"], p["w_out"])
        (mix_g, gconv_new, gS_new), (mix_s, sconv_new, sh_new), bf16_weights = _fused_call(
            kern, g, s, (B, L // Lb), ("parallel", "arbitrary"), "mix_prompt", cast=f32_weights)
        p["wg16"], p["wu16"], p["wd16"], p["w_out16"] = bf16_weights
    else:
        gconv, gS, sconv, sh = states
        nb = CHUNK // L
        tap_major = lambda a: jnp.swapaxes(a, 0, 1)
        g = _gdn_sample_spec(proj, tap_major(gconv), gS, *gdn_w, B=B, L=L, nb=nb)
        s = _ssm_sample_spec(proj, tap_major(sconv), sh.reshape(pair_shape), *ssm_w, B=B, L=L)
        kern = functools.partial(_mix_sample_kernel, nb=nb, L=L)
        (mix_g, gconv_new, gS_new), (mix_s, sconv_new, sh_new) = _fused_call(
            kern, g, s, (B // nb,), ("parallel",), "mix_sample")
        gconv_new, sconv_new = tap_major(gconv_new), tap_major(sconv_new)
    sh_new = sh_new.reshape(B, SSM_HEADS, SSM_P, SSM_N)
    x1 = _out_proj(x2d, mix_g, mix_s, p["w_out16"], tm=OUT_PROJ_TM)
    y = _ffn(x1, p["ffn_norm_w"], p["wg16"], p["wu16"], p["wd16"], p["final_norm_w"], tm=FFN_TM, tf=FFN_TF)
    return y.reshape(B, L, D_MODEL), (gconv_new[None], gS_new[None], sconv_new[None], sh_new[None])


def kernel(x_prompt, x_sample, state_gdn_conv, state_gdn, state_ssm_conv, state_ssm,
           attn_norm_w, w_in, gdn_conv_w, gdn_A_log, gdn_dt_bias, gdn_norm_w,
           ssm_conv_w, ssm_conv_b, ssm_A_log, ssm_dt_bias, ssm_D, ssm_norm_w,
           w_out, ffn_norm_w, w_gate, w_up, w_down, final_norm_w):
    assert w_in.shape[0] == 1, "single-layer trunk"
    assert x_prompt.shape[1] % PROMPT_ROWS == 0 and CHUNK % x_sample.shape[1] == 0 and x_sample.shape[0] % (CHUNK // x_sample.shape[1]) == 0
    assert w_in.shape[2] == D_IN_PROJ
    w_in_r = _w_in_prep(jnp.swapaxes(w_in, 1, 2), tk=W_PREP_TK)
    zeros8 = jnp.zeros((GDN_HEADS,), F32)
    tail = jnp.zeros((LANES - GATE_DT2 - SSM_HEADS,), F32)
    gbias = jnp.concatenate([zeros8, gdn_dt_bias[0], ssm_dt_bias[0], ssm_dt_bias[0], tail])[None]
    galog = jnp.concatenate([zeros8, gdn_A_log[0], ssm_A_log[0], ssm_A_log[0], tail])[None]
    p = dict(
        attn_norm_w=attn_norm_w, w_in_r=w_in_r, gdn_conv_w=gdn_conv_w[0], gbias=gbias, galog=galog,
        gdn_norm_w=gdn_norm_w,
        cwx=ssm_conv_w[0][:, :SSM_DI], cbx=ssm_conv_b[:, :SSM_DI],
        cwbc=ssm_conv_w[0][:, SSM_DI:], cbbc=ssm_conv_b[:, SSM_DI:],
        dcols=jnp.repeat(ssm_D[0], SSM_P)[None], ssm_norm_w=ssm_norm_w,
        w_out=w_out[0], w_down=w_down[0], ffn_norm_w=ffn_norm_w,
        w_gate=w_gate[0], w_up=w_up[0],
        final_norm_w=final_norm_w[None],
    )
    y_p, st_p = _trunk(x_prompt, None, p)
    y_s, st_s = _trunk(x_sample, (state_gdn_conv[0], state_gdn[0], state_ssm_conv[0], state_ssm[0]), p)
    return (y_p, y_s, st_p[0], st_p[1], st_p[2], st_p[3], st_s[0], st_s[1], st_s[2], st_s[3])
```

```python
import functools
import itertools

import jax
import jax.numpy as jnp
from jax import lax
from jax.experimental import pallas as pl
from jax.experimental.pallas import tpu as pltpu

F32 = jnp.float32
BF16 = jnp.bfloat16

D_MODEL = 2048
GDN_HEADS = 8
GDN_DK = 128
GDN_DV = 128
GDN_QK = GDN_HEADS * GDN_DK
GDN_V = GDN_HEADS * GDN_DV
GDN_CONV_CH = 2 * GDN_QK + GDN_V
SSM_P = 64
SSM_N = 128
SSM_GROUPS = 2
SSM_DI = 1024
SSM_HEADS = SSM_DI // SSM_P
SSM_BC = SSM_GROUPS * SSM_N
SSM_CONV_CH = SSM_DI + 2 * SSM_BC
CONV_W = 4
CHUNK = 64
D_FF = 5632
EPS = 1e-6

OFF_QKV = 0
OFF_ZG = OFF_QKV + GDN_CONV_CH
OFF_ZS = OFF_ZG + GDN_V
OFF_XS = OFF_ZS + SSM_DI
OFF_BC = OFF_XS + SSM_DI
OFF_GATE = OFF_BC + 2 * SSM_BC
LANES = 128
SUBLANES = 8
CONV_PAD = SUBLANES
CONV_HIST = CONV_PAD - (CONV_W - 1)
GATE_B = 0
GATE_A = GATE_B + GDN_HEADS
GATE_DT = GATE_A + GDN_HEADS
GATE_DT2 = GATE_DT + SSM_HEADS
N_PROJ = 6912

VMEM_LIMIT = 52 * 1024 * 1024
W_PREP_TK = 256
IN_PROJ_TM, IN_PROJ_TN = 1024, 768
OUT_PROJ_TM = 512
FFN_TM, FFN_TF = 1024, 512

def _tril_matmul_f32(tril_f, x):
    n = x.shape[1]
    x1 = x.astype(BF16)
    r1 = x - x1.astype(F32)
    x2 = r1.astype(BF16)
    x3 = (r1 - x2.astype(F32)).astype(BF16)
    y = _dot(tril_f.astype(BF16), jnp.concatenate([x1, x2, x3], axis=1))
    return y[:, :n] + y[:, n:2 * n] + y[:, 2 * n:]


def _silu(x):
    h = 0.5 * x
    return h + h * jnp.tanh(h)


def _softplus(x):
    return jnp.maximum(x, 0.0) + jnp.log1p(jnp.exp(-jnp.abs(x)))


def _dot(a, b):
    return jnp.dot(a, b, preferred_element_type=F32)


def _dot_nt(a, b):
    return lax.dot_general(a, b, (((1,), (1,)), ((), ())), preferred_element_type=F32)


def _dot_tn(a, b):
    return lax.dot_general(a, b, (((0,), (0,)), ((), ())), preferred_element_type=F32)


def _rms_rows(x, w):
    return x * lax.rsqrt(jnp.mean(x * x, axis=-1, keepdims=True) + EPS) * w


W_B = OFF_ZS
W_ZS = W_B + 2 * GDN_HEADS
W_DT = W_ZS + SSM_DI + SSM_CONV_CH
D_IN_PROJ = W_DT + SSM_HEADS


def _wprep_kernel(w_ref, o_ref):
    cols = o_ref.shape[1]
    o_ref[:W_B, :] = w_ref[0, :W_B, :].astype(BF16)
    o_ref[W_B:OFF_GATE, :] = w_ref[0, W_ZS:W_DT, :].astype(BF16)
    dt = w_ref[0, W_DT:D_IN_PROJ, :].astype(BF16)
    o_ref[OFF_GATE:OFF_GATE + GATE_DT, :] = w_ref[0, W_B:W_ZS, :].astype(BF16)
    o_ref[OFF_GATE + GATE_DT:OFF_GATE + GATE_DT2, :] = dt
    o_ref[OFF_GATE + GATE_DT2:OFF_GATE + GATE_DT2 + SSM_HEADS, :] = dt
    o_ref[OFF_GATE + GATE_DT2 + SSM_HEADS:, :] = jnp.zeros((N_PROJ - OFF_GATE - GATE_DT2 - SSM_HEADS, cols), BF16)


def _w_in_prep(w_in_t, *, tk):
    return pl.pallas_call(
        _wprep_kernel,
        grid=(D_MODEL // tk,),
        in_specs=[pl.BlockSpec((1, D_IN_PROJ, tk), lambda i: (0, 0, i))],
        out_specs=pl.BlockSpec((N_PROJ, tk), lambda i: (0, i)),
        out_shape=jax.ShapeDtypeStruct((N_PROJ, D_MODEL), BF16),
        compiler_params=pltpu.CompilerParams(
            dimension_semantics=("parallel",), vmem_limit_bytes=VMEM_LIMIT),
        name="w_in_prep",
    )(w_in_t)


INPROJ_SUB = 4


def _inproj_kernel(x_ref, nw_ref, wt_ref, o_ref, h_ref):
    j = pl.program_id(1)

    @pl.when(j == 0)
    def _first():
        rs = h_ref.shape[0] // INPROJ_SUB
        for r in range(INPROJ_SUB):
            rows = slice(r * rs, (r + 1) * rs)
            h = _rms_rows(x_ref[rows, :], nw_ref[...]).astype(BF16)
            h_ref[rows, :] = h
            o_ref[rows, :] = _dot_nt(h, wt_ref[...])

    @pl.when(j > 0)
    def _rest():
        o_ref[...] = _dot_nt(h_ref[...], wt_ref[...])


def _in_proj(x2d, norm_w, w_in_r, *, tm, tn):
    T = x2d.shape[0]
    return pl.pallas_call(
        _inproj_kernel,
        grid=(T // tm, N_PROJ // tn),
        in_specs=[
            pl.BlockSpec((tm, D_MODEL), lambda i, j: (i, 0)),
            pl.BlockSpec((1, D_MODEL), lambda i, j: (0, 0)),
            pl.BlockSpec((tn, D_MODEL), lambda i, j: (j, 0)),
        ],
        out_specs=pl.BlockSpec((tm, tn), lambda i, j: (i, j)),
        out_shape=jax.ShapeDtypeStruct((T, N_PROJ), F32),
        scratch_shapes=[pltpu.VMEM((tm, D_MODEL), BF16)],
        compiler_params=pltpu.CompilerParams(
            dimension_semantics=("parallel", "arbitrary"), vmem_limit_bytes=VMEM_LIMIT),
        name="in_proj",
    )(x2d, norm_w, w_in_r)


def _conv_block(x_ref, xpad_ref, cw_ref, cb_ref, dst_ref, Lb, C, post):
    xpad_ref[CONV_PAD:CONV_PAD + Lb, :] = x_ref[...]
    rs = min(Lb, CHUNK)
    for sb in range(Lb // rs):
        r = sb * rs
        for s in range(C // LANES):
            cols = slice(s * LANES, (s + 1) * LANES)
            acc = xpad_ref[CONV_PAD + r:CONV_PAD + r + rs, cols] * cw_ref[CONV_W - 1:CONV_W, cols]
            for i in range(CONV_W - 1):
                acc = acc + xpad_ref[CONV_HIST + i + r:CONV_HIST + i + r + rs, cols] * cw_ref[i:i + 1, cols]
            if cb_ref is not None:
                acc = acc + cb_ref[:, cols]
            dst_ref[r:r + rs, cols] = post(s, _silu(acc))
    hist = xpad_ref[Lb + CONV_HIST:Lb + CONV_PAD, :]
    xpad_ref[CONV_HIST:CONV_PAD, :] = hist
    return hist


STACK = 128


def _gdn_qk_post(s, y):
    if s < 2 * GDN_HEADS:
        y = y * lax.rsqrt(jnp.sum(y * y, axis=-1, keepdims=True) + EPS)
        if s < GDN_HEADS:
            y = y * (GDN_DK ** -0.5)
    return y


def _gdn_local(items, glen):
    sh = glen.bit_length() - 1
    row = lax.broadcasted_iota(jnp.int32, (STACK, STACK), 0)
    col = lax.broadcasted_iota(jnp.int32, (STACK, STACK), 1)
    same = (row >> sh) == (col >> sh)
    incl = same & (row >= col)
    strict = same & (row > col)
    eye = (row == col).astype(F32)

    g_rows = [it["g_row"] if "g_row" in it else it["g"].T for it in items]
    decay = [jnp.exp(jnp.where(incl, it["g"] - gr, -jnp.inf)) for it, gr in zip(items, g_rows)]
    kb = [it["k"] * it["beta"] for it in items]
    qkk = [_dot_nt(jnp.concatenate([it["q"], b], axis=0).astype(BF16), it["k"].astype(BF16))
           for it, b in zip(items, kb)]
    qk = [x[:STACK] * d for x, d in zip(qkk, decay)]
    nmat = [jnp.where(strict, -(x[STACK:] * d), 0.0) for x, d in zip(qkk, decay)]
    tinv = [eye + n for n in nmat]
    if sh >= 2:
        pw = [_dot(n.astype(BF16), n.astype(BF16)) for n in nmat]
        for _ in range(sh - 2):
            x = [_dot(jnp.concatenate([t, p], axis=0).astype(BF16), p.astype(BF16))
                 for t, p in zip(tinv, pw)]
            tinv = [t + y[:STACK] for t, y in zip(tinv, x)]
            pw = [y[STACK:] for y in x]
        tinv = [t + _dot(t.astype(BF16), p.astype(BF16)) for t, p in zip(tinv, pw)]
    uw = [_dot(t.astype(BF16),
               jnp.concatenate([it["v"] * it["beta"], b * it["eg"]], axis=1).astype(BF16))
          for t, it, b in zip(tinv, items, kb)]
    return [(x[:, :GDN_DV], x[:, GDN_DV:]) for x in uw], qk


GDN_LOCAL_CHUNKS = 2


def _gdn_prompt_parts(qkv_ref, zg_ref, gate_ref, cw_ref, gbias_ref, galog_ref, nw_ref,
                      o_ref, cst_out_ref, sst_out_ref,
                      xpad_ref, qkvc_ref, s_ref, u_ref, wq16_ref, kd16_ref, qk16_ref, egl_ref, *, Lb, gates):
    c = CHUNK

    def init():
        xpad_ref[0:CONV_PAD, :] = jnp.zeros((CONV_PAD, GDN_CONV_CH), F32)
        s_ref[...] = jnp.zeros(s_ref.shape, F32)

    def conv():
        _conv_block(qkv_ref, xpad_ref, cw_ref, None, qkvc_ref, Lb, GDN_CONV_CH, _gdn_qk_post)

    nw = nw_ref[...]
    n_pairs = GDN_HEADS // 2

    def local_chunk_items(ci):
        rows = slice(ci * c, (ci + 1) * c)
        graw, _, G = gates(ci)
        beta_all = jax.nn.sigmoid(graw)
        glast = G[c - 1:c, :]
        egl_ref[ci] = jnp.broadcast_to(jnp.exp(glast), (SUBLANES, LANES))

        def heads(off, a, b):
            return jnp.concatenate([qkvc_ref[rows, off + a * LANES:off + (a + 1) * LANES],
                                    qkvc_ref[rows, off + b * LANES:off + (b + 1) * LANES]], axis=0)

        def colstack(m, a, b):
            return jnp.concatenate([jnp.broadcast_to(m[:, a:a + 1], (c, LANES)),
                                    jnp.broadcast_to(m[:, b:b + 1], (c, LANES))], axis=0)

        gt2 = jnp.concatenate([G, G], axis=0).T
        first_head = lax.broadcasted_iota(jnp.int32, (1, STACK), 1) < c
        items = []
        for pr in range(n_pairs):
            a, b = 2 * pr, 2 * pr + 1
            g_tile = colstack(G, GATE_A + a, GATE_A + b)
            g_row = jnp.where(first_head, gt2[GATE_A + a:GATE_A + a + 1, :], gt2[GATE_A + b:GATE_A + b + 1, :])
            items.append(dict(
                q=heads(0, a, b), k=heads(GDN_QK, a, b), v=heads(2 * GDN_QK, a, b),
                beta=colstack(beta_all, GATE_B + a, GATE_B + b),
                g=g_tile, g_row=g_row, eg=jnp.exp(g_tile),
                egrev=jnp.exp(colstack(glast, GATE_A + a, GATE_A + b) - g_tile)))
        return items

    def local_group(cis):
        items = [it for ci in cis for it in local_chunk_items(ci)]
        uw, qk = _gdn_local(items, c)
        for n, it in enumerate(items):
            idx = cis[0] * n_pairs + n
            u, w = uw[n]
            qd = it["q"] * it["eg"]
            u_ref[idx] = u
            for hh in range(2):
                hs = slice(hh * c, (hh + 1) * c)
                wq16_ref[2 * idx + hh] = jnp.concatenate([w[hs], qd[hs]], axis=0).astype(BF16)
            kd16_ref[idx] = (it["k"] * it["egrev"]).astype(BF16)
            qk16_ref[idx] = qk[n].astype(BF16)

    def recurrent(ci):
        rows = slice(ci * c, (ci + 1) * c)
        egl = egl_ref[ci][0:1]
        r = [[_dot(wq16_ref[2 * (ci * n_pairs + pr) + hh], s_ref[2 * pr + hh].astype(BF16)) for hh in range(2)]
             for pr in range(n_pairs)]
        v16 = [(u_ref[ci * n_pairs + pr] - jnp.concatenate([r[pr][0][:c], r[pr][1][:c]], axis=0)).astype(BF16)
               for pr in range(n_pairs)]
        o = [jnp.concatenate([r[pr][0][c:], r[pr][1][c:]], axis=0) + _dot(qk16_ref[ci * n_pairs + pr], v16[pr])
             for pr in range(n_pairs)]
        for pr in range(n_pairs):
            kd16 = kd16_ref[ci * n_pairs + pr]
            for hh in range(2):
                h = 2 * pr + hh
                ga = GATE_A + h
                hs = slice(hh * c, (hh + 1) * c)
                s_ref[h] = s_ref[h] * egl[:, ga:ga + 1] + _dot_tn(kd16[hs], v16[pr][hs])
        for pr in range(n_pairs):
            for hh in range(2):
                h = 2 * pr + hh
                z = zg_ref[rows, h * GDN_DV:(h + 1) * GDN_DV]
                o_ref[rows, h * GDN_DV:(h + 1) * GDN_DV] = (
                    _rms_rows(o[pr][hh * c:(hh + 1) * c], nw) * _silu(z)).astype(o_ref.dtype)

    steps = [conv]
    for g0 in range(0, Lb // c, GDN_LOCAL_CHUNKS):
        cis = list(range(g0, g0 + GDN_LOCAL_CHUNKS))
        steps.append(functools.partial(local_group, cis))
        steps += [functools.partial(recurrent, ci) for ci in cis]

    def final():
        cst_out_ref[0] = xpad_ref[CONV_HIST:CONV_PAD, :]
        sst_out_ref[0] = s_ref[...]

    return init, steps, final


def _gdn_prompt_spec(proj, cw, gbias, galog, nw, *, B, L, Lb):
    nl = L // Lb
    n_tiles = (Lb // CHUNK) * (GDN_HEADS // 2)
    row = lambda b, l: b * nl + l
    const = lambda shape: pl.BlockSpec(shape, lambda b, l: (0,) * len(shape))
    return dict(
        in_specs=[
            pl.BlockSpec((Lb, GDN_CONV_CH), lambda b, l: (row(b, l), OFF_QKV // GDN_CONV_CH)),
            pl.BlockSpec((Lb, GDN_V), lambda b, l: (row(b, l), OFF_ZG // GDN_V)),
            pl.BlockSpec((Lb, LANES), lambda b, l: (row(b, l), OFF_GATE // LANES)),
            const((CONV_W, GDN_CONV_CH)), const((1, LANES)), const((1, LANES)), const((1, GDN_DV)),
        ],
        out_specs=[
            pl.BlockSpec((Lb, GDN_V), lambda b, l: (row(b, l), 0)),
            pl.BlockSpec((1, CONV_W - 1, GDN_CONV_CH), lambda b, l: (b, 0, 0)),
            pl.BlockSpec((1, GDN_HEADS, GDN_DK, GDN_DV), lambda b, l: (b, 0, 0, 0)),
        ],
        out_shape=[
            jax.ShapeDtypeStruct((B * L, GDN_V), BF16),
            jax.ShapeDtypeStruct((B, CONV_W - 1, GDN_CONV_CH), F32),
            jax.ShapeDtypeStruct((B, GDN_HEADS, GDN_DK, GDN_DV), F32),
        ],
        scratch=[
            pltpu.VMEM((Lb + CONV_PAD, GDN_CONV_CH), F32),
            pltpu.VMEM((Lb, GDN_CONV_CH), F32),
            pltpu.VMEM((GDN_HEADS, GDN_DK, GDN_DV), F32),
            pltpu.VMEM((n_tiles, STACK, GDN_DV), F32),
            pltpu.VMEM((2 * n_tiles, STACK, GDN_DK), BF16),
            pltpu.VMEM((n_tiles, STACK, GDN_DK), BF16),
            pltpu.VMEM((n_tiles, STACK, STACK), BF16),
            pltpu.VMEM((Lb // CHUNK, SUBLANES, LANES), F32),
        ],
        args=[proj, proj, proj, cw, gbias, galog, nw])


def _gdn_sample_kernel(qkv_ref, zg_ref, gate_ref, cst_ref, sst_ref, cw_ref, gbias_ref, galog_ref, nw_ref,
                       o_ref, cst_out_ref, sst_out_ref, xpad_ref, qkvc_ref, *, nb, L):
    R = nb * L
    sh = L.bit_length() - 1
    for bi in range(nb):
        xp = xpad_ref.at[bi]
        xp[CONV_HIST:CONV_PAD, :] = cst_ref[:, bi, :]
        cst_out_ref[:, bi, :] = _conv_block(
            qkv_ref.at[pl.ds(bi * L, L)], xp, cw_ref, None, qkvc_ref.at[pl.ds(bi * L, L)],
            L, GDN_CONV_CH, _gdn_qk_post)

    row_i = lax.broadcasted_iota(jnp.int32, (R, R), 0)
    col_i = lax.broadcasted_iota(jnp.int32, (R, R), 1)
    tril_f = (((row_i >> sh) == (col_i >> sh)) & (row_i >= col_i)).astype(F32)
    graw = gate_ref[...]
    sp = _softplus(graw + gbias_ref[...])
    beta_all = jax.nn.sigmoid(graw)
    G = _tril_matmul_f32(tril_f, -jnp.exp(galog_ref[...]) * sp)
    glast = [G[bi * L + L - 1:bi * L + L, :] for bi in range(nb)]
    glast_rows = jnp.concatenate([jnp.broadcast_to(x, (L, LANES)) for x in glast], axis=0)
    egl = [jnp.exp(x) for x in glast]
    nw = nw_ref[...]

    n_st = R // (2 * L)

    def tiles(ref, st, off):
        return jnp.concatenate(
            [ref[st * 2 * L:(st + 1) * 2 * L, off + h * LANES:off + (h + 1) * LANES]
             for h in range(GDN_HEADS)], axis=0)

    def colstack(m, st, off):
        return jnp.concatenate(
            [jnp.broadcast_to(m[st * 2 * L:(st + 1) * 2 * L, off + h:off + h + 1], (2 * L, LANES))
             for h in range(GDN_HEADS)], axis=0)

    items = []
    for st in range(n_st):
        g_tile = colstack(G, st, GATE_A)
        items.append(dict(
            q=tiles(qkvc_ref, st, 0), k=tiles(qkvc_ref, st, GDN_QK), v=tiles(qkvc_ref, st, 2 * GDN_QK),
            beta=colstack(beta_all, st, GATE_B), g=g_tile, eg=jnp.exp(g_tile),
            egrev=jnp.exp(colstack(glast_rows, st, GATE_A) - g_tile)))
    uw, qk = _gdn_local(items, L)

    groups = [(h, bi) for h in range(GDN_HEADS) for bi in range(2)]
    r = []
    for st, it in enumerate(items):
        w = uw[st][1]
        qd = it["q"] * it["eg"]
        r.append([
            _dot(jnp.concatenate([w[gi * L:(gi + 1) * L], qd[gi * L:(gi + 1) * L]], axis=0).astype(BF16),
                 sst_ref[2 * st + bi, h].astype(BF16))
            for gi, (h, bi) in enumerate(groups)])
    v_new = [uw[st][0] - jnp.concatenate([x[:L] for x in r[st]], axis=0) for st in range(n_st)]
    o = [jnp.concatenate([x[L:] for x in r[st]], axis=0)
         + _dot(qk[st].astype(BF16), v_new[st].astype(BF16)) for st in range(n_st)]
    for st, it in enumerate(items):
        kd = it["k"] * it["egrev"]
        for gi, (h, bi) in enumerate(groups):
            b = 2 * st + bi
            ga = GATE_A + h
            rs = slice(gi * L, (gi + 1) * L)
            sst_out_ref[b, h] = (sst_ref[b, h] * egl[b][:, ga:ga + 1]
                                 + _dot_tn(kd[rs].astype(BF16), v_new[st][rs].astype(BF16)))
    for st in range(n_st):
        out = (_rms_rows(o[st], nw) * _silu(tiles(zg_ref, st, 0))).astype(o_ref.dtype)
        for h in range(GDN_HEADS):
            o_ref[st * 2 * L:(st + 1) * 2 * L, h * GDN_DV:(h + 1) * GDN_DV] = out[h * 2 * L:(h + 1) * 2 * L]


def _gdn_sample_spec(proj, conv_state, S_state, cw, gbias, galog, nw, *, B, L, nb):
    R = nb * L
    const = lambda shape: pl.BlockSpec(shape, lambda i: (0,) * len(shape))
    return dict(
        in_specs=[
            pl.BlockSpec((R, GDN_CONV_CH), lambda i: (i, OFF_QKV // GDN_CONV_CH)),
            pl.BlockSpec((R, GDN_V), lambda i: (i, OFF_ZG // GDN_V)),
            pl.BlockSpec((R, LANES), lambda i: (i, OFF_GATE // LANES)),
            pl.BlockSpec((CONV_W - 1, nb, GDN_CONV_CH), lambda i: (0, i, 0)),
            pl.BlockSpec((nb, GDN_HEADS, GDN_DK, GDN_DV), lambda i: (i, 0, 0, 0)),
            const((CONV_W, GDN_CONV_CH)), const((1, LANES)), const((1, LANES)), const((1, GDN_DV)),
        ],
        out_specs=[
            pl.BlockSpec((R, GDN_V), lambda i: (i, 0)),
            pl.BlockSpec((CONV_W - 1, nb, GDN_CONV_CH), lambda i: (0, i, 0)),
            pl.BlockSpec((nb, GDN_HEADS, GDN_DK, GDN_DV), lambda i: (i, 0, 0, 0)),
        ],
        out_shape=[
            jax.ShapeDtypeStruct((B * L, GDN_V), BF16),
            jax.ShapeDtypeStruct((CONV_W - 1, B, GDN_CONV_CH), F32),
            jax.ShapeDtypeStruct((B, GDN_HEADS, GDN_DK, GDN_DV), F32),
        ],
        scratch=[
            pltpu.VMEM((nb, L + CONV_PAD, GDN_CONV_CH), F32),
            pltpu.VMEM((R, GDN_CONV_CH), F32),
        ],
        args=[proj, proj, proj, conv_state, S_state, cw, gbias, galog, nw])


N_PAIRS = SSM_HEADS // 2
PAIRS_PER_GROUP = N_PAIRS // SSM_GROUPS
GROUP_W = SSM_DI // SSM_GROUPS


def _gate_cumsum(graw, gbias, nega, glen):
    c = graw.shape[0]
    sh = glen.bit_length() - 1
    ri = lax.broadcasted_iota(jnp.int32, (c, c), 0)
    ci = lax.broadcasted_iota(jnp.int32, (c, c), 1)
    tril_f = (((ri >> sh) == (ci >> sh)) & (ri >= ci)).astype(F32)
    sp = _softplus(graw + gbias)
    return sp, _tril_matmul_f32(tril_f, nega * sp)


def _ssm_tile(graw, gbias, nega, ld_x, ld_b, ld_c, ld_z, dcols_ref, nw_ref, st_o, get_h, set_h, glen,
              sp_acum=None):
    c = CHUNK
    P = SSM_P
    nseq = c // glen
    sh = glen.bit_length() - 1
    sp, acum = sp_acum if sp_acum is not None else _gate_cumsum(graw, gbias, nega, glen)
    lasts = [acum[s * glen + glen - 1:(s + 1) * glen, :] for s in range(nseq)]
    alast = jnp.concatenate([jnp.broadcast_to(x, (glen, LANES)) for x in lasts], axis=0)
    dtrev = sp * jnp.exp(alast - acum)
    eal = [jnp.exp(x) for x in lasts]
    lane = lax.broadcasted_iota(jnp.int32, (c, LANES), 1)
    row = lax.broadcasted_iota(jnp.int32, (c, LANES), 0)
    m = jnp.where(lane < GATE_DT2, acum, sp)
    mt = jnp.concatenate([m, m], axis=0).T
    left = lane < P
    left_row = left[0:1]
    j = jnp.where(left, lane, lane - P)
    tril2 = ((row >> sh) == (j >> sh)) & (row >= j)
    rowh = lax.broadcasted_iota(jnp.int32, (2 * P, SSM_N), 0) < P

    def expand(mat, c0):
        return jnp.where(left, jnp.broadcast_to(mat[:, c0:c0 + 1], (c, LANES)),
                         jnp.broadcast_to(mat[:, c0 + 1:c0 + 2], (c, LANES)))

    def rowsel(base, e):
        return jnp.where(left_row, mt[base + 2 * e:base + 2 * e + 1, :], mt[base + 2 * e + 1:base + 2 * e + 2, :])

    for g in range(SSM_GROUPS):
        Bg = ld_b(g)
        Cg = ld_c(g)
        Bg16 = Bg.astype(BF16)
        Cg16 = Cg.astype(BF16)
        cb2 = _dot_nt(Cg16, jnp.concatenate([Bg16, Bg16], axis=0))
        pairs = [g * PAIRS_PER_GROUP + e4 for e4 in range(PAIRS_PER_GROUP)]
        acol = [expand(acum, GATE_DT + 2 * e) for e in pairs]
        scores16 = [
            (cb2 * jnp.exp(jnp.where(tril2, a - rowsel(GATE_DT, e), -jnp.inf)) * rowsel(GATE_DT2, e)).astype(BF16)
            for a, e in zip(acol, pairs)]
        xp = [ld_x(e) for e in pairs]
        bd16 = [jnp.concatenate([jnp.where(left, x, 0.0), jnp.where(left, 0.0, x)], axis=0).astype(BF16)
                for x in xp]
        ydiag = [_dot(s, b) for s, b in zip(scores16, bd16)]
        if nseq == 1:
            yoff = [_dot_nt(Cg16, get_h(0, e).astype(BF16)) for e in pairs]
        else:
            yoff = [jnp.concatenate(
                [_dot_nt(Cg[s * glen:(s + 1) * glen].astype(BF16), get_h(s, e).astype(BF16))
                 for s in range(nseq)], axis=0) for e in pairs]
        y = [yd + yo * jnp.exp(a) + dcols_ref[:, e * 2 * P:(e + 1) * 2 * P] * x
             for yd, yo, a, e, x in zip(ydiag, yoff, acol, pairs, xp)]
        xdr = [x * expand(dtrev, GATE_DT + 2 * e) for x, e in zip(xp, pairs)]
        for e, xd in zip(pairs, xdr):
            c0 = GATE_DT + 2 * e
            for s in range(nseq):
                rs = slice(s * glen, (s + 1) * glen)
                ealcol = jnp.where(rowh, eal[s][:, c0:c0 + 1], eal[s][:, c0 + 1:c0 + 2])
                set_h(s, e, get_h(s, e) * ealcol + _dot_tn(xd[rs].astype(BF16), Bg[rs].astype(BF16)))
        yg = jnp.concatenate(y, axis=1) * _silu(ld_z(g))
        gcols = slice(g * GROUP_W, (g + 1) * GROUP_W)
        st_o(g, _rms_rows(yg, nw_ref[:, gcols]))


def _ssm_prompt_parts(xs_ref, bc_ref, zs_ref, gate_ref, cwx_ref, cbx_ref, cwbc_ref, cbbc_ref,
                      gbias_ref, galog_ref, dcols_ref, nw_ref,
                      o_ref, cst_out_ref, hst_out_ref, xpadx_ref, xpadbc_ref, xc_ref, bcc_ref, hh_ref, *, Lb, gates):
    c = CHUNK

    def init():
        hh_ref[...] = jnp.zeros(hh_ref.shape, F32)
        xpadx_ref[0:CONV_PAD, :] = jnp.zeros((CONV_PAD, SSM_DI), F32)
        xpadbc_ref[0:CONV_PAD, :] = jnp.zeros((CONV_PAD, 2 * SSM_BC), F32)

    def conv():
        ident = lambda s, y: y
        _conv_block(xs_ref, xpadx_ref, cwx_ref, cbx_ref, xc_ref, Lb, SSM_DI, ident)
        _conv_block(bc_ref, xpadbc_ref, cwbc_ref, cbbc_ref, bcc_ref, Lb, 2 * SSM_BC, ident)

    def set_h(s, e, val):
        hh_ref[e] = val

    def tile(ci):
        rows = slice(ci * c, (ci + 1) * c)

        def st_o(g, val):
            o_ref[rows, g * GROUP_W:(g + 1) * GROUP_W] = val.astype(o_ref.dtype)

        graw, sp, acum = gates(ci)
        _ssm_tile(
            graw, None, None,
            lambda e: xc_ref[rows, e * LANES:(e + 1) * LANES],
            lambda g: bcc_ref[rows, g * SSM_N:(g + 1) * SSM_N],
            lambda g: bcc_ref[rows, SSM_BC + g * SSM_N:SSM_BC + (g + 1) * SSM_N],
            lambda g: zs_ref[rows, g * GROUP_W:(g + 1) * GROUP_W],
            dcols_ref, nw_ref, st_o, lambda s, e: hh_ref[e], set_h, c, sp_acum=(sp, acum))

    steps = [conv] + [functools.partial(tile, ci) for ci in range(Lb // c)]

    def final():
        cst_out_ref[0, :, :SSM_DI] = xpadx_ref[CONV_HIST:CONV_PAD, :]
        cst_out_ref[0, :, SSM_DI:] = xpadbc_ref[CONV_HIST:CONV_PAD, :]
        hst_out_ref[0] = hh_ref[...]

    return init, steps, final


def _ssm_sample_kernel(xs_ref, bc_ref, zs_ref, gate_ref, cst_ref, hst_ref, cwx_ref, cbx_ref, cwbc_ref, cbbc_ref,
                       gbias_ref, galog_ref, dcols_ref, nw_ref,
                       o_ref, cst_out_ref, hst_out_ref, xpadx_ref, xpadbc_ref, xc_ref, bcc_ref, *, L):
    ident = lambda s, y: y
    for bi in range(CHUNK // L):
        rs = pl.ds(bi * L, L)
        xpx = xpadx_ref.at[bi]
        xpb = xpadbc_ref.at[bi]
        xpx[CONV_HIST:CONV_PAD, :] = cst_ref[:, bi, :SSM_DI]
        xpb[CONV_HIST:CONV_PAD, :] = cst_ref[:, bi, SSM_DI:]
        cst_out_ref[:, bi, :SSM_DI] = _conv_block(
            xs_ref.at[rs], xpx, cwx_ref, cbx_ref, xc_ref.at[rs], L, SSM_DI, ident)
        cst_out_ref[:, bi, SSM_DI:] = _conv_block(
            bc_ref.at[rs], xpb, cwbc_ref, cbbc_ref, bcc_ref.at[rs], L, 2 * SSM_BC, ident)

    def st_o(g, val):
        o_ref[:, g * GROUP_W:(g + 1) * GROUP_W] = val.astype(o_ref.dtype)

    def set_h(s, e, val):
        hst_out_ref[s, e] = val

    _ssm_tile(
        gate_ref[...], gbias_ref[...], -jnp.exp(galog_ref[...]),
        lambda e: xc_ref[:, e * LANES:(e + 1) * LANES],
        lambda g: bcc_ref[:, g * SSM_N:(g + 1) * SSM_N],
        lambda g: bcc_ref[:, SSM_BC + g * SSM_N:SSM_BC + (g + 1) * SSM_N],
        lambda g: zs_ref[:, g * GROUP_W:(g + 1) * GROUP_W],
        dcols_ref, nw_ref, st_o, lambda s, e: hst_ref[s, e], set_h, L)


def _ssm_const_specs():
    const = lambda shape: pl.BlockSpec(shape, lambda *idx: (0,) * len(shape))
    return [
        const((CONV_W, SSM_DI)), const((1, SSM_DI)), const((CONV_W, 2 * SSM_BC)), const((1, 2 * SSM_BC)),
        const((1, LANES)), const((1, LANES)), const((1, SSM_DI)), const((1, SSM_DI)),
    ]


def _ssm_prompt_spec(proj, cwx, cbx, cwbc, cbbc, gbias, galog, dcols, nw, *, B, L, Lb):
    nl = L // Lb
    row = lambda b, l: b * nl + l
    return dict(
        in_specs=[
            pl.BlockSpec((Lb, SSM_DI), lambda b, l: (row(b, l), OFF_XS // SSM_DI)),
            pl.BlockSpec((Lb, 2 * SSM_BC), lambda b, l: (row(b, l), OFF_BC // (2 * SSM_BC))),
            pl.BlockSpec((Lb, SSM_DI), lambda b, l: (row(b, l), OFF_ZS // SSM_DI)),
            pl.BlockSpec((Lb, LANES), lambda b, l: (row(b, l), OFF_GATE // LANES)),
        ] + _ssm_const_specs(),
        out_specs=[
            pl.BlockSpec((Lb, SSM_DI), lambda b, l: (row(b, l), 0)),
            pl.BlockSpec((1, CONV_W - 1, SSM_CONV_CH), lambda b, l: (b, 0, 0)),
            pl.BlockSpec((1, N_PAIRS, 2 * SSM_P, SSM_N), lambda b, l: (b, 0, 0, 0)),
        ],
        out_shape=[
            jax.ShapeDtypeStruct((B * L, SSM_DI), BF16),
            jax.ShapeDtypeStruct((B, CONV_W - 1, SSM_CONV_CH), F32),
            jax.ShapeDtypeStruct((B, N_PAIRS, 2 * SSM_P, SSM_N), F32),
        ],
        scratch=[
            pltpu.VMEM((Lb + CONV_PAD, SSM_DI), F32),
            pltpu.VMEM((Lb + CONV_PAD, 2 * SSM_BC), F32),
            pltpu.VMEM((Lb, SSM_DI), F32),
            pltpu.VMEM((Lb, 2 * SSM_BC), F32),
            pltpu.VMEM((N_PAIRS, 2 * SSM_P, SSM_N), F32),
        ],
        args=[proj, proj, proj, proj, cwx, cbx, cwbc, cbbc, gbias, galog, dcols, nw])


def _ssm_sample_spec(proj, conv_state, h_pairs, cwx, cbx, cwbc, cbbc, gbias, galog, dcols, nw, *, B, L):
    nb = CHUNK // L
    return dict(
        in_specs=[
            pl.BlockSpec((CHUNK, SSM_DI), lambda i: (i, OFF_XS // SSM_DI)),
            pl.BlockSpec((CHUNK, 2 * SSM_BC), lambda i: (i, OFF_BC // (2 * SSM_BC))),
            pl.BlockSpec((CHUNK, SSM_DI), lambda i: (i, OFF_ZS // SSM_DI)),
            pl.BlockSpec((CHUNK, LANES), lambda i: (i, OFF_GATE // LANES)),
            pl.BlockSpec((CONV_W - 1, nb, SSM_CONV_CH), lambda i: (0, i, 0)),
            pl.BlockSpec((nb, N_PAIRS, 2 * SSM_P, SSM_N), lambda i: (i, 0, 0, 0)),
        ] + _ssm_const_specs(),
        out_specs=[
            pl.BlockSpec((CHUNK, SSM_DI), lambda i: (i, 0)),
            pl.BlockSpec((CONV_W - 1, nb, SSM_CONV_CH), lambda i: (0, i, 0)),
            pl.BlockSpec((nb, N_PAIRS, 2 * SSM_P, SSM_N), lambda i: (i, 0, 0, 0)),
        ],
        out_shape=[
            jax.ShapeDtypeStruct((B * L, SSM_DI), BF16),
            jax.ShapeDtypeStruct((CONV_W - 1, B, SSM_CONV_CH), F32),
            jax.ShapeDtypeStruct((B, N_PAIRS, 2 * SSM_P, SSM_N), F32),
        ],
        scratch=[
            pltpu.VMEM((nb, L + CONV_PAD, SSM_DI), F32),
            pltpu.VMEM((nb, L + CONV_PAD, 2 * SSM_BC), F32),
            pltpu.VMEM((CHUNK, SSM_DI), F32),
            pltpu.VMEM((CHUNK, 2 * SSM_BC), F32),
        ],
        args=[proj, proj, proj, proj, conv_state, h_pairs, cwx, cbx, cwbc, cbbc, gbias, galog, dcols, nw])


def _split_refs(refs, g, s, n_cast=0):
    it = iter(refs)
    take = lambda n: [next(it) for _ in range(n)]
    g_in, s_in, c_in = take(len(g["in_specs"])), take(len(s["in_specs"])), take(n_cast)
    g_out, s_out, c_out = take(len(g["out_specs"])), take(len(s["out_specs"])), take(n_cast)
    g_scr, s_scr = take(len(g["scratch"])), take(len(s["scratch"]))
    return g_in + g_out + g_scr, s_in + s_out + s_scr, list(zip(c_in, c_out))


def _mix_prompt_kernel(*refs, g, s, n_cast, Lb):
    g_refs, s_refs, casts = _split_refs(refs, g, s, n_cast)
    def cast_step(src_ref, dst_ref):
        dst_ref[...] = src_ref[...].astype(BF16)

    c_steps = [functools.partial(cast_step, a, b) for a, b in casts]
    gate_ref, gbias_ref, galog_ref = g_refs[2], g_refs[4], g_refs[5]
    memo = {}

    def gates(ci):
        if ci not in memo:
            graw = gate_ref[ci * CHUNK:(ci + 1) * CHUNK, :]
            memo[ci] = (graw,) + _gate_cumsum(graw, gbias_ref[...], -jnp.exp(galog_ref[...]), CHUNK)
        return memo[ci]

    g_init, g_steps, g_final = _gdn_prompt_parts(*g_refs, Lb=Lb, gates=gates)
    s_init, s_steps, s_final = _ssm_prompt_parts(*s_refs, Lb=Lb, gates=gates)
    l = pl.program_id(1)

    @pl.when(l == 0)
    def _init():
        g_init()
        s_init()

    for g_step, s_step, c_step in itertools.zip_longest(g_steps, s_steps, [None, None] + c_steps):
        for step in (g_step, c_step, s_step):
            if step is not None:
                step()

    @pl.when(l == pl.num_programs(1) - 1)
    def _final():
        g_final()
        s_final()


def _mix_sample_kernel(*refs, g, s, nb, L):
    g_refs, s_refs, _ = _split_refs(refs, g, s)
    _gdn_sample_kernel(*g_refs, nb=nb, L=L)
    _ssm_sample_kernel(*s_refs, L=L)


def _fused_call(kernel, g, s, grid, semantics, name, cast=()):
    n_steps = functools.reduce(lambda a, b: a * b, grid)
    step = (lambda b, l: (b * grid[1] + l, 0)) if len(grid) == 2 else (lambda i: (i, 0))
    assert all(w.shape[0] % (16 * n_steps) == 0 for w in cast), "cast row blocks must be whole bf16 tiles"
    c_specs = [pl.BlockSpec((w.shape[0] // n_steps, w.shape[1]), step) for w in cast]
    c_shape = [jax.ShapeDtypeStruct(w.shape, BF16) for w in cast]
    kw = dict(n_cast=len(cast)) if cast else {}
    outs = pl.pallas_call(
        functools.partial(kernel, g={k: g[k] for k in ("in_specs", "out_specs", "scratch")},
                          s={k: s[k] for k in ("in_specs", "out_specs", "scratch")}, **kw),
        grid=grid,
        in_specs=g["in_specs"] + s["in_specs"] + c_specs,
        out_specs=g["out_specs"] + s["out_specs"] + c_specs,
        out_shape=g["out_shape"] + s["out_shape"] + c_shape,
        scratch_shapes=g["scratch"] + s["scratch"],
        compiler_params=pltpu.CompilerParams(dimension_semantics=semantics, vmem_limit_bytes=VMEM_LIMIT),
        name=name,
    )(*g["args"], *s["args"], *cast)
    n, m = len(g["out_specs"]), len(g["out_specs"]) + len(s["out_specs"])
    return (outs[:n], outs[n:m]) + ((outs[m:],) if cast else ())


def _outproj_kernel(x_ref, mg_ref, ms_ref, w_ref, o_ref):
    acc = _dot(mg_ref[...].astype(BF16), w_ref[:GDN_V, :])
    acc = acc + _dot(ms_ref[...].astype(BF16), w_ref[GDN_V:, :])
    o_ref[...] = x_ref[...] + acc


def _out_proj(x2d, mix_g, mix_s, w_out16, *, tm):
    T = x2d.shape[0]
    return pl.pallas_call(
        _outproj_kernel,
        grid=(T // tm,),
        in_specs=[
            pl.BlockSpec((tm, D_MODEL), lambda i: (i, 0)),
            pl.BlockSpec((tm, GDN_V), lambda i: (i, 0)),
            pl.BlockSpec((tm, SSM_DI), lambda i: (i, 0)),
            pl.BlockSpec((D_MODEL, D_MODEL), lambda i: (0, 0)),
        ],
        out_specs=pl.BlockSpec((tm, D_MODEL), lambda i: (i, 0)),
        out_shape=jax.ShapeDtypeStruct((T, D_MODEL), F32),
        compiler_params=pltpu.CompilerParams(
            dimension_semantics=("parallel",), vmem_limit_bytes=VMEM_LIMIT),
        name="out_proj",
    )(x2d, mix_g, mix_s, w_out16)


FFN_SUB = 4


def _ffn_kernel(x_ref, nw_ref, wg_ref, wu_ref, wd_ref, fnw_ref, o_ref, h_ref):
    f = pl.program_id(1)
    nf = pl.num_programs(1)
    rs = h_ref.shape[0] // FFN_SUB

    def step(first, last):
        def gate_up(r):
            rows = slice(r * rs, (r + 1) * rs)
            if first:
                h = _rms_rows(x_ref[rows, :], nw_ref[...]).astype(BF16)
                h_ref[rows, :] = h
            else:
                h = h_ref[rows, :]
            return _dot(h, wg_ref[...]), _dot(h, wu_ref[...])

        def down(r, gu):
            rows = slice(r * rs, (r + 1) * rs)
            d = _dot((_silu(gu[0]) * gu[1]).astype(BF16), wd_ref[...])
            acc = d if first else o_ref[rows, :] + d
            if last:
                o_ref[rows, :] = _rms_rows(x_ref[rows, :] + acc, fnw_ref[...])
            else:
                o_ref[rows, :] = acc

        gu = gate_up(0)
        for r in range(1, FFN_SUB):
            gu_next = gate_up(r)
            down(r - 1, gu)
            gu = gu_next
        down(FFN_SUB - 1, gu)

    pl.when(f == 0)(lambda: step(True, False))
    pl.when((f > 0) & (f < nf - 1))(lambda: step(False, False))
    pl.when(f == nf - 1)(lambda: step(False, True))


def _ffn(x2d, norm_w, wg16, wu16, wd16, final_w, *, tm, tf):
    T = x2d.shape[0]
    return pl.pallas_call(
        _ffn_kernel,
        grid=(T // tm, D_FF // tf),
        in_specs=[
            pl.BlockSpec((tm, D_MODEL), lambda i, f: (i, 0)),
            pl.BlockSpec((1, D_MODEL), lambda i, f: (0, 0)),
            pl.BlockSpec((D_MODEL, tf), lambda i, f: (0, f)),
            pl.BlockSpec((D_MODEL, tf), lambda i, f: (0, f)),
            pl.BlockSpec((tf, D_MODEL), lambda i, f: (f, 0)),
            pl.BlockSpec((1, D_MODEL), lambda i, f: (0, 0)),
        ],
        out_specs=pl.BlockSpec((tm, D_MODEL), lambda i, f: (i, 0)),
        out_shape=jax.ShapeDtypeStruct((T, D_MODEL), F32),
        scratch_shapes=[pltpu.VMEM((tm, D_MODEL), BF16)],
        compiler_params=pltpu.CompilerParams(
            dimension_semantics=("parallel", "arbitrary"), vmem_limit_bytes=VMEM_LIMIT),
        name="ffn",
    )(x2d, norm_w, wg16, wu16, wd16, final_w)


PROMPT_ROWS = 256


def _trunk(x, states, p):
    B, L, _ = x.shape
    x2d = x.reshape(B * L, D_MODEL)
    proj = _in_proj(x2d, p["attn_norm_w"], p["w_in_r"], tm=IN_PROJ_TM, tn=IN_PROJ_TN)
    gdn_w = (p["gdn_conv_w"], p["gbias"], p["galog"], p["gdn_norm_w"])
    ssm_w = (p["cwx"], p["cbx"], p["cwbc"], p["cbbc"], p["gbias"], p["galog"], p["dcols"], p["ssm_norm_w"])
    pair_shape = (B, N_PAIRS, 2 * SSM_P, SSM_N)
    if states is None:
        Lb = PROMPT_ROWS
        g = _gdn_prompt_spec(proj, *gdn_w, B=B, L=L, Lb=Lb)
        s = _ssm_prompt_spec(proj, *ssm_w, B=B, L=L, Lb=Lb)
        kern = functools.partial(_mix_prompt_kernel, Lb=Lb)
        f32_weights = (p["w_gate"], p["w_up"], p["w_down"], p["w_out"])
        (mix_g, gconv_new, gS_new), (mix_s, sconv_new, sh_new), bf16_weights = _fused_call(
            kern, g, s, (B, L // Lb), ("parallel", "arbitrary"), "mix_prompt", cast=f32_weights)
        p["wg16"], p["wu16"], p["wd16"], p["w_out16"] = bf16_weights
    else:
        gconv, gS, sconv, sh = states
        nb = CHUNK // L
        tap_major = lambda a: jnp.swapaxes(a, 0, 1)
        g = _gdn_sample_spec(proj, tap_major(gconv), gS, *gdn_w, B=B, L=L, nb=nb)
        s = _ssm_sample_spec(proj, tap_major(sconv), sh.reshape(pair_shape), *ssm_w, B=B, L=L)
        kern = functools.partial(_mix_sample_kernel, nb=nb, L=L)
        (mix_g, gconv_new, gS_new), (mix_s, sconv_new, sh_new) = _fused_call(
            kern, g, s, (B // nb,), ("parallel",), "mix_sample")
        gconv_new, sconv_new = tap_major(gconv_new), tap_major(sconv_new)
    sh_new = sh_new.reshape(B, SSM_HEADS, SSM_P, SSM_N)
    x1 = _out_proj(x2d, mix_g, mix_s, p["w_out16"], tm=OUT_PROJ_TM)
    y = _ffn(x1, p["ffn_norm_w"], p["wg16"], p["wu16"], p["wd16"], p["final_norm_w"], tm=FFN_TM, tf=FFN_TF)
    return y.reshape(B, L, D_MODEL), (gconv_new[None], gS_new[None], sconv_new[None], sh_new[None])


def kernel(x_prompt, x_sample, state_gdn_conv, state_gdn, state_ssm_conv, state_ssm,
           attn_norm_w, w_in, gdn_conv_w, gdn_A_log, gdn_dt_bias, gdn_norm_w,
           ssm_conv_w, ssm_conv_b, ssm_A_log, ssm_dt_bias, ssm_D, ssm_norm_w,
           w_out, ffn_norm_w, w_gate, w_up, w_down, final_norm_w):
    assert w_in.shape[0] == 1, "single-layer trunk"
    assert x_prompt.shape[1] % PROMPT_ROWS == 0 and CHUNK % x_sample.shape[1] == 0 and x_sample.shape[0] % (CHUNK // x_sample.shape[1]) == 0
    assert w_in.shape[2] == D_IN_PROJ
    w_in_r = _w_in_prep(jnp.swapaxes(w_in, 1, 2), tk=W_PREP_TK)
    zeros8 = jnp.zeros((GDN_HEADS,), F32)
    tail = jnp.zeros((LANES - GATE_DT2 - SSM_HEADS,), F32)
    gbias = jnp.concatenate([zeros8, gdn_dt_bias[0], ssm_dt_bias[0], ssm_dt_bias[0], tail])[None]
    galog = jnp.concatenate([zeros8, gdn_A_log[0], ssm_A_log[0], ssm_A_log[0], tail])[None]
    p = dict(
        attn_norm_w=attn_norm_w, w_in_r=w_in_r, gdn_conv_w=gdn_conv_w[0], gbias=gbias, galog=galog,
        gdn_norm_w=gdn_norm_w,
        cwx=ssm_conv_w[0][:, :SSM_DI], cbx=ssm_conv_b[:, :SSM_DI],
        cwbc=ssm_conv_w[0][:, SSM_DI:], cbbc=ssm_conv_b[:, SSM_DI:],
        dcols=jnp.repeat(ssm_D[0], SSM_P)[None], ssm_norm_w=ssm_norm_w,
        w_out=w_out[0], w_down=w_down[0], ffn_norm_w=ffn_norm_w,
        w_gate=w_gate[0], w_up=w_up[0],
        final_norm_w=final_norm_w[None],
    )
    y_p, st_p = _trunk(x_prompt, None, p)
    y_s, st_s = _trunk(x_sample, (state_gdn_conv[0], state_gdn[0], state_ssm_conv[0], state_ssm[0]), p)
    return (y_p, y_s, st_p[0], st_p[1], st_p[2], st_p[3], st_s[0], st_s[1], st_s[2], st_s[3])
```
